```python
import math
import jax, jax.numpy as jnp
from jax import lax
import numpy as np

D_MODEL = 2048
BATCH = 8
SEQ = 8192
DEPTH = 4

CHUNK = 64
N_MIXERS = 3
N_A = (DEPTH + 2) // 3
N_B = (DEPTH + 1) // 3
N_C = DEPTH // 3
CONV_WIDTH = 31
POOL_WINDOWS = (2, 4, 8, 16)
N_POOL_GROUPS = len(POOL_WINDOWS)
POOL_GC = D_MODEL // N_POOL_GROUPS
HEAD_DIM = 64
N_HEADS = D_MODEL // HEAD_DIM
N_KV = 8
GROUP = N_HEADS // N_KV
WINDOW = 128
WINDOW_CHUNKS = WINDOW // CHUNK
QBLOCK = 128
NUM_BUCKETS = 32
REL_MAX_DIST = 128
D_FF = ((8 * D_MODEL // 3 + 255) // 256) * 256
PLE_DIM = 256
EPS = 1e-6
NEG_INF = -1e30

kernel_name = "hybrid_conv_pool_swa_trunk"


def rms_norm(x, g):
    xf = x.astype(jnp.float32)
    y = xf * lax.rsqrt(jnp.mean(xf * xf, axis=-1, keepdims=True) + EPS)
    return (y * g.astype(jnp.float32)).astype(x.dtype)


def layer_norm(x, g, b):
    xf = x.astype(jnp.float32)
    mu = jnp.mean(xf, axis=-1, keepdims=True)
    xc = xf - mu
    y = xc * lax.rsqrt(jnp.mean(xc * xc, axis=-1, keepdims=True) + EPS)
    return (y * g.astype(jnp.float32) + b.astype(jnp.float32)).astype(x.dtype)


def conformer_conv(h, w_in, b_in, w_dw, b_dw, ln_g, ln_b, w_out, b_out):
    u = h @ w_in + b_in
    a, gate = jnp.split(u, 2, axis=-1)
    u = a * jax.nn.sigmoid(gate)
    u = lax.conv_general_dilated(
        u, w_dw[:, None, :], window_strides=(1,), padding=[(CONV_WIDTH - 1, 0)],
        dimension_numbers=("NWC", "WIO", "NWC"), feature_group_count=D_MODEL) + b_dw
    u = jax.nn.silu(layer_norm(u, ln_g, ln_b))
    return u @ w_out + b_out


def multiscale_pool(h, w_grp, scale):
    B, S, D = h.shape
    hf = h.astype(jnp.float32)
    cs = jnp.concatenate([jnp.zeros((B, 1, D), jnp.float32), jnp.cumsum(hf, axis=1)], axis=1)
    t = jnp.arange(S)
    pooled = []
    for g, w in enumerate(POOL_WINDOWS):
        sl = slice(g * POOL_GC, (g + 1) * POOL_GC)
        start = jnp.maximum(t + 1 - w, 0)
        s = cs[:, 1:, sl] - cs[:, start, sl]
        cnt = (t + 1 - start).astype(jnp.float32)[None, :, None]
        pooled.append(s / cnt)
    pooled = jnp.stack(pooled, axis=2)
    mix = (pooled - hf.reshape(B, S, N_POOL_GROUPS, POOL_GC)).astype(h.dtype)
    y = jnp.einsum('bsgc,gcd->bsgd', mix, w_grp).reshape(B, S, D)
    return y * scale


def t5_bucket(rel):
    nb = NUM_BUCKETS // 2
    n = -rel
    ret = jnp.where(n < 0, nb, 0)
    n = jnp.abs(n)
    max_exact = nb // 2
    nf = jnp.maximum(n, 1).astype(jnp.float32)
    large = max_exact + (jnp.log(nf / max_exact) / math.log(REL_MAX_DIST / max_exact)
                         * (nb - max_exact)).astype(jnp.int32)
    large = jnp.minimum(large, nb - 1)
    return ret + jnp.where(n < max_exact, n, large)


def band_bias_and_mask(rel_bias, n_blocks):
    i = jnp.arange(QBLOCK)[:, None]
    j = jnp.arange(2 * QBLOCK)[None, :]
    rel = j - QBLOCK - i
    bias = rel_bias[t5_bucket(rel)]
    bias = jnp.transpose(bias, (2, 0, 1)).reshape(N_KV, GROUP, QBLOCK, 2 * QBLOCK)
    qc = i // CHUNK
    kc = jnp.floor_divide(j - QBLOCK, CHUNK)
    chunk_ok = (kc <= qc) & (kc >= qc - WINDOW_CHUNKS)
    blk = jnp.arange(n_blocks)[:, None, None]
    mask = chunk_ok[None] & ((blk > 0) | (j[None] >= QBLOCK))
    return bias, mask


def swa_sink_attention(h, w_qkv, q_g, k_g, sinks, w_o, rel_bias):
    B, S, _ = h.shape
    NB = S // QBLOCK
    qkv = h @ w_qkv
    q, k, v = jnp.split(qkv, [N_HEADS * HEAD_DIM, (N_HEADS + N_KV) * HEAD_DIM], axis=-1)
    q = rms_norm(q.reshape(B, S, N_KV, GROUP, HEAD_DIM), q_g)
    k = rms_norm(k.reshape(B, S, N_KV, HEAD_DIM), k_g)
    v = v.reshape(B, S, N_KV, HEAD_DIM)
    q = q.reshape(B, NB, QBLOCK, N_KV, GROUP, HEAD_DIM)

    def band(t):
        tb = t.reshape(B, NB, QBLOCK, N_KV, HEAD_DIM)
        prev = jnp.concatenate([jnp.zeros_like(tb[:, :1]), tb[:, :-1]], axis=1)
        return jnp.concatenate([prev, tb], axis=2)

    kb, vb = band(k), band(v)
    bias, mask = band_bias_and_mask(rel_bias, NB)
    logits = jnp.einsum('bnqhgd,bnkhd->bnhgqk', q, kb,
                        preferred_element_type=jnp.float32) * (HEAD_DIM ** -0.5)
    logits = logits + bias.astype(jnp.float32)
    logits = jnp.where(mask[None, :, None, None], logits, NEG_INF)
    sink = sinks.astype(jnp.float32).reshape(1, 1, N_KV, GROUP, 1, 1)
    m = jnp.maximum(jnp.max(logits, axis=-1, keepdims=True), sink)
    e = jnp.exp(logits - m)
    denom = jnp.sum(e, axis=-1, keepdims=True) + jnp.exp(sink - m)
    probs = (e / denom).astype(v.dtype)
    o = jnp.einsum('bnhgqk,bnkhd->bnqhgd', probs, vb).reshape(B, S, N_HEADS * HEAD_DIM)
    return o @ w_o


def _fwd_setup_inputs(seed: int = 0) -> dict:
    key = jax.random.key(seed)
    ks = jax.random.split(key, 32)
    f32 = jnp.float32
    nrm = lambda k, shape, scale: jax.random.normal(k, shape, f32) * scale
    gain = lambda k, shape: 1.0 + 0.02 * jax.random.normal(k, shape, f32)
    D = D_MODEL
    return {
        "x": nrm(ks[0], (BATCH, SEQ, D), 1.0),
        "p": nrm(ks[1], (DEPTH, BATCH, SEQ, PLE_DIM), 1.0),
        "norm_mix": gain(ks[2], (DEPTH, D)),
        "norm_ffn": gain(ks[3], (DEPTH, D)),
        "norm_ple": gain(ks[4], (DEPTH, D)),
        "conv_w_in": nrm(ks[5], (N_A, D, 2 * D), D ** -0.5),
        "conv_b_in": nrm(ks[6], (N_A, 2 * D), 0.02),
        "conv_w_dw": nrm(ks[7], (N_A, CONV_WIDTH, D), CONV_WIDTH ** -0.5),
        "conv_b_dw": nrm(ks[8], (N_A, D), 0.02),
        "conv_ln_g": gain(ks[9], (N_A, D)),
        "conv_ln_b": nrm(ks[10], (N_A, D), 0.02),
        "conv_w_out": nrm(ks[11], (N_A, D, D), D ** -0.5),
        "conv_b_out": nrm(ks[12], (N_A, D), 0.02),
        "pool_w": nrm(ks[13], (N_B, N_POOL_GROUPS, POOL_GC, POOL_GC), POOL_GC ** -0.5),
        "pool_scale": 0.5 + 0.05 * jax.random.normal(ks[14], (N_B, D), f32),
        "attn_w_qkv": nrm(ks[15], (N_C, D, (N_HEADS + 2 * N_KV) * HEAD_DIM), D ** -0.5),
        "attn_q_norm": gain(ks[16], (N_C, HEAD_DIM)),
        "attn_k_norm": gain(ks[17], (N_C, HEAD_DIM)),
        "attn_sinks": nrm(ks[18], (N_C, N_HEADS), 0.5),
        "attn_w_o": nrm(ks[19], (N_C, N_HEADS * HEAD_DIM, D), (N_HEADS * HEAD_DIM) ** -0.5),
        "rel_bias": nrm(ks[20], (NUM_BUCKETS, N_HEADS), 0.5),
        "ffn_w_gate": nrm(ks[21], (DEPTH, D, D_FF), D ** -0.5),
        "ffn_w_up": nrm(ks[22], (DEPTH, D, D_FF), D ** -0.5),
        "ffn_w_down": nrm(ks[23], (DEPTH, D_FF, D), D_FF ** -0.5),
        "ple_w_proj": nrm(ks[24], (DEPTH, PLE_DIM, D), PLE_DIM ** -0.5),
        "ple_w_gate": nrm(ks[25], (DEPTH, D, D), D ** -0.5),
        "ple_b_gate": nrm(ks[26], (DEPTH, D), 0.02),
    }


def _fwd_reference(x, p, norm_mix, norm_ffn, norm_ple,
              conv_w_in, conv_b_in, conv_w_dw, conv_b_dw, conv_ln_g, conv_ln_b, conv_w_out, conv_b_out,
              pool_w, pool_scale,
              attn_w_qkv, attn_q_norm, attn_k_norm, attn_sinks, attn_w_o, rel_bias,
              ffn_w_gate, ffn_w_up, ffn_w_down,
              ple_w_proj, ple_w_gate, ple_b_gate):
    for i in range(DEPTH):
        kind, j = i % N_MIXERS, i // N_MIXERS
        h = rms_norm(x, norm_mix[i])
        if kind == 0:
            y = conformer_conv(h, conv_w_in[j], conv_b_in[j], conv_w_dw[j], conv_b_dw[j],
                               conv_ln_g[j], conv_ln_b[j], conv_w_out[j], conv_b_out[j])
        elif kind == 1:
            y = multiscale_pool(h, pool_w[j], pool_scale[j])
        else:
            y = swa_sink_attention(h, attn_w_qkv[j], attn_q_norm[j], attn_k_norm[j],
                                   attn_sinks[j], attn_w_o[j], rel_bias)
        x = x + y
        h = rms_norm(x, norm_ffn[i])
        x = x + (jax.nn.silu(h @ ffn_w_gate[i]) * (h @ ffn_w_up[i])) @ ffn_w_down[i]
        g = jax.nn.sigmoid(rms_norm(x, norm_ple[i]) @ ple_w_gate[i] + ple_b_gate[i])
        x = x + g * (p[i] @ ple_w_proj[i])
    return x


import jax as _jax
import jax.numpy as _jnp

TWIN_FORMAT = 'train_step'
FWD_PARAMS = ['x', 'p', 'norm_mix', 'norm_ffn', 'norm_ple', 'conv_w_in', 'conv_b_in', 'conv_w_dw', 'conv_b_dw', 'conv_ln_g', 'conv_ln_b', 'conv_w_out', 'conv_b_out', 'pool_w', 'pool_scale', 'attn_w_qkv', 'attn_q_norm', 'attn_k_norm', 'attn_sinks', 'attn_w_o', 'rel_bias', 'ffn_w_gate', 'ffn_w_up', 'ffn_w_down', 'ple_w_proj', 'ple_w_gate', 'ple_b_gate']
TWIN_WEIGHTS = ['norm_mix', 'norm_ffn', 'norm_ple', 'conv_w_in', 'conv_b_in', 'conv_w_dw', 'conv_b_dw', 'conv_ln_g', 'conv_ln_b', 'conv_w_out', 'conv_b_out', 'pool_w', 'pool_scale', 'attn_w_qkv', 'attn_q_norm', 'attn_k_norm', 'attn_sinks', 'attn_w_o', 'rel_bias', 'ffn_w_gate', 'ffn_w_up', 'ffn_w_down', 'ple_w_proj', 'ple_w_gate', 'ple_b_gate']
TWIN_DIFF_INPUT = 'x'
TWIN_INPUTS = ['x', 'p', 'norm_mix', 'norm_ffn', 'norm_ple', 'conv_w_in', 'conv_b_in', 'conv_w_dw', 'conv_b_dw', 'conv_ln_g', 'conv_ln_b', 'conv_w_out', 'conv_b_out', 'pool_w', 'pool_scale', 'attn_w_qkv', 'attn_q_norm', 'attn_k_norm', 'attn_sinks', 'attn_w_o', 'rel_bias', 'ffn_w_gate', 'ffn_w_up', 'ffn_w_down', 'ple_w_proj', 'ple_w_gate', 'ple_b_gate', 'loss_target', 'm_norm_mix', 'm_norm_ffn', 'm_norm_ple', 'm_conv_w_in', 'm_conv_b_in', 'm_conv_w_dw', 'm_conv_b_dw', 'm_conv_ln_g', 'm_conv_ln_b', 'm_conv_w_out', 'm_conv_b_out', 'm_pool_w', 'm_pool_scale', 'm_attn_w_qkv', 'm_attn_q_norm', 'm_attn_k_norm', 'm_attn_sinks', 'm_attn_w_o', 'm_rel_bias', 'm_ffn_w_gate', 'm_ffn_w_up', 'm_ffn_w_down', 'm_ple_w_proj', 'm_ple_w_gate', 'm_ple_b_gate', 'v_norm_mix', 'v_norm_ffn', 'v_norm_ple', 'v_conv_w_in', 'v_conv_b_in', 'v_conv_w_dw', 'v_conv_b_dw', 'v_conv_ln_g', 'v_conv_ln_b', 'v_conv_w_out', 'v_conv_b_out', 'v_pool_w', 'v_pool_scale', 'v_attn_w_qkv', 'v_attn_q_norm', 'v_attn_k_norm', 'v_attn_sinks', 'v_attn_w_o', 'v_rel_bias', 'v_ffn_w_gate', 'v_ffn_w_up', 'v_ffn_w_down', 'v_ple_w_proj', 'v_ple_w_gate', 'v_ple_b_gate']
TWIN_OUTPUTS = ['loss', 'grad_x', 'grad_norm_mix', 'grad_norm_ffn', 'grad_norm_ple', 'grad_conv_w_in', 'grad_conv_b_in', 'grad_conv_w_dw', 'grad_conv_b_dw', 'grad_conv_ln_g', 'grad_conv_ln_b', 'grad_conv_w_out', 'grad_conv_b_out', 'grad_pool_w', 'grad_pool_scale', 'grad_attn_w_qkv', 'grad_attn_q_norm', 'grad_attn_k_norm', 'grad_attn_sinks', 'grad_attn_w_o', 'grad_rel_bias', 'grad_ffn_w_gate', 'grad_ffn_w_up', 'grad_ffn_w_down', 'grad_ple_w_proj', 'grad_ple_w_gate', 'grad_ple_b_gate', 'delta_norm_mix', 'delta_norm_ffn', 'delta_norm_ple', 'delta_conv_w_in', 'delta_conv_b_in', 'delta_conv_w_dw', 'delta_conv_b_dw', 'delta_conv_ln_g', 'delta_conv_ln_b', 'delta_conv_w_out', 'delta_conv_b_out', 'delta_pool_w', 'delta_pool_scale', 'delta_attn_w_qkv', 'delta_attn_q_norm', 'delta_attn_k_norm', 'delta_attn_sinks', 'delta_attn_w_o', 'delta_rel_bias', 'delta_ffn_w_gate', 'delta_ffn_w_up', 'delta_ffn_w_down', 'delta_ple_w_proj', 'delta_ple_w_gate', 'delta_ple_b_gate', 'new_m_norm_mix', 'new_m_norm_ffn', 'new_m_norm_ple', 'new_m_conv_w_in', 'new_m_conv_b_in', 'new_m_conv_w_dw', 'new_m_conv_b_dw', 'new_m_conv_ln_g', 'new_m_conv_ln_b', 'new_m_conv_w_out', 'new_m_conv_b_out', 'new_m_pool_w', 'new_m_pool_scale', 'new_m_attn_w_qkv', 'new_m_attn_q_norm', 'new_m_attn_k_norm', 'new_m_attn_sinks', 'new_m_attn_w_o', 'new_m_rel_bias', 'new_m_ffn_w_gate', 'new_m_ffn_w_up', 'new_m_ffn_w_down', 'new_m_ple_w_proj', 'new_m_ple_w_gate', 'new_m_ple_b_gate', 'new_v_norm_mix', 'new_v_norm_ffn', 'new_v_norm_ple', 'new_v_conv_w_in', 'new_v_conv_b_in', 'new_v_conv_w_dw', 'new_v_conv_b_dw', 'new_v_conv_ln_g', 'new_v_conv_ln_b', 'new_v_conv_w_out', 'new_v_conv_b_out', 'new_v_pool_w', 'new_v_pool_scale', 'new_v_attn_w_qkv', 'new_v_attn_q_norm', 'new_v_attn_k_norm', 'new_v_attn_sinks', 'new_v_attn_w_o', 'new_v_rel_bias', 'new_v_ffn_w_gate', 'new_v_ffn_w_up', 'new_v_ffn_w_down', 'new_v_ple_w_proj', 'new_v_ple_w_gate', 'new_v_ple_b_gate']
TWIN_LEAF_KINDS = {'loss': 'loss', 'grad_x': 'grad_x', 'grad_norm_mix': 'grad_w', 'grad_norm_ffn': 'grad_w', 'grad_norm_ple': 'grad_w', 'grad_conv_w_in': 'grad_w', 'grad_conv_b_in': 'grad_w', 'grad_conv_w_dw': 'grad_w', 'grad_conv_b_dw': 'grad_w', 'grad_conv_ln_g': 'grad_w', 'grad_conv_ln_b': 'grad_w', 'grad_conv_w_out': 'grad_w', 'grad_conv_b_out': 'grad_w', 'grad_pool_w': 'grad_w', 'grad_pool_scale': 'grad_w', 'grad_attn_w_qkv': 'grad_w', 'grad_attn_q_norm': 'grad_w', 'grad_attn_k_norm': 'grad_w', 'grad_attn_sinks': 'grad_w', 'grad_attn_w_o': 'grad_w', 'grad_rel_bias': 'grad_w', 'grad_ffn_w_gate': 'grad_w', 'grad_ffn_w_up': 'grad_w', 'grad_ffn_w_down': 'grad_w', 'grad_ple_w_proj': 'grad_w', 'grad_ple_w_gate': 'grad_w', 'grad_ple_b_gate': 'grad_w', 'delta_norm_mix': 'delta_w', 'delta_norm_ffn': 'delta_w', 'delta_norm_ple': 'delta_w', 'delta_conv_w_in': 'delta_w', 'delta_conv_b_in': 'delta_w', 'delta_conv_w_dw': 'delta_w', 'delta_conv_b_dw': 'delta_w', 'delta_conv_ln_g': 'delta_w', 'delta_conv_ln_b': 'delta_w', 'delta_conv_w_out': 'delta_w', 'delta_conv_b_out': 'delta_w', 'delta_pool_w': 'delta_w', 'delta_pool_scale': 'delta_w', 'delta_attn_w_qkv': 'delta_w', 'delta_attn_q_norm': 'delta_w', 'delta_attn_k_norm': 'delta_w', 'delta_attn_sinks': 'delta_w', 'delta_attn_w_o': 'delta_w', 'delta_rel_bias': 'delta_w', 'delta_ffn_w_gate': 'delta_w', 'delta_ffn_w_up': 'delta_w', 'delta_ffn_w_down': 'delta_w', 'delta_ple_w_proj': 'delta_w', 'delta_ple_w_gate': 'delta_w', 'delta_ple_b_gate': 'delta_w', 'new_m_norm_mix': 'new_m', 'new_m_norm_ffn': 'new_m', 'new_m_norm_ple': 'new_m', 'new_m_conv_w_in': 'new_m', 'new_m_conv_b_in': 'new_m', 'new_m_conv_w_dw': 'new_m', 'new_m_conv_b_dw': 'new_m', 'new_m_conv_ln_g': 'new_m', 'new_m_conv_ln_b': 'new_m', 'new_m_conv_w_out': 'new_m', 'new_m_conv_b_out': 'new_m', 'new_m_pool_w': 'new_m', 'new_m_pool_scale': 'new_m', 'new_m_attn_w_qkv': 'new_m', 'new_m_attn_q_norm': 'new_m', 'new_m_attn_k_norm': 'new_m', 'new_m_attn_sinks': 'new_m', 'new_m_attn_w_o': 'new_m', 'new_m_rel_bias': 'new_m', 'new_m_ffn_w_gate': 'new_m', 'new_m_ffn_w_up': 'new_m', 'new_m_ffn_w_down': 'new_m', 'new_m_ple_w_proj': 'new_m', 'new_m_ple_w_gate': 'new_m', 'new_m_ple_b_gate': 'new_m', 'new_v_norm_mix': 'new_v', 'new_v_norm_ffn': 'new_v', 'new_v_norm_ple': 'new_v', 'new_v_conv_w_in': 'new_v', 'new_v_conv_b_in': 'new_v', 'new_v_conv_w_dw': 'new_v', 'new_v_conv_b_dw': 'new_v', 'new_v_conv_ln_g': 'new_v', 'new_v_conv_ln_b': 'new_v', 'new_v_conv_w_out': 'new_v', 'new_v_conv_b_out': 'new_v', 'new_v_pool_w': 'new_v', 'new_v_pool_scale': 'new_v', 'new_v_attn_w_qkv': 'new_v', 'new_v_attn_q_norm': 'new_v', 'new_v_attn_k_norm': 'new_v', 'new_v_attn_sinks': 'new_v', 'new_v_attn_w_o': 'new_v', 'new_v_rel_bias': 'new_v', 'new_v_ffn_w_gate': 'new_v', 'new_v_ffn_w_up': 'new_v', 'new_v_ffn_w_down': 'new_v', 'new_v_ple_w_proj': 'new_v', 'new_v_ple_w_gate': 'new_v', 'new_v_ple_b_gate': 'new_v'}


def _forward(args):
    return _fwd_reference(*[args[k] for k in FWD_PARAMS])


def _output_shape():
    def fwd():
        inp = _fwd_setup_inputs(0)
        return _fwd_reference(*[inp[k] for k in FWD_PARAMS])
    out = _jax.eval_shape(fwd)
    return out.shape, out.dtype

N_MICROBATCH = 1
ADAM_LR = 0.001
ADAM_B1 = 0.9
ADAM_B2 = 0.999
ADAM_EPS = 1e-08
ADAM_WD = 0.01
ADAM_STEP = 10
PER_EXAMPLE_BATCH_AXIS = {'x': 0, 'p': 1, 'loss_target': 0}
SHARED_INPUTS = []
_WEIGHT_DTYPES = {'norm_mix': _jnp.float32, 'norm_ffn': _jnp.float32, 'norm_ple': _jnp.float32, 'conv_w_in': _jnp.float32, 'conv_b_in': _jnp.float32, 'conv_w_dw': _jnp.float32, 'conv_b_dw': _jnp.float32, 'conv_ln_g': _jnp.float32, 'conv_ln_b': _jnp.float32, 'conv_w_out': _jnp.float32, 'conv_b_out': _jnp.float32, 'pool_w': _jnp.float32, 'pool_scale': _jnp.float32, 'attn_w_qkv': _jnp.float32, 'attn_q_norm': _jnp.float32, 'attn_k_norm': _jnp.float32, 'attn_sinks': _jnp.float32, 'attn_w_o': _jnp.float32, 'rel_bias': _jnp.float32, 'ffn_w_gate': _jnp.float32, 'ffn_w_up': _jnp.float32, 'ffn_w_down': _jnp.float32, 'ple_w_proj': _jnp.float32, 'ple_w_gate': _jnp.float32, 'ple_b_gate': _jnp.float32}
MOMENT_SCALE = {'norm_mix': 3.015026e+00, 'norm_ffn': 2.461130e+01, 'norm_ple': 1.041157e+00, 'conv_w_in': 3.375967e-01, 'conv_b_in': 5.377078e+00, 'conv_w_dw': 8.240369e-01, 'conv_b_dw': 1.308270e+01, 'conv_ln_g': 1.553100e+01, 'conv_ln_b': 1.206545e+01, 'conv_w_out': 2.941546e+00, 'conv_b_out': 1.497280e+01, 'pool_w': 6.315918e-01, 'pool_scale': 1.213134e+01, 'attn_w_qkv': 1.521285e+00, 'attn_q_norm': 9.678684e+00, 'attn_k_norm': 9.690657e+00, 'attn_sinks': 1.624697e-01, 'attn_w_o': 1.514775e+00, 'rel_bias': 5.388858e-01, 'ffn_w_gate': 4.258315e-01, 'ffn_w_up': 2.979282e-01, 'ffn_w_down': 4.670629e-01, 'ple_w_proj': 4.383600e-01, 'ple_w_gate': 4.062002e-01, 'ple_b_gate': 3.294523e+00}


def _to_microbatches(a, axis):
    t = _jnp.moveaxis(a, axis, 0)
    t = t.reshape((N_MICROBATCH, t.shape[0] // N_MICROBATCH) + t.shape[1:])
    return _jnp.moveaxis(t, 1, axis + 1)


def setup_inputs(seed: int = 0) -> dict:
    inp = _fwd_setup_inputs(seed)
    key = _jax.random.fold_in(_jax.random.key(seed), 7919)
    shape, _ = _output_shape()
    out = dict(inp)
    out["loss_target"] = _jax.random.normal(_jax.random.fold_in(key, 0), shape, _jnp.float32)
    for i, name in enumerate(TWIN_WEIGHTS):
        w = inp[name].astype(_jnp.float32)
        if MOMENT_SCALE is None:
            s = _jnp.sqrt(_jnp.mean(_jnp.square(w)) + 1e-30)
        else:
            s = MOMENT_SCALE[name]
        km, kv = _jax.random.split(_jax.random.fold_in(key, i + 1))
        out[name] = w
        out["m_" + name] = s * _jax.random.normal(km, w.shape, _jnp.float32)
        out["v_" + name] = (s * s) * _jax.random.uniform(kv, w.shape, _jnp.float32, 0.5, 1.5)
    if N_MICROBATCH > 1:
        for name, axis in PER_EXAMPLE_BATCH_AXIS.items():
            out[name] = _to_microbatches(out[name], axis)
    return {'x': out['x'], 'p': out['p'], 'norm_mix': out['norm_mix'], 'norm_ffn': out['norm_ffn'], 'norm_ple': out['norm_ple'], 'conv_w_in': out['conv_w_in'], 'conv_b_in': out['conv_b_in'], 'conv_w_dw': out['conv_w_dw'], 'conv_b_dw': out['conv_b_dw'], 'conv_ln_g': out['conv_ln_g'], 'conv_ln_b': out['conv_ln_b'], 'conv_w_out': out['conv_w_out'], 'conv_b_out': out['conv_b_out'], 'pool_w': out['pool_w'], 'pool_scale': out['pool_scale'], 'attn_w_qkv': out['attn_w_qkv'], 'attn_q_norm': out['attn_q_norm'], 'attn_k_norm': out['attn_k_norm'], 'attn_sinks': out['attn_sinks'], 'attn_w_o': out['attn_w_o'], 'rel_bias': out['rel_bias'], 'ffn_w_gate': out['ffn_w_gate'], 'ffn_w_up': out['ffn_w_up'], 'ffn_w_down': out['ffn_w_down'], 'ple_w_proj': out['ple_w_proj'], 'ple_w_gate': out['ple_w_gate'], 'ple_b_gate': out['ple_b_gate'], 'loss_target': out['loss_target'], 'm_norm_mix': out['m_norm_mix'], 'm_norm_ffn': out['m_norm_ffn'], 'm_norm_ple': out['m_norm_ple'], 'm_conv_w_in': out['m_conv_w_in'], 'm_conv_b_in': out['m_conv_b_in'], 'm_conv_w_dw': out['m_conv_w_dw'], 'm_conv_b_dw': out['m_conv_b_dw'], 'm_conv_ln_g': out['m_conv_ln_g'], 'm_conv_ln_b': out['m_conv_ln_b'], 'm_conv_w_out': out['m_conv_w_out'], 'm_conv_b_out': out['m_conv_b_out'], 'm_pool_w': out['m_pool_w'], 'm_pool_scale': out['m_pool_scale'], 'm_attn_w_qkv': out['m_attn_w_qkv'], 'm_attn_q_norm': out['m_attn_q_norm'], 'm_attn_k_norm': out['m_attn_k_norm'], 'm_attn_sinks': out['m_attn_sinks'], 'm_attn_w_o': out['m_attn_w_o'], 'm_rel_bias': out['m_rel_bias'], 'm_ffn_w_gate': out['m_ffn_w_gate'], 'm_ffn_w_up': out['m_ffn_w_up'], 'm_ffn_w_down': out['m_ffn_w_down'], 'm_ple_w_proj': out['m_ple_w_proj'], 'm_ple_w_gate': out['m_ple_w_gate'], 'm_ple_b_gate': out['m_ple_b_gate'], 'v_norm_mix': out['v_norm_mix'], 'v_norm_ffn': out['v_norm_ffn'], 'v_norm_ple': out['v_norm_ple'], 'v_conv_w_in': out['v_conv_w_in'], 'v_conv_b_in': out['v_conv_b_in'], 'v_conv_w_dw': out['v_conv_w_dw'], 'v_conv_b_dw': out['v_conv_b_dw'], 'v_conv_ln_g': out['v_conv_ln_g'], 'v_conv_ln_b': out['v_conv_ln_b'], 'v_conv_w_out': out['v_conv_w_out'], 'v_conv_b_out': out['v_conv_b_out'], 'v_pool_w': out['v_pool_w'], 'v_pool_scale': out['v_pool_scale'], 'v_attn_w_qkv': out['v_attn_w_qkv'], 'v_attn_q_norm': out['v_attn_q_norm'], 'v_attn_k_norm': out['v_attn_k_norm'], 'v_attn_sinks': out['v_attn_sinks'], 'v_attn_w_o': out['v_attn_w_o'], 'v_rel_bias': out['v_rel_bias'], 'v_ffn_w_gate': out['v_ffn_w_gate'], 'v_ffn_w_up': out['v_ffn_w_up'], 'v_ffn_w_down': out['v_ffn_w_down'], 'v_ple_w_proj': out['v_ple_w_proj'], 'v_ple_w_gate': out['v_ple_w_gate'], 'v_ple_b_gate': out['v_ple_b_gate']}


def _loss(weights, diff, rest, loss_target):
    with _jax.named_scope("forward"):
        args = {**rest, TWIN_DIFF_INPUT: diff, **{k: w.astype(_WEIGHT_DTYPES[k]) for k, w in weights.items()}}
        y = _forward(args)
    with _jax.named_scope("loss_head"):
        err = _jnp.square(y.astype(_jnp.float32) - loss_target)
        return 0.5 * _jnp.sum(_jnp.mean(err, axis=-1)) if err.ndim else 0.5 * err


def _adamw(w, g, m, v):
    m = ADAM_B1 * m + (1.0 - ADAM_B1) * g
    v = ADAM_B2 * v + (1.0 - ADAM_B2) * _jnp.square(g)
    m_hat = m / (1.0 - ADAM_B1 ** ADAM_STEP)
    v_hat = v / (1.0 - ADAM_B2 ** ADAM_STEP)
    delta = -ADAM_LR * (m_hat / (_jnp.sqrt(v_hat) + ADAM_EPS) + ADAM_WD * w)
    return delta, m, v


def reference(x, p, norm_mix, norm_ffn, norm_ple, conv_w_in, conv_b_in, conv_w_dw, conv_b_dw, conv_ln_g, conv_ln_b, conv_w_out, conv_b_out, pool_w, pool_scale, attn_w_qkv, attn_q_norm, attn_k_norm, attn_sinks, attn_w_o, rel_bias, ffn_w_gate, ffn_w_up, ffn_w_down, ple_w_proj, ple_w_gate, ple_b_gate, loss_target, m_norm_mix, m_norm_ffn, m_norm_ple, m_conv_w_in, m_conv_b_in, m_conv_w_dw, m_conv_b_dw, m_conv_ln_g, m_conv_ln_b, m_conv_w_out, m_conv_b_out, m_pool_w, m_pool_scale, m_attn_w_qkv, m_attn_q_norm, m_attn_k_norm, m_attn_sinks, m_attn_w_o, m_rel_bias, m_ffn_w_gate, m_ffn_w_up, m_ffn_w_down, m_ple_w_proj, m_ple_w_gate, m_ple_b_gate, v_norm_mix, v_norm_ffn, v_norm_ple, v_conv_w_in, v_conv_b_in, v_conv_w_dw, v_conv_b_dw, v_conv_ln_g, v_conv_ln_b, v_conv_w_out, v_conv_b_out, v_pool_w, v_pool_scale, v_attn_w_qkv, v_attn_q_norm, v_attn_k_norm, v_attn_sinks, v_attn_w_o, v_rel_bias, v_ffn_w_gate, v_ffn_w_up, v_ffn_w_down, v_ple_w_proj, v_ple_w_gate, v_ple_b_gate):
    given = dict(x=x, p=p, norm_mix=norm_mix, norm_ffn=norm_ffn, norm_ple=norm_ple, conv_w_in=conv_w_in, conv_b_in=conv_b_in, conv_w_dw=conv_w_dw, conv_b_dw=conv_b_dw, conv_ln_g=conv_ln_g, conv_ln_b=conv_ln_b, conv_w_out=conv_w_out, conv_b_out=conv_b_out, pool_w=pool_w, pool_scale=pool_scale, attn_w_qkv=attn_w_qkv, attn_q_norm=attn_q_norm, attn_k_norm=attn_k_norm, attn_sinks=attn_sinks, attn_w_o=attn_w_o, rel_bias=rel_bias, ffn_w_gate=ffn_w_gate, ffn_w_up=ffn_w_up, ffn_w_down=ffn_w_down, ple_w_proj=ple_w_proj, ple_w_gate=ple_w_gate, ple_b_gate=ple_b_gate, loss_target=loss_target, m_norm_mix=m_norm_mix, m_norm_ffn=m_norm_ffn, m_norm_ple=m_norm_ple, m_conv_w_in=m_conv_w_in, m_conv_b_in=m_conv_b_in, m_conv_w_dw=m_conv_w_dw, m_conv_b_dw=m_conv_b_dw, m_conv_ln_g=m_conv_ln_g, m_conv_ln_b=m_conv_ln_b, m_conv_w_out=m_conv_w_out, m_conv_b_out=m_conv_b_out, m_pool_w=m_pool_w, m_pool_scale=m_pool_scale, m_attn_w_qkv=m_attn_w_qkv, m_attn_q_norm=m_attn_q_norm, m_attn_k_norm=m_attn_k_norm, m_attn_sinks=m_attn_sinks, m_attn_w_o=m_attn_w_o, m_rel_bias=m_rel_bias, m_ffn_w_gate=m_ffn_w_gate, m_ffn_w_up=m_ffn_w_up, m_ffn_w_down=m_ffn_w_down, m_ple_w_proj=m_ple_w_proj, m_ple_w_gate=m_ple_w_gate, m_ple_b_gate=m_ple_b_gate, v_norm_mix=v_norm_mix, v_norm_ffn=v_norm_ffn, v_norm_ple=v_norm_ple, v_conv_w_in=v_conv_w_in, v_conv_b_in=v_conv_b_in, v_conv_w_dw=v_conv_w_dw, v_conv_b_dw=v_conv_b_dw, v_conv_ln_g=v_conv_ln_g, v_conv_ln_b=v_conv_ln_b, v_conv_w_out=v_conv_w_out, v_conv_b_out=v_conv_b_out, v_pool_w=v_pool_w, v_pool_scale=v_pool_scale, v_attn_w_qkv=v_attn_w_qkv, v_attn_q_norm=v_attn_q_norm, v_attn_k_norm=v_attn_k_norm, v_attn_sinks=v_attn_sinks, v_attn_w_o=v_attn_w_o, v_rel_bias=v_rel_bias, v_ffn_w_gate=v_ffn_w_gate, v_ffn_w_up=v_ffn_w_up, v_ffn_w_down=v_ffn_w_down, v_ple_w_proj=v_ple_w_proj, v_ple_w_gate=v_ple_w_gate, v_ple_b_gate=v_ple_b_gate)
    weights = {n: given[n] for n in TWIN_WEIGHTS}
    shared = {n: given[n] for n in SHARED_INPUTS}
    per_example = {n: given[n] for n in ['x', 'p']}
    grad_fn = _jax.value_and_grad(_loss, argnums=(0, 1))

    def one_microbatch(ex, loss_target):
        ex = dict(ex)
        diff = ex.pop(TWIN_DIFF_INPUT)
        return grad_fn(weights, diff, {**shared, **ex}, loss_target)

    if N_MICROBATCH == 1:
        loss, (grad_w, grad_x) = one_microbatch(per_example, given["loss_target"])
    else:
        def body(carry, xs):
            loss_sum, grad_sum = carry
            l_k, (gw_k, gx_k) = one_microbatch(xs[0], xs[1])
            with _jax.named_scope("update"):
                return (loss_sum + l_k, _jax.tree.map(_jnp.add, grad_sum, gw_k)), gx_k

        init = (_jnp.zeros((), _jnp.float32), _jax.tree.map(_jnp.zeros_like, weights))
        (loss, grad_w), grad_x = _jax.lax.scan(body, init, (per_example, given["loss_target"]))
    with _jax.named_scope("update"):
        delta_w, new_m, new_v = {}, {}, {}
        for n in TWIN_WEIGHTS:
            delta_w[n], new_m[n], new_v[n] = _adamw(weights[n], grad_w[n], given["m_" + n], given["v_" + n])
    return (loss, grad_x, *[grad_w[n] for n in TWIN_WEIGHTS], *[delta_w[n] for n in TWIN_WEIGHTS],
            *[new_m[n] for n in TWIN_WEIGHTS], *[new_v[n] for n in TWIN_WEIGHTS])
```

```python
import functools
import math

import jax
import jax.numpy as jnp
import numpy as np
from jax import lax
from jax.experimental import pallas as pl
from jax.experimental.pallas import tpu as pltpu

F32 = jnp.float32
BF16 = jnp.bfloat16
MESH = pl.DeviceIdType.MESH

CHUNK = 64
CONV_WIDTH = 31
POOL_WINDOWS = (2, 4, 8, 16)
HEAD_DIM = 64
WINDOW_CHUNKS = 2
QBLOCK = 128
NUM_BUCKETS = 32
REL_MAX_DIST = 128
EPS = 1e-6
NEG_INF = -1e30
ADAM_LR, ADAM_B1, ADAM_B2, ADAM_EPS, ADAM_WD, ADAM_STEP = 0.001, 0.9, 0.999, 1e-08, 0.01, 10
N_SLOTS = 4
LANES = 128
VMEM_LIMIT_BYTES = 56 * 1024 * 1024


def _cparams(sem):
    return pltpu.CompilerParams(dimension_semantics=sem, vmem_limit_bytes=VMEM_LIMIT_BYTES)


def _tile(n, pref, mult=LANES):
    if n <= pref:
        return n
    t = (pref // mult) * mult
    while t >= mult:
        if n % t == 0:
            return t
        t -= mult
    return n


def _sig(z):
    return 1.0 / (1.0 + jnp.exp(-z))


def _op(arr, lead=None, ro=0, co=0, fn=None):
    return (arr, lead, ro, co, fn)


_DOT_DIMS = {"nn": (((1,), (0,)), ((), ())), "nt": (((1,), (1,)), ((), ())), "tn": (((0,), (0,)), ((), ()))}


def _mm(name, mode, dims, tiles, terms, n_acc, epilogue, outs, extras=()):
    M, N, K = dims
    tm, tn, tk = tiles
    assert M % tm == 0 and N % tn == 0 and K % tk == 0, (name, dims, tiles)
    nk = K // tk
    a_tile = (tk, tm) if mode == "tn" else (tm, tk)
    b_tile = (tn, tk) if mode == "nt" else (tk, tn)
    a_fn = (lambda i, j, k: (k, i)) if mode == "tn" else (lambda i, j, k: (i, k))
    b_fn = (lambda i, j, k: (j, k)) if mode == "nt" else (lambda i, j, k: (k, j))
    dn = _DOT_DIMS[mode]

    operands, specs, seen = [], [], {}

    def add(op, tshape, default_fn):
        arr, lead, ro, co, fn = op
        fn = fn or default_fn
        key = (id(arr), lead, ro, co, id(fn) if op[4] is not None else None, tshape)
        if key in seen:
            return seen[key]

        def imap(i, j, k, fn=fn, lead=lead, ro=ro, co=co):
            r, c = fn(i, j, k)
            return (r + ro, c + co) if lead is None else (lead, r + ro, c + co)

        operands.append(arr)
        specs.append(pl.BlockSpec(tshape if lead is None else (None,) + tshape, imap))
        seen[key] = len(operands) - 1
        return seen[key]

    term_idx = [(add(a, a_tile, a_fn), add(b, b_tile, b_fn), acc) for a, b, acc in terms]
    extra_idx = []
    for arr, kind, lead, co in extras:
        if kind == "tile":
            extra_idx.append(add(_op(arr, lead, 0, co), (tm, tn), lambda i, j, k: (i, j)))
        elif kind == "row":
            arr3 = arr.reshape(arr.shape[0], 1, arr.shape[1])
            extra_idx.append(add(_op(arr3, 0 if lead is None else lead, 0, co), (1, tn), lambda i, j, k: (0, j)))
        else:
            extra_idx.append(add(_op(arr, None, 0, 0), (tm, 1), lambda i, j, k: (i, 0)))
    n_in = len(operands)
    out_shapes, out_specs, aliases = [], [], {}
    for oi, (shape, dtype, lead, alias) in enumerate(outs):
        out_shapes.append(jax.ShapeDtypeStruct(shape, dtype))
        if lead is None:
            out_specs.append(pl.BlockSpec((tm, tn), lambda i, j, k: (i, j)))
        else:
            out_specs.append(pl.BlockSpec((None, tm, tn), lambda i, j, k, lead=lead: (lead, i, j)))
        if alias is not None:
            operands.append(alias)
            specs.append(pl.BlockSpec(memory_space=pl.ANY))
            aliases[len(operands) - 1] = oi
    n_all_in = len(operands)
    n_out = len(outs)

    def body(*refs):
        ins = refs[:n_in]
        o_refs = refs[n_all_in:n_all_in + n_out]
        accs = refs[n_all_in + n_out:]
        k = pl.program_id(2)

        @pl.when(k == 0)
        def _():
            for acc in accs:
                acc[...] = jnp.zeros_like(acc)

        for ai, bi, acc_i in term_idx:
            a = ins[ai][...]
            b = ins[bi][...]
            if a.dtype != BF16:
                a = a.astype(BF16)
            if b.dtype != BF16:
                b = b.astype(BF16)
            accs[acc_i][...] += lax.dot_general(a, b, dn, preferred_element_type=F32)

        @pl.when(k == nk - 1)
        def _():
            res = epilogue([acc[...] for acc in accs], [ins[e][...] for e in extra_idx])
            for o, r in zip(o_refs, res):
                o[...] = r.astype(o.dtype)

    res = pl.pallas_call(
        body, name=name, grid=(M // tm, N // tn, nk), in_specs=specs, out_specs=out_specs, out_shape=out_shapes,
        scratch_shapes=[pltpu.VMEM((tm, tn), F32) for _ in range(n_acc)], input_output_aliases=aliases,
        compiler_params=_cparams(("parallel", "parallel", "arbitrary")),
    )(*operands)
    return res


def _rowk(name, T, tm, ins, outs, body, scratch=()):
    assert T % tm == 0, (name, T, tm)
    n = T // tm
    specs = []
    for arr, kind in ins:
        w = arr.shape[-1]
        if kind == "tile":
            specs.append(pl.BlockSpec((tm, w), lambda i: (i, 0)))
        elif kind == "prev":
            specs.append(pl.BlockSpec((tm, w), lambda i: (jnp.maximum(i - 1, 0), 0)))
        elif kind == "next":
            specs.append(pl.BlockSpec((tm, w), lambda i, n=n: (jnp.minimum(i + 1, n - 1), 0)))
        else:
            specs.append(pl.BlockSpec(arr.shape, lambda i, nd=arr.ndim: (0,) * nd))
    out_shapes, out_specs = [], []
    for shape, dtype, kind in outs:
        out_shapes.append(jax.ShapeDtypeStruct(shape, dtype))
        if kind == "tile":
            out_specs.append(pl.BlockSpec((tm, shape[-1]), lambda i: (i, 0)))
        else:
            out_specs.append(pl.BlockSpec(shape, lambda i, nd=len(shape): (0,) * nd))
    n_in, n_out = len(ins), len(outs)

    def kbody(*refs):
        body(pl.program_id(0), n, refs[:n_in], refs[n_in:n_in + n_out], refs[n_in + n_out:])

    return pl.pallas_call(
        kbody, name=name, grid=(n,), in_specs=specs, out_specs=out_specs, out_shape=out_shapes,
        scratch_shapes=list(scratch), compiler_params=_cparams(("arbitrary",)),
    )(*[a for a, _ in ins])


def _accum(ref, i, val):
    @pl.when(i == 0)
    def _():
        ref[...] = val

    @pl.when(i > 0)
    def _():
        ref[...] += val


def _colsum(v):
    return jnp.sum(v, axis=0, keepdims=True)


def _rms_r(x):
    return lax.rsqrt(jnp.mean(x * x, axis=-1, keepdims=True) + EPS)


def _rms_fwd(name, x, g, tm):
    T, D = x.shape

    def body(i, n, ins, outs, scr):
        xv = ins[0][...]
        outs[0][...] = (xv * _rms_r(xv) * ins[1][...]).astype(BF16)

    return _rowk(name, T, tm, [(x, "tile"), (g, "full")], [((T, D), BF16, "tile")], body)[0]


def _rms_bwd(name, dres, x, g, dh, tm, want_bf16=False, want_colsum=False):
    T, D = x.shape

    def body(i, n, ins, outs, scr):
        xv = ins[1][...]
        gv = ins[2][...]
        dhv = ins[3][...].astype(F32)
        r = _rms_r(xv)
        xh = xv * r
        dhg = dhv * gv
        dx = ins[0][...] + r * (dhg - xh * jnp.mean(dhg * xh, axis=-1, keepdims=True))
        outs[0][...] = dx
        _accum(outs[1], i, _colsum(dhv * xh))
        o = 2
        if want_bf16:
            outs[o][...] = dx.astype(BF16)
            o += 1
        if want_colsum:
            _accum(outs[o], i, _colsum(dx))

    outs = [((T, D), F32, "tile"), ((1, D), F32, "acc")]
    if want_bf16:
        outs.append(((T, D), BF16, "tile"))
    if want_colsum:
        outs.append(((1, D), F32, "acc"))
    return _rowk(name, T, tm, [(dres, "tile"), (x, "tile"), (g, "full"), (dh, "tile")], outs, body)


def _loss_head(y, target, tm):
    T, D = y.shape

    def body(i, n, ins, outs, scr):
        d = ins[0][...] - ins[1][...]
        outs[0][...] = d * (1.0 / D)
        _accum(outs[1], i, jnp.sum(_colsum(d * d), axis=1, keepdims=True) * (0.5 / D))

    return _rowk("loss_head", T, tm, [(y, "tile"), (target, "tile")], [((T, D), F32, "tile"), ((1, 1), F32, "acc")], body)


def _ple_bwd_elem(name, dx, gt, q, tm):
    T, D = dx.shape

    def body(i, n, ins, outs, scr):
        d = ins[0][...]
        g = ins[1][...]
        dz = d * ins[2][...] * g * (1.0 - g)
        outs[0][...] = (d * g).astype(BF16)
        outs[1][...] = dz.astype(BF16)
        _accum(outs[2], i, _colsum(dz))

    return _rowk(name, T, tm, [(dx, "tile"), (gt, "tile"), (q, "tile")],
                 [((T, D), BF16, "tile"), ((T, D), BF16, "tile"), ((1, D), F32, "acc")], body)


CONV_ROWS = 128


def _dwconv_fwd(name, u1, w_dw, b_dw, tm):
    T, D = u1.shape
    rc_n = tm // CONV_ROWS if tm >= CONV_ROWS else 1
    rows = min(CONV_ROWS, tm)
    halo = CONV_WIDTH - 1

    def body(i, n, ins, outs, scr):
        buf = scr[0]
        buf[pl.ds(0, tm), :] = jnp.where(i > 0, ins[0][...], 0.0)
        buf[pl.ds(tm, tm), :] = ins[1][...]
        w_ref, b_ref, o_ref = ins[2], ins[3], outs[0]

        def chunk(lc, carry):
            l0 = pl.multiple_of(lc * LANES, LANES)
            for rc in range(rc_n):
                acc = jnp.zeros((rows, LANES), F32) + b_ref[:, pl.ds(l0, LANES)]
                for k in range(CONV_WIDTH):
                    acc = acc + buf[pl.ds(tm - halo + k + rc * rows, rows), pl.ds(l0, LANES)] * w_ref[pl.ds(k, 1), pl.ds(l0, LANES)]
                o_ref[pl.ds(rc * rows, rows), pl.ds(l0, LANES)] = acc
            return carry

        lax.fori_loop(0, D // LANES, chunk, 0)

    return _rowk(name, T, tm, [(u1, "prev"), (u1, "tile"), (w_dw, "full"), (b_dw, "full")], [((T, D), F32, "tile")], body,
                 scratch=[pltpu.VMEM((2 * tm, D), F32)])[0]


def _ln_silu_fwd(name, u2, g, b, tm):
    T, D = u2.shape

    def body(i, n, ins, outs, scr):
        v = ins[0][...]
        mu = jnp.mean(v, axis=-1, keepdims=True)
        xc = v - mu
        y = xc * lax.rsqrt(jnp.mean(xc * xc, axis=-1, keepdims=True) + EPS) * ins[1][...] + ins[2][...]
        outs[0][...] = (y * _sig(y)).astype(BF16)

    return _rowk(name, T, tm, [(u2, "tile"), (g, "full"), (b, "full")], [((T, D), BF16, "tile")], body)[0]


def _ln_silu_bwd(name, du4, u2, g, b, tm):
    T, D = u2.shape

    def body(i, n, ins, outs, scr):
        v = ins[1][...]
        gv = ins[2][...]
        mu = jnp.mean(v, axis=-1, keepdims=True)
        xc = v - mu
        r = lax.rsqrt(jnp.mean(xc * xc, axis=-1, keepdims=True) + EPS)
        xh = xc * r
        y = xh * gv + ins[3][...]
        s = _sig(y)
        dy = ins[0][...] * (s * (1.0 + y * (1.0 - s)))
        dyg = dy * gv
        du2 = r * (dyg - jnp.mean(dyg, axis=-1, keepdims=True) - xh * jnp.mean(dyg * xh, axis=-1, keepdims=True))
        outs[0][...] = du2
        _accum(outs[1], i, _colsum(dy * xh))
        _accum(outs[2], i, _colsum(dy))
        _accum(outs[3], i, _colsum(du2))

    return _rowk(name, T, tm, [(du4, "tile"), (u2, "tile"), (g, "full"), (b, "full")],
                 [((T, D), F32, "tile"), ((1, D), F32, "acc"), ((1, D), F32, "acc"), ((1, D), F32, "acc")], body)


def _dwconv_glu_bwd(name, du2, u1, a_, gate, w_dw, tm):
    T, D = u1.shape
    rc_n = tm // CONV_ROWS if tm >= CONV_ROWS else 1
    rows = min(CONV_ROWS, tm)
    halo = CONV_WIDTH - 1

    def body(i, n, ins, outs, scr):
        bu, bd = scr
        bd[pl.ds(0, tm), :] = ins[0][...]
        bd[pl.ds(tm, tm), :] = jnp.where(i < n - 1, ins[1][...], 0.0)
        bu[pl.ds(0, tm), :] = jnp.where(i > 0, ins[2][...], 0.0)
        bu[pl.ds(tm, tm), :] = ins[3][...]
        a_ref, g_ref, w_ref = ins[4], ins[5], ins[6]
        dag_ref, dw_ref, db_ref = outs

        @pl.when(i == 0)
        def _():
            dw_ref[...] = jnp.zeros_like(dw_ref)
            db_ref[...] = jnp.zeros_like(db_ref)

        def chunk(lc, carry):
            l0 = pl.multiple_of(lc * LANES, LANES)
            l1 = pl.multiple_of(D + lc * LANES, LANES)
            for rc in range(rc_n):
                r0 = rc * rows
                d_here = bd[pl.ds(r0, rows), pl.ds(l0, LANES)]
                acc = jnp.zeros((rows, LANES), F32)
                for k in range(CONV_WIDTH):
                    wk = w_ref[pl.ds(k, 1), pl.ds(l0, LANES)]
                    acc = acc + bd[pl.ds(r0 + halo - k, rows), pl.ds(l0, LANES)] * wk
                    dw_ref[pl.ds(k, 1), pl.ds(l0, LANES)] += _colsum(d_here * bu[pl.ds(tm - halo + k + r0, rows), pl.ds(l0, LANES)])
                av = a_ref[pl.ds(r0, rows), pl.ds(l0, LANES)].astype(F32)
                sg = _sig(g_ref[pl.ds(r0, rows), pl.ds(l0, LANES)].astype(F32))
                da = acc * sg
                dg = acc * av * sg * (1.0 - sg)
                dag_ref[pl.ds(r0, rows), pl.ds(l0, LANES)] = da.astype(BF16)
                dag_ref[pl.ds(r0, rows), pl.ds(l1, LANES)] = dg.astype(BF16)
                db_ref[:, pl.ds(l0, LANES)] += _colsum(da)
                db_ref[:, pl.ds(l1, LANES)] += _colsum(dg)
            return carry

        lax.fori_loop(0, D // LANES, chunk, 0)

    return _rowk(name, T, tm, [(du2, "tile"), (du2, "next"), (u1, "prev"), (u1, "tile"), (a_, "tile"), (gate, "tile"), (w_dw, "full")],
                 [((T, 2 * D), BF16, "tile"), ((CONV_WIDTH, D), F32, "acc"), ((1, 2 * D), F32, "acc")], body,
                 scratch=[pltpu.VMEM((2 * tm, D), F32), pltpu.VMEM((2 * tm, D), F32)])


def _row_index(i, tm, r0, rows):
    return (i * tm + r0 + lax.broadcasted_iota(jnp.int32, (rows, 1), 0)).astype(F32)


def _pool_fwd(name, x, g, tm):
    T, D = x.shape
    gc = D // len(POOL_WINDOWS)
    rows = min(CONV_ROWS, tm)
    rc_n = tm // rows

    def body(i, n, ins, outs, scr):
        buf = scr[0]
        xp = ins[0][...]
        buf[pl.ds(0, tm), :] = jnp.where(i > 0, xp * _rms_r(xp) * ins[2][...], 0.0)
        xc = ins[1][...]
        buf[pl.ds(tm, tm), :] = xc * _rms_r(xc) * ins[2][...]
        o_ref = outs[0]
        for gi, w in enumerate(POOL_WINDOWS):
            def chunk(lc, carry, gi=gi, w=w):
                l0 = pl.multiple_of(gi * gc + lc * LANES, LANES)
                for rc in range(rc_n):
                    r0 = rc * rows
                    acc = buf[pl.ds(tm + r0, rows), pl.ds(l0, LANES)]
                    here = acc
                    for d in range(1, w):
                        acc = acc + buf[pl.ds(tm + r0 - d, rows), pl.ds(l0, LANES)]
                    cnt = jnp.minimum(_row_index(i, tm, r0, rows) + 1.0, float(w))
                    o_ref[pl.ds(r0, rows), pl.ds(l0, LANES)] = (acc / cnt - here).astype(BF16)
                return carry

            lax.fori_loop(0, gc // LANES, chunk, 0)

    return _rowk(name, T, tm, [(x, "prev"), (x, "tile"), (g, "full")], [((T, D), BF16, "tile")], body,
                 scratch=[pltpu.VMEM((2 * tm, D), F32)])[0]


def _pool_bwd(name, dmix, tm):
    T, D = dmix.shape
    gc = D // len(POOL_WINDOWS)
    rows = min(CONV_ROWS, tm)
    rc_n = tm // rows

    def body(i, n, ins, outs, scr):
        buf = scr[0]
        o_ref = outs[0]
        t_here = (i * tm + lax.broadcasted_iota(jnp.int32, (tm, 1), 0)).astype(F32) + 1.0
        for gi, w in enumerate(POOL_WINDOWS):
            cols = pl.ds(gi * gc, gc)
            buf[pl.ds(0, tm), cols] = ins[0][:, cols] / jnp.minimum(t_here, float(w))
            buf[pl.ds(tm, tm), cols] = jnp.where(i < n - 1, ins[1][:, cols] / float(w), 0.0)

            def chunk(lc, carry, gi=gi, w=w):
                l0 = pl.multiple_of(gi * gc + lc * LANES, LANES)
                for rc in range(rc_n):
                    r0 = rc * rows
                    acc = -ins[0][pl.ds(r0, rows), pl.ds(l0, LANES)]
                    for d in range(w):
                        acc = acc + buf[pl.ds(r0 + d, rows), pl.ds(l0, LANES)]
                    o_ref[pl.ds(r0, rows), pl.ds(l0, LANES)] = acc
                return carry

            lax.fori_loop(0, gc // LANES, chunk, 0)

    return _rowk(name, T, tm, [(dmix, "tile"), (dmix, "next")], [((T, D), F32, "tile")], body,
                 scratch=[pltpu.VMEM((2 * tm, D), F32)])[0]


def _pool_scale_bwd(name, dy, y0, scale, tm):
    T, D = dy.shape

    def body(i, n, ins, outs, scr):
        d = ins[0][...]
        outs[0][...] = (d * ins[2][...]).astype(BF16)
        _accum(outs[1], i, _colsum(d * ins[1][...]))

    return _rowk(name, T, tm, [(dy, "tile"), (y0, "tile"), (scale, "full")], [((T, D), BF16, "tile"), ((1, D), F32, "acc")], body)


def _t5_bucket_np():
    i = np.arange(QBLOCK)[:, None]
    j = np.arange(2 * QBLOCK)[None, :]
    rel = j - QBLOCK - i
    nb = NUM_BUCKETS // 2
    n = -rel
    ret = np.where(n < 0, nb, 0)
    n = np.abs(n)
    max_exact = nb // 2
    nf = np.maximum(n, 1).astype(np.float32)
    large = max_exact + (np.log(nf / np.float32(max_exact)) / np.float32(math.log(REL_MAX_DIST / max_exact))
                         * np.float32(nb - max_exact)).astype(np.int32)
    large = np.minimum(large, nb - 1)
    return (ret + np.where(n < max_exact, n, large)).astype(np.int32)


def _bias_fwd(rel_bias, bucket):
    nb, nh = rel_bias.shape

    def body(rb_ref, bk_ref, o_ref):
        h = pl.program_id(0)
        bk = bk_ref[...]
        acc = jnp.zeros(bk.shape, F32)
        for b in range(nb):
            acc = jnp.where(bk == b, rb_ref[b, h], acc)
        o_ref[...] = acc

    return pl.pallas_call(
        body, name="attn_bias_fwd", grid=(nh,),
        in_specs=[pl.BlockSpec(memory_space=pltpu.SMEM), pl.BlockSpec(bucket.shape, lambda h: (0, 0))],
        out_specs=pl.BlockSpec((None,) + bucket.shape, lambda h: (h, 0, 0)),
        out_shape=jax.ShapeDtypeStruct((nh,) + bucket.shape, F32), compiler_params=_cparams(("arbitrary",)),
    )(rel_bias, bucket)


def _bias_bwd(dbias, bucket, nb):
    nh = dbias.shape[0]

    def body(db_ref, bk_ref, o_ref):
        h = pl.program_id(0)
        bk = bk_ref[...]
        d = db_ref[...]
        for b in range(nb):
            o_ref[h, b] = jnp.sum(jnp.where(bk == b, d, 0.0))

    return pl.pallas_call(
        body, name="attn_bias_bwd", grid=(nh,),
        in_specs=[pl.BlockSpec((None,) + bucket.shape, lambda h: (h, 0, 0)), pl.BlockSpec(bucket.shape, lambda h: (0, 0))],
        out_specs=pl.BlockSpec(memory_space=pltpu.SMEM),
        out_shape=jax.ShapeDtypeStruct((nh, nb), F32), compiler_params=_cparams(("arbitrary",)),
    )(dbias, bucket)


def _head_norm_fwd(name, q2, g, tm):
    R, W = q2.shape

    def body(i, n, ins, outs, scr):
        v = ins[0][...]
        outs[0][...] = (v * _rms_r(v) * ins[1][...]).astype(BF16)

    return _rowk(name, R, tm, [(q2, "tile"), (g, "full")], [((R, W), BF16, "tile")], body)[0]


def _head_norm_bwd(name, dqn, q2, g, tm):
    R, W = q2.shape

    def body(i, n, ins, outs, scr):
        v = ins[1][...]
        d = ins[0][...]
        r = _rms_r(v)
        xh = v * r
        dg = d * ins[2][...]
        outs[0][...] = (r * (dg - xh * jnp.mean(dg * xh, axis=-1, keepdims=True))).astype(BF16)
        _accum(outs[1], i, _colsum(d * xh))

    return _rowk(name, R, tm, [(dqn, "tile"), (q2, "tile"), (g, "full")], [((R, W), BF16, "tile"), ((1, W), F32, "acc")], body)


def _band_merge(name, own, prev, nblk, k2=None, g=None):
    R, W = own.shape
    tm = QBLOCK

    def body(i, n, ins, outs, scr):
        last = (i % nblk) == (nblk - 1)
        d = ins[0][...] + jnp.where(last, 0.0, ins[1][...])
        if k2 is None:
            outs[0][...] = d.astype(BF16)
        else:
            v = ins[2][...]
            r = _rms_r(v)
            xh = v * r
            dg = d * ins[3][...]
            outs[0][...] = (r * (dg - xh * jnp.mean(dg * xh, axis=-1, keepdims=True))).astype(BF16)
            _accum(outs[1], i, _colsum(d * xh))

    ins = [(own, "tile"), (prev, "next")]
    outs = [((R, W), BF16, "tile")]
    if k2 is not None:
        ins += [(k2, "tile"), (g, "full")]
        outs.append(((1, W), F32, "acc"))
    return _rowk(name, R, tm, ins, outs, body)


def _attn_logits(q, kb, bias, sink, n):
    rows = q.shape[0]
    s = lax.dot_general(q, kb, _DOT_DIMS["nt"], preferred_element_type=F32) * (HEAD_DIM ** -0.5) + bias
    qc = (lax.broadcasted_iota(jnp.int32, (rows, 1), 0) % QBLOCK) // CHUNK
    j = lax.broadcasted_iota(jnp.int32, (1, 2 * QBLOCK), 1)
    kc = j // CHUNK - QBLOCK // CHUNK
    ok = (kc <= qc) & (kc >= qc - WINDOW_CHUNKS) & ((n > 0) | (j >= QBLOCK))
    s = jnp.where(ok, s, NEG_INF)
    m = jnp.maximum(jnp.max(s, axis=-1, keepdims=True), sink)
    e = jnp.exp(s - m)
    es = jnp.exp(sink - m)
    den = jnp.sum(e, axis=-1, keepdims=True) + es
    return e / den, es / den


def _attn_fwd(qn, kn, v, bias, sink_rows, n_kv, group, T):
    nblk = T // QBLOCK
    rows = group * QBLOCK

    def body(q_ref, kp_ref, kc_ref, vp_ref, vc_ref, b_ref, s_ref, o_ref):
        n = pl.program_id(1)
        q = q_ref[...].reshape(rows, HEAD_DIM)
        kb = jnp.concatenate([kp_ref[...], kc_ref[...]], axis=0)
        vb = jnp.concatenate([vp_ref[...], vc_ref[...]], axis=0)
        p, _ = _attn_logits(q, kb, b_ref[...], s_ref[...], n)
        o = lax.dot_general(p.astype(BF16), vb, _DOT_DIMS["nn"], preferred_element_type=F32)
        o_ref[...] = o.reshape(group, QBLOCK, HEAD_DIM).astype(BF16)

    blk = lambda hn, fn: pl.BlockSpec((hn, QBLOCK, HEAD_DIM) if hn else (None, QBLOCK, HEAD_DIM), fn)
    cur = lambda h, n: (h, n, 0)
    prv = lambda h, n: (h, jnp.maximum(n - 1, 0), 0)
    return pl.pallas_call(
        body, name="attn_fwd", grid=(n_kv, nblk),
        in_specs=[blk(group, cur), blk(None, prv), blk(None, cur), blk(None, prv), blk(None, cur),
                  pl.BlockSpec((None, rows, 2 * QBLOCK), lambda h, n: (h, 0, 0)), pl.BlockSpec((None, rows, 1), lambda h, n: (h, 0, 0))],
        out_specs=blk(group, cur), out_shape=jax.ShapeDtypeStruct(qn.shape, BF16),
        compiler_params=_cparams(("arbitrary", "arbitrary")),
    )(qn, kn, kn, v, v, bias, sink_rows)


def _attn_bwd(qn, kn, v, bias, sink_rows, do, n_kv, group, T):
    nblk = T // QBLOCK
    rows = group * QBLOCK
    scale = HEAD_DIM ** -0.5

    def body(q_ref, kp_ref, kc_ref, vp_ref, vc_ref, b_ref, s_ref, do_ref, dq_ref, dko_ref, dkp_ref, dvo_ref, dvp_ref, db_ref, ds_ref):
        n = pl.program_id(1)
        q = q_ref[...].reshape(rows, HEAD_DIM)
        dov = do_ref[...].reshape(rows, HEAD_DIM)
        kb = jnp.concatenate([kp_ref[...], kc_ref[...]], axis=0)
        vb = jnp.concatenate([vp_ref[...], vc_ref[...]], axis=0)
        p, ps = _attn_logits(q, kb, b_ref[...], s_ref[...], n)
        dp = lax.dot_general(dov, vb, _DOT_DIMS["nt"], preferred_element_type=F32)
        delta = jnp.sum(p * dp, axis=-1, keepdims=True)
        dl = p * (dp - delta)
        dlb = dl.astype(BF16)
        dq = lax.dot_general(dlb, kb, _DOT_DIMS["nn"], preferred_element_type=F32) * scale
        dkb = lax.dot_general(dlb, q, _DOT_DIMS["tn"], preferred_element_type=F32) * scale
        dvb = lax.dot_general(p.astype(BF16), dov, _DOT_DIMS["tn"], preferred_element_type=F32)
        dq_ref[...] = dq.reshape(group, QBLOCK, HEAD_DIM)
        dkp_ref[...] = dkb[:QBLOCK]
        dko_ref[...] = dkb[QBLOCK:]
        dvp_ref[...] = dvb[:QBLOCK]
        dvo_ref[...] = dvb[QBLOCK:]
        _accum(db_ref, n, dl)
        _accum(ds_ref, n, -ps * delta)

    blk = lambda hn, fn: pl.BlockSpec((hn, QBLOCK, HEAD_DIM) if hn else (None, QBLOCK, HEAD_DIM), fn)
    cur = lambda h, n: (h, n, 0)
    prv = lambda h, n: (h, jnp.maximum(n - 1, 0), 0)
    bsp = pl.BlockSpec((None, rows, 2 * QBLOCK), lambda h, n: (h, 0, 0))
    ssp = pl.BlockSpec((None, rows, 1), lambda h, n: (h, 0, 0))
    kv_shape = jax.ShapeDtypeStruct(kn.shape, F32)
    return pl.pallas_call(
        body, name="attn_bwd", grid=(n_kv, nblk),
        in_specs=[blk(group, cur), blk(None, prv), blk(None, cur), blk(None, prv), blk(None, cur), bsp, ssp, blk(group, cur)],
        out_specs=[blk(group, cur), blk(None, cur), blk(None, cur), blk(None, cur), blk(None, cur), bsp, ssp],
        out_shape=[jax.ShapeDtypeStruct(qn.shape, F32), kv_shape, kv_shape, kv_shape, kv_shape,
                   jax.ShapeDtypeStruct(bias.shape, F32), jax.ShapeDtypeStruct(sink_rows.shape, F32)],
        compiler_params=_cparams(("arbitrary", "arbitrary")),
    )(qn, kn, kn, v, v, bias, sink_rows, do)


def _adamw(name, w, g, m, v):
    shape = w.shape
    w2, g2, m2, v2 = (a.reshape(-1, shape[-1]) for a in (w, g, m, v))
    R, W = w2.shape
    tm = _tile(R, max(8, (1 << 19) // W), 8)
    d1 = 1.0 - ADAM_B1 ** ADAM_STEP
    d2 = 1.0 - ADAM_B2 ** ADAM_STEP

    def body(i, n, ins, outs, scr):
        wv, gv = ins[0][...], ins[1][...]
        mn = ADAM_B1 * ins[2][...] + (1.0 - ADAM_B1) * gv
        vn = ADAM_B2 * ins[3][...] + (1.0 - ADAM_B2) * (gv * gv)
        outs[0][...] = -ADAM_LR * ((mn / d1) / (jnp.sqrt(vn / d2) + ADAM_EPS) + ADAM_WD * wv)
        outs[1][...] = mn
        outs[2][...] = vn

    d, mn, vn = _rowk(name, R, tm, [(w2, "tile"), (g2, "tile"), (m2, "tile"), (v2, "tile")],
                      [((R, W), F32, "tile")] * 3, body)
    return d.reshape(shape), mn.reshape(shape), vn.reshape(shape)


def _first(accs, extras):
    return [accs[0]]


def _local_step(x, p, target, wts, small):
    T, D = x.shape
    L, _, PLE = p.shape
    gu, down, cin, sq, qkv_w = wts["gu"], wts["down"], wts["cin"], wts["sq"], wts["qkv"]
    FF = gu.shape[2]
    NA, NC = cin.shape[0], qkv_w.shape[0]
    QW = qkv_w.shape[2]
    KVD = (QW - D) // 2
    n_heads, n_kv = D // HEAD_DIM, KVD // HEAD_DIM
    group = n_heads // n_kv
    nblk = T // QBLOCK
    GC = D // len(POOL_WINDOWS)

    tr = _tile(T, 256, 8)
    tmm = _tile(T, 2048)
    tD = _tile(D, 1024)
    tD2 = _tile(D, 512)
    tF = _tile(FF, 512)
    tFk = _tile(FF, 1024)
    tP = _tile(PLE, 512)
    tQ = _tile(QW, 768)
    tT = _tile(T, 1024)

    bucket = jnp.asarray(_t5_bucket_np())
    saved = []
    xs = x

    for i in range(L):
        kind, j = i % 3, i // 3
        sv = {"x": xs}
        h1 = _rms_fwd(f"rms_mix_{i}", xs, small["norm_mix"][i:i + 1], tr)
        if kind == 0:
            a_, gate, u1 = _mm(
                f"conv_in_{i}", "nn", (T, D, D), (tmm, tD2, tD),
                [(_op(h1), _op(cin, j), 0), (_op(h1), _op(cin, j, 0, D // tD2), 1)], 2,
                lambda accs, ex: (lambda a, g: [a, g, a * _sig(g)])(accs[0] + ex[0], accs[1] + ex[1]),
                [((T, D), BF16, None, None), ((T, D), BF16, None, None), ((T, D), F32, None, None)],
                extras=[(small["conv_b_in"], "row", j, 0), (small["conv_b_in"], "row", j, D // tD2)])
            u2 = _dwconv_fwd(f"dwconv_{i}", u1, small["conv_w_dw"][j], small["conv_b_dw"][j:j + 1], tr)
            u4 = _ln_silu_fwd(f"ln_silu_{i}", u2, small["conv_ln_g"][j:j + 1], small["conv_ln_b"][j:j + 1], tr)
            x1, = _mm(f"conv_out_{i}", "nn", (T, D, D), (tmm, tD2, tD), [(_op(u4), _op(sq, j), 0)], 1,
                      lambda accs, ex: [accs[0] + ex[0] + ex[1]], [((T, D), F32, None, None)],
                      extras=[(small["conv_b_out"], "row", j, 0), (xs, "tile", None, 0)])
            sv.update(h1=h1, a=a_, gate=gate, u1=u1, u2=u2, u4=u4)
        elif kind == 1:
            mix = _pool_fwd(f"pool_{i}", xs, small["norm_mix"][i:i + 1], tr)
            pw = small["pool_w"][j].reshape(len(POOL_WINDOWS) * GC, GC)
            kb = GC // _tile(GC, 512)
            tg = _tile(GC, 512)
            y0, x1 = _mm(f"pool_mm_{i}", "nn", (T, D, GC), (tmm, GC, tg),
                         [(_op(mix, fn=lambda i_, j_, k_, kb=kb: (i_, j_ * kb + k_)), _op(pw, fn=lambda i_, j_, k_, kb=kb: (j_ * kb + k_, 0)), 0)], 1,
                         lambda accs, ex: [accs[0], ex[1] + accs[0] * ex[0]],
                         [((T, D), F32, None, None), ((T, D), F32, None, None)],
                         extras=[(small["pool_scale"][j:j + 1], "row", None, 0), (xs, "tile", None, 0)])
            sv.update(mix=mix, y0=y0, pw=pw)
        else:
            qkv, = _mm(f"qkv_{i}", "nn", (T, QW, D), (tmm, tQ, tD), [(_op(h1), _op(qkv_w, j), 0)], 1, _first,
                       [((T, QW), F32, None, None)])
            q_hm = qkv[:, :D].reshape(T, n_heads, HEAD_DIM).transpose(1, 0, 2).reshape(n_heads * T, HEAD_DIM)
            k_hm = qkv[:, D:D + KVD].reshape(T, n_kv, HEAD_DIM).transpose(1, 0, 2).reshape(n_kv * T, HEAD_DIM)
            v_hm = qkv[:, D + KVD:].reshape(T, n_kv, HEAD_DIM).transpose(1, 0, 2).astype(BF16)
            th = _tile(T, 2048, 8)
            qn = _head_norm_fwd(f"qnorm_{i}", q_hm, small["attn_q_norm"][j:j + 1], th).reshape(n_heads, T, HEAD_DIM)
            kn = _head_norm_fwd(f"knorm_{i}", k_hm, small["attn_k_norm"][j:j + 1], th).reshape(n_kv, T, HEAD_DIM)
            bias = _bias_fwd(small["rel_bias"], bucket).reshape(n_kv, group * QBLOCK, 2 * QBLOCK)
            sink_rows = jnp.broadcast_to(small["attn_sinks"][j].reshape(n_kv, group, 1, 1), (n_kv, group, QBLOCK, 1)).reshape(n_kv, group * QBLOCK, 1)
            o_hm = _attn_fwd(qn, kn, v_hm, bias, sink_rows, n_kv, group, T)
            o = o_hm.transpose(1, 0, 2).reshape(T, D)
            x1, = _mm(f"attn_o_{i}", "nn", (T, D, D), (tmm, tD2, tD), [(_op(o), _op(sq, NA + j), 0)], 1,
                      lambda accs, ex: [accs[0] + ex[0]], [((T, D), F32, None, None)], extras=[(xs, "tile", None, 0)])
            sv.update(h1=h1, q_hm=q_hm, k_hm=k_hm, v_hm=v_hm, qn=qn, kn=kn, bias=bias, sink_rows=sink_rows, o=o)
        h2 = _rms_fwd(f"rms_ffn_{i}", x1, small["norm_ffn"][i:i + 1], tr)
        a, b, f = _mm(f"ffn_up_{i}", "nn", (T, FF, D), (tmm, tF, tD), [(_op(h2), _op(gu, 2 * i), 0), (_op(h2), _op(gu, 2 * i + 1), 1)], 2,
                      lambda accs, ex: [accs[0], accs[1], accs[0] * _sig(accs[0]) * accs[1]],
                      [((T, FF), BF16, None, None)] * 3)
        x2, = _mm(f"ffn_down_{i}", "nn", (T, D, FF), (tmm, tD2, tFk), [(_op(f), _op(down, i), 0)], 1,
                  lambda accs, ex: [accs[0] + ex[0]], [((T, D), F32, None, None)], extras=[(x1, "tile", None, 0)])
        h3 = _rms_fwd(f"rms_ple_{i}", x2, small["norm_ple"][i:i + 1], tr)
        q, = _mm(f"ple_proj_{i}", "nn", (T, D, PLE), (tmm, tD2, tP), [(_op(p, i), _op(small["ple_w_proj"], i), 0)], 1, _first,
                 [((T, D), F32, None, None)])
        gt, x3 = _mm(f"ple_gate_{i}", "nn", (T, D, D), (tmm, tD2, tD), [(_op(h3), _op(sq, NA + NC + i), 0)], 1,
                     lambda accs, ex: (lambda g_, q_, x_: [g_, x_ + g_ * q_])(_sig(accs[0] + ex[0]), ex[1], ex[2]),
                     [((T, D), F32, None, None), ((T, D), F32, None, None)],
                     extras=[(small["ple_b_gate"], "row", i, 0), (q, "tile", None, 0), (x2, "tile", None, 0)])
        sv.update(x1=x1, h2=h2, a=sv.get("a"), fa=a, fb=b, f=f, x2=x2, h3=h3, q=q, gt=gt)
        saved.append(sv)
        xs = x3

    dx, loss = _loss_head(xs, target, tr)

    g_gu = jnp.zeros(gu.shape, BF16)
    g_down = jnp.zeros(down.shape, BF16)
    g_cin = jnp.zeros(cin.shape, BF16)
    g_sq = jnp.zeros(sq.shape, BF16)
    g_qkv = jnp.zeros(qkv_w.shape, BF16)
    gs = {k: [None] * v.shape[0] for k, v in small.items() if k != "rel_bias"}
    gs["rel_bias"] = None
    gs["ple_w_proj"] = [None] * L

    for i in reversed(range(L)):
        kind, j = i % 3, i // 3
        sv = saved[i]
        dq, dz, dbg = _ple_bwd_elem(f"ple_bwd_{i}", dx, sv["gt"], sv["q"], tr)
        gs["ple_b_gate"][i] = dbg
        gs["ple_w_proj"][i], = _mm(f"d_ple_proj_{i}", "tn", (PLE, D, T), (tP, tD, tT), [(_op(p, i), _op(dq), 0)], 1, _first,
                                   [((PLE, D), F32, None, None)])
        g_sq, = _mm(f"d_ple_gate_{i}", "tn", (D, D, T), (tD, tD, tT), [(_op(sv["h3"]), _op(dz), 0)], 1, _first,
                    [(sq.shape, BF16, NA + NC + i, g_sq)])
        dh3, = _mm(f"dh_ple_{i}", "nt", (T, D, D), (tmm, tD2, tD), [(_op(dz), _op(sq, NA + NC + i), 0)], 1, _first,
                   [((T, D), F32, None, None)])
        dx2, gs["norm_ple"][i], dx2b = _rms_bwd(f"rms_ple_bwd_{i}", dx, sv["x2"], small["norm_ple"][i:i + 1], dh3, tr, want_bf16=True)
        da, db = _mm(f"d_ffn_act_{i}", "nt", (T, FF, D), (tmm, tF, tD), [(_op(dx2b), _op(down, i), 0)], 1,
                     lambda accs, ex: (lambda df, a_, b_, s_: [df * b_ * s_ * (1.0 + a_ * (1.0 - s_)), df * a_ * s_])(
                         accs[0], ex[0].astype(F32), ex[1].astype(F32), _sig(ex[0].astype(F32))),
                     [((T, FF), BF16, None, None)] * 2, extras=[(sv["fa"], "tile", None, 0), (sv["fb"], "tile", None, 0)])
        g_down, = _mm(f"d_ffn_down_{i}", "tn", (FF, D, T), (tF, tD, tT), [(_op(sv["f"]), _op(dx2b), 0)], 1, _first,
                      [(down.shape, BF16, i, g_down)])
        g_gu, = _mm(f"d_ffn_gate_{i}", "tn", (D, FF, T), (tD, tF, tT), [(_op(sv["h2"]), _op(da), 0)], 1, _first,
                    [(gu.shape, BF16, 2 * i, g_gu)])
        g_gu, = _mm(f"d_ffn_up_{i}", "tn", (D, FF, T), (tD, tF, tT), [(_op(sv["h2"]), _op(db), 0)], 1, _first,
                    [(gu.shape, BF16, 2 * i + 1, g_gu)])
        dh2, = _mm(f"dh_ffn_{i}", "nt", (T, D, FF), (tmm, tD2, tFk), [(_op(da), _op(gu, 2 * i), 0), (_op(db), _op(gu, 2 * i + 1), 0)], 1, _first,
                   [((T, D), F32, None, None)])
        want_cs = kind == 0
        res = _rms_bwd(f"rms_ffn_bwd_{i}", dx2, sv["x1"], small["norm_ffn"][i:i + 1], dh2, tr, want_bf16=True, want_colsum=want_cs)
        dx1, gs["norm_ffn"][i], dx1b = res[:3]
        xin = sv["x"]
        if kind == 0:
            gs["conv_b_out"][j] = res[3]
            g_sq, = _mm(f"d_conv_out_{i}", "tn", (D, D, T), (tD, tD, tT), [(_op(sv["u4"]), _op(dx1b), 0)], 1, _first,
                        [(sq.shape, BF16, j, g_sq)])
            du4, = _mm(f"dh_conv_out_{i}", "nt", (T, D, D), (tmm, tD2, tD), [(_op(dx1b), _op(sq, j), 0)], 1, _first,
                       [((T, D), F32, None, None)])
            du2, gs["conv_ln_g"][j], gs["conv_ln_b"][j], gs["conv_b_dw"][j] = _ln_silu_bwd(
                f"ln_silu_bwd_{i}", du4, sv["u2"], small["conv_ln_g"][j:j + 1], small["conv_ln_b"][j:j + 1], tr)
            dag, gs["conv_w_dw"][j], gs["conv_b_in"][j] = _dwconv_glu_bwd(
                f"dwconv_bwd_{i}", du2, sv["u1"], sv["a"], sv["gate"], small["conv_w_dw"][j], tr)
            g_cin, = _mm(f"d_conv_in_{i}", "tn", (D, 2 * D, T), (tD, tD, tT), [(_op(sv["h1"]), _op(dag), 0)], 1, _first,
                         [(cin.shape, BF16, j, g_cin)])
            dh1, = _mm(f"dh_conv_in_{i}", "nt", (T, D, 2 * D), (tmm, tD2, tD), [(_op(dag), _op(cin, j), 0)], 1, _first,
                       [((T, D), F32, None, None)])
        elif kind == 1:
            dys, gs["pool_scale"][j] = _pool_scale_bwd(f"pool_scale_bwd_{i}", dx1, sv["y0"], small["pool_scale"][j:j + 1], tr)
            tg = _tile(GC, 512)
            kb = GC // tg
            dmix, = _mm(f"dh_pool_{i}", "nt", (T, D, GC), (tmm, GC, tg),
                        [(_op(dys, fn=lambda i_, j_, k_, kb=kb: (i_, j_ * kb + k_)), _op(sv["pw"]), 0)], 1, _first,
                        [((T, D), F32, None, None)])
            ng = len(POOL_WINDOWS)
            gs["pool_w"][j], = _mm(f"d_pool_w_{i}", "tn", (D, GC, T), (GC, GC, tT),
                                   [(_op(sv["mix"]), _op(dys, fn=lambda i_, j_, k_: (k_, i_)), 0)], 1, _first,
                                   [((D, GC), F32, None, None)])
            gs["pool_w"][j] = gs["pool_w"][j].reshape(ng, GC, GC)
            dh1 = _pool_bwd(f"pool_bwd_{i}", dmix, tr)
        else:
            g_sq, = _mm(f"d_attn_o_{i}", "tn", (D, D, T), (tD, tD, tT), [(_op(sv["o"]), _op(dx1b), 0)], 1, _first,
                        [(sq.shape, BF16, NA + j, g_sq)])
            do, = _mm(f"dh_attn_o_{i}", "nt", (T, D, D), (tmm, tD2, tD), [(_op(dx1b), _op(sq, NA + j), 0)], 1, _first,
                      [((T, D), BF16, None, None)])
            do_hm = do.reshape(T, n_heads, HEAD_DIM).transpose(1, 0, 2)
            dqn, dko, dkp, dvo, dvp, dbias, dsink = _attn_bwd(sv["qn"], sv["kn"], sv["v_hm"], sv["bias"], sv["sink_rows"], do_hm, n_kv, group, T)
            th = _tile(T, 2048, 8)
            dq_hm, gs["attn_q_norm"][j] = _head_norm_bwd(f"qnorm_bwd_{i}", dqn.reshape(n_heads * T, HEAD_DIM), sv["q_hm"],
                                                         small["attn_q_norm"][j:j + 1], th)
            flat = lambda t_: t_.reshape(n_kv * T, HEAD_DIM)
            dk_hm, gs["attn_k_norm"][j] = _band_merge(f"knorm_bwd_{i}", flat(dko), flat(dkp), nblk, sv["k_hm"], small["attn_k_norm"][j:j + 1])
            dv_hm, = _band_merge(f"v_merge_{i}", flat(dvo), flat(dvp), nblk)
            gs["attn_sinks"][j] = jnp.sum(dsink.reshape(n_heads, QBLOCK), axis=1).reshape(1, n_heads)
            rb = _bias_bwd(dbias.reshape(n_heads, QBLOCK, 2 * QBLOCK), bucket, NUM_BUCKETS).T
            gs["rel_bias"] = rb if gs["rel_bias"] is None else gs["rel_bias"] + rb
            tok = lambda t_, nh: t_.reshape(nh, T, HEAD_DIM).transpose(1, 0, 2).reshape(T, nh * HEAD_DIM)
            dqkv = jnp.concatenate([tok(dq_hm, n_heads), tok(dk_hm, n_kv), tok(dv_hm, n_kv)], axis=1)
            g_qkv, = _mm(f"d_qkv_{i}", "tn", (D, QW, T), (tD, tQ, tT), [(_op(sv["h1"]), _op(dqkv), 0)], 1, _first,
                         [(qkv_w.shape, BF16, j, g_qkv)])
            dh1, = _mm(f"dh_qkv_{i}", "nt", (T, D, QW), (tmm, tD2, tQ), [(_op(dqkv), _op(qkv_w, j), 0)], 1, _first,
                       [((T, D), F32, None, None)])
        dx, gs["norm_mix"][i] = _rms_bwd(f"rms_mix_bwd_{i}", dx1, xin, small["norm_mix"][i:i + 1], dh1, tr)

    big = {"gu": g_gu, "down": g_down, "cin": g_cin, "sq": g_sq, "qkv": g_qkv}
    gsmall = {}
    for k, v in gs.items():
        if k == "rel_bias":
            gsmall[k] = v
        else:
            gsmall[k] = jnp.stack([t.reshape(small[k].shape[1:]) for t in v], axis=0)
    return loss, dx, big, gsmall


_ANY = pl.BlockSpec(memory_space=pl.ANY)


def _place():
    x, y, c = lax.axis_index("x"), lax.axis_index("y"), lax.axis_index("c")
    return x, y, c, [(1 - x, y), (x, 1 - y), (1 - x, 1 - y)]


def _lane_start(s, w):
    return pl.multiple_of(s * w, LANES) if w % LANES == 0 else s * w


def _slot(ref, kind, s):
    if kind == "col":
        w = ref.shape[2] // N_SLOTS
        return ref.at[:, :, pl.ds(_lane_start(s, w), w)]
    return ref.at[:, pl.ds(s, 1)]


def _rows_half(ref, h):
    n = ref.shape[-2] // 2
    if len(ref.shape) == 3:
        return ref.at[:, pl.ds(h * n, n), :]
    return ref.at[:, :, pl.ds(h * n, n), :]


def _full_shape(shard, kind):
    s = shard.shape
    return (s[0], s[1], N_SLOTS * s[2]) if kind == "col" else (s[0], N_SLOTS, s[2], s[3])


def _gather(shards, kinds):
    ng = len(shards)

    def body(*refs):
        sh, out = refs[:ng], refs[ng:2 * ng]
        send, recv, loc = refs[2 * ng:]
        x, y, c, chips = _place()
        s = 2 * x + y
        sib = (x, y, 1 - c)

        def rcopy(g, k, src, dst, dev):
            return pltpu.make_async_remote_copy(src_ref=src, dst_ref=dst, send_sem=send.at[g * 6 + k], recv_sem=recv.at[g * 6 + k],
                                                device_id=dev, device_id_type=MESH)

        own = [pltpu.make_async_copy(sh[g], _slot(out[g], kinds[g], s), loc.at[g]) for g in range(ng)]
        for cp in own:
            cp.start()
        sent = []
        for j, (cx, cy) in enumerate(chips):
            for g in range(ng):
                sent.append(rcopy(g, j, _rows_half(sh[g], c), _rows_half(_slot(out[g], kinds[g], s), c), (cx, cy, c)))
                sent[-1].start()
        for j, (cx, cy) in enumerate(chips):
            for g in range(ng):
                landed = _rows_half(_slot(out[g], kinds[g], 2 * cx + cy), c)
                rcopy(g, j, landed, landed, (cx, cy, c)).wait_recv()
                sent.append(rcopy(g, 3 + j, landed, landed, sib))
                sent[-1].start()
        for j, (cx, cy) in enumerate(chips):
            for g in range(ng):
                handed = _rows_half(_slot(out[g], kinds[g], 2 * cx + cy), 1 - c)
                rcopy(g, 3 + j, handed, handed, sib).wait_recv()
        for cp in sent:
            cp.wait_send()
        for cp in own:
            cp.wait()

    return pl.pallas_call(
        body, name="gather_weights", in_specs=[_ANY] * ng, out_specs=[_ANY] * ng,
        out_shape=[jax.ShapeDtypeStruct(_full_shape(a, k), a.dtype) for a, k in zip(shards, kinds)],
        scratch_shapes=[pltpu.SemaphoreType.DMA((6 * ng,)), pltpu.SemaphoreType.DMA((6 * ng,)), pltpu.SemaphoreType.DMA((ng,))],
    )(*shards)


def _pair_send(grads):
    ng = len(grads)

    def half_shape(a):
        s = list(a.shape)
        s[-2] //= 2
        return tuple(s)

    def body(*refs):
        gr, out = refs[:ng], refs[ng:2 * ng]
        send, recv = refs[2 * ng:]
        x, y, c, _ = _place()
        cps = [pltpu.make_async_remote_copy(src_ref=_rows_half(gr[g], 1 - c), dst_ref=out[g], send_sem=send.at[g], recv_sem=recv.at[g],
                                            device_id=(x, y, 1 - c), device_id_type=MESH) for g in range(ng)]
        for cp in cps:
            cp.start()
        for cp in cps:
            cp.wait()

    return pl.pallas_call(
        body, name="grad_pair_send", in_specs=[_ANY] * ng, out_specs=[_ANY] * ng,
        out_shape=[jax.ShapeDtypeStruct(half_shape(a), a.dtype) for a in grads],
        scratch_shapes=[pltpu.SemaphoreType.DMA((ng,)), pltpu.SemaphoreType.DMA((ng,))],
    )(*grads)


def _add_half(name, g3, pa3, c_arr):
    n, R, N = g3.shape
    rh = R // 2
    tr = _tile(rh, max(16, (1 << 19) // N), 16)
    nb = rh // tr

    def body(c_ref, g_ref, p_ref, o_ref):
        o_ref[...] = (g_ref[...].astype(F32) + p_ref[...].astype(F32)).astype(o_ref.dtype)

    return pl.pallas_call(
        body, name=name,
        grid_spec=pltpu.PrefetchScalarGridSpec(
            num_scalar_prefetch=1, grid=(n, nb),
            in_specs=[pl.BlockSpec((None, tr, N), lambda l, i, c, nb=nb: (l, c[0] * nb + i, 0)), pl.BlockSpec((None, tr, N), lambda l, i, c: (l, i, 0))],
            out_specs=pl.BlockSpec((None, tr, N), lambda l, i, c: (l, i, 0))),
        out_shape=jax.ShapeDtypeStruct(pa3.shape, g3.dtype), compiler_params=_cparams(("arbitrary", "arbitrary")),
    )(c_arr, g3, pa3)


def _ici_exchange(psums, kinds):
    ng = len(psums)

    def recv_shape(a, kind):
        s = a.shape
        return (3, s[0], s[1], s[2] // N_SLOTS) if kind == "col" else (3, s[0], 1, s[2], s[3])

    def body(*refs):
        ps, out = refs[:ng], refs[ng:2 * ng]
        send, recv = refs[2 * ng:]
        x, y, c, chips = _place()
        cps = []
        for j, (cx, cy) in enumerate(chips):
            for g in range(ng):
                cps.append(pltpu.make_async_remote_copy(src_ref=_slot(ps[g], kinds[g], 2 * cx + cy), dst_ref=out[g].at[j], send_sem=send.at[g * 3 + j],
                                                        recv_sem=recv.at[g * 3 + j], device_id=(cx, cy, c), device_id_type=MESH))
                cps[-1].start()
        for cp in cps:
            cp.wait()

    return pl.pallas_call(
        body, name="grad_ici_exchange", in_specs=[_ANY] * ng, out_specs=[_ANY] * ng,
        out_shape=[jax.ShapeDtypeStruct(recv_shape(a, k), a.dtype) for a, k in zip(psums, kinds)],
        scratch_shapes=[pltpu.SemaphoreType.DMA((3 * ng,)), pltpu.SemaphoreType.DMA((3 * ng,))],
    )(*psums)


def _sum4(name, p3, rc3, s_arr, lead, kind):
    R = p3.shape[1]
    W = rc3.shape[2]
    tr = _tile(R, max(16, (1 << 18) // W), 16)
    if kind == "col":
        p_spec = pl.BlockSpec((None, tr, W), lambda l, i, s: (l, i, s[0]))
    else:
        p_spec = pl.BlockSpec((None, tr, W), lambda l, i, s: (l * N_SLOTS + s[0], i, 0))
    r_specs = [pl.BlockSpec((None, tr, W), lambda l, i, s, j=j: (j * lead + l, i, 0)) for j in range(3)]

    def body(s_ref, p_ref, r0, r1, r2, o_ref):
        o_ref[...] = ((p_ref[...].astype(F32) + r0[...].astype(F32)) + r1[...].astype(F32)) + r2[...].astype(F32)

    return pl.pallas_call(
        body, name=name,
        grid_spec=pltpu.PrefetchScalarGridSpec(num_scalar_prefetch=1, grid=(lead, R // tr), in_specs=[p_spec] + r_specs,
                                               out_specs=pl.BlockSpec((None, tr, W), lambda l, i, s: (l, i, 0))),
        out_shape=jax.ShapeDtypeStruct((lead, R, W), F32), compiler_params=_cparams(("arbitrary", "arbitrary")),
    )(s_arr, p3, rc3, rc3, rc3)


def _pair_share(halves):
    ng = len(halves)

    def body(*refs):
        hv, out = refs[:ng], refs[ng:2 * ng]
        send, recv, loc = refs[2 * ng:]
        x, y, c, _ = _place()
        own = [pltpu.make_async_copy(hv[g], _rows_half(out[g], c), loc.at[g]) for g in range(ng)]
        far = [pltpu.make_async_remote_copy(src_ref=hv[g], dst_ref=_rows_half(out[g], c), send_sem=send.at[g], recv_sem=recv.at[g],
                                            device_id=(x, y, 1 - c), device_id_type=MESH) for g in range(ng)]
        for cp in own + far:
            cp.start()
        for cp in far:
            cp.wait_send()
        for g in range(ng):
            theirs = _rows_half(out[g], 1 - c)
            pltpu.make_async_remote_copy(src_ref=theirs, dst_ref=theirs, send_sem=send.at[g], recv_sem=recv.at[g],
                                         device_id=(x, y, 1 - c), device_id_type=MESH).wait_recv()
        for cp in own:
            cp.wait()

    return pl.pallas_call(
        body, name="grad_pair_share", in_specs=[_ANY] * ng, out_specs=[_ANY] * ng,
        out_shape=[jax.ShapeDtypeStruct((a.shape[0], 2 * a.shape[1], a.shape[2]), a.dtype) for a in halves],
        scratch_shapes=[pltpu.SemaphoreType.DMA((ng,)), pltpu.SemaphoreType.DMA((ng,)), pltpu.SemaphoreType.DMA((ng,))],
    )(*halves)


N_DEVICES = 8


def _allreduce_small(v):
    rows, m = v.shape

    def body(v_ref, o_ref, buf, send, recv):
        x, y, c, _ = _place()
        me = 4 * x + 2 * y + c
        buf[me] = v_ref[...]
        cps = []
        for k in range(1, N_DEVICES):
            peer = me ^ k
            cps.append(pltpu.make_async_remote_copy(src_ref=v_ref, dst_ref=buf.at[me], send_sem=send.at[k - 1], recv_sem=recv.at[k - 1],
                                                    device_id=((peer >> 2) & 1, (peer >> 1) & 1, peer & 1), device_id_type=MESH))
            cps[-1].start()
        for k in range(1, N_DEVICES):
            theirs = buf.at[me ^ k]
            pltpu.make_async_remote_copy(src_ref=v_ref, dst_ref=theirs, send_sem=send.at[k - 1], recv_sem=recv.at[k - 1],
                                         device_id=(x, y, c), device_id_type=MESH).wait_recv()
        for cp in cps:
            cp.wait_send()
        acc = buf[0]
        for d in range(1, N_DEVICES):
            acc = acc + buf[d]
        o_ref[...] = acc

    vm = pl.BlockSpec(memory_space=pltpu.VMEM)
    return pl.pallas_call(
        body, name="allreduce_small", in_specs=[vm], out_specs=vm, out_shape=jax.ShapeDtypeStruct(v.shape, F32),
        scratch_shapes=[pltpu.VMEM((N_DEVICES, rows, m), F32), pltpu.SemaphoreType.DMA((N_DEVICES - 1,)), pltpu.SemaphoreType.DMA((N_DEVICES - 1,))],
    )(v)


def _pad_rows(a, mult):
    r = (-a.shape[0]) % mult
    return a if r == 0 else jnp.concatenate([a, jnp.zeros((r,) + a.shape[1:], a.dtype)], axis=0)


def _pack_rows(parts, width, mult=16):
    rows, offs, at = [], [], 0
    for a in parts:
        a2 = _pad_rows(a.reshape(-1, width), mult)
        offs.append((at, a.size // width))
        rows.append(a2)
        at += a2.shape[0]
    return jnp.concatenate(rows, axis=0), offs


SMALL_SHARDED = ("ple_w_proj", "pool_w", "conv_w_dw", "conv_b_dw", "conv_ln_g", "conv_ln_b", "conv_b_out", "conv_b_in")
SMALL_REPLICATED = ("norm_mix", "norm_ffn", "norm_ple", "pool_scale", "attn_q_norm", "attn_k_norm", "attn_sinks", "rel_bias", "ple_b_gate")


def _small_to_full(name, slots):
    if name == "pool_w":
        return jnp.moveaxis(slots, 0, 2).reshape(slots.shape[1], slots.shape[2], N_SLOTS * slots.shape[3], slots.shape[4])
    return jnp.moveaxis(slots, 0, -2).reshape(slots.shape[1:-1] + (N_SLOTS * slots.shape[-1],))


def _small_to_slots(name, full):
    if name == "pool_w":
        nb, ng, gc, _ = full.shape
        return jnp.moveaxis(full.reshape(nb, ng, N_SLOTS, gc // N_SLOTS, gc), 2, 0)
    w = full.shape[-1] // N_SLOTS
    return jnp.moveaxis(full.reshape(full.shape[:-1] + (N_SLOTS, w)), -2, 0)


W_NAMES = ("norm_mix", "norm_ffn", "norm_ple", "conv_w_in", "conv_b_in", "conv_w_dw", "conv_b_dw", "conv_ln_g", "conv_ln_b", "conv_w_out",
           "conv_b_out", "pool_w", "pool_scale", "attn_w_qkv", "attn_q_norm", "attn_k_norm", "attn_sinks", "attn_w_o", "rel_bias",
           "ffn_w_gate", "ffn_w_up", "ffn_w_down", "ple_w_proj", "ple_w_gate", "ple_b_gate")


def _step(x, p, target, w, m, v):
    T, D = x.shape[1], x.shape[2]
    L = p.shape[0]
    NA, NC = w["conv_w_in"].shape[0], w["attn_w_qkv"].shape[0]
    xi, yi, ci = lax.axis_index("x"), lax.axis_index("y"), lax.axis_index("c")
    c_arr = jnp.reshape(ci, (1,)).astype(jnp.int32)
    s_arr = jnp.reshape(2 * xi + yi, (1,)).astype(jnp.int32)

    ws = w["ffn_w_gate"].shape[2]
    wq = D // N_SLOTS
    sm_pack, sm_offs = _pack_rows([w[k] for k in SMALL_SHARDED], wq)
    shards = [
        jnp.stack([w["ffn_w_gate"], w["ffn_w_up"]], axis=1).reshape(2 * L, D, ws).astype(BF16),
        w["ffn_w_down"].astype(BF16)[:, None],
        w["conv_w_in"].astype(BF16),
        jnp.concatenate([w["conv_w_out"], w["attn_w_o"], w["ple_w_gate"]], axis=0).astype(BF16)[:, None],
        w["attn_w_qkv"].astype(BF16),
        sm_pack[None, None],
    ]
    kinds = ["col", "row", "col", "row", "col", "row"]
    full = _gather(shards, kinds)
    wts = {"gu": full[0], "down": full[1].reshape(L, -1, D), "cin": full[2], "sq": full[3].reshape(NA + NC + L, D, D), "qkv": full[4]}
    small = {k: w[k] for k in SMALL_REPLICATED}
    for k, (at, n) in zip(SMALL_SHARDED, sm_offs):
        small[k] = _small_to_full(k, full[5][0, :, at:at + n].reshape((N_SLOTS,) + w[k].shape))

    loss, dx, big, gsmall = _local_step(x[0], p[:, 0], target[0], wts, small)

    rep_parts = [gsmall[k] for k in SMALL_REPLICATED] + [loss]
    flat = jnp.concatenate([a.reshape(-1) for a in rep_parts])
    n_flat = flat.shape[0]
    m_cols = -(-n_flat // (8 * LANES)) * LANES
    flat = jnp.concatenate([flat, jnp.zeros((8 * m_cols - n_flat,), F32)]).reshape(8, m_cols)
    red = _allreduce_small(flat).reshape(-1)
    grads, at = {}, 0
    for k in SMALL_REPLICATED:
        grads[k] = red[at:at + w[k].size].reshape(w[k].shape)
        at += w[k].size
    loss_out = red[at]

    slots = {k: _small_to_slots(k, gsmall[k]) for k in SMALL_SHARDED}
    gsm = jnp.stack([_pack_rows([slots[k][s] for k in SMALL_SHARDED], wq)[0] for s in range(N_SLOTS)], axis=0)
    local = [big["gu"], big["down"].reshape(L, N_SLOTS, -1, D), big["cin"], big["sq"].reshape(NA + NC + L, N_SLOTS, -1, D), big["qkv"],
             gsm[None]]
    theirs = _pair_send(local)
    psums = []
    for g, (a, t) in enumerate(zip(local, theirs)):
        if kinds[g] == "col":
            psums.append(_add_half(f"pair_add_{g}", a, t, c_arr))
        else:
            n4 = a.shape[0] * N_SLOTS
            psums.append(_add_half(f"pair_add_{g}", a.reshape(n4, a.shape[2], a.shape[3]), t.reshape(n4, t.shape[2], t.shape[3]), c_arr).reshape(t.shape))
    got = _ici_exchange(psums, kinds)
    halves = []
    for g, (ps, rc) in enumerate(zip(psums, got)):
        lead = ps.shape[0]
        if kinds[g] == "col":
            halves.append(_sum4(f"slot_sum_{g}", ps, rc.reshape(3 * lead, rc.shape[2], rc.shape[3]), s_arr, lead, "col"))
        else:
            halves.append(_sum4(f"slot_sum_{g}", ps.reshape(lead * N_SLOTS, ps.shape[2], ps.shape[3]), rc.reshape(3 * lead, rc.shape[3], rc.shape[4]),
                                s_arr, lead, "row"))
    gsh = _pair_share(halves)
    gu_g = gsh[0].reshape(L, 2, D, ws)
    grads["ffn_w_gate"], grads["ffn_w_up"] = gu_g[:, 0], gu_g[:, 1]
    grads["ffn_w_down"] = gsh[1]
    grads["conv_w_in"] = gsh[2]
    grads["conv_w_out"], grads["attn_w_o"], grads["ple_w_gate"] = gsh[3][:NA], gsh[3][NA:NA + NC], gsh[3][NA + NC:]
    grads["attn_w_qkv"] = gsh[4]
    for k, (at, n) in zip(SMALL_SHARDED, sm_offs):
        grads[k] = gsh[5][0, at:at + n].reshape(w[k].shape)

    outs_d, outs_m, outs_v = [], [], []
    for k in W_NAMES:
        d_, m_, v_ = _adamw(f"adamw_{k}", w[k], grads[k], m[k], v[k])
        outs_d.append(d_)
        outs_m.append(m_)
        outs_v.append(v_)
    return (loss_out, dx[None], *[grads[k] for k in W_NAMES], *outs_d, *outs_m, *outs_v)


def kernel(x, p, norm_mix, norm_ffn, norm_ple, conv_w_in, conv_b_in, conv_w_dw, conv_b_dw, conv_ln_g, conv_ln_b, conv_w_out, conv_b_out, pool_w, pool_scale, attn_w_qkv, attn_q_norm, attn_k_norm, attn_sinks, attn_w_o, rel_bias, ffn_w_gate, ffn_w_up, ffn_w_down, ple_w_proj, ple_w_gate, ple_b_gate, loss_target, m_norm_mix, m_norm_ffn, m_norm_ple, m_conv_w_in, m_conv_b_in, m_conv_w_dw, m_conv_b_dw, m_conv_ln_g, m_conv_ln_b, m_conv_w_out, m_conv_b_out, m_pool_w, m_pool_scale, m_attn_w_qkv, m_attn_q_norm, m_attn_k_norm, m_attn_sinks, m_attn_w_o, m_rel_bias, m_ffn_w_gate, m_ffn_w_up, m_ffn_w_down, m_ple_w_proj, m_ple_w_gate, m_ple_b_gate, v_norm_mix, v_norm_ffn, v_norm_ple, v_conv_w_in, v_conv_b_in, v_conv_w_dw, v_conv_b_dw, v_conv_ln_g, v_conv_ln_b, v_conv_w_out, v_conv_b_out, v_pool_w, v_pool_scale, v_attn_w_qkv, v_attn_q_norm, v_attn_k_norm, v_attn_sinks, v_attn_w_o, v_rel_bias, v_ffn_w_gate, v_ffn_w_up, v_ffn_w_down, v_ple_w_proj, v_ple_w_gate, v_ple_b_gate):
    ws_ = (norm_mix, norm_ffn, norm_ple, conv_w_in, conv_b_in, conv_w_dw, conv_b_dw, conv_ln_g, conv_ln_b, conv_w_out, conv_b_out, pool_w, pool_scale, attn_w_qkv, attn_q_norm, attn_k_norm, attn_sinks, attn_w_o, rel_bias, ffn_w_gate, ffn_w_up, ffn_w_down, ple_w_proj, ple_w_gate, ple_b_gate)
    ms_ = (m_norm_mix, m_norm_ffn, m_norm_ple, m_conv_w_in, m_conv_b_in, m_conv_w_dw, m_conv_b_dw, m_conv_ln_g, m_conv_ln_b, m_conv_w_out, m_conv_b_out, m_pool_w, m_pool_scale, m_attn_w_qkv, m_attn_q_norm, m_attn_k_norm, m_attn_sinks, m_attn_w_o, m_rel_bias, m_ffn_w_gate, m_ffn_w_up, m_ffn_w_down, m_ple_w_proj, m_ple_w_gate, m_ple_b_gate)
    vs_ = (v_norm_mix, v_norm_ffn, v_norm_ple, v_conv_w_in, v_conv_b_in, v_conv_w_dw, v_conv_b_dw, v_conv_ln_g, v_conv_ln_b, v_conv_w_out, v_conv_b_out, v_pool_w, v_pool_scale, v_attn_w_qkv, v_attn_q_norm, v_attn_k_norm, v_attn_sinks, v_attn_w_o, v_rel_bias, v_ffn_w_gate, v_ffn_w_up, v_ffn_w_down, v_ple_w_proj, v_ple_w_gate, v_ple_b_gate)
    return _step(x, p, loss_target, dict(zip(W_NAMES, ws_)), dict(zip(W_NAMES, ms_)), dict(zip(W_NAMES, vs_)))
```

```python
import functools
import math

import jax
import jax.numpy as jnp
import numpy as np
from jax import lax
from jax.experimental import pallas as pl
from jax.experimental.pallas import tpu as pltpu

F32 = jnp.float32
BF16 = jnp.bfloat16
MESH = pl.DeviceIdType.MESH

CHUNK = 64
CONV_WIDTH = 31
POOL_WINDOWS = (2, 4, 8, 16)
HEAD_DIM = 64
WINDOW_CHUNKS = 2
QBLOCK = 128
NUM_BUCKETS = 32
REL_MAX_DIST = 128
EPS = 1e-6
NEG_INF = -1e30
ADAM_LR, ADAM_B1, ADAM_B2, ADAM_EPS, ADAM_WD, ADAM_STEP = 0.001, 0.9, 0.999, 1e-08, 0.01, 10
N_SLOTS = 4
LANES = 128
VMEM_LIMIT_BYTES = 56 * 1024 * 1024


def _cparams(sem):
    return pltpu.CompilerParams(dimension_semantics=sem, vmem_limit_bytes=VMEM_LIMIT_BYTES)


def _tile(n, pref, mult=LANES):
    if n <= pref:
        return n
    t = (pref // mult) * mult
    while t >= mult:
        if n % t == 0:
            return t
        t -= mult
    return n


def _sig(z):
    return 1.0 / (1.0 + jnp.exp(-z))


def _op(arr, lead=None, ro=0, co=0, fn=None):
    return (arr, lead, ro, co, fn)


_DOT_DIMS = {"nn": (((1,), (0,)), ((), ())), "nt": (((1,), (1,)), ((), ())), "tn": (((0,), (0,)), ((), ()))}


def _mm(name, mode, dims, tiles, terms, n_acc, epilogue, outs, extras=()):
    M, N, K = dims
    tm, tn, tk = tiles
    assert M % tm == 0 and N % tn == 0 and K % tk == 0, (name, dims, tiles)
    nk = K // tk
    a_tile = (tk, tm) if mode == "tn" else (tm, tk)
    b_tile = (tn, tk) if mode == "nt" else (tk, tn)
    a_fn = (lambda i, j, k: (k, i)) if mode == "tn" else (lambda i, j, k: (i, k))
    b_fn = (lambda i, j, k: (j, k)) if mode == "nt" else (lambda i, j, k: (k, j))
    dn = _DOT_DIMS[mode]

    operands, specs, seen = [], [], {}

    def add(op, tshape, default_fn):
        arr, lead, ro, co, fn = op
        fn = fn or default_fn
        key = (id(arr), lead, ro, co, id(fn) if op[4] is not None else None, tshape)
        if key in seen:
            return seen[key]

        def imap(i, j, k, fn=fn, lead=lead, ro=ro, co=co):
            r, c = fn(i, j, k)
            return (r + ro, c + co) if lead is None else (lead, r + ro, c + co)

        operands.append(arr)
        specs.append(pl.BlockSpec(tshape if lead is None else (None,) + tshape, imap))
        seen[key] = len(operands) - 1
        return seen[key]

    term_idx = [(add(a, a_tile, a_fn), add(b, b_tile, b_fn), acc) for a, b, acc in terms]
    extra_idx = []
    for arr, kind, lead, co in extras:
        if kind == "tile":
            extra_idx.append(add(_op(arr, lead, 0, co), (tm, tn), lambda i, j, k: (i, j)))
        elif kind == "row":
            arr3 = arr.reshape(arr.shape[0], 1, arr.shape[1])
            extra_idx.append(add(_op(arr3, 0 if lead is None else lead, 0, co), (1, tn), lambda i, j, k: (0, j)))
        else:
            extra_idx.append(add(_op(arr, None, 0, 0), (tm, 1), lambda i, j, k: (i, 0)))
    n_in = len(operands)
    out_shapes, out_specs, aliases = [], [], {}
    for oi, (shape, dtype, lead, alias) in enumerate(outs):
        out_shapes.append(jax.ShapeDtypeStruct(shape, dtype))
        if lead is None:
            out_specs.append(pl.BlockSpec((tm, tn), lambda i, j, k: (i, j)))
        else:
            out_specs.append(pl.BlockSpec((None, tm, tn), lambda i, j, k, lead=lead: (lead, i, j)))
        if alias is not None:
            operands.append(alias)
            specs.append(pl.BlockSpec(memory_space=pl.ANY))
            aliases[len(operands) - 1] = oi
    n_all_in = len(operands)
    n_out = len(outs)

    def body(*refs):
        ins = refs[:n_in]
        o_refs = refs[n_all_in:n_all_in + n_out]
        accs = refs[n_all_in + n_out:]

        def dots():
            sums = [None] * n_acc
            for ai, bi, acc_i in term_idx:
                a = ins[ai][...]
                b = ins[bi][...]
                if a.dtype != BF16:
                    a = a.astype(BF16)
                if b.dtype != BF16:
                    b = b.astype(BF16)
                d = lax.dot_general(a, b, dn, preferred_element_type=F32)
                sums[acc_i] = d if sums[acc_i] is None else sums[acc_i] + d
            return sums

        def finish(vals):
            res = epilogue(vals, [ins[e][...] for e in extra_idx])
            for o, r in zip(o_refs, res):
                o[...] = r.astype(o.dtype)

        if nk == 1:
            finish(dots())
            return
        k = pl.program_id(2)

        @pl.when(k == 0)
        def _():
            for acc, d in zip(accs, dots()):
                acc[...] = d

        if nk > 2:
            @pl.when((k > 0) & (k < nk - 1))
            def _():
                for acc, d in zip(accs, dots()):
                    acc[...] += d

        @pl.when(k == nk - 1)
        def _():
            finish([acc[...] + d for acc, d in zip(accs, dots())])

    res = pl.pallas_call(
        body, name=name, grid=(M // tm, N // tn, nk), in_specs=specs, out_specs=out_specs, out_shape=out_shapes,
        scratch_shapes=[pltpu.VMEM((tm, tn), F32) for _ in range(n_acc if nk > 1 else 0)], input_output_aliases=aliases,
        compiler_params=_cparams(("parallel", "parallel", "arbitrary")),
    )(*operands)
    return res


def _rowk(name, T, tm, ins, outs, body, scratch=()):
    assert T % tm == 0, (name, T, tm)
    n = T // tm
    specs = []
    for arr, kind in ins:
        w = arr.shape[-1]
        if kind == "tile":
            specs.append(pl.BlockSpec((tm, w), lambda i: (i, 0)))
        elif kind == "prev":
            specs.append(pl.BlockSpec((tm, w), lambda i: (jnp.maximum(i - 1, 0), 0)))
        elif kind == "next":
            specs.append(pl.BlockSpec((tm, w), lambda i, n=n: (jnp.minimum(i + 1, n - 1), 0)))
        else:
            specs.append(pl.BlockSpec(arr.shape, lambda i, nd=arr.ndim: (0,) * nd))
    out_shapes, out_specs = [], []
    for shape, dtype, kind in outs:
        out_shapes.append(jax.ShapeDtypeStruct(shape, dtype))
        if kind == "tile":
            out_specs.append(pl.BlockSpec((tm, shape[-1]), lambda i: (i, 0)))
        else:
            out_specs.append(pl.BlockSpec(shape, lambda i, nd=len(shape): (0,) * nd))
    n_in, n_out = len(ins), len(outs)

    def kbody(*refs):
        body(pl.program_id(0), n, refs[:n_in], refs[n_in:n_in + n_out], refs[n_in + n_out:])

    return pl.pallas_call(
        kbody, name=name, grid=(n,), in_specs=specs, out_specs=out_specs, out_shape=out_shapes,
        scratch_shapes=list(scratch), compiler_params=_cparams(("arbitrary",)),
    )(*[a for a, _ in ins])


def _accum(ref, i, val):
    @pl.when(i == 0)
    def _():
        ref[...] = val

    @pl.when(i > 0)
    def _():
        ref[...] += val


def _colsum(v):
    return jnp.sum(v, axis=0, keepdims=True)


def _rms_r(x):
    return lax.rsqrt(jnp.mean(x * x, axis=-1, keepdims=True) + EPS)


def _rms_fwd(name, x, g, tm):
    T, D = x.shape

    def body(i, n, ins, outs, scr):
        xv = ins[0][...]
        outs[0][...] = (xv * _rms_r(xv) * ins[1][...]).astype(BF16)

    return _rowk(name, T, tm, [(x, "tile"), (g, "full")], [((T, D), BF16, "tile")], body)[0]


def _rms_bwd(name, dres, x, g, dh, tm, want_bf16=False, want_colsum=False):
    T, D = x.shape

    def body(i, n, ins, outs, scr):
        xv = ins[1][...]
        gv = ins[2][...]
        dhv = ins[3][...].astype(F32)
        r = _rms_r(xv)
        xh = xv * r
        dhg = dhv * gv
        dx = ins[0][...] + r * (dhg - xh * jnp.mean(dhg * xh, axis=-1, keepdims=True))
        outs[0][...] = dx
        _accum(outs[1], i, _colsum(dhv * xh))
        o = 2
        if want_bf16:
            outs[o][...] = dx.astype(BF16)
            o += 1
        if want_colsum:
            _accum(outs[o], i, _colsum(dx))

    outs = [((T, D), F32, "tile"), ((1, D), F32, "acc")]
    if want_bf16:
        outs.append(((T, D), BF16, "tile"))
    if want_colsum:
        outs.append(((1, D), F32, "acc"))
    return _rowk(name, T, tm, [(dres, "tile"), (x, "tile"), (g, "full"), (dh, "tile")], outs, body)


def _loss_head(y, target, tm):
    T, D = y.shape

    def body(i, n, ins, outs, scr):
        d = ins[0][...] - ins[1][...]
        outs[0][...] = d * (1.0 / D)
        _accum(outs[1], i, jnp.sum(_colsum(d * d), axis=1, keepdims=True) * (0.5 / D))

    return _rowk("loss_head", T, tm, [(y, "tile"), (target, "tile")], [((T, D), F32, "tile"), ((1, 1), F32, "acc")], body)


def _ple_bwd_elem(name, dx, gt, q, tm):
    T, D = dx.shape

    def body(i, n, ins, outs, scr):
        d = ins[0][...]
        g = ins[1][...]
        dz = d * ins[2][...] * g * (1.0 - g)
        outs[0][...] = (d * g).astype(BF16)
        outs[1][...] = dz.astype(BF16)
        _accum(outs[2], i, _colsum(dz))

    return _rowk(name, T, tm, [(dx, "tile"), (gt, "tile"), (q, "tile")],
                 [((T, D), BF16, "tile"), ((T, D), BF16, "tile"), ((1, D), F32, "acc")], body)


CONV_ROWS = 128


def _dwconv_fwd(name, u1, w_dw, b_dw, tm):
    T, D = u1.shape
    rc_n = tm // CONV_ROWS if tm >= CONV_ROWS else 1
    rows = min(CONV_ROWS, tm)
    halo = CONV_WIDTH - 1

    def body(i, n, ins, outs, scr):
        buf = scr[0]
        buf[pl.ds(0, tm), :] = jnp.where(i > 0, ins[0][...], 0.0)
        buf[pl.ds(tm, tm), :] = ins[1][...]
        w_ref, b_ref, o_ref = ins[2], ins[3], outs[0]

        def chunk(lc, carry):
            l0 = pl.multiple_of(lc * LANES, LANES)
            for rc in range(rc_n):
                acc = jnp.zeros((rows, LANES), F32) + b_ref[:, pl.ds(l0, LANES)]
                for k in range(CONV_WIDTH):
                    acc = acc + buf[pl.ds(tm - halo + k + rc * rows, rows), pl.ds(l0, LANES)] * w_ref[pl.ds(k, 1), pl.ds(l0, LANES)]
                o_ref[pl.ds(rc * rows, rows), pl.ds(l0, LANES)] = acc
            return carry

        lax.fori_loop(0, D // LANES, chunk, 0)

    return _rowk(name, T, tm, [(u1, "prev"), (u1, "tile"), (w_dw, "full"), (b_dw, "full")], [((T, D), F32, "tile")], body,
                 scratch=[pltpu.VMEM((2 * tm, D), F32)])[0]


def _ln_silu_fwd(name, u2, g, b, tm):
    T, D = u2.shape

    def body(i, n, ins, outs, scr):
        v = ins[0][...]
        mu = jnp.mean(v, axis=-1, keepdims=True)
        xc = v - mu
        y = xc * lax.rsqrt(jnp.mean(xc * xc, axis=-1, keepdims=True) + EPS) * ins[1][...] + ins[2][...]
        outs[0][...] = (y * _sig(y)).astype(BF16)

    return _rowk(name, T, tm, [(u2, "tile"), (g, "full"), (b, "full")], [((T, D), BF16, "tile")], body)[0]


def _ln_silu_bwd(name, du4, u2, g, b, tm):
    T, D = u2.shape

    def body(i, n, ins, outs, scr):
        v = ins[1][...]
        gv = ins[2][...]
        mu = jnp.mean(v, axis=-1, keepdims=True)
        xc = v - mu
        r = lax.rsqrt(jnp.mean(xc * xc, axis=-1, keepdims=True) + EPS)
        xh = xc * r
        y = xh * gv + ins[3][...]
        s = _sig(y)
        dy = ins[0][...] * (s * (1.0 + y * (1.0 - s)))
        dyg = dy * gv
        du2 = r * (dyg - jnp.mean(dyg, axis=-1, keepdims=True) - xh * jnp.mean(dyg * xh, axis=-1, keepdims=True))
        outs[0][...] = du2
        _accum(outs[1], i, _colsum(dy * xh))
        _accum(outs[2], i, _colsum(dy))
        _accum(outs[3], i, _colsum(du2))

    return _rowk(name, T, tm, [(du4, "tile"), (u2, "tile"), (g, "full"), (b, "full")],
                 [((T, D), F32, "tile"), ((1, D), F32, "acc"), ((1, D), F32, "acc"), ((1, D), F32, "acc")], body)


def _dwconv_glu_bwd(name, du2, u1, a_, gate, w_dw, tm):
    T, D = u1.shape
    rc_n = tm // CONV_ROWS if tm >= CONV_ROWS else 1
    rows = min(CONV_ROWS, tm)
    halo = CONV_WIDTH - 1

    def body(i, n, ins, outs, scr):
        bu, bd = scr
        bd[pl.ds(0, tm), :] = ins[0][...]
        bd[pl.ds(tm, tm), :] = jnp.where(i < n - 1, ins[1][...], 0.0)
        bu[pl.ds(0, tm), :] = jnp.where(i > 0, ins[2][...], 0.0)
        bu[pl.ds(tm, tm), :] = ins[3][...]
        a_ref, g_ref, w_ref = ins[4], ins[5], ins[6]
        dag_ref, dw_ref, db_ref = outs

        @pl.when(i == 0)
        def _():
            dw_ref[...] = jnp.zeros_like(dw_ref)
            db_ref[...] = jnp.zeros_like(db_ref)

        def chunk(lc, carry):
            l0 = pl.multiple_of(lc * LANES, LANES)
            l1 = pl.multiple_of(D + lc * LANES, LANES)
            for rc in range(rc_n):
                r0 = rc * rows
                d_here = bd[pl.ds(r0, rows), pl.ds(l0, LANES)]
                acc = jnp.zeros((rows, LANES), F32)
                for k in range(CONV_WIDTH):
                    wk = w_ref[pl.ds(k, 1), pl.ds(l0, LANES)]
                    acc = acc + bd[pl.ds(r0 + halo - k, rows), pl.ds(l0, LANES)] * wk
                    dw_ref[pl.ds(k, 1), pl.ds(l0, LANES)] += _colsum(d_here * bu[pl.ds(tm - halo + k + r0, rows), pl.ds(l0, LANES)])
                av = a_ref[pl.ds(r0, rows), pl.ds(l0, LANES)].astype(F32)
                sg = _sig(g_ref[pl.ds(r0, rows), pl.ds(l0, LANES)].astype(F32))
                da = acc * sg
                dg = acc * av * sg * (1.0 - sg)
                dag_ref[pl.ds(r0, rows), pl.ds(l0, LANES)] = da.astype(BF16)
                dag_ref[pl.ds(r0, rows), pl.ds(l1, LANES)] = dg.astype(BF16)
                db_ref[:, pl.ds(l0, LANES)] += _colsum(da)
                db_ref[:, pl.ds(l1, LANES)] += _colsum(dg)
            return carry

        lax.fori_loop(0, D // LANES, chunk, 0)

    return _rowk(name, T, tm, [(du2, "tile"), (du2, "next"), (u1, "prev"), (u1, "tile"), (a_, "tile"), (gate, "tile"), (w_dw, "full")],
                 [((T, 2 * D), BF16, "tile"), ((CONV_WIDTH, D), F32, "acc"), ((1, 2 * D), F32, "acc")], body,
                 scratch=[pltpu.VMEM((2 * tm, D), F32), pltpu.VMEM((2 * tm, D), F32)])


def _row_index(i, tm, r0, rows):
    return (i * tm + r0 + lax.broadcasted_iota(jnp.int32, (rows, 1), 0)).astype(F32)


def _pool_fwd(name, x, g, tm):
    T, D = x.shape
    gc = D // len(POOL_WINDOWS)
    rows = min(CONV_ROWS, tm)
    rc_n = tm // rows

    def body(i, n, ins, outs, scr):
        buf = scr[0]
        xp = ins[0][...]
        buf[pl.ds(0, tm), :] = jnp.where(i > 0, xp * _rms_r(xp) * ins[2][...], 0.0)
        xc = ins[1][...]
        buf[pl.ds(tm, tm), :] = xc * _rms_r(xc) * ins[2][...]
        o_ref = outs[0]
        for gi, w in enumerate(POOL_WINDOWS):
            def chunk(lc, carry, gi=gi, w=w):
                l0 = pl.multiple_of(gi * gc + lc * LANES, LANES)
                for rc in range(rc_n):
                    r0 = rc * rows
                    acc = buf[pl.ds(tm + r0, rows), pl.ds(l0, LANES)]
                    here = acc
                    for d in range(1, w):
                        acc = acc + buf[pl.ds(tm + r0 - d, rows), pl.ds(l0, LANES)]
                    cnt = jnp.minimum(_row_index(i, tm, r0, rows) + 1.0, float(w))
                    o_ref[pl.ds(r0, rows), pl.ds(l0, LANES)] = (acc / cnt - here).astype(BF16)
                return carry

            lax.fori_loop(0, gc // LANES, chunk, 0)

    return _rowk(name, T, tm, [(x, "prev"), (x, "tile"), (g, "full")], [((T, D), BF16, "tile")], body,
                 scratch=[pltpu.VMEM((2 * tm, D), F32)])[0]


def _pool_bwd(name, dmix, tm):
    T, D = dmix.shape
    gc = D // len(POOL_WINDOWS)
    rows = min(CONV_ROWS, tm)
    rc_n = tm // rows

    def body(i, n, ins, outs, scr):
        buf = scr[0]
        o_ref = outs[0]
        t_here = (i * tm + lax.broadcasted_iota(jnp.int32, (tm, 1), 0)).astype(F32) + 1.0
        for gi, w in enumerate(POOL_WINDOWS):
            cols = pl.ds(gi * gc, gc)
            buf[pl.ds(0, tm), cols] = ins[0][:, cols] / jnp.minimum(t_here, float(w))
            buf[pl.ds(tm, tm), cols] = jnp.where(i < n - 1, ins[1][:, cols] / float(w), 0.0)

            def chunk(lc, carry, gi=gi, w=w):
                l0 = pl.multiple_of(gi * gc + lc * LANES, LANES)
                for rc in range(rc_n):
                    r0 = rc * rows
                    acc = -ins[0][pl.ds(r0, rows), pl.ds(l0, LANES)]
                    for d in range(w):
                        acc = acc + buf[pl.ds(r0 + d, rows), pl.ds(l0, LANES)]
                    o_ref[pl.ds(r0, rows), pl.ds(l0, LANES)] = acc
                return carry

            lax.fori_loop(0, gc // LANES, chunk, 0)

    return _rowk(name, T, tm, [(dmix, "tile"), (dmix, "next")], [((T, D), F32, "tile")], body,
                 scratch=[pltpu.VMEM((2 * tm, D), F32)])[0]


def _pool_scale_bwd(name, dy, y0, scale, tm):
    T, D = dy.shape

    def body(i, n, ins, outs, scr):
        d = ins[0][...]
        outs[0][...] = (d * ins[2][...]).astype(BF16)
        _accum(outs[1], i, _colsum(d * ins[1][...]))

    return _rowk(name, T, tm, [(dy, "tile"), (y0, "tile"), (scale, "full")], [((T, D), BF16, "tile"), ((1, D), F32, "acc")], body)


def _t5_bucket_np():
    i = np.arange(QBLOCK)[:, None]
    j = np.arange(2 * QBLOCK)[None, :]
    rel = j - QBLOCK - i
    nb = NUM_BUCKETS // 2
    n = -rel
    ret = np.where(n < 0, nb, 0)
    n = np.abs(n)
    max_exact = nb // 2
    nf = np.maximum(n, 1).astype(np.float32)
    large = max_exact + (np.log(nf / np.float32(max_exact)) / np.float32(math.log(REL_MAX_DIST / max_exact))
                         * np.float32(nb - max_exact)).astype(np.int32)
    large = np.minimum(large, nb - 1)
    return (ret + np.where(n < max_exact, n, large)).astype(np.int32)


def _bias_fwd(rel_bias, bucket):
    nb, nh = rel_bias.shape

    def body(rb_ref, bk_ref, o_ref):
        h = pl.program_id(0)
        bk = bk_ref[...]
        acc = jnp.zeros(bk.shape, F32)
        for b in range(nb):
            acc = jnp.where(bk == b, rb_ref[b, h], acc)
        o_ref[...] = acc

    return pl.pallas_call(
        body, name="attn_bias_fwd", grid=(nh,),
        in_specs=[pl.BlockSpec(memory_space=pltpu.SMEM), pl.BlockSpec(bucket.shape, lambda h: (0, 0))],
        out_specs=pl.BlockSpec((None,) + bucket.shape, lambda h: (h, 0, 0)),
        out_shape=jax.ShapeDtypeStruct((nh,) + bucket.shape, F32), compiler_params=_cparams(("arbitrary",)),
    )(rel_bias, bucket)


def _bias_bwd(dbias, bucket, nb):
    nh = dbias.shape[0]

    def body(db_ref, bk_ref, o_ref):
        h = pl.program_id(0)
        bk = bk_ref[...]
        d = db_ref[...]
        for b in range(nb):
            o_ref[h, b] = jnp.sum(jnp.where(bk == b, d, 0.0))

    return pl.pallas_call(
        body, name="attn_bias_bwd", grid=(nh,),
        in_specs=[pl.BlockSpec((None,) + bucket.shape, lambda h: (h, 0, 0)), pl.BlockSpec(bucket.shape, lambda h: (0, 0))],
        out_specs=pl.BlockSpec(memory_space=pltpu.SMEM),
        out_shape=jax.ShapeDtypeStruct((nh, nb), F32), compiler_params=_cparams(("arbitrary",)),
    )(dbias, bucket)


def _head_norm_fwd(name, q2, g, tm):
    R, W = q2.shape

    def body(i, n, ins, outs, scr):
        v = ins[0][...]
        outs[0][...] = (v * _rms_r(v) * ins[1][...]).astype(BF16)

    return _rowk(name, R, tm, [(q2, "tile"), (g, "full")], [((R, W), BF16, "tile")], body)[0]


def _head_norm_bwd(name, dqn, q2, g, tm):
    R, W = q2.shape

    def body(i, n, ins, outs, scr):
        v = ins[1][...]
        d = ins[0][...]
        r = _rms_r(v)
        xh = v * r
        dg = d * ins[2][...]
        outs[0][...] = (r * (dg - xh * jnp.mean(dg * xh, axis=-1, keepdims=True))).astype(BF16)
        _accum(outs[1], i, _colsum(d * xh))

    return _rowk(name, R, tm, [(dqn, "tile"), (q2, "tile"), (g, "full")], [((R, W), BF16, "tile"), ((1, W), F32, "acc")], body)


def _band_merge(name, own, prev, k3=None, g=None):
    H, T, W = own.shape
    nblk = T // QBLOCK

    def body(*refs):
        o_ref, p_ref = refs[0], refs[1]
        out_ref = refs[4] if k3 is not None else refs[2]

        def blk(m, carry):
            r0 = pl.multiple_of(m * QBLOCK, QBLOCK)
            rn = pl.multiple_of(jnp.minimum(m + 1, nblk - 1) * QBLOCK, QBLOCK)
            d = o_ref[pl.ds(r0, QBLOCK), :] + jnp.where(m < nblk - 1, p_ref[pl.ds(rn, QBLOCK), :], 0.0)
            if k3 is None:
                out_ref[pl.ds(r0, QBLOCK), :] = d.astype(BF16)
                return carry
            v = refs[2][pl.ds(r0, QBLOCK), :]
            r = _rms_r(v)
            xh = v * r
            dg = d * refs[3][...]
            out_ref[pl.ds(r0, QBLOCK), :] = (r * (dg - xh * jnp.mean(dg * xh, axis=-1, keepdims=True))).astype(BF16)
            return carry + _colsum(d * xh)

        tot = lax.fori_loop(0, nblk, blk, jnp.zeros((1, W), F32))
        if k3 is not None:
            _accum(refs[5], pl.program_id(0), tot)

    head = pl.BlockSpec((None, T, W), lambda h: (h, 0, 0))
    ins, in_specs = [own, prev], [head, head]
    out_shape, out_specs = [jax.ShapeDtypeStruct((H, T, W), BF16)], [head]
    if k3 is not None:
        ins += [k3, g]
        in_specs += [head, pl.BlockSpec(g.shape, lambda h: (0, 0))]
        out_shape.append(jax.ShapeDtypeStruct((1, W), F32))
        out_specs.append(pl.BlockSpec((1, W), lambda h: (0, 0)))
    return pl.pallas_call(body, name=name, grid=(H,), in_specs=in_specs, out_specs=out_specs, out_shape=out_shape,
                          compiler_params=_cparams(("arbitrary",)))(*ins)


def _attn_logits(q, kb, bias, sink, n):
    rows = q.shape[0]
    s = lax.dot_general(q, kb, _DOT_DIMS["nt"], preferred_element_type=F32) * (HEAD_DIM ** -0.5) + bias
    qc = (lax.broadcasted_iota(jnp.int32, (rows, 1), 0) % QBLOCK) // CHUNK
    j = lax.broadcasted_iota(jnp.int32, (1, 2 * QBLOCK), 1)
    kc = j // CHUNK - QBLOCK // CHUNK
    ok = (kc <= qc) & (kc >= qc - WINDOW_CHUNKS) & ((n > 0) | (j >= QBLOCK))
    s = jnp.where(ok, s, NEG_INF)
    m = jnp.maximum(jnp.max(s, axis=-1, keepdims=True), sink)
    e = jnp.exp(s - m)
    es = jnp.exp(sink - m)
    den = jnp.sum(e, axis=-1, keepdims=True) + es
    return e / den, es / den


def _heads_per_step(n_kv):
    return 2 if n_kv % 2 == 0 else 1


def _attn_specs(group, hp, rows):
    blk = lambda hn, fn: pl.BlockSpec((hn, QBLOCK, HEAD_DIM), fn)
    cur = lambda h, n: (h, n, 0)
    prv = lambda h, n: (h, jnp.maximum(n - 1, 0), 0)
    bsp = pl.BlockSpec((hp, rows, 2 * QBLOCK), lambda h, n: (h, 0, 0))
    ssp = pl.BlockSpec((hp, rows, 1), lambda h, n: (h, 0, 0))
    return blk(hp * group, cur), blk(hp, prv), blk(hp, cur), bsp, ssp


def _attn_fwd(qn, kn, v, bias, sink_rows, n_kv, group, T):
    nblk = T // QBLOCK
    rows = group * QBLOCK
    hp = _heads_per_step(n_kv)

    def body(q_ref, kp_ref, kc_ref, vp_ref, vc_ref, b_ref, s_ref, o_ref):
        n = pl.program_id(1)
        for hh in range(hp):
            q = q_ref[pl.ds(hh * group, group)].reshape(rows, HEAD_DIM)
            kb = jnp.concatenate([kp_ref[hh], kc_ref[hh]], axis=0)
            vb = jnp.concatenate([vp_ref[hh], vc_ref[hh]], axis=0)
            p, _ = _attn_logits(q, kb, b_ref[hh], s_ref[hh], n)
            o = lax.dot_general(p.astype(BF16), vb, _DOT_DIMS["nn"], preferred_element_type=F32)
            o_ref[pl.ds(hh * group, group)] = o.reshape(group, QBLOCK, HEAD_DIM).astype(BF16)

    qs, kp, kc, bsp, ssp = _attn_specs(group, hp, rows)
    return pl.pallas_call(
        body, name="attn_fwd", grid=(n_kv // hp, nblk), in_specs=[qs, kp, kc, kp, kc, bsp, ssp],
        out_specs=qs, out_shape=jax.ShapeDtypeStruct(qn.shape, BF16), compiler_params=_cparams(("arbitrary", "arbitrary")),
    )(qn, kn, kn, v, v, bias, sink_rows)


def _attn_bwd(qn, kn, v, bias, sink_rows, do, n_kv, group, T):
    nblk = T // QBLOCK
    rows = group * QBLOCK
    scale = HEAD_DIM ** -0.5
    hp = _heads_per_step(n_kv)

    def body(q_ref, kp_ref, kc_ref, vp_ref, vc_ref, b_ref, s_ref, do_ref, dq_ref, dko_ref, dkp_ref, dvo_ref, dvp_ref, db_ref, ds_ref):
        n = pl.program_id(1)
        for hh in range(hp):
            q = q_ref[pl.ds(hh * group, group)].reshape(rows, HEAD_DIM)
            dov = do_ref[pl.ds(hh * group, group)].reshape(rows, HEAD_DIM)
            kb = jnp.concatenate([kp_ref[hh], kc_ref[hh]], axis=0)
            vb = jnp.concatenate([vp_ref[hh], vc_ref[hh]], axis=0)
            p, ps = _attn_logits(q, kb, b_ref[hh], s_ref[hh], n)
            dp = lax.dot_general(dov, vb, _DOT_DIMS["nt"], preferred_element_type=F32)
            delta = jnp.sum(p * dp, axis=-1, keepdims=True)
            dl = p * (dp - delta)
            dlb = dl.astype(BF16)
            dq = lax.dot_general(dlb, kb, _DOT_DIMS["nn"], preferred_element_type=F32) * scale
            dkb = lax.dot_general(dlb, q, _DOT_DIMS["tn"], preferred_element_type=F32) * scale
            dvb = lax.dot_general(p.astype(BF16), dov, _DOT_DIMS["tn"], preferred_element_type=F32)
            dq_ref[pl.ds(hh * group, group)] = dq.reshape(group, QBLOCK, HEAD_DIM)
            dkp_ref[hh] = dkb[:QBLOCK]
            dko_ref[hh] = dkb[QBLOCK:]
            dvp_ref[hh] = dvb[:QBLOCK]
            dvo_ref[hh] = dvb[QBLOCK:]
            dsink = -ps * delta

            @pl.when(n == 0)
            def _(hh=hh, dl=dl, dsink=dsink):
                db_ref[hh] = dl
                ds_ref[hh] = dsink

            @pl.when(n > 0)
            def _(hh=hh, dl=dl, dsink=dsink):
                db_ref[hh] += dl
                ds_ref[hh] += dsink

    qs, kp, kc, bsp, ssp = _attn_specs(group, hp, rows)
    kv_shape = jax.ShapeDtypeStruct(kn.shape, F32)
    return pl.pallas_call(
        body, name="attn_bwd", grid=(n_kv // hp, nblk), in_specs=[qs, kp, kc, kp, kc, bsp, ssp, qs],
        out_specs=[qs, kc, kc, kc, kc, bsp, ssp],
        out_shape=[jax.ShapeDtypeStruct(qn.shape, F32), kv_shape, kv_shape, kv_shape, kv_shape,
                   jax.ShapeDtypeStruct(bias.shape, F32), jax.ShapeDtypeStruct(sink_rows.shape, F32)],
        compiler_params=_cparams(("arbitrary", "arbitrary")),
    )(qn, kn, kn, v, v, bias, sink_rows, do)


def _adamw(name, w, g, m, v):
    shape = w.shape
    w2, g2, m2, v2 = (a.reshape(-1, shape[-1]) for a in (w, g, m, v))
    R, W = w2.shape
    tm = _tile(R, max(8, (1 << 19) // W), 8)
    d1 = 1.0 - ADAM_B1 ** ADAM_STEP
    d2 = 1.0 - ADAM_B2 ** ADAM_STEP

    def body(i, n, ins, outs, scr):
        wv, gv = ins[0][...], ins[1][...]
        mn = ADAM_B1 * ins[2][...] + (1.0 - ADAM_B1) * gv
        vn = ADAM_B2 * ins[3][...] + (1.0 - ADAM_B2) * (gv * gv)
        outs[0][...] = -ADAM_LR * ((mn / d1) / (jnp.sqrt(vn / d2) + ADAM_EPS) + ADAM_WD * wv)
        outs[1][...] = mn
        outs[2][...] = vn

    d, mn, vn = _rowk(name, R, tm, [(w2, "tile"), (g2, "tile"), (m2, "tile"), (v2, "tile")],
                      [((R, W), F32, "tile")] * 3, body)
    return d.reshape(shape), mn.reshape(shape), vn.reshape(shape)


def _first(accs, extras):
    return [accs[0]]


def _local_step(x, p, target, wts, small):
    T, D = x.shape
    L, _, PLE = p.shape
    gu, down, cin, sq, qkv_w = wts["gu"], wts["down"], wts["cin"], wts["sq"], wts["qkv"]
    FF = gu.shape[2]
    NA, NC = cin.shape[0], qkv_w.shape[0]
    QW = qkv_w.shape[2]
    KVD = (QW - D) // 2
    n_heads, n_kv = D // HEAD_DIM, KVD // HEAD_DIM
    group = n_heads // n_kv
    nblk = T // QBLOCK
    GC = D // len(POOL_WINDOWS)

    tr = _tile(T, 256, 8)
    tmm = _tile(T, 1024)
    tD = _tile(D, 1024)
    tDk = _tile(D, 2048)
    tD2 = _tile(D, 512)
    tF = _tile(FF, 512)
    tFk = _tile(FF, 2816)
    tP = _tile(PLE, 512)
    tQ = _tile(QW, 768)
    tT = _tile(T, 1024)

    bucket = jnp.asarray(_t5_bucket_np())
    saved = []
    xs = x

    for i in range(L):
        kind, j = i % 3, i // 3
        sv = {"x": xs}
        h1 = _rms_fwd(f"rms_mix_{i}", xs, small["norm_mix"][i:i + 1], tr)
        if kind == 0:
            a_, gate, u1 = _mm(
                f"conv_in_{i}", "nn", (T, D, D), (tmm, tD2, tDk),
                [(_op(h1), _op(cin, j), 0), (_op(h1), _op(cin, j, 0, D // tD2), 1)], 2,
                lambda accs, ex: (lambda a, g: [a, g, a * _sig(g)])(accs[0] + ex[0], accs[1] + ex[1]),
                [((T, D), BF16, None, None), ((T, D), BF16, None, None), ((T, D), F32, None, None)],
                extras=[(small["conv_b_in"], "row", j, 0), (small["conv_b_in"], "row", j, D // tD2)])
            u2 = _dwconv_fwd(f"dwconv_{i}", u1, small["conv_w_dw"][j], small["conv_b_dw"][j:j + 1], tr)
            u4 = _ln_silu_fwd(f"ln_silu_{i}", u2, small["conv_ln_g"][j:j + 1], small["conv_ln_b"][j:j + 1], tr)
            x1, = _mm(f"conv_out_{i}", "nn", (T, D, D), (tmm, tD2, tDk), [(_op(u4), _op(sq, j), 0)], 1,
                      lambda accs, ex: [accs[0] + ex[0] + ex[1]], [((T, D), F32, None, None)],
                      extras=[(small["conv_b_out"], "row", j, 0), (xs, "tile", None, 0)])
            sv.update(h1=h1, a=a_, gate=gate, u1=u1, u2=u2, u4=u4)
        elif kind == 1:
            mix = _pool_fwd(f"pool_{i}", xs, small["norm_mix"][i:i + 1], tr)
            pw = small["pool_w"][j].reshape(len(POOL_WINDOWS) * GC, GC)
            kb = GC // _tile(GC, 512)
            tg = _tile(GC, 512)
            y0, x1 = _mm(f"pool_mm_{i}", "nn", (T, D, GC), (tmm, GC, tg),
                         [(_op(mix, fn=lambda i_, j_, k_, kb=kb: (i_, j_ * kb + k_)), _op(pw, fn=lambda i_, j_, k_, kb=kb: (j_ * kb + k_, 0)), 0)], 1,
                         lambda accs, ex: [accs[0], ex[1] + accs[0] * ex[0]],
                         [((T, D), F32, None, None), ((T, D), F32, None, None)],
                         extras=[(small["pool_scale"][j:j + 1], "row", None, 0), (xs, "tile", None, 0)])
            sv.update(mix=mix, y0=y0, pw=pw)
        else:
            qkv, = _mm(f"qkv_{i}", "nn", (T, QW, D), (tmm, tQ, tDk), [(_op(h1), _op(qkv_w, j), 0)], 1, _first,
                       [((T, QW), F32, None, None)])
            q_hm = qkv[:, :D].reshape(T, n_heads, HEAD_DIM).transpose(1, 0, 2).reshape(n_heads * T, HEAD_DIM)
            k_hm = qkv[:, D:D + KVD].reshape(T, n_kv, HEAD_DIM).transpose(1, 0, 2).reshape(n_kv * T, HEAD_DIM)
            v_hm = qkv[:, D + KVD:].reshape(T, n_kv, HEAD_DIM).transpose(1, 0, 2).astype(BF16)
            th = _tile(T, 2048, 8)
            qn = _head_norm_fwd(f"qnorm_{i}", q_hm, small["attn_q_norm"][j:j + 1], th).reshape(n_heads, T, HEAD_DIM)
            kn = _head_norm_fwd(f"knorm_{i}", k_hm, small["attn_k_norm"][j:j + 1], th).reshape(n_kv, T, HEAD_DIM)
            bias = _bias_fwd(small["rel_bias"], bucket).reshape(n_kv, group * QBLOCK, 2 * QBLOCK)
            sink_rows = jnp.broadcast_to(small["attn_sinks"][j].reshape(n_kv, group, 1, 1), (n_kv, group, QBLOCK, 1)).reshape(n_kv, group * QBLOCK, 1)
            o_hm = _attn_fwd(qn, kn, v_hm, bias, sink_rows, n_kv, group, T)
            o = o_hm.transpose(1, 0, 2).reshape(T, D)
            x1, = _mm(f"attn_o_{i}", "nn", (T, D, D), (tmm, tD2, tDk), [(_op(o), _op(sq, NA + j), 0)], 1,
                      lambda accs, ex: [accs[0] + ex[0]], [((T, D), F32, None, None)], extras=[(xs, "tile", None, 0)])
            sv.update(h1=h1, q_hm=q_hm, k_hm=k_hm, v_hm=v_hm, qn=qn, kn=kn, bias=bias, sink_rows=sink_rows, o=o)
        h2 = _rms_fwd(f"rms_ffn_{i}", x1, small["norm_ffn"][i:i + 1], tr)
        a, b, f = _mm(f"ffn_up_{i}", "nn", (T, FF, D), (tmm, tF, tDk), [(_op(h2), _op(gu, 2 * i), 0), (_op(h2), _op(gu, 2 * i + 1), 1)], 2,
                      lambda accs, ex: [accs[0], accs[1], accs[0] * _sig(accs[0]) * accs[1]],
                      [((T, FF), BF16, None, None)] * 3)
        x2, = _mm(f"ffn_down_{i}", "nn", (T, D, FF), (tmm, tD2, tFk), [(_op(f), _op(down, i), 0)], 1,
                  lambda accs, ex: [accs[0] + ex[0]], [((T, D), F32, None, None)], extras=[(x1, "tile", None, 0)])
        h3 = _rms_fwd(f"rms_ple_{i}", x2, small["norm_ple"][i:i + 1], tr)
        q, = _mm(f"ple_proj_{i}", "nn", (T, D, PLE), (tmm, tD2, tP), [(_op(p, i), _op(small["ple_w_proj"], i), 0)], 1, _first,
                 [((T, D), F32, None, None)])
        gt, x3 = _mm(f"ple_gate_{i}", "nn", (T, D, D), (tmm, tD2, tDk), [(_op(h3), _op(sq, NA + NC + i), 0)], 1,
                     lambda accs, ex: (lambda g_, q_, x_: [g_, x_ + g_ * q_])(_sig(accs[0] + ex[0]), ex[1], ex[2]),
                     [((T, D), F32, None, None), ((T, D), F32, None, None)],
                     extras=[(small["ple_b_gate"], "row", i, 0), (q, "tile", None, 0), (x2, "tile", None, 0)])
        sv.update(x1=x1, h2=h2, a=sv.get("a"), fa=a, fb=b, f=f, x2=x2, h3=h3, q=q, gt=gt)
        saved.append(sv)
        xs = x3

    dx, loss = _loss_head(xs, target, tr)

    g_gu = jnp.zeros(gu.shape, BF16)
    g_down = jnp.zeros(down.shape, BF16)
    g_cin = jnp.zeros(cin.shape, BF16)
    g_sq = jnp.zeros(sq.shape, BF16)
    g_qkv = jnp.zeros(qkv_w.shape, BF16)
    gs = {k: [None] * v.shape[0] for k, v in small.items() if k != "rel_bias"}
    gs["rel_bias"] = None
    gs["ple_w_proj"] = [None] * L

    for i in reversed(range(L)):
        kind, j = i % 3, i // 3
        sv = saved[i]
        dq, dz, dbg = _ple_bwd_elem(f"ple_bwd_{i}", dx, sv["gt"], sv["q"], tr)
        gs["ple_b_gate"][i] = dbg
        gs["ple_w_proj"][i], = _mm(f"d_ple_proj_{i}", "tn", (PLE, D, T), (tP, tD, tT), [(_op(p, i), _op(dq), 0)], 1, _first,
                                   [((PLE, D), F32, None, None)])
        g_sq, = _mm(f"d_ple_gate_{i}", "tn", (D, D, T), (tD, tD, tT), [(_op(sv["h3"]), _op(dz), 0)], 1, _first,
                    [(sq.shape, BF16, NA + NC + i, g_sq)])
        dh3, = _mm(f"dh_ple_{i}", "nt", (T, D, D), (tmm, tD2, tDk), [(_op(dz), _op(sq, NA + NC + i), 0)], 1, _first,
                   [((T, D), F32, None, None)])
        dx2, gs["norm_ple"][i], dx2b = _rms_bwd(f"rms_ple_bwd_{i}", dx, sv["x2"], small["norm_ple"][i:i + 1], dh3, tr, want_bf16=True)
        da, db = _mm(f"d_ffn_act_{i}", "nt", (T, FF, D), (tmm, tF, tDk), [(_op(dx2b), _op(down, i), 0)], 1,
                     lambda accs, ex: (lambda df, a_, b_, s_: [df * b_ * s_ * (1.0 + a_ * (1.0 - s_)), df * a_ * s_])(
                         accs[0], ex[0].astype(F32), ex[1].astype(F32), _sig(ex[0].astype(F32))),
                     [((T, FF), BF16, None, None)] * 2, extras=[(sv["fa"], "tile", None, 0), (sv["fb"], "tile", None, 0)])
        g_down, = _mm(f"d_ffn_down_{i}", "tn", (FF, D, T), (tF, tD, tT), [(_op(sv["f"]), _op(dx2b), 0)], 1, _first,
                      [(down.shape, BF16, i, g_down)])
        g_gu, = _mm(f"d_ffn_gate_{i}", "tn", (D, FF, T), (tD, tF, tT), [(_op(sv["h2"]), _op(da), 0)], 1, _first,
                    [(gu.shape, BF16, 2 * i, g_gu)])
        g_gu, = _mm(f"d_ffn_up_{i}", "tn", (D, FF, T), (tD, tF, tT), [(_op(sv["h2"]), _op(db), 0)], 1, _first,
                    [(gu.shape, BF16, 2 * i + 1, g_gu)])
        dh2, = _mm(f"dh_ffn_{i}", "nt", (T, D, FF), (tmm, tD2, tFk), [(_op(da), _op(gu, 2 * i), 0), (_op(db), _op(gu, 2 * i + 1), 0)], 1, _first,
                   [((T, D), F32, None, None)])
        want_cs = kind == 0
        res = _rms_bwd(f"rms_ffn_bwd_{i}", dx2, sv["x1"], small["norm_ffn"][i:i + 1], dh2, tr, want_bf16=True, want_colsum=want_cs)
        dx1, gs["norm_ffn"][i], dx1b = res[:3]
        xin = sv["x"]
        if kind == 0:
            gs["conv_b_out"][j] = res[3]
            g_sq, = _mm(f"d_conv_out_{i}", "tn", (D, D, T), (tD, tD, tT), [(_op(sv["u4"]), _op(dx1b), 0)], 1, _first,
                        [(sq.shape, BF16, j, g_sq)])
            du4, = _mm(f"dh_conv_out_{i}", "nt", (T, D, D), (tmm, tD2, tDk), [(_op(dx1b), _op(sq, j), 0)], 1, _first,
                       [((T, D), F32, None, None)])
            du2, gs["conv_ln_g"][j], gs["conv_ln_b"][j], gs["conv_b_dw"][j] = _ln_silu_bwd(
                f"ln_silu_bwd_{i}", du4, sv["u2"], small["conv_ln_g"][j:j + 1], small["conv_ln_b"][j:j + 1], tr)
            dag, gs["conv_w_dw"][j], gs["conv_b_in"][j] = _dwconv_glu_bwd(
                f"dwconv_bwd_{i}", du2, sv["u1"], sv["a"], sv["gate"], small["conv_w_dw"][j], tr)
            g_cin, = _mm(f"d_conv_in_{i}", "tn", (D, 2 * D, T), (tD, tD, tT), [(_op(sv["h1"]), _op(dag), 0)], 1, _first,
                         [(cin.shape, BF16, j, g_cin)])
            dh1, = _mm(f"dh_conv_in_{i}", "nt", (T, D, 2 * D), (tmm, tD2, tDk), [(_op(dag), _op(cin, j), 0)], 1, _first,
                       [((T, D), F32, None, None)])
        elif kind == 1:
            dys, gs["pool_scale"][j] = _pool_scale_bwd(f"pool_scale_bwd_{i}", dx1, sv["y0"], small["pool_scale"][j:j + 1], tr)
            tg = _tile(GC, 512)
            kb = GC // tg
            dmix, = _mm(f"dh_pool_{i}", "nt", (T, D, GC), (tmm, GC, tg),
                        [(_op(dys, fn=lambda i_, j_, k_, kb=kb: (i_, j_ * kb + k_)), _op(sv["pw"]), 0)], 1, _first,
                        [((T, D), F32, None, None)])
            ng = len(POOL_WINDOWS)
            gs["pool_w"][j], = _mm(f"d_pool_w_{i}", "tn", (D, GC, T), (GC, GC, tT),
                                   [(_op(sv["mix"]), _op(dys, fn=lambda i_, j_, k_: (k_, i_)), 0)], 1, _first,
                                   [((D, GC), F32, None, None)])
            gs["pool_w"][j] = gs["pool_w"][j].reshape(ng, GC, GC)
            dh1 = _pool_bwd(f"pool_bwd_{i}", dmix, tr)
        else:
            g_sq, = _mm(f"d_attn_o_{i}", "tn", (D, D, T), (tD, tD, tT), [(_op(sv["o"]), _op(dx1b), 0)], 1, _first,
                        [(sq.shape, BF16, NA + j, g_sq)])
            do, = _mm(f"dh_attn_o_{i}", "nt", (T, D, D), (tmm, tD2, tDk), [(_op(dx1b), _op(sq, NA + j), 0)], 1, _first,
                      [((T, D), BF16, None, None)])
            do_hm = do.reshape(T, n_heads, HEAD_DIM).transpose(1, 0, 2)
            dqn, dko, dkp, dvo, dvp, dbias, dsink = _attn_bwd(sv["qn"], sv["kn"], sv["v_hm"], sv["bias"], sv["sink_rows"], do_hm, n_kv, group, T)
            th = _tile(T, 2048, 8)
            dq_hm, gs["attn_q_norm"][j] = _head_norm_bwd(f"qnorm_bwd_{i}", dqn.reshape(n_heads * T, HEAD_DIM), sv["q_hm"],
                                                         small["attn_q_norm"][j:j + 1], th)
            dk_hm, gs["attn_k_norm"][j] = _band_merge(f"knorm_bwd_{i}", dko, dkp, sv["k_hm"].reshape(n_kv, T, HEAD_DIM), small["attn_k_norm"][j:j + 1])
            dv_hm, = _band_merge(f"v_merge_{i}", dvo, dvp)
            gs["attn_sinks"][j] = jnp.sum(dsink.reshape(n_heads, QBLOCK), axis=1).reshape(1, n_heads)
            rb = _bias_bwd(dbias.reshape(n_heads, QBLOCK, 2 * QBLOCK), bucket, NUM_BUCKETS).T
            gs["rel_bias"] = rb if gs["rel_bias"] is None else gs["rel_bias"] + rb
            tok = lambda t_, nh: t_.reshape(nh, T, HEAD_DIM).transpose(1, 0, 2).reshape(T, nh * HEAD_DIM)
            dqkv = jnp.concatenate([tok(dq_hm, n_heads), tok(dk_hm, n_kv), tok(dv_hm, n_kv)], axis=1)
            g_qkv, = _mm(f"d_qkv_{i}", "tn", (D, QW, T), (tD, tQ, tT), [(_op(sv["h1"]), _op(dqkv), 0)], 1, _first,
                         [(qkv_w.shape, BF16, j, g_qkv)])
            dh1, = _mm(f"dh_qkv_{i}", "nt", (T, D, QW), (tmm, tD2, tQ), [(_op(dqkv), _op(qkv_w, j), 0)], 1, _first,
                       [((T, D), F32, None, None)])
        dx, gs["norm_mix"][i] = _rms_bwd(f"rms_mix_bwd_{i}", dx1, xin, small["norm_mix"][i:i + 1], dh1, tr)

    big = {"gu": g_gu, "down": g_down, "cin": g_cin, "sq": g_sq, "qkv": g_qkv}
    gsmall = {}
    for k, v in gs.items():
        if k == "rel_bias":
            gsmall[k] = v
        else:
            gsmall[k] = jnp.stack([t.reshape(small[k].shape[1:]) for t in v], axis=0)
    return loss, dx, big, gsmall


_ANY = pl.BlockSpec(memory_space=pl.ANY)


def _place():
    x, y, c = lax.axis_index("x"), lax.axis_index("y"), lax.axis_index("c")
    return x, y, c, [(1 - x, y), (x, 1 - y), (1 - x, 1 - y)]


def _lane_start(s, w):
    return pl.multiple_of(s * w, LANES) if w % LANES == 0 else s * w


def _slot(ref, kind, s):
    if kind == "col":
        w = ref.shape[2] // N_SLOTS
        return ref.at[:, :, pl.ds(_lane_start(s, w), w)]
    return ref.at[:, pl.ds(s, 1)]


def _rows_half(ref, h):
    n = ref.shape[-2] // 2
    if len(ref.shape) == 3:
        return ref.at[:, pl.ds(h * n, n), :]
    return ref.at[:, :, pl.ds(h * n, n), :]


def _place_own(name, shard, kind, s_arr):
    if kind == "col":
        lead, R, W = shard.shape
        full = (lead, R, N_SLOTS * W)
    else:
        lead, _, R, W = shard.shape
        full = (lead, N_SLOTS, R, W)
    tr = _tile(R, max(16, (1 << 19) // W), 16)
    if kind == "col":
        i_spec = pl.BlockSpec((None, tr, W), lambda l, i, s: (l, i, 0))
        o_spec = pl.BlockSpec((None, tr, W), lambda l, i, s: (l, i, s[0]))
    else:
        i_spec = pl.BlockSpec((None, None, tr, W), lambda l, i, s: (l, 0, i, 0))
        o_spec = pl.BlockSpec((None, None, tr, W), lambda l, i, s: (l, s[0], i, 0))

    def body(s_ref, i_ref, o_ref):
        o_ref[...] = i_ref[...]

    return pl.pallas_call(
        body, name=name,
        grid_spec=pltpu.PrefetchScalarGridSpec(num_scalar_prefetch=1, grid=(lead, R // tr), in_specs=[i_spec], out_specs=o_spec),
        out_shape=jax.ShapeDtypeStruct(full, shard.dtype), compiler_params=_cparams(("arbitrary", "arbitrary")),
    )(s_arr, shard)


def _gather(shards, fulls, kinds):
    ng = len(shards)

    def body(*refs):
        sh, out = refs[:ng], refs[2 * ng:3 * ng]
        send, recv = refs[3 * ng:]
        x, y, c, chips = _place()
        s = 2 * x + y
        sib = (x, y, 1 - c)

        def rcopy(g, k, src, dst, dev):
            return pltpu.make_async_remote_copy(src_ref=src, dst_ref=dst, send_sem=send.at[g * 6 + k], recv_sem=recv.at[g * 6 + k],
                                                device_id=dev, device_id_type=MESH)

        sent = []
        for j, (cx, cy) in enumerate(chips):
            for g in range(ng):
                sent.append(rcopy(g, j, _rows_half(sh[g], c), _rows_half(_slot(out[g], kinds[g], s), c), (cx, cy, c)))
                sent[-1].start()
        for j, (cx, cy) in enumerate(chips):
            for g in range(ng):
                landed = _rows_half(_slot(out[g], kinds[g], 2 * cx + cy), c)
                rcopy(g, j, landed, landed, (cx, cy, c)).wait_recv()
                sent.append(rcopy(g, 3 + j, landed, landed, sib))
                sent[-1].start()
        for j, (cx, cy) in enumerate(chips):
            for g in range(ng):
                handed = _rows_half(_slot(out[g], kinds[g], 2 * cx + cy), 1 - c)
                rcopy(g, 3 + j, handed, handed, sib).wait_recv()
        for cp in sent:
            cp.wait_send()

    return pl.pallas_call(
        body, name="gather_weights", in_specs=[_ANY] * (2 * ng), out_specs=[_ANY] * ng,
        out_shape=[jax.ShapeDtypeStruct(a.shape, a.dtype) for a in fulls],
        input_output_aliases={ng + g: g for g in range(ng)},
        scratch_shapes=[pltpu.SemaphoreType.DMA((6 * ng,)), pltpu.SemaphoreType.DMA((6 * ng,))],
    )(*shards, *fulls)


def _pair_send(grads):
    ng = len(grads)

    def half_shape(a):
        s = list(a.shape)
        s[-2] //= 2
        return tuple(s)

    def body(*refs):
        gr, out = refs[:ng], refs[ng:2 * ng]
        send, recv = refs[2 * ng:]
        x, y, c, _ = _place()
        cps = [pltpu.make_async_remote_copy(src_ref=_rows_half(gr[g], 1 - c), dst_ref=out[g], send_sem=send.at[g], recv_sem=recv.at[g],
                                            device_id=(x, y, 1 - c), device_id_type=MESH) for g in range(ng)]
        for cp in cps:
            cp.start()
        for cp in cps:
            cp.wait()

    return pl.pallas_call(
        body, name="grad_pair_send", in_specs=[_ANY] * ng, out_specs=[_ANY] * ng,
        out_shape=[jax.ShapeDtypeStruct(half_shape(a), a.dtype) for a in grads],
        scratch_shapes=[pltpu.SemaphoreType.DMA((ng,)), pltpu.SemaphoreType.DMA((ng,))],
    )(*grads)


def _add_half(name, g3, pa3, c_arr):
    n, R, N = g3.shape
    rh = R // 2
    tr = _tile(rh, max(16, (1 << 19) // N), 16)
    nb = rh // tr

    def body(c_ref, g_ref, p_ref, o_ref):
        o_ref[...] = (g_ref[...].astype(F32) + p_ref[...].astype(F32)).astype(o_ref.dtype)

    return pl.pallas_call(
        body, name=name,
        grid_spec=pltpu.PrefetchScalarGridSpec(
            num_scalar_prefetch=1, grid=(n, nb),
            in_specs=[pl.BlockSpec((None, tr, N), lambda l, i, c, nb=nb: (l, c[0] * nb + i, 0)), pl.BlockSpec((None, tr, N), lambda l, i, c: (l, i, 0))],
            out_specs=pl.BlockSpec((None, tr, N), lambda l, i, c: (l, i, 0))),
        out_shape=jax.ShapeDtypeStruct(pa3.shape, g3.dtype), compiler_params=_cparams(("arbitrary", "arbitrary")),
    )(c_arr, g3, pa3)


def _ici_exchange(psums, kinds):
    ng = len(psums)

    def recv_shape(a, kind):
        s = a.shape
        return (3, s[0], s[1], s[2] // N_SLOTS) if kind == "col" else (3, s[0], 1, s[2], s[3])

    def body(*refs):
        ps, out = refs[:ng], refs[ng:2 * ng]
        send, recv = refs[2 * ng:]
        x, y, c, chips = _place()
        cps = []
        for j, (cx, cy) in enumerate(chips):
            for g in range(ng):
                cps.append(pltpu.make_async_remote_copy(src_ref=_slot(ps[g], kinds[g], 2 * cx + cy), dst_ref=out[g].at[j], send_sem=send.at[g * 3 + j],
                                                        recv_sem=recv.at[g * 3 + j], device_id=(cx, cy, c), device_id_type=MESH))
                cps[-1].start()
        for cp in cps:
            cp.wait()

    return pl.pallas_call(
        body, name="grad_ici_exchange", in_specs=[_ANY] * ng, out_specs=[_ANY] * ng,
        out_shape=[jax.ShapeDtypeStruct(recv_shape(a, k), a.dtype) for a, k in zip(psums, kinds)],
        scratch_shapes=[pltpu.SemaphoreType.DMA((3 * ng,)), pltpu.SemaphoreType.DMA((3 * ng,))],
    )(*psums)


def _sum4(name, p3, rc3, s_arr, lead, kind):
    R = p3.shape[1]
    W = rc3.shape[2]
    tr = _tile(R, max(16, (1 << 18) // W), 16)
    if kind == "col":
        p_spec = pl.BlockSpec((None, tr, W), lambda l, i, s: (l, i, s[0]))
    else:
        p_spec = pl.BlockSpec((None, tr, W), lambda l, i, s: (l * N_SLOTS + s[0], i, 0))
    r_specs = [pl.BlockSpec((None, tr, W), lambda l, i, s, j=j: (j * lead + l, i, 0)) for j in range(3)]

    def body(s_ref, p_ref, r0, r1, r2, o_ref):
        o_ref[...] = ((p_ref[...].astype(F32) + r0[...].astype(F32)) + r1[...].astype(F32)) + r2[...].astype(F32)

    return pl.pallas_call(
        body, name=name,
        grid_spec=pltpu.PrefetchScalarGridSpec(num_scalar_prefetch=1, grid=(lead, R // tr), in_specs=[p_spec] + r_specs,
                                               out_specs=pl.BlockSpec((None, tr, W), lambda l, i, s: (l, i, 0))),
        out_shape=jax.ShapeDtypeStruct((lead, R, W), F32), compiler_params=_cparams(("arbitrary", "arbitrary")),
    )(s_arr, p3, rc3, rc3, rc3)


def _pair_swap(halves):
    ng = len(halves)

    def body(*refs):
        hv, out = refs[:ng], refs[ng:2 * ng]
        send, recv = refs[2 * ng:]
        x, y, c, _ = _place()
        cps = [pltpu.make_async_remote_copy(src_ref=hv[g], dst_ref=out[g], send_sem=send.at[g], recv_sem=recv.at[g],
                                            device_id=(x, y, 1 - c), device_id_type=MESH) for g in range(ng)]
        for cp in cps:
            cp.start()
        for cp in cps:
            cp.wait()

    return pl.pallas_call(
        body, name="grad_pair_swap", in_specs=[_ANY] * ng, out_specs=[_ANY] * ng,
        out_shape=[jax.ShapeDtypeStruct(a.shape, a.dtype) for a in halves],
        scratch_shapes=[pltpu.SemaphoreType.DMA((ng,)), pltpu.SemaphoreType.DMA((ng,))],
    )(*halves)


N_DEVICES = 8


def _allreduce_small(v):
    rows, m = v.shape

    def body(v_ref, o_ref, buf, send, recv):
        x, y, c, _ = _place()
        me = 4 * x + 2 * y + c
        buf[me] = v_ref[...]
        cps = []
        for k in range(1, N_DEVICES):
            peer = me ^ k
            cps.append(pltpu.make_async_remote_copy(src_ref=v_ref, dst_ref=buf.at[me], send_sem=send.at[k - 1], recv_sem=recv.at[k - 1],
                                                    device_id=((peer >> 2) & 1, (peer >> 1) & 1, peer & 1), device_id_type=MESH))
            cps[-1].start()
        for k in range(1, N_DEVICES):
            theirs = buf.at[me ^ k]
            pltpu.make_async_remote_copy(src_ref=v_ref, dst_ref=theirs, send_sem=send.at[k - 1], recv_sem=recv.at[k - 1],
                                         device_id=(x, y, c), device_id_type=MESH).wait_recv()
        for cp in cps:
            cp.wait_send()
        acc = buf[0]
        for d in range(1, N_DEVICES):
            acc = acc + buf[d]
        o_ref[...] = acc

    vm = pl.BlockSpec(memory_space=pltpu.VMEM)
    return pl.pallas_call(
        body, name="allreduce_small", in_specs=[vm], out_specs=vm, out_shape=jax.ShapeDtypeStruct(v.shape, F32),
        scratch_shapes=[pltpu.VMEM((N_DEVICES, rows, m), F32), pltpu.SemaphoreType.DMA((N_DEVICES - 1,)), pltpu.SemaphoreType.DMA((N_DEVICES - 1,))],
    )(v)


def _pad_rows(a, mult):
    r = (-a.shape[0]) % mult
    return a if r == 0 else jnp.concatenate([a, jnp.zeros((r,) + a.shape[1:], a.dtype)], axis=0)


def _pack_rows(parts, width, mult=16):
    rows, offs, at = [], [], 0
    for a in parts:
        a2 = _pad_rows(a.reshape(-1, width), mult)
        offs.append((at, a.size // width))
        rows.append(a2)
        at += a2.shape[0]
    return jnp.concatenate(rows, axis=0), offs


SMALL_SHARDED = ("ple_w_proj", "pool_w", "conv_w_dw", "conv_b_dw", "conv_ln_g", "conv_ln_b", "conv_b_out", "conv_b_in")
SMALL_REPLICATED = ("norm_mix", "norm_ffn", "norm_ple", "pool_scale", "attn_q_norm", "attn_k_norm", "attn_sinks", "rel_bias", "ple_b_gate")


def _small_to_full(name, slots):
    if name == "pool_w":
        return jnp.moveaxis(slots, 0, 2).reshape(slots.shape[1], slots.shape[2], N_SLOTS * slots.shape[3], slots.shape[4])
    return jnp.moveaxis(slots, 0, -2).reshape(slots.shape[1:-1] + (N_SLOTS * slots.shape[-1],))


def _small_to_slots(name, full):
    if name == "pool_w":
        nb, ng, gc, _ = full.shape
        return jnp.moveaxis(full.reshape(nb, ng, N_SLOTS, gc // N_SLOTS, gc), 2, 0)
    w = full.shape[-1] // N_SLOTS
    return jnp.moveaxis(full.reshape(full.shape[:-1] + (N_SLOTS, w)), -2, 0)


W_NAMES = ("norm_mix", "norm_ffn", "norm_ple", "conv_w_in", "conv_b_in", "conv_w_dw", "conv_b_dw", "conv_ln_g", "conv_ln_b", "conv_w_out",
           "conv_b_out", "pool_w", "pool_scale", "attn_w_qkv", "attn_q_norm", "attn_k_norm", "attn_sinks", "attn_w_o", "rel_bias",
           "ffn_w_gate", "ffn_w_up", "ffn_w_down", "ple_w_proj", "ple_w_gate", "ple_b_gate")


def _step(x, p, target, w, m, v):
    T, D = x.shape[1], x.shape[2]
    L = p.shape[0]
    NA, NC = w["conv_w_in"].shape[0], w["attn_w_qkv"].shape[0]
    xi, yi, ci = lax.axis_index("x"), lax.axis_index("y"), lax.axis_index("c")
    c_arr = jnp.reshape(ci, (1,)).astype(jnp.int32)
    s_arr = jnp.reshape(2 * xi + yi, (1,)).astype(jnp.int32)

    ws = w["ffn_w_gate"].shape[2]
    wq = D // N_SLOTS
    sm_pack, sm_offs = _pack_rows([w[k] for k in SMALL_SHARDED], wq)
    shards = [
        jnp.stack([w["ffn_w_gate"], w["ffn_w_up"]], axis=1).reshape(2 * L, D, ws).astype(BF16),
        w["ffn_w_down"].astype(BF16)[:, None],
        w["conv_w_in"].astype(BF16),
        jnp.concatenate([w["conv_w_out"], w["attn_w_o"], w["ple_w_gate"]], axis=0).astype(BF16)[:, None],
        w["attn_w_qkv"].astype(BF16),
        sm_pack[None, None],
    ]
    kinds = ["col", "row", "col", "row", "col", "row"]
    full = _gather(shards, [_place_own(f"place_own_{g}", a, k, s_arr) for g, (a, k) in enumerate(zip(shards, kinds))], kinds)
    wts = {"gu": full[0], "down": full[1].reshape(L, -1, D), "cin": full[2], "sq": full[3].reshape(NA + NC + L, D, D), "qkv": full[4]}
    small = {k: w[k] for k in SMALL_REPLICATED}
    for k, (at, n) in zip(SMALL_SHARDED, sm_offs):
        small[k] = _small_to_full(k, full[5][0, :, at:at + n].reshape((N_SLOTS,) + w[k].shape))

    loss, dx, big, gsmall = _local_step(x[0], p[:, 0], target[0], wts, small)

    rep_parts = [gsmall[k] for k in SMALL_REPLICATED] + [loss]
    flat = jnp.concatenate([a.reshape(-1) for a in rep_parts])
    n_flat = flat.shape[0]
    m_cols = -(-n_flat // (8 * LANES)) * LANES
    flat = jnp.concatenate([flat, jnp.zeros((8 * m_cols - n_flat,), F32)]).reshape(8, m_cols)
    red = _allreduce_small(flat).reshape(-1)
    grads, at = {}, 0
    for k in SMALL_REPLICATED:
        grads[k] = red[at:at + w[k].size].reshape(w[k].shape)
        at += w[k].size
    loss_out = red[at]

    slots = {k: _small_to_slots(k, gsmall[k]) for k in SMALL_SHARDED}
    gsm = jnp.stack([_pack_rows([slots[k][s] for k in SMALL_SHARDED], wq)[0] for s in range(N_SLOTS)], axis=0)
    local = [big["gu"], big["down"].reshape(L, N_SLOTS, -1, D), big["cin"], big["sq"].reshape(NA + NC + L, N_SLOTS, -1, D), big["qkv"],
             gsm[None]]
    theirs = _pair_send(local)
    psums = []
    for g, (a, t) in enumerate(zip(local, theirs)):
        if kinds[g] == "col":
            psums.append(_add_half(f"pair_add_{g}", a, t, c_arr))
        else:
            n4 = a.shape[0] * N_SLOTS
            psums.append(_add_half(f"pair_add_{g}", a.reshape(n4, a.shape[2], a.shape[3]), t.reshape(n4, t.shape[2], t.shape[3]), c_arr).reshape(t.shape))
    got = _ici_exchange(psums, kinds)
    halves = []
    for g, (ps, rc) in enumerate(zip(psums, got)):
        lead = ps.shape[0]
        if kinds[g] == "col":
            halves.append(_sum4(f"slot_sum_{g}", ps, rc.reshape(3 * lead, rc.shape[2], rc.shape[3]), s_arr, lead, "col"))
        else:
            halves.append(_sum4(f"slot_sum_{g}", ps.reshape(lead * N_SLOTS, ps.shape[2], ps.shape[3]), rc.reshape(3 * lead, rc.shape[3], rc.shape[4]),
                                s_arr, lead, "row"))
    first = ci == 0
    gsh = [jnp.concatenate([jnp.where(first, a, b), jnp.where(first, b, a)], axis=1) for a, b in zip(halves, _pair_swap(halves))]
    gu_g = gsh[0].reshape(L, 2, D, ws)
    grads["ffn_w_gate"], grads["ffn_w_up"] = gu_g[:, 0], gu_g[:, 1]
    grads["ffn_w_down"] = gsh[1]
    grads["conv_w_in"] = gsh[2]
    grads["conv_w_out"], grads["attn_w_o"], grads["ple_w_gate"] = gsh[3][:NA], gsh[3][NA:NA + NC], gsh[3][NA + NC:]
    grads["attn_w_qkv"] = gsh[4]
    for k, (at, n) in zip(SMALL_SHARDED, sm_offs):
        grads[k] = gsh[5][0, at:at + n].reshape(w[k].shape)

    outs_d, outs_m, outs_v = [], [], []
    for k in W_NAMES:
        d_, m_, v_ = _adamw(f"adamw_{k}", w[k], grads[k], m[k], v[k])
        outs_d.append(d_)
        outs_m.append(m_)
        outs_v.append(v_)
    return (loss_out, dx[None], *[grads[k] for k in W_NAMES], *outs_d, *outs_m, *outs_v)


def kernel(x, p, norm_mix, norm_ffn, norm_ple, conv_w_in, conv_b_in, conv_w_dw, conv_b_dw, conv_ln_g, conv_ln_b, conv_w_out, conv_b_out, pool_w, pool_scale, attn_w_qkv, attn_q_norm, attn_k_norm, attn_sinks, attn_w_o, rel_bias, ffn_w_gate, ffn_w_up, ffn_w_down, ple_w_proj, ple_w_gate, ple_b_gate, loss_target, m_norm_mix, m_norm_ffn, m_norm_ple, m_conv_w_in, m_conv_b_in, m_conv_w_dw, m_conv_b_dw, m_conv_ln_g, m_conv_ln_b, m_conv_w_out, m_conv_b_out, m_pool_w, m_pool_scale, m_attn_w_qkv, m_attn_q_norm, m_attn_k_norm, m_attn_sinks, m_attn_w_o, m_rel_bias, m_ffn_w_gate, m_ffn_w_up, m_ffn_w_down, m_ple_w_proj, m_ple_w_gate, m_ple_b_gate, v_norm_mix, v_norm_ffn, v_norm_ple, v_conv_w_in, v_conv_b_in, v_conv_w_dw, v_conv_b_dw, v_conv_ln_g, v_conv_ln_b, v_conv_w_out, v_conv_b_out, v_pool_w, v_pool_scale, v_attn_w_qkv, v_attn_q_norm, v_attn_k_norm, v_attn_sinks, v_attn_w_o, v_rel_bias, v_ffn_w_gate, v_ffn_w_up, v_ffn_w_down, v_ple_w_proj, v_ple_w_gate, v_ple_b_gate):
    ws_ = (norm_mix, norm_ffn, norm_ple, conv_w_in, conv_b_in, conv_w_dw, conv_b_dw, conv_ln_g, conv_ln_b, conv_w_out, conv_b_out, pool_w, pool_scale, attn_w_qkv, attn_q_norm, attn_k_norm, attn_sinks, attn_w_o, rel_bias, ffn_w_gate, ffn_w_up, ffn_w_down, ple_w_proj, ple_w_gate, ple_b_gate)
    ms_ = (m_norm_mix, m_norm_ffn, m_norm_ple, m_conv_w_in, m_conv_b_in, m_conv_w_dw, m_conv_b_dw, m_conv_ln_g, m_conv_ln_b, m_conv_w_out, m_conv_b_out, m_pool_w, m_pool_scale, m_attn_w_qkv, m_attn_q_norm, m_attn_k_norm, m_attn_sinks, m_attn_w_o, m_rel_bias, m_ffn_w_gate, m_ffn_w_up, m_ffn_w_down, m_ple_w_proj, m_ple_w_gate, m_ple_b_gate)
    vs_ = (v_norm_mix, v_norm_ffn, v_norm_ple, v_conv_w_in, v_conv_b_in, v_conv_w_dw, v_conv_b_dw, v_conv_ln_g, v_conv_ln_b, v_conv_w_out, v_conv_b_out, v_pool_w, v_pool_scale, v_attn_w_qkv, v_attn_q_norm, v_attn_k_norm, v_attn_sinks, v_attn_w_o, v_rel_bias, v_ffn_w_gate, v_ffn_w_up, v_ffn_w_down, v_ple_w_proj, v_ple_w_gate, v_ple_b_gate)
    return _step(x, p, loss_target, dict(zip(W_NAMES, ws_)), dict(zip(W_NAMES, ms_)), dict(zip(W_NAMES, vs_)))
```

```python
import functools
import math

import jax
import jax.numpy as jnp
import numpy as np
from jax import lax
from jax.experimental import pallas as pl
from jax.experimental.pallas import tpu as pltpu

F32 = jnp.float32
BF16 = jnp.bfloat16
MESH = pl.DeviceIdType.MESH

CHUNK = 64
CONV_WIDTH = 31
POOL_WINDOWS = (2, 4, 8, 16)
HEAD_DIM = 64
WINDOW_CHUNKS = 2
QBLOCK = 128
NUM_BUCKETS = 32
REL_MAX_DIST = 128
EPS = 1e-6
NEG_INF = -1e30
ADAM_LR, ADAM_B1, ADAM_B2, ADAM_EPS, ADAM_WD, ADAM_STEP = 0.001, 0.9, 0.999, 1e-08, 0.01, 10
N_SLOTS = 4
LANES = 128
VMEM_LIMIT_BYTES = 56 * 1024 * 1024


def _cparams(sem):
    return pltpu.CompilerParams(dimension_semantics=sem, vmem_limit_bytes=VMEM_LIMIT_BYTES)


def _tile(n, pref, mult=LANES):
    if n <= pref:
        return n
    t = (pref // mult) * mult
    while t >= mult:
        if n % t == 0:
            return t
        t -= mult
    return n


def _sig(z):
    return 1.0 / (1.0 + jnp.exp(-z))


def _op(arr, lead=None, ro=0, co=0, fn=None):
    return (arr, lead, ro, co, fn)


_DOT_DIMS = {"nn": (((1,), (0,)), ((), ())), "nt": (((1,), (1,)), ((), ())), "tn": (((0,), (0,)), ((), ()))}


def _mm(name, mode, dims, tiles, terms, n_acc, epilogue, outs, extras=()):
    M, N, K = dims
    tm, tn, tk = tiles
    assert M % tm == 0 and N % tn == 0 and K % tk == 0, (name, dims, tiles)
    nk = K // tk
    a_tile = (tk, tm) if mode == "tn" else (tm, tk)
    b_tile = (tn, tk) if mode == "nt" else (tk, tn)
    a_fn = (lambda i, j, k: (k, i)) if mode == "tn" else (lambda i, j, k: (i, k))
    b_fn = (lambda i, j, k: (j, k)) if mode == "nt" else (lambda i, j, k: (k, j))
    dn = _DOT_DIMS[mode]

    operands, specs, seen = [], [], {}

    def add(op, tshape, default_fn):
        arr, lead, ro, co, fn = op
        fn = fn or default_fn
        key = (id(arr), lead, ro, co, id(fn) if op[4] is not None else None, tshape)
        if key in seen:
            return seen[key]

        def imap(i, j, k, fn=fn, lead=lead, ro=ro, co=co):
            r, c = fn(i, j, k)
            return (r + ro, c + co) if lead is None else (lead, r + ro, c + co)

        operands.append(arr)
        specs.append(pl.BlockSpec(tshape if lead is None else (None,) + tshape, imap))
        seen[key] = len(operands) - 1
        return seen[key]

    term_idx = [(add(a, a_tile, a_fn), add(b, b_tile, b_fn), acc) for a, b, acc in terms]
    extra_idx = []
    for arr, kind, lead, co in extras:
        if kind == "tile":
            extra_idx.append(add(_op(arr, lead, 0, co), (tm, tn), lambda i, j, k: (i, j)))
        elif kind == "row":
            arr3 = arr.reshape(arr.shape[0], 1, arr.shape[1])
            extra_idx.append(add(_op(arr3, 0 if lead is None else lead, 0, co), (1, tn), lambda i, j, k: (0, j)))
        else:
            extra_idx.append(add(_op(arr, None, 0, 0), (tm, 1), lambda i, j, k: (i, 0)))
    n_in = len(operands)
    out_shapes, out_specs, aliases = [], [], {}
    for oi, (shape, dtype, lead, alias) in enumerate(outs):
        out_shapes.append(jax.ShapeDtypeStruct(shape, dtype))
        if lead is None:
            out_specs.append(pl.BlockSpec((tm, tn), lambda i, j, k: (i, j)))
        else:
            out_specs.append(pl.BlockSpec((None, tm, tn), lambda i, j, k, lead=lead: (lead, i, j)))
        if alias is not None:
            operands.append(alias)
            specs.append(pl.BlockSpec(memory_space=pl.ANY))
            aliases[len(operands) - 1] = oi
    n_all_in = len(operands)
    n_out = len(outs)

    def body(*refs):
        ins = refs[:n_in]
        o_refs = refs[n_all_in:n_all_in + n_out]
        accs = refs[n_all_in + n_out:]

        def dots():
            sums = [None] * n_acc
            for ai, bi, acc_i in term_idx:
                a = ins[ai][...]
                b = ins[bi][...]
                if a.dtype != BF16:
                    a = a.astype(BF16)
                if b.dtype != BF16:
                    b = b.astype(BF16)
                d = lax.dot_general(a, b, dn, preferred_element_type=F32)
                sums[acc_i] = d if sums[acc_i] is None else sums[acc_i] + d
            return sums

        def finish(vals):
            res = epilogue(vals, [ins[e][...] for e in extra_idx])
            for o, r in zip(o_refs, res):
                o[...] = r.astype(o.dtype)

        if nk == 1:
            finish(dots())
            return
        k = pl.program_id(2)

        @pl.when(k == 0)
        def _():
            for acc, d in zip(accs, dots()):
                acc[...] = d

        if nk > 2:
            @pl.when((k > 0) & (k < nk - 1))
            def _():
                for acc, d in zip(accs, dots()):
                    acc[...] += d

        @pl.when(k == nk - 1)
        def _():
            finish([acc[...] + d for acc, d in zip(accs, dots())])

    res = pl.pallas_call(
        body, name=name, grid=(M // tm, N // tn, nk), in_specs=specs, out_specs=out_specs, out_shape=out_shapes,
        scratch_shapes=[pltpu.VMEM((tm, tn), F32) for _ in range(n_acc if nk > 1 else 0)], input_output_aliases=aliases,
        compiler_params=_cparams(("parallel", "parallel", "arbitrary")),
    )(*operands)
    return res


def _rowk(name, T, tm, ins, outs, body, scratch=()):
    assert T % tm == 0, (name, T, tm)
    n = T // tm
    specs = []
    for arr, kind in ins:
        w = arr.shape[-1]
        if kind == "tile":
            specs.append(pl.BlockSpec((tm, w), lambda i: (i, 0)))
        elif kind == "prev":
            specs.append(pl.BlockSpec((tm, w), lambda i: (jnp.maximum(i - 1, 0), 0)))
        elif kind == "next":
            specs.append(pl.BlockSpec((tm, w), lambda i, n=n: (jnp.minimum(i + 1, n - 1), 0)))
        else:
            specs.append(pl.BlockSpec(arr.shape, lambda i, nd=arr.ndim: (0,) * nd))
    out_shapes, out_specs = [], []
    for shape, dtype, kind in outs:
        out_shapes.append(jax.ShapeDtypeStruct(shape, dtype))
        if kind == "tile":
            out_specs.append(pl.BlockSpec((tm, shape[-1]), lambda i: (i, 0)))
        else:
            out_specs.append(pl.BlockSpec(shape, lambda i, nd=len(shape): (0,) * nd))
    n_in, n_out = len(ins), len(outs)

    def kbody(*refs):
        body(pl.program_id(0), n, refs[:n_in], refs[n_in:n_in + n_out], refs[n_in + n_out:])

    return pl.pallas_call(
        kbody, name=name, grid=(n,), in_specs=specs, out_specs=out_specs, out_shape=out_shapes,
        scratch_shapes=list(scratch), compiler_params=_cparams(("arbitrary",)),
    )(*[a for a, _ in ins])


def _accum(ref, i, val):
    @pl.when(i == 0)
    def _():
        ref[...] = val

    @pl.when(i > 0)
    def _():
        ref[...] += val


def _colsum(v):
    return jnp.sum(v, axis=0, keepdims=True)


def _rms_r(x):
    return lax.rsqrt(jnp.mean(x * x, axis=-1, keepdims=True) + EPS)


def _rms_fwd(name, x, g, tm):
    T, D = x.shape

    def body(i, n, ins, outs, scr):
        xv = ins[0][...]
        outs[0][...] = (xv * _rms_r(xv) * ins[1][...]).astype(BF16)

    return _rowk(name, T, tm, [(x, "tile"), (g, "full")], [((T, D), BF16, "tile")], body)[0]


def _rms_bwd(name, dres, x, g, dh, tm, want_bf16=False, want_colsum=False):
    T, D = x.shape

    def body(i, n, ins, outs, scr):
        xv = ins[1][...]
        gv = ins[2][...]
        dhv = ins[3][...].astype(F32)
        r = _rms_r(xv)
        xh = xv * r
        dhg = dhv * gv
        dx = ins[0][...] + r * (dhg - xh * jnp.mean(dhg * xh, axis=-1, keepdims=True))
        outs[0][...] = dx
        _accum(outs[1], i, _colsum(dhv * xh))
        o = 2
        if want_bf16:
            outs[o][...] = dx.astype(BF16)
            o += 1
        if want_colsum:
            _accum(outs[o], i, _colsum(dx))

    outs = [((T, D), F32, "tile"), ((1, D), F32, "acc")]
    if want_bf16:
        outs.append(((T, D), BF16, "tile"))
    if want_colsum:
        outs.append(((1, D), F32, "acc"))
    return _rowk(name, T, tm, [(dres, "tile"), (x, "tile"), (g, "full"), (dh, "tile")], outs, body)


def _loss_head(y, target, tm):
    T, D = y.shape

    def body(i, n, ins, outs, scr):
        d = ins[0][...] - ins[1][...]
        outs[0][...] = d * (1.0 / D)
        _accum(outs[1], i, jnp.sum(_colsum(d * d), axis=1, keepdims=True) * (0.5 / D))

    return _rowk("loss_head", T, tm, [(y, "tile"), (target, "tile")], [((T, D), F32, "tile"), ((1, 1), F32, "acc")], body)


def _ple_bwd_elem(name, dx, gt, q, tm):
    T, D = dx.shape

    def body(i, n, ins, outs, scr):
        d = ins[0][...]
        g = ins[1][...]
        dz = d * ins[2][...] * g * (1.0 - g)
        outs[0][...] = (d * g).astype(BF16)
        outs[1][...] = dz.astype(BF16)
        _accum(outs[2], i, _colsum(dz))

    return _rowk(name, T, tm, [(dx, "tile"), (gt, "tile"), (q, "tile")],
                 [((T, D), BF16, "tile"), ((T, D), BF16, "tile"), ((1, D), F32, "acc")], body)


CONV_ROWS = 128
SUBLANES = 8
_PHASE_PAD = 32


def _phase_scratch(tm):
    return pltpu.VMEM((SUBLANES, tm + _PHASE_PAD, LANES), F32)


def _phase_copies(buf, shf, base, l0, tm):
    n = tm + _PHASE_PAD - SUBLANES
    for r in range(1, SUBLANES):
        shf[r, pl.ds(0, n), :] = buf[pl.ds(base + r, n), pl.ds(l0, LANES)]


def _phase_window(buf, shf, base, q, row0, rows, l0):
    r = q % SUBLANES
    a = q - r
    if r == 0:
        return buf[pl.ds(base + a + row0, rows), pl.ds(l0, LANES)]
    return shf[r, pl.ds(a + row0, rows), :]


def _dwconv_fwd(name, u1, w_dw, b_dw, tm):
    T, D = u1.shape
    rc_n = tm // CONV_ROWS if tm >= CONV_ROWS else 1
    rows = min(CONV_ROWS, tm)
    halo = CONV_WIDTH - 1

    base = tm - _PHASE_PAD

    def body(i, n, ins, outs, scr):
        buf, shf = scr
        buf[pl.ds(0, tm), :] = jnp.where(i > 0, ins[0][...], 0.0)
        buf[pl.ds(tm, tm), :] = ins[1][...]
        w_ref, b_ref, o_ref = ins[2], ins[3], outs[0]

        def chunk(lc, carry):
            l0 = pl.multiple_of(lc * LANES, LANES)
            _phase_copies(buf, shf, base, l0, tm)
            for rc in range(rc_n):
                acc = jnp.zeros((rows, LANES), F32) + b_ref[:, pl.ds(l0, LANES)]
                for k in range(CONV_WIDTH):
                    acc = acc + _phase_window(buf, shf, base, k + _PHASE_PAD - halo, rc * rows, rows, l0) * w_ref[pl.ds(k, 1), pl.ds(l0, LANES)]
                o_ref[pl.ds(rc * rows, rows), pl.ds(l0, LANES)] = acc
            return carry

        lax.fori_loop(0, D // LANES, chunk, 0)

    return _rowk(name, T, tm, [(u1, "prev"), (u1, "tile"), (w_dw, "full"), (b_dw, "full")], [((T, D), F32, "tile")], body,
                 scratch=[pltpu.VMEM((2 * tm, D), F32), _phase_scratch(tm)])[0]


def _ln_silu_fwd(name, u2, g, b, tm):
    T, D = u2.shape

    def body(i, n, ins, outs, scr):
        v = ins[0][...]
        mu = jnp.mean(v, axis=-1, keepdims=True)
        xc = v - mu
        y = xc * lax.rsqrt(jnp.mean(xc * xc, axis=-1, keepdims=True) + EPS) * ins[1][...] + ins[2][...]
        outs[0][...] = (y * _sig(y)).astype(BF16)

    return _rowk(name, T, tm, [(u2, "tile"), (g, "full"), (b, "full")], [((T, D), BF16, "tile")], body)[0]


def _ln_silu_bwd(name, du4, u2, g, b, tm):
    T, D = u2.shape

    def body(i, n, ins, outs, scr):
        v = ins[1][...]
        gv = ins[2][...]
        mu = jnp.mean(v, axis=-1, keepdims=True)
        xc = v - mu
        r = lax.rsqrt(jnp.mean(xc * xc, axis=-1, keepdims=True) + EPS)
        xh = xc * r
        y = xh * gv + ins[3][...]
        s = _sig(y)
        dy = ins[0][...] * (s * (1.0 + y * (1.0 - s)))
        dyg = dy * gv
        du2 = r * (dyg - jnp.mean(dyg, axis=-1, keepdims=True) - xh * jnp.mean(dyg * xh, axis=-1, keepdims=True))
        outs[0][...] = du2
        _accum(outs[1], i, _colsum(dy * xh))
        _accum(outs[2], i, _colsum(dy))
        _accum(outs[3], i, _colsum(du2))

    return _rowk(name, T, tm, [(du4, "tile"), (u2, "tile"), (g, "full"), (b, "full")],
                 [((T, D), F32, "tile"), ((1, D), F32, "acc"), ((1, D), F32, "acc"), ((1, D), F32, "acc")], body)


def _dwconv_glu_bwd(name, du2, u1, a_, gate, w_dw, tm):
    T, D = u1.shape
    rc_n = tm // CONV_ROWS if tm >= CONV_ROWS else 1
    rows = min(CONV_ROWS, tm)
    halo = CONV_WIDTH - 1

    base = tm - _PHASE_PAD

    def body(i, n, ins, outs, scr):
        bu, bd, shu, shd = scr
        bd[pl.ds(0, tm), :] = ins[0][...]
        bd[pl.ds(tm, tm), :] = jnp.where(i < n - 1, ins[1][...], 0.0)
        bu[pl.ds(0, tm), :] = jnp.where(i > 0, ins[2][...], 0.0)
        bu[pl.ds(tm, tm), :] = ins[3][...]
        a_ref, g_ref, w_ref = ins[4], ins[5], ins[6]
        dag_ref, dw_ref, db_ref = outs

        @pl.when(i == 0)
        def _():
            dw_ref[...] = jnp.zeros_like(dw_ref)
            db_ref[...] = jnp.zeros_like(db_ref)

        def chunk(lc, carry):
            l0 = pl.multiple_of(lc * LANES, LANES)
            l1 = pl.multiple_of(D + lc * LANES, LANES)
            _phase_copies(bd, shd, 0, l0, tm)
            _phase_copies(bu, shu, base, l0, tm)
            for rc in range(rc_n):
                r0 = rc * rows
                d_here = bd[pl.ds(r0, rows), pl.ds(l0, LANES)]
                acc = jnp.zeros((rows, LANES), F32)
                for k in range(CONV_WIDTH):
                    wk = w_ref[pl.ds(k, 1), pl.ds(l0, LANES)]
                    acc = acc + _phase_window(bd, shd, 0, halo - k, r0, rows, l0) * wk
                    dw_ref[pl.ds(k, 1), pl.ds(l0, LANES)] += _colsum(d_here * _phase_window(bu, shu, base, k + _PHASE_PAD - halo, r0, rows, l0))
                av = a_ref[pl.ds(r0, rows), pl.ds(l0, LANES)].astype(F32)
                sg = _sig(g_ref[pl.ds(r0, rows), pl.ds(l0, LANES)].astype(F32))
                da = acc * sg
                dg = acc * av * sg * (1.0 - sg)
                dag_ref[pl.ds(r0, rows), pl.ds(l0, LANES)] = da.astype(BF16)
                dag_ref[pl.ds(r0, rows), pl.ds(l1, LANES)] = dg.astype(BF16)
                db_ref[:, pl.ds(l0, LANES)] += _colsum(da)
                db_ref[:, pl.ds(l1, LANES)] += _colsum(dg)
            return carry

        lax.fori_loop(0, D // LANES, chunk, 0)

    return _rowk(name, T, tm, [(du2, "tile"), (du2, "next"), (u1, "prev"), (u1, "tile"), (a_, "tile"), (gate, "tile"), (w_dw, "full")],
                 [((T, 2 * D), BF16, "tile"), ((CONV_WIDTH, D), F32, "acc"), ((1, 2 * D), F32, "acc")], body,
                 scratch=[pltpu.VMEM((2 * tm, D), F32), pltpu.VMEM((2 * tm, D), F32), _phase_scratch(tm), _phase_scratch(tm)])


def _row_index(i, tm, r0, rows):
    return (i * tm + r0 + lax.broadcasted_iota(jnp.int32, (rows, 1), 0)).astype(F32)


def _pool_fwd(name, x, g, tm):
    T, D = x.shape
    gc = D // len(POOL_WINDOWS)
    rows = min(CONV_ROWS, tm)
    rc_n = tm // rows

    def body(i, n, ins, outs, scr):
        buf = scr[0]
        xp = ins[0][...]
        buf[pl.ds(0, tm), :] = jnp.where(i > 0, xp * _rms_r(xp) * ins[2][...], 0.0)
        xc = ins[1][...]
        buf[pl.ds(tm, tm), :] = xc * _rms_r(xc) * ins[2][...]
        o_ref = outs[0]
        for gi, w in enumerate(POOL_WINDOWS):
            def chunk(lc, carry, gi=gi, w=w):
                l0 = pl.multiple_of(gi * gc + lc * LANES, LANES)
                for rc in range(rc_n):
                    r0 = rc * rows
                    acc = buf[pl.ds(tm + r0, rows), pl.ds(l0, LANES)]
                    here = acc
                    for d in range(1, w):
                        acc = acc + buf[pl.ds(tm + r0 - d, rows), pl.ds(l0, LANES)]
                    cnt = jnp.minimum(_row_index(i, tm, r0, rows) + 1.0, float(w))
                    o_ref[pl.ds(r0, rows), pl.ds(l0, LANES)] = (acc / cnt - here).astype(BF16)
                return carry

            lax.fori_loop(0, gc // LANES, chunk, 0)

    return _rowk(name, T, tm, [(x, "prev"), (x, "tile"), (g, "full")], [((T, D), BF16, "tile")], body,
                 scratch=[pltpu.VMEM((2 * tm, D), F32)])[0]


def _pool_bwd(name, dmix, tm):
    T, D = dmix.shape
    gc = D // len(POOL_WINDOWS)
    rows = min(CONV_ROWS, tm)
    rc_n = tm // rows

    def body(i, n, ins, outs, scr):
        buf = scr[0]
        o_ref = outs[0]
        t_here = (i * tm + lax.broadcasted_iota(jnp.int32, (tm, 1), 0)).astype(F32) + 1.0
        for gi, w in enumerate(POOL_WINDOWS):
            cols = pl.ds(gi * gc, gc)
            buf[pl.ds(0, tm), cols] = ins[0][:, cols] / jnp.minimum(t_here, float(w))
            buf[pl.ds(tm, tm), cols] = jnp.where(i < n - 1, ins[1][:, cols] / float(w), 0.0)

            def chunk(lc, carry, gi=gi, w=w):
                l0 = pl.multiple_of(gi * gc + lc * LANES, LANES)
                for rc in range(rc_n):
                    r0 = rc * rows
                    acc = -ins[0][pl.ds(r0, rows), pl.ds(l0, LANES)]
                    for d in range(w):
                        acc = acc + buf[pl.ds(r0 + d, rows), pl.ds(l0, LANES)]
                    o_ref[pl.ds(r0, rows), pl.ds(l0, LANES)] = acc
                return carry

            lax.fori_loop(0, gc // LANES, chunk, 0)

    return _rowk(name, T, tm, [(dmix, "tile"), (dmix, "next")], [((T, D), F32, "tile")], body,
                 scratch=[pltpu.VMEM((2 * tm, D), F32)])[0]


def _pool_scale_bwd(name, dy, y0, scale, tm):
    T, D = dy.shape

    def body(i, n, ins, outs, scr):
        d = ins[0][...]
        outs[0][...] = (d * ins[2][...]).astype(BF16)
        _accum(outs[1], i, _colsum(d * ins[1][...]))

    return _rowk(name, T, tm, [(dy, "tile"), (y0, "tile"), (scale, "full")], [((T, D), BF16, "tile"), ((1, D), F32, "acc")], body)


def _t5_bucket_np():
    i = np.arange(QBLOCK)[:, None]
    j = np.arange(2 * QBLOCK)[None, :]
    rel = j - QBLOCK - i
    nb = NUM_BUCKETS // 2
    n = -rel
    ret = np.where(n < 0, nb, 0)
    n = np.abs(n)
    max_exact = nb // 2
    nf = np.maximum(n, 1).astype(np.float32)
    large = max_exact + (np.log(nf / np.float32(max_exact)) / np.float32(math.log(REL_MAX_DIST / max_exact))
                         * np.float32(nb - max_exact)).astype(np.int32)
    large = np.minimum(large, nb - 1)
    return (ret + np.where(n < max_exact, n, large)).astype(np.int32)


def _bias_fwd(rel_bias, bucket):
    nb, nh = rel_bias.shape

    def body(rb_ref, bk_ref, o_ref):
        h = pl.program_id(0)
        bk = bk_ref[...]
        acc = jnp.zeros(bk.shape, F32)
        for b in range(nb):
            acc = jnp.where(bk == b, rb_ref[b, h], acc)
        o_ref[...] = acc

    return pl.pallas_call(
        body, name="attn_bias_fwd", grid=(nh,),
        in_specs=[pl.BlockSpec(memory_space=pltpu.SMEM), pl.BlockSpec(bucket.shape, lambda h: (0, 0))],
        out_specs=pl.BlockSpec((None,) + bucket.shape, lambda h: (h, 0, 0)),
        out_shape=jax.ShapeDtypeStruct((nh,) + bucket.shape, F32), compiler_params=_cparams(("arbitrary",)),
    )(rel_bias, bucket)


def _bias_bwd(dbias, bucket, nb):
    nh = dbias.shape[0]

    def body(db_ref, bk_ref, o_ref):
        h = pl.program_id(0)
        bk = bk_ref[...]
        d = db_ref[...]
        for b in range(nb):
            o_ref[h, b] = jnp.sum(jnp.where(bk == b, d, 0.0))

    return pl.pallas_call(
        body, name="attn_bias_bwd", grid=(nh,),
        in_specs=[pl.BlockSpec((None,) + bucket.shape, lambda h: (h, 0, 0)), pl.BlockSpec(bucket.shape, lambda h: (0, 0))],
        out_specs=pl.BlockSpec(memory_space=pltpu.SMEM),
        out_shape=jax.ShapeDtypeStruct((nh, nb), F32), compiler_params=_cparams(("arbitrary",)),
    )(dbias, bucket)


def _head_norm_fwd(name, q2, g, tm):
    R, W = q2.shape

    def body(i, n, ins, outs, scr):
        v = ins[0][...]
        outs[0][...] = (v * _rms_r(v) * ins[1][...]).astype(BF16)

    return _rowk(name, R, tm, [(q2, "tile"), (g, "full")], [((R, W), BF16, "tile")], body)[0]


def _head_norm_bwd(name, dqn, q2, g, tm):
    R, W = q2.shape

    def body(i, n, ins, outs, scr):
        v = ins[1][...]
        d = ins[0][...]
        r = _rms_r(v)
        xh = v * r
        dg = d * ins[2][...]
        outs[0][...] = (r * (dg - xh * jnp.mean(dg * xh, axis=-1, keepdims=True))).astype(BF16)
        _accum(outs[1], i, _colsum(d * xh))

    return _rowk(name, R, tm, [(dqn, "tile"), (q2, "tile"), (g, "full")], [((R, W), BF16, "tile"), ((1, W), F32, "acc")], body)


def _band_merge(name, own, prev, k3=None, g=None):
    H, T, W = own.shape
    nblk = T // QBLOCK

    def body(*refs):
        o_ref, p_ref = refs[0], refs[1]
        out_ref = refs[4] if k3 is not None else refs[2]

        def blk(m, carry):
            r0 = pl.multiple_of(m * QBLOCK, QBLOCK)
            rn = pl.multiple_of(jnp.minimum(m + 1, nblk - 1) * QBLOCK, QBLOCK)
            d = o_ref[pl.ds(r0, QBLOCK), :] + jnp.where(m < nblk - 1, p_ref[pl.ds(rn, QBLOCK), :], 0.0)
            if k3 is None:
                out_ref[pl.ds(r0, QBLOCK), :] = d.astype(BF16)
                return carry
            v = refs[2][pl.ds(r0, QBLOCK), :]
            r = _rms_r(v)
            xh = v * r
            dg = d * refs[3][...]
            out_ref[pl.ds(r0, QBLOCK), :] = (r * (dg - xh * jnp.mean(dg * xh, axis=-1, keepdims=True))).astype(BF16)
            return carry + _colsum(d * xh)

        tot = lax.fori_loop(0, nblk, blk, jnp.zeros((1, W), F32))
        if k3 is not None:
            _accum(refs[5], pl.program_id(0), tot)

    head = pl.BlockSpec((None, T, W), lambda h: (h, 0, 0))
    ins, in_specs = [own, prev], [head, head]
    out_shape, out_specs = [jax.ShapeDtypeStruct((H, T, W), BF16)], [head]
    if k3 is not None:
        ins += [k3, g]
        in_specs += [head, pl.BlockSpec(g.shape, lambda h: (0, 0))]
        out_shape.append(jax.ShapeDtypeStruct((1, W), F32))
        out_specs.append(pl.BlockSpec((1, W), lambda h: (0, 0)))
    return pl.pallas_call(body, name=name, grid=(H,), in_specs=in_specs, out_specs=out_specs, out_shape=out_shape,
                          compiler_params=_cparams(("arbitrary",)))(*ins)


def _attn_logits(q, kb, bias, sink, n):
    rows = q.shape[0]
    s = lax.dot_general(q, kb, _DOT_DIMS["nt"], preferred_element_type=F32) * (HEAD_DIM ** -0.5) + bias
    qc = (lax.broadcasted_iota(jnp.int32, (rows, 1), 0) % QBLOCK) // CHUNK
    j = lax.broadcasted_iota(jnp.int32, (1, 2 * QBLOCK), 1)
    kc = j // CHUNK - QBLOCK // CHUNK
    ok = (kc <= qc) & (kc >= qc - WINDOW_CHUNKS) & ((n > 0) | (j >= QBLOCK))
    s = jnp.where(ok, s, NEG_INF)
    m = jnp.maximum(jnp.max(s, axis=-1, keepdims=True), sink)
    e = jnp.exp(s - m)
    es = jnp.exp(sink - m)
    den = jnp.sum(e, axis=-1, keepdims=True) + es
    return e / den, es / den


def _heads_per_step(n_kv):
    return 2 if n_kv % 2 == 0 else 1


def _attn_specs(group, hp, rows):
    blk = lambda hn, fn: pl.BlockSpec((hn, QBLOCK, HEAD_DIM), fn)
    cur = lambda h, n: (h, n, 0)
    prv = lambda h, n: (h, jnp.maximum(n - 1, 0), 0)
    bsp = pl.BlockSpec((hp, rows, 2 * QBLOCK), lambda h, n: (h, 0, 0))
    ssp = pl.BlockSpec((hp, rows, 1), lambda h, n: (h, 0, 0))
    return blk(hp * group, cur), blk(hp, prv), blk(hp, cur), bsp, ssp


def _attn_fwd(qn, kn, v, bias, sink_rows, n_kv, group, T):
    nblk = T // QBLOCK
    rows = group * QBLOCK
    hp = _heads_per_step(n_kv)

    def body(q_ref, kp_ref, kc_ref, vp_ref, vc_ref, b_ref, s_ref, o_ref):
        n = pl.program_id(1)
        for hh in range(hp):
            q = q_ref[pl.ds(hh * group, group)].reshape(rows, HEAD_DIM)
            kb = jnp.concatenate([kp_ref[hh], kc_ref[hh]], axis=0)
            vb = jnp.concatenate([vp_ref[hh], vc_ref[hh]], axis=0)
            p, _ = _attn_logits(q, kb, b_ref[hh], s_ref[hh], n)
            o = lax.dot_general(p.astype(BF16), vb, _DOT_DIMS["nn"], preferred_element_type=F32)
            o_ref[pl.ds(hh * group, group)] = o.reshape(group, QBLOCK, HEAD_DIM).astype(BF16)

    qs, kp, kc, bsp, ssp = _attn_specs(group, hp, rows)
    return pl.pallas_call(
        body, name="attn_fwd", grid=(n_kv // hp, nblk), in_specs=[qs, kp, kc, kp, kc, bsp, ssp],
        out_specs=qs, out_shape=jax.ShapeDtypeStruct(qn.shape, BF16), compiler_params=_cparams(("arbitrary", "arbitrary")),
    )(qn, kn, kn, v, v, bias, sink_rows)


def _attn_bwd(qn, kn, v, bias, sink_rows, do, n_kv, group, T):
    nblk = T // QBLOCK
    rows = group * QBLOCK
    scale = HEAD_DIM ** -0.5
    hp = _heads_per_step(n_kv)

    def body(q_ref, kp_ref, kc_ref, vp_ref, vc_ref, b_ref, s_ref, do_ref, dq_ref, dko_ref, dkp_ref, dvo_ref, dvp_ref, db_ref, ds_ref):
        n = pl.program_id(1)
        for hh in range(hp):
            q = q_ref[pl.ds(hh * group, group)].reshape(rows, HEAD_DIM)
            dov = do_ref[pl.ds(hh * group, group)].reshape(rows, HEAD_DIM)
            kb = jnp.concatenate([kp_ref[hh], kc_ref[hh]], axis=0)
            vb = jnp.concatenate([vp_ref[hh], vc_ref[hh]], axis=0)
            p, ps = _attn_logits(q, kb, b_ref[hh], s_ref[hh], n)
            dp = lax.dot_general(dov, vb, _DOT_DIMS["nt"], preferred_element_type=F32)
            delta = jnp.sum(p * dp, axis=-1, keepdims=True)
            dl = p * (dp - delta)
            dlb = dl.astype(BF16)
            dq = lax.dot_general(dlb, kb, _DOT_DIMS["nn"], preferred_element_type=F32) * scale
            dkb = lax.dot_general(dlb, q, _DOT_DIMS["tn"], preferred_element_type=F32) * scale
            dvb = lax.dot_general(p.astype(BF16), dov, _DOT_DIMS["tn"], preferred_element_type=F32)
            dq_ref[pl.ds(hh * group, group)] = dq.reshape(group, QBLOCK, HEAD_DIM)
            dkp_ref[hh] = dkb[:QBLOCK]
            dko_ref[hh] = dkb[QBLOCK:]
            dvp_ref[hh] = dvb[:QBLOCK]
            dvo_ref[hh] = dvb[QBLOCK:]
            dsink = -ps * delta

            @pl.when(n == 0)
            def _(hh=hh, dl=dl, dsink=dsink):
                db_ref[hh] = dl
                ds_ref[hh] = dsink

            @pl.when(n > 0)
            def _(hh=hh, dl=dl, dsink=dsink):
                db_ref[hh] += dl
                ds_ref[hh] += dsink

    qs, kp, kc, bsp, ssp = _attn_specs(group, hp, rows)
    kv_shape = jax.ShapeDtypeStruct(kn.shape, F32)
    return pl.pallas_call(
        body, name="attn_bwd", grid=(n_kv // hp, nblk), in_specs=[qs, kp, kc, kp, kc, bsp, ssp, qs],
        out_specs=[qs, kc, kc, kc, kc, bsp, ssp],
        out_shape=[jax.ShapeDtypeStruct(qn.shape, F32), kv_shape, kv_shape, kv_shape, kv_shape,
                   jax.ShapeDtypeStruct(bias.shape, F32), jax.ShapeDtypeStruct(sink_rows.shape, F32)],
        compiler_params=_cparams(("arbitrary", "arbitrary")),
    )(qn, kn, kn, v, v, bias, sink_rows, do)


def _adamw(name, w, g, m, v):
    shape = w.shape
    w2, g2, m2, v2 = (a.reshape(-1, shape[-1]) for a in (w, g, m, v))
    R, W = w2.shape
    tm = _tile(R, max(8, (1 << 19) // W), 8)
    d1 = 1.0 - ADAM_B1 ** ADAM_STEP
    d2 = 1.0 - ADAM_B2 ** ADAM_STEP

    def body(i, n, ins, outs, scr):
        wv, gv = ins[0][...], ins[1][...]
        mn = ADAM_B1 * ins[2][...] + (1.0 - ADAM_B1) * gv
        vn = ADAM_B2 * ins[3][...] + (1.0 - ADAM_B2) * (gv * gv)
        outs[0][...] = -ADAM_LR * ((mn / d1) / (jnp.sqrt(vn / d2) + ADAM_EPS) + ADAM_WD * wv)
        outs[1][...] = mn
        outs[2][...] = vn

    d, mn, vn = _rowk(name, R, tm, [(w2, "tile"), (g2, "tile"), (m2, "tile"), (v2, "tile")],
                      [((R, W), F32, "tile")] * 3, body)
    return d.reshape(shape), mn.reshape(shape), vn.reshape(shape)


def _first(accs, extras):
    return [accs[0]]


def _local_step(x, p, target, wts, small):
    T, D = x.shape
    L, _, PLE = p.shape
    gu, down, cin, sq, qkv_w = wts["gu"], wts["down"], wts["cin"], wts["sq"], wts["qkv"]
    FF = gu.shape[2]
    NA, NC = cin.shape[0], qkv_w.shape[0]
    QW = qkv_w.shape[2]
    KVD = (QW - D) // 2
    n_heads, n_kv = D // HEAD_DIM, KVD // HEAD_DIM
    group = n_heads // n_kv
    nblk = T // QBLOCK
    GC = D // len(POOL_WINDOWS)

    tr = _tile(T, 256, 8)
    tmm = _tile(T, 1024)
    tD = _tile(D, 1024)
    tDk = _tile(D, 2048)
    tD2 = _tile(D, 512)
    tF = _tile(FF, 512)
    tFk = _tile(FF, 2816)
    tFw = _tile(FF, 1408)
    tP = _tile(PLE, 512)
    tQ = _tile(QW, 768)
    tT = _tile(T, 1024)

    bucket = jnp.asarray(_t5_bucket_np())
    saved = []
    xs = x

    for i in range(L):
        kind, j = i % 3, i // 3
        sv = {"x": xs}
        h1 = _rms_fwd(f"rms_mix_{i}", xs, small["norm_mix"][i:i + 1], tr)
        if kind == 0:
            a_, gate, u1 = _mm(
                f"conv_in_{i}", "nn", (T, D, D), (tmm, tD2, tDk),
                [(_op(h1), _op(cin, j), 0), (_op(h1), _op(cin, j, 0, D // tD2), 1)], 2,
                lambda accs, ex: (lambda a, g: [a, g, a * _sig(g)])(accs[0] + ex[0], accs[1] + ex[1]),
                [((T, D), BF16, None, None), ((T, D), BF16, None, None), ((T, D), F32, None, None)],
                extras=[(small["conv_b_in"], "row", j, 0), (small["conv_b_in"], "row", j, D // tD2)])
            u2 = _dwconv_fwd(f"dwconv_{i}", u1, small["conv_w_dw"][j], small["conv_b_dw"][j:j + 1], tr)
            u4 = _ln_silu_fwd(f"ln_silu_{i}", u2, small["conv_ln_g"][j:j + 1], small["conv_ln_b"][j:j + 1], tr)
            x1, = _mm(f"conv_out_{i}", "nn", (T, D, D), (tmm, tD2, tDk), [(_op(u4), _op(sq, j), 0)], 1,
                      lambda accs, ex: [accs[0] + ex[0] + ex[1]], [((T, D), F32, None, None)],
                      extras=[(small["conv_b_out"], "row", j, 0), (xs, "tile", None, 0)])
            sv.update(h1=h1, a=a_, gate=gate, u1=u1, u2=u2, u4=u4)
        elif kind == 1:
            mix = _pool_fwd(f"pool_{i}", xs, small["norm_mix"][i:i + 1], tr)
            pw = small["pool_w"][j].reshape(len(POOL_WINDOWS) * GC, GC)
            kb = GC // _tile(GC, 512)
            tg = _tile(GC, 512)
            y0, x1 = _mm(f"pool_mm_{i}", "nn", (T, D, GC), (tmm, GC, tg),
                         [(_op(mix, fn=lambda i_, j_, k_, kb=kb: (i_, j_ * kb + k_)), _op(pw, fn=lambda i_, j_, k_, kb=kb: (j_ * kb + k_, 0)), 0)], 1,
                         lambda accs, ex: [accs[0], ex[1] + accs[0] * ex[0]],
                         [((T, D), F32, None, None), ((T, D), F32, None, None)],
                         extras=[(small["pool_scale"][j:j + 1], "row", None, 0), (xs, "tile", None, 0)])
            sv.update(mix=mix, y0=y0, pw=pw)
        else:
            qkv, = _mm(f"qkv_{i}", "nn", (T, QW, D), (tmm, tQ, tDk), [(_op(h1), _op(qkv_w, j), 0)], 1, _first,
                       [((T, QW), F32, None, None)])
            q_hm = qkv[:, :D].reshape(T, n_heads, HEAD_DIM).transpose(1, 0, 2).reshape(n_heads * T, HEAD_DIM)
            k_hm = qkv[:, D:D + KVD].reshape(T, n_kv, HEAD_DIM).transpose(1, 0, 2).reshape(n_kv * T, HEAD_DIM)
            v_hm = qkv[:, D + KVD:].reshape(T, n_kv, HEAD_DIM).transpose(1, 0, 2).astype(BF16)
            th = _tile(T, 2048, 8)
            qn = _head_norm_fwd(f"qnorm_{i}", q_hm, small["attn_q_norm"][j:j + 1], th).reshape(n_heads, T, HEAD_DIM)
            kn = _head_norm_fwd(f"knorm_{i}", k_hm, small["attn_k_norm"][j:j + 1], th).reshape(n_kv, T, HEAD_DIM)
            bias = _bias_fwd(small["rel_bias"], bucket).reshape(n_kv, group * QBLOCK, 2 * QBLOCK)
            sink_rows = jnp.broadcast_to(small["attn_sinks"][j].reshape(n_kv, group, 1, 1), (n_kv, group, QBLOCK, 1)).reshape(n_kv, group * QBLOCK, 1)
            o_hm = _attn_fwd(qn, kn, v_hm, bias, sink_rows, n_kv, group, T)
            o = o_hm.transpose(1, 0, 2).reshape(T, D)
            x1, = _mm(f"attn_o_{i}", "nn", (T, D, D), (tmm, tD2, tDk), [(_op(o), _op(sq, NA + j), 0)], 1,
                      lambda accs, ex: [accs[0] + ex[0]], [((T, D), F32, None, None)], extras=[(xs, "tile", None, 0)])
            sv.update(h1=h1, q_hm=q_hm, k_hm=k_hm, v_hm=v_hm, qn=qn, kn=kn, bias=bias, sink_rows=sink_rows, o=o)
        h2 = _rms_fwd(f"rms_ffn_{i}", x1, small["norm_ffn"][i:i + 1], tr)
        a, b, f = _mm(f"ffn_up_{i}", "nn", (T, FF, D), (tmm, tF, tDk), [(_op(h2), _op(gu, 2 * i), 0), (_op(h2), _op(gu, 2 * i + 1), 1)], 2,
                      lambda accs, ex: [accs[0], accs[1], accs[0] * _sig(accs[0]) * accs[1]],
                      [((T, FF), BF16, None, None)] * 3)
        x2, = _mm(f"ffn_down_{i}", "nn", (T, D, FF), (tmm, tD2, tFk), [(_op(f), _op(down, i), 0)], 1,
                  lambda accs, ex: [accs[0] + ex[0]], [((T, D), F32, None, None)], extras=[(x1, "tile", None, 0)])
        h3 = _rms_fwd(f"rms_ple_{i}", x2, small["norm_ple"][i:i + 1], tr)
        q, = _mm(f"ple_proj_{i}", "nn", (T, D, PLE), (tmm, tD2, tP), [(_op(p, i), _op(small["ple_w_proj"], i), 0)], 1, _first,
                 [((T, D), F32, None, None)])
        gt, x3 = _mm(f"ple_gate_{i}", "nn", (T, D, D), (tmm, tD2, tDk), [(_op(h3), _op(sq, NA + NC + i), 0)], 1,
                     lambda accs, ex: (lambda g_, q_, x_: [g_, x_ + g_ * q_])(_sig(accs[0] + ex[0]), ex[1], ex[2]),
                     [((T, D), F32, None, None), ((T, D), F32, None, None)],
                     extras=[(small["ple_b_gate"], "row", i, 0), (q, "tile", None, 0), (x2, "tile", None, 0)])
        sv.update(x1=x1, h2=h2, a=sv.get("a"), fa=a, fb=b, f=f, x2=x2, h3=h3, q=q, gt=gt)
        saved.append(sv)
        xs = x3

    dx, loss = _loss_head(xs, target, tr)

    g_gu = g_down = g_cin = g_sq = g_qkv = None
    gs = {k: [None] * v.shape[0] for k, v in small.items() if k != "rel_bias"}
    gs["rel_bias"] = None
    gs["ple_w_proj"] = [None] * L

    for i in reversed(range(L)):
        kind, j = i % 3, i // 3
        sv = saved[i]
        dq, dz, dbg = _ple_bwd_elem(f"ple_bwd_{i}", dx, sv["gt"], sv["q"], tr)
        gs["ple_b_gate"][i] = dbg
        gs["ple_w_proj"][i], = _mm(f"d_ple_proj_{i}", "tn", (PLE, D, T), (tP, tD, tT), [(_op(p, i), _op(dq), 0)], 1, _first,
                                   [((PLE, D), F32, None, None)])
        g_sq, = _mm(f"d_ple_gate_{i}", "tn", (D, D, T), (tD, tD, tT), [(_op(sv["h3"]), _op(dz), 0)], 1, _first,
                    [(sq.shape, BF16, NA + NC + i, g_sq)])
        dh3, = _mm(f"dh_ple_{i}", "nt", (T, D, D), (tmm, tD2, tDk), [(_op(dz), _op(sq, NA + NC + i), 0)], 1, _first,
                   [((T, D), F32, None, None)])
        dx2, gs["norm_ple"][i], dx2b = _rms_bwd(f"rms_ple_bwd_{i}", dx, sv["x2"], small["norm_ple"][i:i + 1], dh3, tr, want_bf16=True)
        da, db = _mm(f"d_ffn_act_{i}", "nt", (T, FF, D), (tmm, tF, tDk), [(_op(dx2b), _op(down, i), 0)], 1,
                     lambda accs, ex: (lambda df, a_, b_, s_: [df * b_ * s_ * (1.0 + a_ * (1.0 - s_)), df * a_ * s_])(
                         accs[0], ex[0].astype(F32), ex[1].astype(F32), _sig(ex[0].astype(F32))),
                     [((T, FF), BF16, None, None)] * 2, extras=[(sv["fa"], "tile", None, 0), (sv["fb"], "tile", None, 0)])
        g_down, = _mm(f"d_ffn_down_{i}", "tn", (FF, D, T), (tFw, tD, tT), [(_op(sv["f"]), _op(dx2b), 0)], 1, _first,
                      [(down.shape, BF16, i, g_down)])
        g_gu, = _mm(f"d_ffn_gate_{i}", "tn", (D, FF, T), (tD, tFw, tT), [(_op(sv["h2"]), _op(da), 0)], 1, _first,
                    [(gu.shape, BF16, 2 * i, g_gu)])
        g_gu, = _mm(f"d_ffn_up_{i}", "tn", (D, FF, T), (tD, tFw, tT), [(_op(sv["h2"]), _op(db), 0)], 1, _first,
                    [(gu.shape, BF16, 2 * i + 1, g_gu)])
        dh2, = _mm(f"dh_ffn_{i}", "nt", (T, D, FF), (tmm, tD2, tFk), [(_op(da), _op(gu, 2 * i), 0), (_op(db), _op(gu, 2 * i + 1), 0)], 1, _first,
                   [((T, D), F32, None, None)])
        want_cs = kind == 0
        res = _rms_bwd(f"rms_ffn_bwd_{i}", dx2, sv["x1"], small["norm_ffn"][i:i + 1], dh2, tr, want_bf16=True, want_colsum=want_cs)
        dx1, gs["norm_ffn"][i], dx1b = res[:3]
        xin = sv["x"]
        if kind == 0:
            gs["conv_b_out"][j] = res[3]
            g_sq, = _mm(f"d_conv_out_{i}", "tn", (D, D, T), (tD, tD, tT), [(_op(sv["u4"]), _op(dx1b), 0)], 1, _first,
                        [(sq.shape, BF16, j, g_sq)])
            du4, = _mm(f"dh_conv_out_{i}", "nt", (T, D, D), (tmm, tD2, tDk), [(_op(dx1b), _op(sq, j), 0)], 1, _first,
                       [((T, D), F32, None, None)])
            du2, gs["conv_ln_g"][j], gs["conv_ln_b"][j], gs["conv_b_dw"][j] = _ln_silu_bwd(
                f"ln_silu_bwd_{i}", du4, sv["u2"], small["conv_ln_g"][j:j + 1], small["conv_ln_b"][j:j + 1], tr)
            dag, gs["conv_w_dw"][j], gs["conv_b_in"][j] = _dwconv_glu_bwd(
                f"dwconv_bwd_{i}", du2, sv["u1"], sv["a"], sv["gate"], small["conv_w_dw"][j], tr)
            g_cin, = _mm(f"d_conv_in_{i}", "tn", (D, 2 * D, T), (tD, tD, tT), [(_op(sv["h1"]), _op(dag), 0)], 1, _first,
                         [(cin.shape, BF16, j, g_cin)])
            dh1, = _mm(f"dh_conv_in_{i}", "nt", (T, D, 2 * D), (tmm, tD2, tDk), [(_op(dag), _op(cin, j), 0)], 1, _first,
                       [((T, D), F32, None, None)])
        elif kind == 1:
            dys, gs["pool_scale"][j] = _pool_scale_bwd(f"pool_scale_bwd_{i}", dx1, sv["y0"], small["pool_scale"][j:j + 1], tr)
            tg = _tile(GC, 512)
            kb = GC // tg
            dmix, = _mm(f"dh_pool_{i}", "nt", (T, D, GC), (tmm, GC, tg),
                        [(_op(dys, fn=lambda i_, j_, k_, kb=kb: (i_, j_ * kb + k_)), _op(sv["pw"]), 0)], 1, _first,
                        [((T, D), F32, None, None)])
            ng = len(POOL_WINDOWS)
            gs["pool_w"][j], = _mm(f"d_pool_w_{i}", "tn", (D, GC, T), (GC, GC, tT),
                                   [(_op(sv["mix"]), _op(dys, fn=lambda i_, j_, k_: (k_, i_)), 0)], 1, _first,
                                   [((D, GC), F32, None, None)])
            gs["pool_w"][j] = gs["pool_w"][j].reshape(ng, GC, GC)
            dh1 = _pool_bwd(f"pool_bwd_{i}", dmix, tr)
        else:
            g_sq, = _mm(f"d_attn_o_{i}", "tn", (D, D, T), (tD, tD, tT), [(_op(sv["o"]), _op(dx1b), 0)], 1, _first,
                        [(sq.shape, BF16, NA + j, g_sq)])
            do, = _mm(f"dh_attn_o_{i}", "nt", (T, D, D), (tmm, tD2, tDk), [(_op(dx1b), _op(sq, NA + j), 0)], 1, _first,
                      [((T, D), BF16, None, None)])
            do_hm = do.reshape(T, n_heads, HEAD_DIM).transpose(1, 0, 2)
            dqn, dko, dkp, dvo, dvp, dbias, dsink = _attn_bwd(sv["qn"], sv["kn"], sv["v_hm"], sv["bias"], sv["sink_rows"], do_hm, n_kv, group, T)
            th = _tile(T, 2048, 8)
            dq_hm, gs["attn_q_norm"][j] = _head_norm_bwd(f"qnorm_bwd_{i}", dqn.reshape(n_heads * T, HEAD_DIM), sv["q_hm"],
                                                         small["attn_q_norm"][j:j + 1], th)
            dk_hm, gs["attn_k_norm"][j] = _band_merge(f"knorm_bwd_{i}", dko, dkp, sv["k_hm"].reshape(n_kv, T, HEAD_DIM), small["attn_k_norm"][j:j + 1])
            dv_hm, = _band_merge(f"v_merge_{i}", dvo, dvp)
            gs["attn_sinks"][j] = jnp.sum(dsink.reshape(n_heads, QBLOCK), axis=1).reshape(1, n_heads)
            rb = _bias_bwd(dbias.reshape(n_heads, QBLOCK, 2 * QBLOCK), bucket, NUM_BUCKETS).T
            gs["rel_bias"] = rb if gs["rel_bias"] is None else gs["rel_bias"] + rb
            tok = lambda t_, nh: t_.reshape(nh, T, HEAD_DIM).transpose(1, 0, 2).reshape(T, nh * HEAD_DIM)
            dqkv = jnp.concatenate([tok(dq_hm, n_heads), tok(dk_hm, n_kv), tok(dv_hm, n_kv)], axis=1)
            g_qkv, = _mm(f"d_qkv_{i}", "tn", (D, QW, T), (tD, tQ, tT), [(_op(sv["h1"]), _op(dqkv), 0)], 1, _first,
                         [(qkv_w.shape, BF16, j, g_qkv)])
            dh1, = _mm(f"dh_qkv_{i}", "nt", (T, D, QW), (tmm, tD2, tQ), [(_op(dqkv), _op(qkv_w, j), 0)], 1, _first,
                       [((T, D), F32, None, None)])
        dx, gs["norm_mix"][i] = _rms_bwd(f"rms_mix_bwd_{i}", dx1, xin, small["norm_mix"][i:i + 1], dh1, tr)

    big = {"gu": g_gu, "down": g_down, "cin": g_cin, "sq": g_sq, "qkv": g_qkv}
    gsmall = {}
    for k, v in gs.items():
        if k == "rel_bias":
            gsmall[k] = v
        else:
            gsmall[k] = jnp.stack([t.reshape(small[k].shape[1:]) for t in v], axis=0)
    return loss, dx, big, gsmall


_ANY = pl.BlockSpec(memory_space=pl.ANY)


def _place():
    x, y, c = lax.axis_index("x"), lax.axis_index("y"), lax.axis_index("c")
    return x, y, c, [(1 - x, y), (x, 1 - y), (1 - x, 1 - y)]


def _lane_start(s, w):
    return pl.multiple_of(s * w, LANES) if w % LANES == 0 else s * w


def _slot(ref, kind, s):
    if kind == "col":
        w = ref.shape[2] // N_SLOTS
        return ref.at[:, :, pl.ds(_lane_start(s, w), w)]
    return ref.at[:, pl.ds(s, 1)]


def _rows_half(ref, h):
    n = ref.shape[-2] // 2
    if len(ref.shape) == 3:
        return ref.at[:, pl.ds(h * n, n), :]
    return ref.at[:, :, pl.ds(h * n, n), :]


def _place_own(name, shard, kind, s_arr):
    if kind == "col":
        lead, R, W = shard.shape
        full = (lead, R, N_SLOTS * W)
    else:
        lead, _, R, W = shard.shape
        full = (lead, N_SLOTS, R, W)
    tr = _tile(R, max(16, (1 << 19) // W), 16)
    if kind == "col":
        i_spec = pl.BlockSpec((None, tr, W), lambda l, i, s: (l, i, 0))
        o_spec = pl.BlockSpec((None, tr, W), lambda l, i, s: (l, i, s[0]))
    else:
        i_spec = pl.BlockSpec((None, None, tr, W), lambda l, i, s: (l, 0, i, 0))
        o_spec = pl.BlockSpec((None, None, tr, W), lambda l, i, s: (l, s[0], i, 0))

    def body(s_ref, i_ref, o_ref):
        o_ref[...] = i_ref[...]

    return pl.pallas_call(
        body, name=name,
        grid_spec=pltpu.PrefetchScalarGridSpec(num_scalar_prefetch=1, grid=(lead, R // tr), in_specs=[i_spec], out_specs=o_spec),
        out_shape=jax.ShapeDtypeStruct(full, shard.dtype), compiler_params=_cparams(("arbitrary", "arbitrary")),
    )(s_arr, shard)


def _gather(shards, fulls, kinds):
    ng = len(shards)

    def body(*refs):
        sh, out = refs[:ng], refs[2 * ng:3 * ng]
        send, recv = refs[3 * ng:]
        x, y, c, chips = _place()
        s = 2 * x + y
        sib = (x, y, 1 - c)

        def rcopy(g, k, src, dst, dev):
            return pltpu.make_async_remote_copy(src_ref=src, dst_ref=dst, send_sem=send.at[g * 6 + k], recv_sem=recv.at[g * 6 + k],
                                                device_id=dev, device_id_type=MESH)

        sent = []
        for j, (cx, cy) in enumerate(chips):
            for g in range(ng):
                sent.append(rcopy(g, j, _rows_half(sh[g], c), _rows_half(_slot(out[g], kinds[g], s), c), (cx, cy, c)))
                sent[-1].start()
        for j, (cx, cy) in enumerate(chips):
            for g in range(ng):
                landed = _rows_half(_slot(out[g], kinds[g], 2 * cx + cy), c)
                rcopy(g, j, landed, landed, (cx, cy, c)).wait_recv()
                sent.append(rcopy(g, 3 + j, landed, landed, sib))
                sent[-1].start()
        for j, (cx, cy) in enumerate(chips):
            for g in range(ng):
                handed = _rows_half(_slot(out[g], kinds[g], 2 * cx + cy), 1 - c)
                rcopy(g, 3 + j, handed, handed, sib).wait_recv()
        for cp in sent:
            cp.wait_send()

    return pl.pallas_call(
        body, name="gather_weights", in_specs=[_ANY] * (2 * ng), out_specs=[_ANY] * ng,
        out_shape=[jax.ShapeDtypeStruct(a.shape, a.dtype) for a in fulls],
        input_output_aliases={ng + g: g for g in range(ng)},
        scratch_shapes=[pltpu.SemaphoreType.DMA((6 * ng,)), pltpu.SemaphoreType.DMA((6 * ng,))],
    )(*shards, *fulls)


def _pair_send(grads):
    ng = len(grads)

    def half_shape(a):
        s = list(a.shape)
        s[-2] //= 2
        return tuple(s)

    def body(*refs):
        gr, out = refs[:ng], refs[ng:2 * ng]
        send, recv = refs[2 * ng:]
        x, y, c, _ = _place()
        cps = [pltpu.make_async_remote_copy(src_ref=_rows_half(gr[g], 1 - c), dst_ref=out[g], send_sem=send.at[g], recv_sem=recv.at[g],
                                            device_id=(x, y, 1 - c), device_id_type=MESH) for g in range(ng)]
        for cp in cps:
            cp.start()
        for cp in cps:
            cp.wait()

    return pl.pallas_call(
        body, name="grad_pair_send", in_specs=[_ANY] * ng, out_specs=[_ANY] * ng,
        out_shape=[jax.ShapeDtypeStruct(half_shape(a), a.dtype) for a in grads],
        scratch_shapes=[pltpu.SemaphoreType.DMA((ng,)), pltpu.SemaphoreType.DMA((ng,))],
    )(*grads)


def _add_half(name, g3, pa3, c_arr):
    n, R, N = g3.shape
    rh = R // 2
    tr = _tile(rh, max(16, (1 << 19) // N), 16)
    nb = rh // tr

    def body(c_ref, g_ref, p_ref, o_ref):
        o_ref[...] = (g_ref[...].astype(F32) + p_ref[...].astype(F32)).astype(o_ref.dtype)

    return pl.pallas_call(
        body, name=name,
        grid_spec=pltpu.PrefetchScalarGridSpec(
            num_scalar_prefetch=1, grid=(n, nb),
            in_specs=[pl.BlockSpec((None, tr, N), lambda l, i, c, nb=nb: (l, c[0] * nb + i, 0)), pl.BlockSpec((None, tr, N), lambda l, i, c: (l, i, 0))],
            out_specs=pl.BlockSpec((None, tr, N), lambda l, i, c: (l, i, 0))),
        out_shape=jax.ShapeDtypeStruct(pa3.shape, g3.dtype), compiler_params=_cparams(("arbitrary", "arbitrary")),
    )(c_arr, g3, pa3)


def _ici_exchange(psums, kinds):
    ng = len(psums)

    def recv_shape(a, kind):
        s = a.shape
        return (3, s[0], s[1], s[2] // N_SLOTS) if kind == "col" else (3, s[0], 1, s[2], s[3])

    def body(*refs):
        ps, out = refs[:ng], refs[ng:2 * ng]
        send, recv = refs[2 * ng:]
        x, y, c, chips = _place()
        cps = []
        for j, (cx, cy) in enumerate(chips):
            for g in range(ng):
                cps.append(pltpu.make_async_remote_copy(src_ref=_slot(ps[g], kinds[g], 2 * cx + cy), dst_ref=out[g].at[j], send_sem=send.at[g * 3 + j],
                                                        recv_sem=recv.at[g * 3 + j], device_id=(cx, cy, c), device_id_type=MESH))
                cps[-1].start()
        for cp in cps:
            cp.wait()

    return pl.pallas_call(
        body, name="grad_ici_exchange", in_specs=[_ANY] * ng, out_specs=[_ANY] * ng,
        out_shape=[jax.ShapeDtypeStruct(recv_shape(a, k), a.dtype) for a, k in zip(psums, kinds)],
        scratch_shapes=[pltpu.SemaphoreType.DMA((3 * ng,)), pltpu.SemaphoreType.DMA((3 * ng,))],
    )(*psums)


def _sum4(name, p3, rc3, s_arr, lead, kind):
    R = p3.shape[1]
    W = rc3.shape[2]
    tr = _tile(R, max(16, (1 << 18) // W), 16)
    if kind == "col":
        p_spec = pl.BlockSpec((None, tr, W), lambda l, i, s: (l, i, s[0]))
    else:
        p_spec = pl.BlockSpec((None, tr, W), lambda l, i, s: (l * N_SLOTS + s[0], i, 0))
    r_specs = [pl.BlockSpec((None, tr, W), lambda l, i, s, j=j: (j * lead + l, i, 0)) for j in range(3)]

    def body(s_ref, p_ref, r0, r1, r2, o_ref):
        o_ref[...] = ((p_ref[...].astype(F32) + r0[...].astype(F32)) + r1[...].astype(F32)) + r2[...].astype(F32)

    return pl.pallas_call(
        body, name=name,
        grid_spec=pltpu.PrefetchScalarGridSpec(num_scalar_prefetch=1, grid=(lead, R // tr), in_specs=[p_spec] + r_specs,
                                               out_specs=pl.BlockSpec((None, tr, W), lambda l, i, s: (l, i, 0))),
        out_shape=jax.ShapeDtypeStruct((lead, R, W), F32), compiler_params=_cparams(("arbitrary", "arbitrary")),
    )(s_arr, p3, rc3, rc3, rc3)


def _pair_swap(halves):
    ng = len(halves)

    def body(*refs):
        hv, out = refs[:ng], refs[ng:2 * ng]
        send, recv = refs[2 * ng:]
        x, y, c, _ = _place()
        cps = [pltpu.make_async_remote_copy(src_ref=hv[g], dst_ref=out[g], send_sem=send.at[g], recv_sem=recv.at[g],
                                            device_id=(x, y, 1 - c), device_id_type=MESH) for g in range(ng)]
        for cp in cps:
            cp.start()
        for cp in cps:
            cp.wait()

    return pl.pallas_call(
        body, name="grad_pair_swap", in_specs=[_ANY] * ng, out_specs=[_ANY] * ng,
        out_shape=[jax.ShapeDtypeStruct(a.shape, a.dtype) for a in halves],
        scratch_shapes=[pltpu.SemaphoreType.DMA((ng,)), pltpu.SemaphoreType.DMA((ng,))],
    )(*halves)


N_DEVICES = 8


def _allreduce_small(v):
    rows, m = v.shape

    def body(v_ref, o_ref, buf, send, recv):
        x, y, c, _ = _place()
        me = 4 * x + 2 * y + c
        buf[me] = v_ref[...]
        cps = []
        for k in range(1, N_DEVICES):
            peer = me ^ k
            cps.append(pltpu.make_async_remote_copy(src_ref=v_ref, dst_ref=buf.at[me], send_sem=send.at[k - 1], recv_sem=recv.at[k - 1],
                                                    device_id=((peer >> 2) & 1, (peer >> 1) & 1, peer & 1), device_id_type=MESH))
            cps[-1].start()
        for k in range(1, N_DEVICES):
            theirs = buf.at[me ^ k]
            pltpu.make_async_remote_copy(src_ref=v_ref, dst_ref=theirs, send_sem=send.at[k - 1], recv_sem=recv.at[k - 1],
                                         device_id=(x, y, c), device_id_type=MESH).wait_recv()
        for cp in cps:
            cp.wait_send()
        acc = buf[0]
        for d in range(1, N_DEVICES):
            acc = acc + buf[d]
        o_ref[...] = acc

    vm = pl.BlockSpec(memory_space=pltpu.VMEM)
    return pl.pallas_call(
        body, name="allreduce_small", in_specs=[vm], out_specs=vm, out_shape=jax.ShapeDtypeStruct(v.shape, F32),
        scratch_shapes=[pltpu.VMEM((N_DEVICES, rows, m), F32), pltpu.SemaphoreType.DMA((N_DEVICES - 1,)), pltpu.SemaphoreType.DMA((N_DEVICES - 1,))],
    )(v)


def _pad_rows(a, mult):
    r = (-a.shape[0]) % mult
    return a if r == 0 else jnp.concatenate([a, jnp.zeros((r,) + a.shape[1:], a.dtype)], axis=0)


def _pack_rows(parts, width, mult=16):
    rows, offs, at = [], [], 0
    for a in parts:
        a2 = _pad_rows(a.reshape(-1, width), mult)
        offs.append((at, a.size // width))
        rows.append(a2)
        at += a2.shape[0]
    return jnp.concatenate(rows, axis=0), offs


SMALL_SHARDED = ("ple_w_proj", "pool_w", "conv_w_dw", "conv_b_dw", "conv_ln_g", "conv_ln_b", "conv_b_out", "conv_b_in")
SMALL_REPLICATED = ("norm_mix", "norm_ffn", "norm_ple", "pool_scale", "attn_q_norm", "attn_k_norm", "attn_sinks", "rel_bias", "ple_b_gate")


def _small_to_full(name, slots):
    if name == "pool_w":
        return jnp.moveaxis(slots, 0, 2).reshape(slots.shape[1], slots.shape[2], N_SLOTS * slots.shape[3], slots.shape[4])
    return jnp.moveaxis(slots, 0, -2).reshape(slots.shape[1:-1] + (N_SLOTS * slots.shape[-1],))


def _small_to_slots(name, full):
    if name == "pool_w":
        nb, ng, gc, _ = full.shape
        return jnp.moveaxis(full.reshape(nb, ng, N_SLOTS, gc // N_SLOTS, gc), 2, 0)
    w = full.shape[-1] // N_SLOTS
    return jnp.moveaxis(full.reshape(full.shape[:-1] + (N_SLOTS, w)), -2, 0)


W_NAMES = ("norm_mix", "norm_ffn", "norm_ple", "conv_w_in", "conv_b_in", "conv_w_dw", "conv_b_dw", "conv_ln_g", "conv_ln_b", "conv_w_out",
           "conv_b_out", "pool_w", "pool_scale", "attn_w_qkv", "attn_q_norm", "attn_k_norm", "attn_sinks", "attn_w_o", "rel_bias",
           "ffn_w_gate", "ffn_w_up", "ffn_w_down", "ple_w_proj", "ple_w_gate", "ple_b_gate")


def _step(x, p, target, w, m, v):
    T, D = x.shape[1], x.shape[2]
    L = p.shape[0]
    NA, NC = w["conv_w_in"].shape[0], w["attn_w_qkv"].shape[0]
    xi, yi, ci = lax.axis_index("x"), lax.axis_index("y"), lax.axis_index("c")
    c_arr = jnp.reshape(ci, (1,)).astype(jnp.int32)
    s_arr = jnp.reshape(2 * xi + yi, (1,)).astype(jnp.int32)

    ws = w["ffn_w_gate"].shape[2]
    wq = D // N_SLOTS
    sm_pack, sm_offs = _pack_rows([w[k] for k in SMALL_SHARDED], wq)
    shards = [
        jnp.stack([w["ffn_w_gate"], w["ffn_w_up"]], axis=1).reshape(2 * L, D, ws).astype(BF16),
        w["ffn_w_down"].astype(BF16)[:, None],
        w["conv_w_in"].astype(BF16),
        jnp.concatenate([w["conv_w_out"], w["attn_w_o"], w["ple_w_gate"]], axis=0).astype(BF16)[:, None],
        w["attn_w_qkv"].astype(BF16),
        sm_pack[None, None],
    ]
    kinds = ["col", "row", "col", "row", "col", "row"]
    full = _gather(shards, [_place_own(f"place_own_{g}", a, k, s_arr) for g, (a, k) in enumerate(zip(shards, kinds))], kinds)
    wts = {"gu": full[0], "down": full[1].reshape(L, -1, D), "cin": full[2], "sq": full[3].reshape(NA + NC + L, D, D), "qkv": full[4]}
    small = {k: w[k] for k in SMALL_REPLICATED}
    for k, (at, n) in zip(SMALL_SHARDED, sm_offs):
        small[k] = _small_to_full(k, full[5][0, :, at:at + n].reshape((N_SLOTS,) + w[k].shape))

    loss, dx, big, gsmall = _local_step(x[0], p[:, 0], target[0], wts, small)

    rep_parts = [gsmall[k] for k in SMALL_REPLICATED] + [loss]
    flat = jnp.concatenate([a.reshape(-1) for a in rep_parts])
    n_flat = flat.shape[0]
    m_cols = -(-n_flat // (8 * LANES)) * LANES
    flat = jnp.concatenate([flat, jnp.zeros((8 * m_cols - n_flat,), F32)]).reshape(8, m_cols)
    red = _allreduce_small(flat).reshape(-1)
    grads, at = {}, 0
    for k in SMALL_REPLICATED:
        grads[k] = red[at:at + w[k].size].reshape(w[k].shape)
        at += w[k].size
    loss_out = red[at]

    slots = {k: _small_to_slots(k, gsmall[k]) for k in SMALL_SHARDED}
    gsm = jnp.stack([_pack_rows([slots[k][s] for k in SMALL_SHARDED], wq)[0] for s in range(N_SLOTS)], axis=0)
    local = [big["gu"], big["down"].reshape(L, N_SLOTS, -1, D), big["cin"], big["sq"].reshape(NA + NC + L, N_SLOTS, -1, D), big["qkv"],
             gsm[None]]
    theirs = _pair_send(local)
    psums = []
    for g, (a, t) in enumerate(zip(local, theirs)):
        if kinds[g] == "col":
            psums.append(_add_half(f"pair_add_{g}", a, t, c_arr))
        else:
            n4 = a.shape[0] * N_SLOTS
            psums.append(_add_half(f"pair_add_{g}", a.reshape(n4, a.shape[2], a.shape[3]), t.reshape(n4, t.shape[2], t.shape[3]), c_arr).reshape(t.shape))
    got = _ici_exchange(psums, kinds)
    halves = []
    for g, (ps, rc) in enumerate(zip(psums, got)):
        lead = ps.shape[0]
        if kinds[g] == "col":
            halves.append(_sum4(f"slot_sum_{g}", ps, rc.reshape(3 * lead, rc.shape[2], rc.shape[3]), s_arr, lead, "col"))
        else:
            halves.append(_sum4(f"slot_sum_{g}", ps.reshape(lead * N_SLOTS, ps.shape[2], ps.shape[3]), rc.reshape(3 * lead, rc.shape[3], rc.shape[4]),
                                s_arr, lead, "row"))
    first = ci == 0
    gsh = [jnp.concatenate([jnp.where(first, a, b), jnp.where(first, b, a)], axis=1) for a, b in zip(halves, _pair_swap(halves))]
    gu_g = gsh[0].reshape(L, 2, D, ws)
    grads["ffn_w_gate"], grads["ffn_w_up"] = gu_g[:, 0], gu_g[:, 1]
    grads["ffn_w_down"] = gsh[1]
    grads["conv_w_in"] = gsh[2]
    grads["conv_w_out"], grads["attn_w_o"], grads["ple_w_gate"] = gsh[3][:NA], gsh[3][NA:NA + NC], gsh[3][NA + NC:]
    grads["attn_w_qkv"] = gsh[4]
    for k, (at, n) in zip(SMALL_SHARDED, sm_offs):
        grads[k] = gsh[5][0, at:at + n].reshape(w[k].shape)

    outs_d, outs_m, outs_v = [], [], []
    for k in W_NAMES:
        d_, m_, v_ = _adamw(f"adamw_{k}", w[k], grads[k], m[k], v[k])
        outs_d.append(d_)
        outs_m.append(m_)
        outs_v.append(v_)
    return (loss_out, dx[None], *[grads[k] for k in W_NAMES], *outs_d, *outs_m, *outs_v)


def kernel(x, p, norm_mix, norm_ffn, norm_ple, conv_w_in, conv_b_in, conv_w_dw, conv_b_dw, conv_ln_g, conv_ln_b, conv_w_out, conv_b_out, pool_w, pool_scale, attn_w_qkv, attn_q_norm, attn_k_norm, attn_sinks, attn_w_o, rel_bias, ffn_w_gate, ffn_w_up, ffn_w_down, ple_w_proj, ple_w_gate, ple_b_gate, loss_target, m_norm_mix, m_norm_ffn, m_norm_ple, m_conv_w_in, m_conv_b_in, m_conv_w_dw, m_conv_b_dw, m_conv_ln_g, m_conv_ln_b, m_conv_w_out, m_conv_b_out, m_pool_w, m_pool_scale, m_attn_w_qkv, m_attn_q_norm, m_attn_k_norm, m_attn_sinks, m_attn_w_o, m_rel_bias, m_ffn_w_gate, m_ffn_w_up, m_ffn_w_down, m_ple_w_proj, m_ple_w_gate, m_ple_b_gate, v_norm_mix, v_norm_ffn, v_norm_ple, v_conv_w_in, v_conv_b_in, v_conv_w_dw, v_conv_b_dw, v_conv_ln_g, v_conv_ln_b, v_conv_w_out, v_conv_b_out, v_pool_w, v_pool_scale, v_attn_w_qkv, v_attn_q_norm, v_attn_k_norm, v_attn_sinks, v_attn_w_o, v_rel_bias, v_ffn_w_gate, v_ffn_w_up, v_ffn_w_down, v_ple_w_proj, v_ple_w_gate, v_ple_b_gate):
    ws_ = (norm_mix, norm_ffn, norm_ple, conv_w_in, conv_b_in, conv_w_dw, conv_b_dw, conv_ln_g, conv_ln_b, conv_w_out, conv_b_out, pool_w, pool_scale, attn_w_qkv, attn_q_norm, attn_k_norm, attn_sinks, attn_w_o, rel_bias, ffn_w_gate, ffn_w_up, ffn_w_down, ple_w_proj, ple_w_gate, ple_b_gate)
    ms_ = (m_norm_mix, m_norm_ffn, m_norm_ple, m_conv_w_in, m_conv_b_in, m_conv_w_dw, m_conv_b_dw, m_conv_ln_g, m_conv_ln_b, m_conv_w_out, m_conv_b_out, m_pool_w, m_pool_scale, m_attn_w_qkv, m_attn_q_norm, m_attn_k_norm, m_attn_sinks, m_attn_w_o, m_rel_bias, m_ffn_w_gate, m_ffn_w_up, m_ffn_w_down, m_ple_w_proj, m_ple_w_gate, m_ple_b_gate)
    vs_ = (v_norm_mix, v_norm_ffn, v_norm_ple, v_conv_w_in, v_conv_b_in, v_conv_w_dw, v_conv_b_dw, v_conv_ln_g, v_conv_ln_b, v_conv_w_out, v_conv_b_out, v_pool_w, v_pool_scale, v_attn_w_qkv, v_attn_q_norm, v_attn_k_norm, v_attn_sinks, v_attn_w_o, v_rel_bias, v_ffn_w_gate, v_ffn_w_up, v_ffn_w_down, v_ple_w_proj, v_ple_w_gate, v_ple_b_gate)
    return _step(x, p, loss_target, dict(zip(W_NAMES, ws_)), dict(zip(W_NAMES, ms_)), dict(zip(W_NAMES, vs_)))
```

```python
import functools
import math

import jax
import jax.numpy as jnp
import numpy as np
from jax import lax
from jax.experimental import pallas as pl
from jax.experimental.pallas import tpu as pltpu

F32 = jnp.float32
BF16 = jnp.bfloat16
MESH = pl.DeviceIdType.MESH

CHUNK = 64
CONV_WIDTH = 31
POOL_WINDOWS = (2, 4, 8, 16)
HEAD_DIM = 64
WINDOW_CHUNKS = 2
QBLOCK = 128
NUM_BUCKETS = 32
REL_MAX_DIST = 128
EPS = 1e-6
NEG_INF = -1e30
ADAM_LR, ADAM_B1, ADAM_B2, ADAM_EPS, ADAM_WD, ADAM_STEP = 0.001, 0.9, 0.999, 1e-08, 0.01, 10
N_SLOTS = 4
LANES = 128
VMEM_LIMIT_BYTES = 56 * 1024 * 1024


def _cparams(sem):
    return pltpu.CompilerParams(dimension_semantics=sem, vmem_limit_bytes=VMEM_LIMIT_BYTES)


def _tile(n, pref, mult=LANES):
    if n <= pref:
        return n
    t = (pref // mult) * mult
    while t >= mult:
        if n % t == 0:
            return t
        t -= mult
    return n


def _sig(z):
    return 1.0 / (1.0 + jnp.exp(-z))


def _op(arr, lead=None, ro=0, co=0, fn=None):
    return (arr, lead, ro, co, fn)


_DOT_DIMS = {"nn": (((1,), (0,)), ((), ())), "nt": (((1,), (1,)), ((), ())), "tn": (((0,), (0,)), ((), ()))}


def _mm(name, mode, dims, tiles, terms, n_acc, epilogue, outs, extras=()):
    M, N, K = dims
    tm, tn, tk = tiles
    assert M % tm == 0 and N % tn == 0 and K % tk == 0, (name, dims, tiles)
    nk = K // tk
    a_tile = (tk, tm) if mode == "tn" else (tm, tk)
    b_tile = (tn, tk) if mode == "nt" else (tk, tn)
    a_fn = (lambda i, j, k: (k, i)) if mode == "tn" else (lambda i, j, k: (i, k))
    b_fn = (lambda i, j, k: (j, k)) if mode == "nt" else (lambda i, j, k: (k, j))
    dn = _DOT_DIMS[mode]

    operands, specs, seen = [], [], {}

    def add(op, tshape, default_fn):
        arr, lead, ro, co, fn = op
        fn = fn or default_fn
        key = (id(arr), lead, ro, co, id(fn) if op[4] is not None else None, tshape)
        if key in seen:
            return seen[key]

        def imap(i, j, k, fn=fn, lead=lead, ro=ro, co=co):
            r, c = fn(i, j, k)
            return (r + ro, c + co) if lead is None else (lead, r + ro, c + co)

        operands.append(arr)
        specs.append(pl.BlockSpec(tshape if lead is None else (None,) + tshape, imap))
        seen[key] = len(operands) - 1
        return seen[key]

    term_idx = [(add(a, a_tile, a_fn), add(b, b_tile, b_fn), acc) for a, b, acc in terms]
    extra_idx = []
    for arr, kind, lead, co in extras:
        if kind == "tile":
            extra_idx.append(add(_op(arr, lead, 0, co), (tm, tn), lambda i, j, k: (i, j)))
        elif kind == "row":
            arr3 = arr.reshape(arr.shape[0], 1, arr.shape[1])
            extra_idx.append(add(_op(arr3, 0 if lead is None else lead, 0, co), (1, tn), lambda i, j, k: (0, j)))
        elif kind == "rows":
            extra_idx.append(add(_op(arr, lead, 0, 0), (tm, arr.shape[-1]), lambda i, j, k: (i, 0)))
        elif kind == "cols":
            extra_idx.append(add(_op(arr, lead, 0, co), (arr.shape[-2], tn), lambda i, j, k: (0, j)))
        else:
            extra_idx.append(add(_op(arr, None, 0, 0), (tm, 1), lambda i, j, k: (i, 0)))
    n_in = len(operands)
    out_shapes, out_specs, aliases = [], [], {}
    for oi, (shape, dtype, lead, alias) in enumerate(outs):
        out_shapes.append(jax.ShapeDtypeStruct(shape, dtype))
        if lead == "rowsum":
            out_specs.append(pl.BlockSpec((None, 1, tn), lambda i, j, k: (i, 0, j)))
        elif lead is None:
            out_specs.append(pl.BlockSpec((tm, tn), lambda i, j, k: (i, j)))
        else:
            out_specs.append(pl.BlockSpec((None, tm, tn), lambda i, j, k, lead=lead: (lead, i, j)))
        if alias is not None:
            operands.append(alias)
            specs.append(pl.BlockSpec(memory_space=pl.ANY))
            aliases[len(operands) - 1] = oi
    n_all_in = len(operands)
    n_out = len(outs)

    def body(*refs):
        ins = refs[:n_in]
        o_refs = refs[n_all_in:n_all_in + n_out]
        accs = refs[n_all_in + n_out:]

        def dots():
            sums = [None] * n_acc
            for ai, bi, acc_i in term_idx:
                a = ins[ai][...]
                b = ins[bi][...]
                if a.dtype != BF16:
                    a = a.astype(BF16)
                if b.dtype != BF16:
                    b = b.astype(BF16)
                d = lax.dot_general(a, b, dn, preferred_element_type=F32)
                sums[acc_i] = d if sums[acc_i] is None else sums[acc_i] + d
            return sums

        def finish(vals):
            res = epilogue(vals, [ins[e][...] for e in extra_idx])
            for o, r in zip(o_refs, res):
                o[...] = r.astype(o.dtype)

        if nk == 1:
            finish(dots())
            return
        k = pl.program_id(2)

        @pl.when(k == 0)
        def _():
            for acc, d in zip(accs, dots()):
                acc[...] = d

        if nk > 2:
            @pl.when((k > 0) & (k < nk - 1))
            def _():
                for acc, d in zip(accs, dots()):
                    acc[...] += d

        @pl.when(k == nk - 1)
        def _():
            finish([acc[...] + d for acc, d in zip(accs, dots())])

    res = pl.pallas_call(
        body, name=name, grid=(M // tm, N // tn, nk), in_specs=specs, out_specs=out_specs, out_shape=out_shapes,
        scratch_shapes=[pltpu.VMEM((tm, tn), F32) for _ in range(n_acc if nk > 1 else 0)], input_output_aliases=aliases,
        compiler_params=_cparams(("parallel", "parallel", "arbitrary")),
    )(*operands)
    return res


def _rowk(name, T, tm, ins, outs, body, scratch=()):
    assert T % tm == 0, (name, T, tm)
    n = T // tm
    specs = []
    for arr, kind in ins:
        w = arr.shape[-1]
        if kind == "tile":
            specs.append(pl.BlockSpec((tm, w), lambda i: (i, 0)))
        elif kind == "prev":
            specs.append(pl.BlockSpec((tm, w), lambda i: (jnp.maximum(i - 1, 0), 0)))
        elif kind == "next":
            specs.append(pl.BlockSpec((tm, w), lambda i, n=n: (jnp.minimum(i + 1, n - 1), 0)))
        else:
            specs.append(pl.BlockSpec(arr.shape, lambda i, nd=arr.ndim: (0,) * nd))
    out_shapes, out_specs = [], []
    for shape, dtype, kind in outs:
        out_shapes.append(jax.ShapeDtypeStruct(shape, dtype))
        if kind == "tile":
            out_specs.append(pl.BlockSpec((tm, shape[-1]), lambda i: (i, 0)))
        else:
            out_specs.append(pl.BlockSpec(shape, lambda i, nd=len(shape): (0,) * nd))
    n_in, n_out = len(ins), len(outs)

    def kbody(*refs):
        body(pl.program_id(0), n, refs[:n_in], refs[n_in:n_in + n_out], refs[n_in + n_out:])

    return pl.pallas_call(
        kbody, name=name, grid=(n,), in_specs=specs, out_specs=out_specs, out_shape=out_shapes,
        scratch_shapes=list(scratch), compiler_params=_cparams(("arbitrary",)),
    )(*[a for a, _ in ins])


def _accum(ref, i, val):
    @pl.when(i == 0)
    def _():
        ref[...] = val

    @pl.when(i > 0)
    def _():
        ref[...] += val


def _colsum(v):
    return jnp.sum(v, axis=0, keepdims=True)


def _rms_r(x):
    return lax.rsqrt(jnp.mean(x * x, axis=-1, keepdims=True) + EPS)


def _rms_fwd(name, x, g, tm):
    T, D = x.shape

    def body(i, n, ins, outs, scr):
        xv = ins[0][...]
        outs[0][...] = (xv * _rms_r(xv) * ins[1][...]).astype(BF16)

    return _rowk(name, T, tm, [(x, "tile"), (g, "full")], [((T, D), BF16, "tile")], body)[0]


def _rms_bwd(name, dres, x, g, dh, tm, want_bf16=False, want_colsum=False):
    T, D = x.shape

    def body(i, n, ins, outs, scr):
        xv = ins[1][...]
        gv = ins[2][...]
        dhv = ins[3][...].astype(F32)
        r = _rms_r(xv)
        xh = xv * r
        dhg = dhv * gv
        dx = ins[0][...] + r * (dhg - xh * jnp.mean(dhg * xh, axis=-1, keepdims=True))
        outs[0][...] = dx
        _accum(outs[1], i, _colsum(dhv * xh))
        o = 2
        if want_bf16:
            outs[o][...] = dx.astype(BF16)
            o += 1
        if want_colsum:
            _accum(outs[o], i, _colsum(dx))

    outs = [((T, D), F32, "tile"), ((1, D), F32, "acc")]
    if want_bf16:
        outs.append(((T, D), BF16, "tile"))
    if want_colsum:
        outs.append(((1, D), F32, "acc"))
    return _rowk(name, T, tm, [(dres, "tile"), (x, "tile"), (g, "full"), (dh, "tile")], outs, body)


def _loss_head(y, target, tm):
    T, D = y.shape

    def body(i, n, ins, outs, scr):
        d = ins[0][...] - ins[1][...]
        outs[0][...] = d * (1.0 / D)
        _accum(outs[1], i, jnp.sum(_colsum(d * d), axis=1, keepdims=True) * (0.5 / D))

    return _rowk("loss_head", T, tm, [(y, "tile"), (target, "tile")], [((T, D), F32, "tile"), ((1, 1), F32, "acc")], body)


CONV_ROWS = 128
SUBLANES = 8
_PHASE_PAD = 32


def _phase_scratch(tm):
    return pltpu.VMEM((SUBLANES, tm + _PHASE_PAD, LANES), F32)


def _phase_copies(buf, shf, base, l0, tm):
    n = tm + _PHASE_PAD - SUBLANES
    for r in range(1, SUBLANES):
        shf[r, pl.ds(0, n), :] = buf[pl.ds(base + r, n), pl.ds(l0, LANES)]


def _phase_window(buf, shf, base, q, row0, rows, l0):
    r = q % SUBLANES
    a = q - r
    if r == 0:
        return buf[pl.ds(base + a + row0, rows), pl.ds(l0, LANES)]
    return shf[r, pl.ds(a + row0, rows), :]


def _dwconv_fwd(name, u1, w_dw, b_dw, tm):
    T, D = u1.shape
    rc_n = tm // CONV_ROWS if tm >= CONV_ROWS else 1
    rows = min(CONV_ROWS, tm)
    halo = CONV_WIDTH - 1

    base = tm - _PHASE_PAD

    def body(i, n, ins, outs, scr):
        buf, shf = scr
        buf[pl.ds(0, tm), :] = jnp.where(i > 0, ins[0][...], 0.0)
        buf[pl.ds(tm, tm), :] = ins[1][...]
        w_ref, b_ref, o_ref = ins[2], ins[3], outs[0]

        def chunk(lc, carry):
            l0 = pl.multiple_of(lc * LANES, LANES)
            _phase_copies(buf, shf, base, l0, tm)
            for rc in range(rc_n):
                acc = jnp.zeros((rows, LANES), F32) + b_ref[:, pl.ds(l0, LANES)]
                for k in range(CONV_WIDTH):
                    acc = acc + _phase_window(buf, shf, base, k + _PHASE_PAD - halo, rc * rows, rows, l0) * w_ref[pl.ds(k, 1), pl.ds(l0, LANES)]
                o_ref[pl.ds(rc * rows, rows), pl.ds(l0, LANES)] = acc
            return carry

        lax.fori_loop(0, D // LANES, chunk, 0)

    return _rowk(name, T, tm, [(u1, "prev"), (u1, "tile"), (w_dw, "full"), (b_dw, "full")], [((T, D), F32, "tile")], body,
                 scratch=[pltpu.VMEM((2 * tm, D), F32), _phase_scratch(tm)])[0]


def _ln_silu_fwd(name, u2, g, b, tm):
    T, D = u2.shape

    def body(i, n, ins, outs, scr):
        v = ins[0][...]
        mu = jnp.mean(v, axis=-1, keepdims=True)
        xc = v - mu
        y = xc * lax.rsqrt(jnp.mean(xc * xc, axis=-1, keepdims=True) + EPS) * ins[1][...] + ins[2][...]
        outs[0][...] = (y * _sig(y)).astype(BF16)

    return _rowk(name, T, tm, [(u2, "tile"), (g, "full"), (b, "full")], [((T, D), BF16, "tile")], body)[0]


def _ln_silu_bwd(name, du4, u2, g, b, tm):
    T, D = u2.shape

    def body(i, n, ins, outs, scr):
        v = ins[1][...]
        gv = ins[2][...]
        mu = jnp.mean(v, axis=-1, keepdims=True)
        xc = v - mu
        r = lax.rsqrt(jnp.mean(xc * xc, axis=-1, keepdims=True) + EPS)
        xh = xc * r
        y = xh * gv + ins[3][...]
        s = _sig(y)
        dy = ins[0][...] * (s * (1.0 + y * (1.0 - s)))
        dyg = dy * gv
        du2 = r * (dyg - jnp.mean(dyg, axis=-1, keepdims=True) - xh * jnp.mean(dyg * xh, axis=-1, keepdims=True))
        outs[0][...] = du2
        _accum(outs[1], i, _colsum(dy * xh))
        _accum(outs[2], i, _colsum(dy))
        _accum(outs[3], i, _colsum(du2))

    return _rowk(name, T, tm, [(du4, "tile"), (u2, "tile"), (g, "full"), (b, "full")],
                 [((T, D), F32, "tile"), ((1, D), F32, "acc"), ((1, D), F32, "acc"), ((1, D), F32, "acc")], body)


def _dwconv_glu_bwd(name, du2, u1, a_, gate, w_dw, tm):
    T, D = u1.shape
    rc_n = tm // CONV_ROWS if tm >= CONV_ROWS else 1
    rows = min(CONV_ROWS, tm)
    halo = CONV_WIDTH - 1

    base = tm - _PHASE_PAD

    def body(i, n, ins, outs, scr):
        bu, bd, shu, shd = scr
        bd[pl.ds(0, tm), :] = ins[0][...]
        bd[pl.ds(tm, tm), :] = jnp.where(i < n - 1, ins[1][...], 0.0)
        bu[pl.ds(0, tm), :] = jnp.where(i > 0, ins[2][...], 0.0)
        bu[pl.ds(tm, tm), :] = ins[3][...]
        a_ref, g_ref, w_ref = ins[4], ins[5], ins[6]
        dag_ref, dw_ref, db_ref = outs

        @pl.when(i == 0)
        def _():
            dw_ref[...] = jnp.zeros_like(dw_ref)
            db_ref[...] = jnp.zeros_like(db_ref)

        def chunk(lc, carry):
            l0 = pl.multiple_of(lc * LANES, LANES)
            l1 = pl.multiple_of(D + lc * LANES, LANES)
            _phase_copies(bd, shd, 0, l0, tm)
            _phase_copies(bu, shu, base, l0, tm)
            for rc in range(rc_n):
                r0 = rc * rows
                d_here = bd[pl.ds(r0, rows), pl.ds(l0, LANES)]
                acc = jnp.zeros((rows, LANES), F32)
                for k in range(CONV_WIDTH):
                    wk = w_ref[pl.ds(k, 1), pl.ds(l0, LANES)]
                    acc = acc + _phase_window(bd, shd, 0, halo - k, r0, rows, l0) * wk
                    dw_ref[pl.ds(k, 1), pl.ds(l0, LANES)] += _colsum(d_here * _phase_window(bu, shu, base, k + _PHASE_PAD - halo, r0, rows, l0))
                av = a_ref[pl.ds(r0, rows), pl.ds(l0, LANES)].astype(F32)
                sg = _sig(g_ref[pl.ds(r0, rows), pl.ds(l0, LANES)].astype(F32))
                da = acc * sg
                dg = acc * av * sg * (1.0 - sg)
                dag_ref[pl.ds(r0, rows), pl.ds(l0, LANES)] = da.astype(BF16)
                dag_ref[pl.ds(r0, rows), pl.ds(l1, LANES)] = dg.astype(BF16)
                db_ref[:, pl.ds(l0, LANES)] += _colsum(da)
                db_ref[:, pl.ds(l1, LANES)] += _colsum(dg)
            return carry

        lax.fori_loop(0, D // LANES, chunk, 0)

    return _rowk(name, T, tm, [(du2, "tile"), (du2, "next"), (u1, "prev"), (u1, "tile"), (a_, "tile"), (gate, "tile"), (w_dw, "full")],
                 [((T, 2 * D), BF16, "tile"), ((CONV_WIDTH, D), F32, "acc"), ((1, 2 * D), F32, "acc")], body,
                 scratch=[pltpu.VMEM((2 * tm, D), F32), pltpu.VMEM((2 * tm, D), F32), _phase_scratch(tm), _phase_scratch(tm)])


def _row_index(i, tm, r0, rows):
    return (i * tm + r0 + lax.broadcasted_iota(jnp.int32, (rows, 1), 0)).astype(F32)


def _pool_fwd(name, x, g, tm):
    T, D = x.shape
    gc = D // len(POOL_WINDOWS)
    rows = min(CONV_ROWS, tm)
    rc_n = tm // rows

    def body(i, n, ins, outs, scr):
        buf = scr[0]
        xp = ins[0][...]
        buf[pl.ds(0, tm), :] = jnp.where(i > 0, xp * _rms_r(xp) * ins[2][...], 0.0)
        xc = ins[1][...]
        buf[pl.ds(tm, tm), :] = xc * _rms_r(xc) * ins[2][...]
        o_ref = outs[0]
        for gi, w in enumerate(POOL_WINDOWS):
            def chunk(lc, carry, gi=gi, w=w):
                l0 = pl.multiple_of(gi * gc + lc * LANES, LANES)
                for rc in range(rc_n):
                    r0 = rc * rows
                    acc = buf[pl.ds(tm + r0, rows), pl.ds(l0, LANES)]
                    here = acc
                    for d in range(1, w):
                        acc = acc + buf[pl.ds(tm + r0 - d, rows), pl.ds(l0, LANES)]
                    cnt = jnp.minimum(_row_index(i, tm, r0, rows) + 1.0, float(w))
                    o_ref[pl.ds(r0, rows), pl.ds(l0, LANES)] = (acc / cnt - here).astype(BF16)
                return carry

            lax.fori_loop(0, gc // LANES, chunk, 0)

    return _rowk(name, T, tm, [(x, "prev"), (x, "tile"), (g, "full")], [((T, D), BF16, "tile")], body,
                 scratch=[pltpu.VMEM((2 * tm, D), F32)])[0]


def _pool_bwd(name, dmix, tm):
    T, D = dmix.shape
    gc = D // len(POOL_WINDOWS)
    rows = min(CONV_ROWS, tm)
    rc_n = tm // rows

    def body(i, n, ins, outs, scr):
        buf = scr[0]
        o_ref = outs[0]
        t_here = (i * tm + lax.broadcasted_iota(jnp.int32, (tm, 1), 0)).astype(F32) + 1.0
        for gi, w in enumerate(POOL_WINDOWS):
            cols = pl.ds(gi * gc, gc)
            buf[pl.ds(0, tm), cols] = ins[0][:, cols] / jnp.minimum(t_here, float(w))
            buf[pl.ds(tm, tm), cols] = jnp.where(i < n - 1, ins[1][:, cols] / float(w), 0.0)

            def chunk(lc, carry, gi=gi, w=w):
                l0 = pl.multiple_of(gi * gc + lc * LANES, LANES)
                for rc in range(rc_n):
                    r0 = rc * rows
                    acc = -ins[0][pl.ds(r0, rows), pl.ds(l0, LANES)]
                    for d in range(w):
                        acc = acc + buf[pl.ds(r0 + d, rows), pl.ds(l0, LANES)]
                    o_ref[pl.ds(r0, rows), pl.ds(l0, LANES)] = acc
                return carry

            lax.fori_loop(0, gc // LANES, chunk, 0)

    return _rowk(name, T, tm, [(dmix, "tile"), (dmix, "next")], [((T, D), F32, "tile")], body,
                 scratch=[pltpu.VMEM((2 * tm, D), F32)])[0]


def _pool_scale_bwd(name, dy, y0, scale, tm):
    T, D = dy.shape

    def body(i, n, ins, outs, scr):
        d = ins[0][...]
        outs[0][...] = (d * ins[2][...]).astype(BF16)
        _accum(outs[1], i, _colsum(d * ins[1][...]))

    return _rowk(name, T, tm, [(dy, "tile"), (y0, "tile"), (scale, "full")], [((T, D), BF16, "tile"), ((1, D), F32, "acc")], body)


def _t5_bucket_np():
    i = np.arange(QBLOCK)[:, None]
    j = np.arange(2 * QBLOCK)[None, :]
    rel = j - QBLOCK - i
    nb = NUM_BUCKETS // 2
    n = -rel
    ret = np.where(n < 0, nb, 0)
    n = np.abs(n)
    max_exact = nb // 2
    nf = np.maximum(n, 1).astype(np.float32)
    large = max_exact + (np.log(nf / np.float32(max_exact)) / np.float32(math.log(REL_MAX_DIST / max_exact))
                         * np.float32(nb - max_exact)).astype(np.int32)
    large = np.minimum(large, nb - 1)
    return (ret + np.where(n < max_exact, n, large)).astype(np.int32)


def _bias_fwd(rel_bias, bucket):
    nb, nh = rel_bias.shape

    def body(rb_ref, bk_ref, o_ref):
        h = pl.program_id(0)
        bk = bk_ref[...]
        acc = jnp.zeros(bk.shape, F32)
        for b in range(nb):
            acc = jnp.where(bk == b, rb_ref[b, h], acc)
        o_ref[...] = acc

    return pl.pallas_call(
        body, name="attn_bias_fwd", grid=(nh,),
        in_specs=[pl.BlockSpec(memory_space=pltpu.SMEM), pl.BlockSpec(bucket.shape, lambda h: (0, 0))],
        out_specs=pl.BlockSpec((None,) + bucket.shape, lambda h: (h, 0, 0)),
        out_shape=jax.ShapeDtypeStruct((nh,) + bucket.shape, F32), compiler_params=_cparams(("arbitrary",)),
    )(rel_bias, bucket)


def _bias_bwd(dbias, bucket, nb):
    nh = dbias.shape[0]

    def body(db_ref, bk_ref, o_ref):
        h = pl.program_id(0)
        bk = bk_ref[...]
        d = db_ref[...]
        for b in range(nb):
            o_ref[h, b] = jnp.sum(jnp.where(bk == b, d, 0.0))

    return pl.pallas_call(
        body, name="attn_bias_bwd", grid=(nh,),
        in_specs=[pl.BlockSpec((None,) + bucket.shape, lambda h: (h, 0, 0)), pl.BlockSpec(bucket.shape, lambda h: (0, 0))],
        out_specs=pl.BlockSpec(memory_space=pltpu.SMEM),
        out_shape=jax.ShapeDtypeStruct((nh, nb), F32), compiler_params=_cparams(("arbitrary",)),
    )(dbias, bucket)


def _head_norm_fwd(name, q2, g, tm):
    R, W = q2.shape

    def body(i, n, ins, outs, scr):
        v = ins[0][...]
        outs[0][...] = (v * _rms_r(v) * ins[1][...]).astype(BF16)

    return _rowk(name, R, tm, [(q2, "tile"), (g, "full")], [((R, W), BF16, "tile")], body)[0]


def _head_norm_bwd(name, dqn, q2, g, tm):
    R, W = q2.shape

    def body(i, n, ins, outs, scr):
        v = ins[1][...]
        d = ins[0][...]
        r = _rms_r(v)
        xh = v * r
        dg = d * ins[2][...]
        outs[0][...] = (r * (dg - xh * jnp.mean(dg * xh, axis=-1, keepdims=True))).astype(BF16)
        _accum(outs[1], i, _colsum(d * xh))

    return _rowk(name, R, tm, [(dqn, "tile"), (q2, "tile"), (g, "full")], [((R, W), BF16, "tile"), ((1, W), F32, "acc")], body)


def _band_merge(name, own, prev, k3=None, g=None):
    H, T, W = own.shape
    nblk = T // QBLOCK

    def body(*refs):
        o_ref, p_ref = refs[0], refs[1]
        out_ref = refs[4] if k3 is not None else refs[2]

        def blk(m, carry):
            r0 = pl.multiple_of(m * QBLOCK, QBLOCK)
            rn = pl.multiple_of(jnp.minimum(m + 1, nblk - 1) * QBLOCK, QBLOCK)
            d = o_ref[pl.ds(r0, QBLOCK), :] + jnp.where(m < nblk - 1, p_ref[pl.ds(rn, QBLOCK), :], 0.0)
            if k3 is None:
                out_ref[pl.ds(r0, QBLOCK), :] = d.astype(BF16)
                return carry
            v = refs[2][pl.ds(r0, QBLOCK), :]
            r = _rms_r(v)
            xh = v * r
            dg = d * refs[3][...]
            out_ref[pl.ds(r0, QBLOCK), :] = (r * (dg - xh * jnp.mean(dg * xh, axis=-1, keepdims=True))).astype(BF16)
            return carry + _colsum(d * xh)

        tot = lax.fori_loop(0, nblk, blk, jnp.zeros((1, W), F32))
        if k3 is not None:
            _accum(refs[5], pl.program_id(0), tot)

    head = pl.BlockSpec((None, T, W), lambda h: (h, 0, 0))
    ins, in_specs = [own, prev], [head, head]
    out_shape, out_specs = [jax.ShapeDtypeStruct((H, T, W), BF16)], [head]
    if k3 is not None:
        ins += [k3, g]
        in_specs += [head, pl.BlockSpec(g.shape, lambda h: (0, 0))]
        out_shape.append(jax.ShapeDtypeStruct((1, W), F32))
        out_specs.append(pl.BlockSpec((1, W), lambda h: (0, 0)))
    return pl.pallas_call(body, name=name, grid=(H,), in_specs=in_specs, out_specs=out_specs, out_shape=out_shape,
                          compiler_params=_cparams(("arbitrary",)))(*ins)


def _attn_logits(q, kb, bias, sink, n):
    rows = q.shape[0]
    s = lax.dot_general(q, kb, _DOT_DIMS["nt"], preferred_element_type=F32) * (HEAD_DIM ** -0.5) + bias
    qc = (lax.broadcasted_iota(jnp.int32, (rows, 1), 0) % QBLOCK) // CHUNK
    j = lax.broadcasted_iota(jnp.int32, (1, 2 * QBLOCK), 1)
    kc = j // CHUNK - QBLOCK // CHUNK
    ok = (kc <= qc) & (kc >= qc - WINDOW_CHUNKS) & ((n > 0) | (j >= QBLOCK))
    s = jnp.where(ok, s, NEG_INF)
    m = jnp.maximum(jnp.max(s, axis=-1, keepdims=True), sink)
    e = jnp.exp(s - m)
    es = jnp.exp(sink - m)
    den = jnp.sum(e, axis=-1, keepdims=True) + es
    return e / den, es / den


def _heads_per_step(n_kv):
    return 2 if n_kv % 2 == 0 else 1


def _attn_specs(group, hp, rows):
    blk = lambda hn, fn: pl.BlockSpec((hn, QBLOCK, HEAD_DIM), fn)
    cur = lambda h, n: (h, n, 0)
    prv = lambda h, n: (h, jnp.maximum(n - 1, 0), 0)
    bsp = pl.BlockSpec((hp, rows, 2 * QBLOCK), lambda h, n: (h, 0, 0))
    ssp = pl.BlockSpec((hp, rows, 1), lambda h, n: (h, 0, 0))
    return blk(hp * group, cur), blk(hp, prv), blk(hp, cur), bsp, ssp


def _attn_fwd(qn, kn, v, bias, sink_rows, n_kv, group, T):
    nblk = T // QBLOCK
    rows = group * QBLOCK
    hp = _heads_per_step(n_kv)

    def body(q_ref, kp_ref, kc_ref, vp_ref, vc_ref, b_ref, s_ref, o_ref):
        n = pl.program_id(1)
        for hh in range(hp):
            q = q_ref[pl.ds(hh * group, group)].reshape(rows, HEAD_DIM)
            kb = jnp.concatenate([kp_ref[hh], kc_ref[hh]], axis=0)
            vb = jnp.concatenate([vp_ref[hh], vc_ref[hh]], axis=0)
            p, _ = _attn_logits(q, kb, b_ref[hh], s_ref[hh], n)
            o = lax.dot_general(p.astype(BF16), vb, _DOT_DIMS["nn"], preferred_element_type=F32)
            o_ref[pl.ds(hh * group, group)] = o.reshape(group, QBLOCK, HEAD_DIM).astype(BF16)

    qs, kp, kc, bsp, ssp = _attn_specs(group, hp, rows)
    return pl.pallas_call(
        body, name="attn_fwd", grid=(n_kv // hp, nblk), in_specs=[qs, kp, kc, kp, kc, bsp, ssp],
        out_specs=qs, out_shape=jax.ShapeDtypeStruct(qn.shape, BF16), compiler_params=_cparams(("arbitrary", "arbitrary")),
    )(qn, kn, kn, v, v, bias, sink_rows)


def _attn_bwd(qn, kn, v, bias, sink_rows, do, n_kv, group, T):
    nblk = T // QBLOCK
    rows = group * QBLOCK
    scale = HEAD_DIM ** -0.5
    hp = _heads_per_step(n_kv)

    def body(q_ref, kp_ref, kc_ref, vp_ref, vc_ref, b_ref, s_ref, do_ref, dq_ref, dko_ref, dkp_ref, dvo_ref, dvp_ref, db_ref, ds_ref):
        n = pl.program_id(1)
        for hh in range(hp):
            q = q_ref[pl.ds(hh * group, group)].reshape(rows, HEAD_DIM)
            dov = do_ref[pl.ds(hh * group, group)].reshape(rows, HEAD_DIM)
            kb = jnp.concatenate([kp_ref[hh], kc_ref[hh]], axis=0)
            vb = jnp.concatenate([vp_ref[hh], vc_ref[hh]], axis=0)
            p, ps = _attn_logits(q, kb, b_ref[hh], s_ref[hh], n)
            dp = lax.dot_general(dov, vb, _DOT_DIMS["nt"], preferred_element_type=F32)
            delta = jnp.sum(p * dp, axis=-1, keepdims=True)
            dl = p * (dp - delta)
            dlb = dl.astype(BF16)
            dq = lax.dot_general(dlb, kb, _DOT_DIMS["nn"], preferred_element_type=F32) * scale
            dkb = lax.dot_general(dlb, q, _DOT_DIMS["tn"], preferred_element_type=F32) * scale
            dvb = lax.dot_general(p.astype(BF16), dov, _DOT_DIMS["tn"], preferred_element_type=F32)
            dq_ref[pl.ds(hh * group, group)] = dq.reshape(group, QBLOCK, HEAD_DIM)
            dkp_ref[hh] = dkb[:QBLOCK]
            dko_ref[hh] = dkb[QBLOCK:]
            dvp_ref[hh] = dvb[:QBLOCK]
            dvo_ref[hh] = dvb[QBLOCK:]
            dsink = -ps * delta

            @pl.when(n == 0)
            def _(hh=hh, dl=dl, dsink=dsink):
                db_ref[hh] = dl
                ds_ref[hh] = dsink

            @pl.when(n > 0)
            def _(hh=hh, dl=dl, dsink=dsink):
                db_ref[hh] += dl
                ds_ref[hh] += dsink

    qs, kp, kc, bsp, ssp = _attn_specs(group, hp, rows)
    kv_shape = jax.ShapeDtypeStruct(kn.shape, F32)
    return pl.pallas_call(
        body, name="attn_bwd", grid=(n_kv // hp, nblk), in_specs=[qs, kp, kc, kp, kc, bsp, ssp, qs],
        out_specs=[qs, kc, kc, kc, kc, bsp, ssp],
        out_shape=[jax.ShapeDtypeStruct(qn.shape, F32), kv_shape, kv_shape, kv_shape, kv_shape,
                   jax.ShapeDtypeStruct(bias.shape, F32), jax.ShapeDtypeStruct(sink_rows.shape, F32)],
        compiler_params=_cparams(("arbitrary", "arbitrary")),
    )(qn, kn, kn, v, v, bias, sink_rows, do)


def _adamw(name, w, g, m, v):
    shape = w.shape
    w2, g2, m2, v2 = (a.reshape(-1, shape[-1]) for a in (w, g, m, v))
    R, W = w2.shape
    tm = _tile(R, max(8, (1 << 19) // W), 8)
    d1 = 1.0 - ADAM_B1 ** ADAM_STEP
    d2 = 1.0 - ADAM_B2 ** ADAM_STEP

    def body(i, n, ins, outs, scr):
        wv, gv = ins[0][...], ins[1][...]
        mn = ADAM_B1 * ins[2][...] + (1.0 - ADAM_B1) * gv
        vn = ADAM_B2 * ins[3][...] + (1.0 - ADAM_B2) * (gv * gv)
        outs[0][...] = -ADAM_LR * ((mn / d1) / (jnp.sqrt(vn / d2) + ADAM_EPS) + ADAM_WD * wv)
        outs[1][...] = mn
        outs[2][...] = vn

    d, mn, vn = _rowk(name, R, tm, [(w2, "tile"), (g2, "tile"), (m2, "tile"), (v2, "tile")],
                      [((R, W), F32, "tile")] * 3, body)
    return d.reshape(shape), mn.reshape(shape), vn.reshape(shape)


def _first(accs, extras):
    return [accs[0]]


def _swiglu_fwd(accs, extras):
    a, b = accs
    s = _sig(a)
    t = a * s
    return [t, b * (s + t * (1.0 - s)), t * b]


def _ple_fwd(accs, extras):
    b, p, w_proj, x2 = extras
    g = _sig(accs[0] + b)
    q = lax.dot_general(p.astype(BF16), w_proj.astype(BF16), _DOT_DIMS["nn"], preferred_element_type=F32)
    return [g, x2 + g * q]


def _ple_bwd(accs, extras):
    d, g = extras
    dz = d * accs[0] * g * (1.0 - g)
    return [d * g, dz, _colsum(dz)]


def _swiglu_bwd(accs, extras):
    return [accs[0] * extras[1].astype(F32), accs[0] * extras[0].astype(F32)]


def _local_step(x, p, target, wts, small):
    T, D = x.shape
    L, _, PLE = p.shape
    gu, down, cin, sq, qkv_w = wts["gu"], wts["down"], wts["cin"], wts["sq"], wts["qkv"]
    FF = gu.shape[2]
    NA, NC = cin.shape[0], qkv_w.shape[0]
    QW = qkv_w.shape[2]
    KVD = (QW - D) // 2
    n_heads, n_kv = D // HEAD_DIM, KVD // HEAD_DIM
    group = n_heads // n_kv
    nblk = T // QBLOCK
    GC = D // len(POOL_WINDOWS)

    tr = _tile(T, 256, 8)
    tmm = _tile(T, 1024)
    tD = _tile(D, 1024)
    tDk = _tile(D, 2048)
    tD2 = _tile(D, 512)
    tF = _tile(FF, 512)
    tFk = _tile(FF, 2816)
    tFw = _tile(FF, 1408)
    tP = _tile(PLE, 512)
    tQ = _tile(QW, 768)
    tT = _tile(T, 1024)

    bucket = jnp.asarray(_t5_bucket_np())
    saved = []
    xs = x

    for i in range(L):
        kind, j = i % 3, i // 3
        sv = {"x": xs}
        h1 = _rms_fwd(f"rms_mix_{i}", xs, small["norm_mix"][i:i + 1], tr)
        if kind == 0:
            a_, gate, u1 = _mm(
                f"conv_in_{i}", "nn", (T, D, D), (tmm, tD2, tDk),
                [(_op(h1), _op(cin, j), 0), (_op(h1), _op(cin, j, 0, D // tD2), 1)], 2,
                lambda accs, ex: (lambda a, g: [a, g, a * _sig(g)])(accs[0] + ex[0], accs[1] + ex[1]),
                [((T, D), BF16, None, None), ((T, D), BF16, None, None), ((T, D), F32, None, None)],
                extras=[(small["conv_b_in"], "row", j, 0), (small["conv_b_in"], "row", j, D // tD2)])
            u2 = _dwconv_fwd(f"dwconv_{i}", u1, small["conv_w_dw"][j], small["conv_b_dw"][j:j + 1], tr)
            u4 = _ln_silu_fwd(f"ln_silu_{i}", u2, small["conv_ln_g"][j:j + 1], small["conv_ln_b"][j:j + 1], tr)
            x1, = _mm(f"conv_out_{i}", "nn", (T, D, D), (tmm, tD2, tDk), [(_op(u4), _op(sq, j), 0)], 1,
                      lambda accs, ex: [accs[0] + ex[0] + ex[1]], [((T, D), F32, None, None)],
                      extras=[(small["conv_b_out"], "row", j, 0), (xs, "tile", None, 0)])
            sv.update(h1=h1, a=a_, gate=gate, u1=u1, u2=u2, u4=u4)
        elif kind == 1:
            mix = _pool_fwd(f"pool_{i}", xs, small["norm_mix"][i:i + 1], tr)
            pw = small["pool_w"][j].reshape(len(POOL_WINDOWS) * GC, GC)
            kb = GC // _tile(GC, 512)
            tg = _tile(GC, 512)
            y0, x1 = _mm(f"pool_mm_{i}", "nn", (T, D, GC), (tmm, GC, tg),
                         [(_op(mix, fn=lambda i_, j_, k_, kb=kb: (i_, j_ * kb + k_)), _op(pw, fn=lambda i_, j_, k_, kb=kb: (j_ * kb + k_, 0)), 0)], 1,
                         lambda accs, ex: [accs[0], ex[1] + accs[0] * ex[0]],
                         [((T, D), F32, None, None), ((T, D), F32, None, None)],
                         extras=[(small["pool_scale"][j:j + 1], "row", None, 0), (xs, "tile", None, 0)])
            sv.update(mix=mix, y0=y0, pw=pw)
        else:
            qkv, = _mm(f"qkv_{i}", "nn", (T, QW, D), (tmm, tQ, tDk), [(_op(h1), _op(qkv_w, j), 0)], 1, _first,
                       [((T, QW), F32, None, None)])
            q_hm = qkv[:, :D].reshape(T, n_heads, HEAD_DIM).transpose(1, 0, 2).reshape(n_heads * T, HEAD_DIM)
            k_hm = qkv[:, D:D + KVD].reshape(T, n_kv, HEAD_DIM).transpose(1, 0, 2).reshape(n_kv * T, HEAD_DIM)
            v_hm = qkv[:, D + KVD:].reshape(T, n_kv, HEAD_DIM).transpose(1, 0, 2).astype(BF16)
            th = _tile(T, 2048, 8)
            qn = _head_norm_fwd(f"qnorm_{i}", q_hm, small["attn_q_norm"][j:j + 1], th).reshape(n_heads, T, HEAD_DIM)
            kn = _head_norm_fwd(f"knorm_{i}", k_hm, small["attn_k_norm"][j:j + 1], th).reshape(n_kv, T, HEAD_DIM)
            bias = _bias_fwd(small["rel_bias"], bucket).reshape(n_kv, group * QBLOCK, 2 * QBLOCK)
            sink_rows = jnp.broadcast_to(small["attn_sinks"][j].reshape(n_kv, group, 1, 1), (n_kv, group, QBLOCK, 1)).reshape(n_kv, group * QBLOCK, 1)
            o_hm = _attn_fwd(qn, kn, v_hm, bias, sink_rows, n_kv, group, T)
            o = o_hm.transpose(1, 0, 2).reshape(T, D)
            x1, = _mm(f"attn_o_{i}", "nn", (T, D, D), (tmm, tD2, tDk), [(_op(o), _op(sq, NA + j), 0)], 1,
                      lambda accs, ex: [accs[0] + ex[0]], [((T, D), F32, None, None)], extras=[(xs, "tile", None, 0)])
            sv.update(h1=h1, q_hm=q_hm, k_hm=k_hm, v_hm=v_hm, qn=qn, kn=kn, bias=bias, sink_rows=sink_rows, o=o)
        h2 = _rms_fwd(f"rms_ffn_{i}", x1, small["norm_ffn"][i:i + 1], tr)
        a, b, f = _mm(f"ffn_up_{i}", "nn", (T, FF, D), (tmm, tF, tDk), [(_op(h2), _op(gu, i), 0), (_op(h2), _op(gu, L + i), 1)], 2,
                      _swiglu_fwd, [((T, FF), BF16, None, None)] * 3)
        x2, = _mm(f"ffn_down_{i}", "nn", (T, D, FF), (tmm, tD2, tFk), [(_op(f), _op(down, i), 0)], 1,
                  lambda accs, ex: [accs[0] + ex[0]], [((T, D), F32, None, None)], extras=[(x1, "tile", None, 0)])
        h3 = _rms_fwd(f"rms_ple_{i}", x2, small["norm_ple"][i:i + 1], tr)
        gt, x3 = _mm(f"ple_gate_{i}", "nn", (T, D, D), (tmm, tD2, tDk), [(_op(h3), _op(sq, NA + NC + i), 0)], 1, _ple_fwd,
                     [((T, D), F32, None, None), ((T, D), F32, None, None)],
                     extras=[(small["ple_b_gate"], "row", i, 0), (p, "rows", i, 0), (small["ple_w_proj"], "cols", i, 0), (x2, "tile", None, 0)])
        sv.update(x1=x1, h2=h2, a=sv.get("a"), fa=a, fb=b, f=f, x2=x2, h3=h3, gt=gt)
        saved.append(sv)
        xs = x3

    dx, loss = _loss_head(xs, target, tr)

    g_gu = g_down = g_cin = g_sq = g_qkv = None
    gs = {k: [None] * v.shape[0] for k, v in small.items() if k != "rel_bias"}
    gs["rel_bias"] = None
    gs["ple_w_proj"] = [None] * L

    for i in reversed(range(L)):
        kind, j = i % 3, i // 3
        sv = saved[i]
        dq, dz, dbg = _mm(f"ple_bwd_{i}", "nn", (T, D, PLE), (tmm, tD2, tP), [(_op(p, i), _op(small["ple_w_proj"], i), 0)], 1, _ple_bwd,
                          [((T, D), BF16, None, None), ((T, D), BF16, None, None), ((T // tmm, 1, D), F32, "rowsum", None)],
                          extras=[(dx, "tile", None, 0), (sv["gt"], "tile", None, 0)])
        gs["ple_b_gate"][i] = jnp.sum(dbg, axis=0)
        gs["ple_w_proj"][i], = _mm(f"d_ple_proj_{i}", "tn", (PLE, D, T), (tP, tD, tT), [(_op(p, i), _op(dq), 0)], 1, _first,
                                   [((PLE, D), F32, None, None)])
        g_sq, = _mm(f"d_ple_gate_{i}", "tn", (D, D, T), (tD, tD, tT), [(_op(sv["h3"]), _op(dz), 0)], 1, _first,
                    [(sq.shape, BF16, NA + NC + i, g_sq)])
        dh3, = _mm(f"dh_ple_{i}", "nt", (T, D, D), (tmm, tD2, tDk), [(_op(dz), _op(sq, NA + NC + i), 0)], 1, _first,
                   [((T, D), F32, None, None)])
        dx2, gs["norm_ple"][i], dx2b = _rms_bwd(f"rms_ple_bwd_{i}", dx, sv["x2"], small["norm_ple"][i:i + 1], dh3, tr, want_bf16=True)
        da, db = _mm(f"d_ffn_act_{i}", "nt", (T, FF, D), (tmm, tF, tDk), [(_op(dx2b), _op(down, i), 0)], 1, _swiglu_bwd,
                     [((T, FF), BF16, None, None)] * 2, extras=[(sv["fa"], "tile", None, 0), (sv["fb"], "tile", None, 0)])
        g_down, = _mm(f"d_ffn_down_{i}", "tn", (FF, D, T), (tFw, tD, tT), [(_op(sv["f"]), _op(dx2b), 0)], 1, _first,
                      [(down.shape, BF16, i, g_down)])
        g_gu, = _mm(f"d_ffn_gate_{i}", "tn", (D, FF, T), (tD, tFw, tT), [(_op(sv["h2"]), _op(da), 0)], 1, _first,
                    [(gu.shape, BF16, i, g_gu)])
        g_gu, = _mm(f"d_ffn_up_{i}", "tn", (D, FF, T), (tD, tFw, tT), [(_op(sv["h2"]), _op(db), 0)], 1, _first,
                    [(gu.shape, BF16, L + i, g_gu)])
        dh2, = _mm(f"dh_ffn_{i}", "nt", (T, D, FF), (tmm, tD2, tFk), [(_op(da), _op(gu, i), 0), (_op(db), _op(gu, L + i), 0)], 1, _first,
                   [((T, D), F32, None, None)])
        want_cs = kind == 0
        res = _rms_bwd(f"rms_ffn_bwd_{i}", dx2, sv["x1"], small["norm_ffn"][i:i + 1], dh2, tr, want_bf16=True, want_colsum=want_cs)
        dx1, gs["norm_ffn"][i], dx1b = res[:3]
        xin = sv["x"]
        if kind == 0:
            gs["conv_b_out"][j] = res[3]
            g_sq, = _mm(f"d_conv_out_{i}", "tn", (D, D, T), (tD, tD, tT), [(_op(sv["u4"]), _op(dx1b), 0)], 1, _first,
                        [(sq.shape, BF16, j, g_sq)])
            du4, = _mm(f"dh_conv_out_{i}", "nt", (T, D, D), (tmm, tD2, tDk), [(_op(dx1b), _op(sq, j), 0)], 1, _first,
                       [((T, D), F32, None, None)])
            du2, gs["conv_ln_g"][j], gs["conv_ln_b"][j], gs["conv_b_dw"][j] = _ln_silu_bwd(
                f"ln_silu_bwd_{i}", du4, sv["u2"], small["conv_ln_g"][j:j + 1], small["conv_ln_b"][j:j + 1], tr)
            dag, gs["conv_w_dw"][j], gs["conv_b_in"][j] = _dwconv_glu_bwd(
                f"dwconv_bwd_{i}", du2, sv["u1"], sv["a"], sv["gate"], small["conv_w_dw"][j], tr)
            g_cin, = _mm(f"d_conv_in_{i}", "tn", (D, 2 * D, T), (tD, tD, tT), [(_op(sv["h1"]), _op(dag), 0)], 1, _first,
                         [(cin.shape, BF16, j, g_cin)])
            dh1, = _mm(f"dh_conv_in_{i}", "nt", (T, D, 2 * D), (tmm, tD2, tDk), [(_op(dag), _op(cin, j), 0)], 1, _first,
                       [((T, D), F32, None, None)])
        elif kind == 1:
            dys, gs["pool_scale"][j] = _pool_scale_bwd(f"pool_scale_bwd_{i}", dx1, sv["y0"], small["pool_scale"][j:j + 1], tr)
            tg = _tile(GC, 512)
            kb = GC // tg
            dmix, = _mm(f"dh_pool_{i}", "nt", (T, D, GC), (tmm, GC, tg),
                        [(_op(dys, fn=lambda i_, j_, k_, kb=kb: (i_, j_ * kb + k_)), _op(sv["pw"]), 0)], 1, _first,
                        [((T, D), F32, None, None)])
            ng = len(POOL_WINDOWS)
            gs["pool_w"][j], = _mm(f"d_pool_w_{i}", "tn", (D, GC, T), (GC, GC, tT),
                                   [(_op(sv["mix"]), _op(dys, fn=lambda i_, j_, k_: (k_, i_)), 0)], 1, _first,
                                   [((D, GC), F32, None, None)])
            gs["pool_w"][j] = gs["pool_w"][j].reshape(ng, GC, GC)
            dh1 = _pool_bwd(f"pool_bwd_{i}", dmix, tr)
        else:
            g_sq, = _mm(f"d_attn_o_{i}", "tn", (D, D, T), (tD, tD, tT), [(_op(sv["o"]), _op(dx1b), 0)], 1, _first,
                        [(sq.shape, BF16, NA + j, g_sq)])
            do, = _mm(f"dh_attn_o_{i}", "nt", (T, D, D), (tmm, tD2, tDk), [(_op(dx1b), _op(sq, NA + j), 0)], 1, _first,
                      [((T, D), BF16, None, None)])
            do_hm = do.reshape(T, n_heads, HEAD_DIM).transpose(1, 0, 2)
            dqn, dko, dkp, dvo, dvp, dbias, dsink = _attn_bwd(sv["qn"], sv["kn"], sv["v_hm"], sv["bias"], sv["sink_rows"], do_hm, n_kv, group, T)
            th = _tile(T, 2048, 8)
            dq_hm, gs["attn_q_norm"][j] = _head_norm_bwd(f"qnorm_bwd_{i}", dqn.reshape(n_heads * T, HEAD_DIM), sv["q_hm"],
                                                         small["attn_q_norm"][j:j + 1], th)
            dk_hm, gs["attn_k_norm"][j] = _band_merge(f"knorm_bwd_{i}", dko, dkp, sv["k_hm"].reshape(n_kv, T, HEAD_DIM), small["attn_k_norm"][j:j + 1])
            dv_hm, = _band_merge(f"v_merge_{i}", dvo, dvp)
            gs["attn_sinks"][j] = jnp.sum(dsink.reshape(n_heads, QBLOCK), axis=1).reshape(1, n_heads)
            rb = _bias_bwd(dbias.reshape(n_heads, QBLOCK, 2 * QBLOCK), bucket, NUM_BUCKETS).T
            gs["rel_bias"] = rb if gs["rel_bias"] is None else gs["rel_bias"] + rb
            tok = lambda t_, nh: t_.reshape(nh, T, HEAD_DIM).transpose(1, 0, 2).reshape(T, nh * HEAD_DIM)
            dqkv = jnp.concatenate([tok(dq_hm, n_heads), tok(dk_hm, n_kv), tok(dv_hm, n_kv)], axis=1)
            g_qkv, = _mm(f"d_qkv_{i}", "tn", (D, QW, T), (tD, tQ, tT), [(_op(sv["h1"]), _op(dqkv), 0)], 1, _first,
                         [(qkv_w.shape, BF16, j, g_qkv)])
            dh1, = _mm(f"dh_qkv_{i}", "nt", (T, D, QW), (tmm, tD2, _tile(QW, 3072)), [(_op(dqkv), _op(qkv_w, j), 0)], 1, _first,
                       [((T, D), F32, None, None)])
        dx, gs["norm_mix"][i] = _rms_bwd(f"rms_mix_bwd_{i}", dx1, xin, small["norm_mix"][i:i + 1], dh1, tr)

    big = {"gu": g_gu, "down": g_down, "cin": g_cin, "sq": g_sq, "qkv": g_qkv}
    gsmall = {}
    for k, v in gs.items():
        if k == "rel_bias":
            gsmall[k] = v
        else:
            gsmall[k] = jnp.stack([t.reshape(small[k].shape[1:]) for t in v], axis=0)
    return loss, dx, big, gsmall


_ANY = pl.BlockSpec(memory_space=pl.ANY)


def _place():
    x, y, c = lax.axis_index("x"), lax.axis_index("y"), lax.axis_index("c")
    return x, y, c, [(1 - x, y), (x, 1 - y), (1 - x, 1 - y)]


def _lane_start(s, w):
    return pl.multiple_of(s * w, LANES) if w % LANES == 0 else s * w


def _slot(ref, kind, s):
    if kind == "col":
        w = ref.shape[2] // N_SLOTS
        return ref.at[:, :, pl.ds(_lane_start(s, w), w)]
    return ref.at[:, pl.ds(s, 1)]


def _rows_half(ref, h):
    n = ref.shape[-2] // 2
    if len(ref.shape) == 3:
        return ref.at[:, pl.ds(h * n, n), :]
    return ref.at[:, :, pl.ds(h * n, n), :]


def _place_own(name, shard, kind, s_arr):
    if kind == "col":
        lead, R, W = shard.shape
        full = (lead, R, N_SLOTS * W)
    else:
        lead, _, R, W = shard.shape
        full = (lead, N_SLOTS, R, W)
    tr = _tile(R, max(16, (1 << 19) // W), 16)
    if kind == "col":
        i_spec = pl.BlockSpec((None, tr, W), lambda l, i, s: (l, i, 0))
        o_spec = pl.BlockSpec((None, tr, W), lambda l, i, s: (l, i, s[0]))
    else:
        i_spec = pl.BlockSpec((None, None, tr, W), lambda l, i, s: (l, 0, i, 0))
        o_spec = pl.BlockSpec((None, None, tr, W), lambda l, i, s: (l, s[0], i, 0))

    def body(s_ref, i_ref, o_ref):
        o_ref[...] = i_ref[...]

    return pl.pallas_call(
        body, name=name,
        grid_spec=pltpu.PrefetchScalarGridSpec(num_scalar_prefetch=1, grid=(lead, R // tr), in_specs=[i_spec], out_specs=o_spec),
        out_shape=jax.ShapeDtypeStruct(full, shard.dtype), compiler_params=_cparams(("arbitrary", "arbitrary")),
    )(s_arr, shard)


def _gather(shards, fulls, kinds):
    ng = len(shards)

    def body(*refs):
        sh, out = refs[:ng], refs[2 * ng:3 * ng]
        send, recv = refs[3 * ng:]
        x, y, c, chips = _place()
        s = 2 * x + y
        sib = (x, y, 1 - c)

        def rcopy(g, k, src, dst, dev):
            return pltpu.make_async_remote_copy(src_ref=src, dst_ref=dst, send_sem=send.at[g * 6 + k], recv_sem=recv.at[g * 6 + k],
                                                device_id=dev, device_id_type=MESH)

        sent = []
        for j, (cx, cy) in enumerate(chips):
            for g in range(ng):
                sent.append(rcopy(g, j, _rows_half(sh[g], c), _rows_half(_slot(out[g], kinds[g], s), c), (cx, cy, c)))
                sent[-1].start()
        for j, (cx, cy) in enumerate(chips):
            for g in range(ng):
                landed = _rows_half(_slot(out[g], kinds[g], 2 * cx + cy), c)
                rcopy(g, j, landed, landed, (cx, cy, c)).wait_recv()
                sent.append(rcopy(g, 3 + j, landed, landed, sib))
                sent[-1].start()
        for j, (cx, cy) in enumerate(chips):
            for g in range(ng):
                handed = _rows_half(_slot(out[g], kinds[g], 2 * cx + cy), 1 - c)
                rcopy(g, 3 + j, handed, handed, sib).wait_recv()
        for cp in sent:
            cp.wait_send()

    return pl.pallas_call(
        body, name="gather_weights", in_specs=[_ANY] * (2 * ng), out_specs=[_ANY] * ng,
        out_shape=[jax.ShapeDtypeStruct(a.shape, a.dtype) for a in fulls],
        input_output_aliases={ng + g: g for g in range(ng)},
        scratch_shapes=[pltpu.SemaphoreType.DMA((6 * ng,)), pltpu.SemaphoreType.DMA((6 * ng,))],
    )(*shards, *fulls)


def _pair_send(grads):
    ng = len(grads)

    def half_shape(a):
        s = list(a.shape)
        s[-2] //= 2
        return tuple(s)

    def body(*refs):
        gr, out = refs[:ng], refs[ng:2 * ng]
        send, recv = refs[2 * ng:]
        x, y, c, _ = _place()
        cps = [pltpu.make_async_remote_copy(src_ref=_rows_half(gr[g], 1 - c), dst_ref=out[g], send_sem=send.at[g], recv_sem=recv.at[g],
                                            device_id=(x, y, 1 - c), device_id_type=MESH) for g in range(ng)]
        for cp in cps:
            cp.start()
        for cp in cps:
            cp.wait()

    return pl.pallas_call(
        body, name="grad_pair_send", in_specs=[_ANY] * ng, out_specs=[_ANY] * ng,
        out_shape=[jax.ShapeDtypeStruct(half_shape(a), a.dtype) for a in grads],
        scratch_shapes=[pltpu.SemaphoreType.DMA((ng,)), pltpu.SemaphoreType.DMA((ng,))],
    )(*grads)


def _add_half(name, g3, pa3, c_arr):
    n, R, N = g3.shape
    rh = R // 2
    tr = _tile(rh, max(16, (1 << 19) // N), 16)
    nb = rh // tr

    def body(c_ref, g_ref, p_ref, o_ref):
        o_ref[...] = (g_ref[...].astype(F32) + p_ref[...].astype(F32)).astype(o_ref.dtype)

    return pl.pallas_call(
        body, name=name,
        grid_spec=pltpu.PrefetchScalarGridSpec(
            num_scalar_prefetch=1, grid=(n, nb),
            in_specs=[pl.BlockSpec((None, tr, N), lambda l, i, c, nb=nb: (l, c[0] * nb + i, 0)), pl.BlockSpec((None, tr, N), lambda l, i, c: (l, i, 0))],
            out_specs=pl.BlockSpec((None, tr, N), lambda l, i, c: (l, i, 0))),
        out_shape=jax.ShapeDtypeStruct(pa3.shape, g3.dtype), compiler_params=_cparams(("arbitrary", "arbitrary")),
    )(c_arr, g3, pa3)


def _ici_exchange(psums, kinds):
    ng = len(psums)

    def recv_shape(a, kind):
        s = a.shape
        return (3, s[0], s[1], s[2] // N_SLOTS) if kind == "col" else (3, s[0], 1, s[2], s[3])

    def body(*refs):
        ps, out = refs[:ng], refs[ng:2 * ng]
        send, recv = refs[2 * ng:]
        x, y, c, chips = _place()
        cps = []
        for j, (cx, cy) in enumerate(chips):
            for g in range(ng):
                cps.append(pltpu.make_async_remote_copy(src_ref=_slot(ps[g], kinds[g], 2 * cx + cy), dst_ref=out[g].at[j], send_sem=send.at[g * 3 + j],
                                                        recv_sem=recv.at[g * 3 + j], device_id=(cx, cy, c), device_id_type=MESH))
                cps[-1].start()
        for cp in cps:
            cp.wait()

    return pl.pallas_call(
        body, name="grad_ici_exchange", in_specs=[_ANY] * ng, out_specs=[_ANY] * ng,
        out_shape=[jax.ShapeDtypeStruct(recv_shape(a, k), a.dtype) for a, k in zip(psums, kinds)],
        scratch_shapes=[pltpu.SemaphoreType.DMA((3 * ng,)), pltpu.SemaphoreType.DMA((3 * ng,))],
    )(*psums)


def _sum4(name, p3, rc3, s_arr, lead, kind):
    R = p3.shape[1]
    W = rc3.shape[2]
    tr = _tile(R, max(16, (1 << 18) // W), 16)
    if kind == "col":
        p_spec = pl.BlockSpec((None, tr, W), lambda l, i, s: (l, i, s[0]))
    else:
        p_spec = pl.BlockSpec((None, tr, W), lambda l, i, s: (l * N_SLOTS + s[0], i, 0))
    r_specs = [pl.BlockSpec((None, tr, W), lambda l, i, s, j=j: (j * lead + l, i, 0)) for j in range(3)]

    def body(s_ref, p_ref, r0, r1, r2, o_ref):
        o_ref[...] = ((p_ref[...].astype(F32) + r0[...].astype(F32)) + r1[...].astype(F32)) + r2[...].astype(F32)

    return pl.pallas_call(
        body, name=name,
        grid_spec=pltpu.PrefetchScalarGridSpec(num_scalar_prefetch=1, grid=(lead, R // tr), in_specs=[p_spec] + r_specs,
                                               out_specs=pl.BlockSpec((None, tr, W), lambda l, i, s: (l, i, 0))),
        out_shape=jax.ShapeDtypeStruct((lead, R, W), F32), compiler_params=_cparams(("arbitrary", "arbitrary")),
    )(s_arr, p3, rc3, rc3, rc3)


def _pair_swap(halves):
    ng = len(halves)

    def body(*refs):
        hv, out = refs[:ng], refs[ng:2 * ng]
        send, recv = refs[2 * ng:]
        x, y, c, _ = _place()
        cps = [pltpu.make_async_remote_copy(src_ref=hv[g], dst_ref=out[g], send_sem=send.at[g], recv_sem=recv.at[g],
                                            device_id=(x, y, 1 - c), device_id_type=MESH) for g in range(ng)]
        for cp in cps:
            cp.start()
        for cp in cps:
            cp.wait()

    return pl.pallas_call(
        body, name="grad_pair_swap", in_specs=[_ANY] * ng, out_specs=[_ANY] * ng,
        out_shape=[jax.ShapeDtypeStruct(a.shape, a.dtype) for a in halves],
        scratch_shapes=[pltpu.SemaphoreType.DMA((ng,)), pltpu.SemaphoreType.DMA((ng,))],
    )(*halves)


N_DEVICES = 8


def _allreduce_small(v):
    rows, m = v.shape

    def body(v_ref, o_ref, buf, send, recv):
        x, y, c, _ = _place()
        me = 4 * x + 2 * y + c
        buf[me] = v_ref[...]
        cps = []
        for k in range(1, N_DEVICES):
            peer = me ^ k
            cps.append(pltpu.make_async_remote_copy(src_ref=v_ref, dst_ref=buf.at[me], send_sem=send.at[k - 1], recv_sem=recv.at[k - 1],
                                                    device_id=((peer >> 2) & 1, (peer >> 1) & 1, peer & 1), device_id_type=MESH))
            cps[-1].start()
        for k in range(1, N_DEVICES):
            theirs = buf.at[me ^ k]
            pltpu.make_async_remote_copy(src_ref=v_ref, dst_ref=theirs, send_sem=send.at[k - 1], recv_sem=recv.at[k - 1],
                                         device_id=(x, y, c), device_id_type=MESH).wait_recv()
        for cp in cps:
            cp.wait_send()
        acc = buf[0]
        for d in range(1, N_DEVICES):
            acc = acc + buf[d]
        o_ref[...] = acc

    vm = pl.BlockSpec(memory_space=pltpu.VMEM)
    return pl.pallas_call(
        body, name="allreduce_small", in_specs=[vm], out_specs=vm, out_shape=jax.ShapeDtypeStruct(v.shape, F32),
        scratch_shapes=[pltpu.VMEM((N_DEVICES, rows, m), F32), pltpu.SemaphoreType.DMA((N_DEVICES - 1,)), pltpu.SemaphoreType.DMA((N_DEVICES - 1,))],
    )(v)


def _pad_rows(a, mult):
    r = (-a.shape[0]) % mult
    return a if r == 0 else jnp.concatenate([a, jnp.zeros((r,) + a.shape[1:], a.dtype)], axis=0)


def _pack_rows(parts, width, mult=16):
    rows, offs, at = [], [], 0
    for a in parts:
        a2 = _pad_rows(a.reshape(-1, width), mult)
        offs.append((at, a.size // width))
        rows.append(a2)
        at += a2.shape[0]
    return jnp.concatenate(rows, axis=0), offs


SMALL_SHARDED = ("ple_w_proj", "pool_w", "conv_w_dw", "conv_b_dw", "conv_ln_g", "conv_ln_b", "conv_b_out", "conv_b_in")
SMALL_REPLICATED = ("norm_mix", "norm_ffn", "norm_ple", "pool_scale", "attn_q_norm", "attn_k_norm", "attn_sinks", "rel_bias", "ple_b_gate")


def _small_to_full(name, slots):
    if name == "pool_w":
        return jnp.moveaxis(slots, 0, 2).reshape(slots.shape[1], slots.shape[2], N_SLOTS * slots.shape[3], slots.shape[4])
    return jnp.moveaxis(slots, 0, -2).reshape(slots.shape[1:-1] + (N_SLOTS * slots.shape[-1],))


def _small_to_slots(name, full):
    if name == "pool_w":
        nb, ng, gc, _ = full.shape
        return jnp.moveaxis(full.reshape(nb, ng, N_SLOTS, gc // N_SLOTS, gc), 2, 0)
    w = full.shape[-1] // N_SLOTS
    return jnp.moveaxis(full.reshape(full.shape[:-1] + (N_SLOTS, w)), -2, 0)


W_NAMES = ("norm_mix", "norm_ffn", "norm_ple", "conv_w_in", "conv_b_in", "conv_w_dw", "conv_b_dw", "conv_ln_g", "conv_ln_b", "conv_w_out",
           "conv_b_out", "pool_w", "pool_scale", "attn_w_qkv", "attn_q_norm", "attn_k_norm", "attn_sinks", "attn_w_o", "rel_bias",
           "ffn_w_gate", "ffn_w_up", "ffn_w_down", "ple_w_proj", "ple_w_gate", "ple_b_gate")


def _step(x, p, target, w, m, v):
    T, D = x.shape[1], x.shape[2]
    L = p.shape[0]
    NA, NC = w["conv_w_in"].shape[0], w["attn_w_qkv"].shape[0]
    xi, yi, ci = lax.axis_index("x"), lax.axis_index("y"), lax.axis_index("c")
    c_arr = jnp.reshape(ci, (1,)).astype(jnp.int32)
    s_arr = jnp.reshape(2 * xi + yi, (1,)).astype(jnp.int32)

    wq = D // N_SLOTS
    sm_pack, sm_offs = _pack_rows([w[k] for k in SMALL_SHARDED], wq)
    shards = [
        jnp.concatenate([w["ffn_w_gate"], w["ffn_w_up"]], axis=0).astype(BF16),
        w["ffn_w_down"].astype(BF16)[:, None],
        w["conv_w_in"].astype(BF16),
        jnp.concatenate([w["conv_w_out"], w["attn_w_o"], w["ple_w_gate"]], axis=0).astype(BF16)[:, None],
        w["attn_w_qkv"].astype(BF16),
        sm_pack[None, None],
    ]
    kinds = ["col", "row", "col", "row", "col", "row"]
    full = _gather(shards, [_place_own(f"place_own_{g}", a, k, s_arr) for g, (a, k) in enumerate(zip(shards, kinds))], kinds)
    wts = {"gu": full[0], "down": full[1].reshape(L, -1, D), "cin": full[2], "sq": full[3].reshape(NA + NC + L, D, D), "qkv": full[4]}
    small = {k: w[k] for k in SMALL_REPLICATED}
    for k, (at, n) in zip(SMALL_SHARDED, sm_offs):
        small[k] = _small_to_full(k, full[5][0, :, at:at + n].reshape((N_SLOTS,) + w[k].shape))

    loss, dx, big, gsmall = _local_step(x[0], p[:, 0], target[0], wts, small)

    rep_parts = [gsmall[k] for k in SMALL_REPLICATED] + [loss]
    flat = jnp.concatenate([a.reshape(-1) for a in rep_parts])
    n_flat = flat.shape[0]
    m_cols = -(-n_flat // (8 * LANES)) * LANES
    flat = jnp.concatenate([flat, jnp.zeros((8 * m_cols - n_flat,), F32)]).reshape(8, m_cols)
    red = _allreduce_small(flat).reshape(-1)
    grads, at = {}, 0
    for k in SMALL_REPLICATED:
        grads[k] = red[at:at + w[k].size].reshape(w[k].shape)
        at += w[k].size
    loss_out = red[at]

    slots = {k: _small_to_slots(k, gsmall[k]) for k in SMALL_SHARDED}
    gsm = jnp.stack([_pack_rows([slots[k][s] for k in SMALL_SHARDED], wq)[0] for s in range(N_SLOTS)], axis=0)
    local = [big["gu"], big["down"].reshape(L, N_SLOTS, -1, D), big["cin"], big["sq"].reshape(NA + NC + L, N_SLOTS, -1, D), big["qkv"],
             gsm[None]]
    theirs = _pair_send(local)
    psums = []
    for g, (a, t) in enumerate(zip(local, theirs)):
        if kinds[g] == "col":
            psums.append(_add_half(f"pair_add_{g}", a, t, c_arr))
        else:
            n4 = a.shape[0] * N_SLOTS
            psums.append(_add_half(f"pair_add_{g}", a.reshape(n4, a.shape[2], a.shape[3]), t.reshape(n4, t.shape[2], t.shape[3]), c_arr).reshape(t.shape))
    got = _ici_exchange(psums, kinds)
    halves = []
    for g, (ps, rc) in enumerate(zip(psums, got)):
        lead = ps.shape[0]
        if kinds[g] == "col":
            halves.append(_sum4(f"slot_sum_{g}", ps, rc.reshape(3 * lead, rc.shape[2], rc.shape[3]), s_arr, lead, "col"))
        else:
            halves.append(_sum4(f"slot_sum_{g}", ps.reshape(lead * N_SLOTS, ps.shape[2], ps.shape[3]), rc.reshape(3 * lead, rc.shape[3], rc.shape[4]),
                                s_arr, lead, "row"))
    first = ci == 0
    gsh = [jnp.concatenate([jnp.where(first, a, b), jnp.where(first, b, a)], axis=1) for a, b in zip(halves, _pair_swap(halves))]
    grads["ffn_w_gate"], grads["ffn_w_up"] = gsh[0][:L], gsh[0][L:]
    grads["ffn_w_down"] = gsh[1]
    grads["conv_w_in"] = gsh[2]
    grads["conv_w_out"], grads["attn_w_o"], grads["ple_w_gate"] = gsh[3][:NA], gsh[3][NA:NA + NC], gsh[3][NA + NC:]
    grads["attn_w_qkv"] = gsh[4]
    for k, (at, n) in zip(SMALL_SHARDED, sm_offs):
        grads[k] = gsh[5][0, at:at + n].reshape(w[k].shape)

    outs_d, outs_m, outs_v = [], [], []
    for k in W_NAMES:
        d_, m_, v_ = _adamw(f"adamw_{k}", w[k], grads[k], m[k], v[k])
        outs_d.append(d_)
        outs_m.append(m_)
        outs_v.append(v_)
    return (loss_out, dx[None], *[grads[k] for k in W_NAMES], *outs_d, *outs_m, *outs_v)


def kernel(x, p, norm_mix, norm_ffn, norm_ple, conv_w_in, conv_b_in, conv_w_dw, conv_b_dw, conv_ln_g, conv_ln_b, conv_w_out, conv_b_out, pool_w, pool_scale, attn_w_qkv, attn_q_norm, attn_k_norm, attn_sinks, attn_w_o, rel_bias, ffn_w_gate, ffn_w_up, ffn_w_down, ple_w_proj, ple_w_gate, ple_b_gate, loss_target, m_norm_mix, m_norm_ffn, m_norm_ple, m_conv_w_in, m_conv_b_in, m_conv_w_dw, m_conv_b_dw, m_conv_ln_g, m_conv_ln_b, m_conv_w_out, m_conv_b_out, m_pool_w, m_pool_scale, m_attn_w_qkv, m_attn_q_norm, m_attn_k_norm, m_attn_sinks, m_attn_w_o, m_rel_bias, m_ffn_w_gate, m_ffn_w_up, m_ffn_w_down, m_ple_w_proj, m_ple_w_gate, m_ple_b_gate, v_norm_mix, v_norm_ffn, v_norm_ple, v_conv_w_in, v_conv_b_in, v_conv_w_dw, v_conv_b_dw, v_conv_ln_g, v_conv_ln_b, v_conv_w_out, v_conv_b_out, v_pool_w, v_pool_scale, v_attn_w_qkv, v_attn_q_norm, v_attn_k_norm, v_attn_sinks, v_attn_w_o, v_rel_bias, v_ffn_w_gate, v_ffn_w_up, v_ffn_w_down, v_ple_w_proj, v_ple_w_gate, v_ple_b_gate):
    ws_ = (norm_mix, norm_ffn, norm_ple, conv_w_in, conv_b_in, conv_w_dw, conv_b_dw, conv_ln_g, conv_ln_b, conv_w_out, conv_b_out, pool_w, pool_scale, attn_w_qkv, attn_q_norm, attn_k_norm, attn_sinks, attn_w_o, rel_bias, ffn_w_gate, ffn_w_up, ffn_w_down, ple_w_proj, ple_w_gate, ple_b_gate)
    ms_ = (m_norm_mix, m_norm_ffn, m_norm_ple, m_conv_w_in, m_conv_b_in, m_conv_w_dw, m_conv_b_dw, m_conv_ln_g, m_conv_ln_b, m_conv_w_out, m_conv_b_out, m_pool_w, m_pool_scale, m_attn_w_qkv, m_attn_q_norm, m_attn_k_norm, m_attn_sinks, m_attn_w_o, m_rel_bias, m_ffn_w_gate, m_ffn_w_up, m_ffn_w_down, m_ple_w_proj, m_ple_w_gate, m_ple_b_gate)
    vs_ = (v_norm_mix, v_norm_ffn, v_norm_ple, v_conv_w_in, v_conv_b_in, v_conv_w_dw, v_conv_b_dw, v_conv_ln_g, v_conv_ln_b, v_conv_w_out, v_conv_b_out, v_pool_w, v_pool_scale, v_attn_w_qkv, v_attn_q_norm, v_attn_k_norm, v_attn_sinks, v_attn_w_o, v_rel_bias, v_ffn_w_gate, v_ffn_w_up, v_ffn_w_down, v_ple_w_proj, v_ple_w_gate, v_ple_b_gate)
    return _step(x, p, loss_target, dict(zip(W_NAMES, ws_)), dict(zip(W_NAMES, ms_)), dict(zip(W_NAMES, vs_)))
```

```python
import functools
import math

import jax
import jax.numpy as jnp
import numpy as np
from jax import lax
from jax.experimental import pallas as pl
from jax.experimental.pallas import tpu as pltpu

F32 = jnp.float32
BF16 = jnp.bfloat16
MESH = pl.DeviceIdType.MESH

CHUNK = 64
CONV_WIDTH = 31
POOL_WINDOWS = (2, 4, 8, 16)
HEAD_DIM = 64
WINDOW_CHUNKS = 2
QBLOCK = 128
NUM_BUCKETS = 32
REL_MAX_DIST = 128
EPS = 1e-6
NEG_INF = -1e30
ADAM_LR, ADAM_B1, ADAM_B2, ADAM_EPS, ADAM_WD, ADAM_STEP = 0.001, 0.9, 0.999, 1e-08, 0.01, 10
N_SLOTS = 4
LANES = 128
VMEM_LIMIT_BYTES = 56 * 1024 * 1024


def _cparams(sem):
    return pltpu.CompilerParams(dimension_semantics=sem, vmem_limit_bytes=VMEM_LIMIT_BYTES)


def _tile(n, pref, mult=LANES):
    if n <= pref:
        return n
    t = (pref // mult) * mult
    while t >= mult:
        if n % t == 0:
            return t
        t -= mult
    return n


def _sig(z):
    return 1.0 / (1.0 + jnp.exp(-z))


def _op(arr, lead=None, ro=0, co=0, fn=None):
    return (arr, lead, ro, co, fn)


_DOT_DIMS = {"nn": (((1,), (0,)), ((), ())), "nt": (((1,), (1,)), ((), ())), "tn": (((0,), (0,)), ((), ()))}


def _mm(name, mode, dims, tiles, terms, n_acc, epilogue, outs, extras=()):
    M, N, K = dims
    tm, tn, tk = tiles
    assert M % tm == 0 and N % tn == 0 and K % tk == 0, (name, dims, tiles)
    nk = K // tk
    a_tile = (tk, tm) if mode == "tn" else (tm, tk)
    b_tile = (tn, tk) if mode == "nt" else (tk, tn)
    a_fn = (lambda i, j, k: (k, i)) if mode == "tn" else (lambda i, j, k: (i, k))
    b_fn = (lambda i, j, k: (j, k)) if mode == "nt" else (lambda i, j, k: (k, j))
    dn = _DOT_DIMS[mode]

    operands, specs, seen = [], [], {}

    def add(op, tshape, default_fn):
        arr, lead, ro, co, fn = op
        fn = fn or default_fn
        key = (id(arr), lead, ro, co, id(fn) if op[4] is not None else None, tshape)
        if key in seen:
            return seen[key]

        def imap(i, j, k, fn=fn, lead=lead, ro=ro, co=co):
            r, c = fn(i, j, k)
            return (r + ro, c + co) if lead is None else (lead, r + ro, c + co)

        operands.append(arr)
        specs.append(pl.BlockSpec(tshape if lead is None else (None,) + tshape, imap))
        seen[key] = len(operands) - 1
        return seen[key]

    term_idx = [(add(a, a_tile, a_fn), add(b, b_tile, b_fn), acc) for a, b, acc in terms]
    extra_idx = []
    for arr, kind, lead, co in extras:
        if kind == "tile":
            extra_idx.append(add(_op(arr, lead, 0, co), (tm, tn), lambda i, j, k: (i, j)))
        elif kind == "row":
            arr3 = arr.reshape(arr.shape[0], 1, arr.shape[1])
            extra_idx.append(add(_op(arr3, 0 if lead is None else lead, 0, co), (1, tn), lambda i, j, k: (0, j)))
        elif kind == "rows":
            extra_idx.append(add(_op(arr, lead, 0, 0), (tm, arr.shape[-1]), lambda i, j, k: (i, 0)))
        elif kind == "cols":
            extra_idx.append(add(_op(arr, lead, 0, co), (arr.shape[-2], tn), lambda i, j, k: (0, j)))
        else:
            extra_idx.append(add(_op(arr, None, 0, 0), (tm, 1), lambda i, j, k: (i, 0)))
    n_in = len(operands)
    out_shapes, out_specs, aliases = [], [], {}
    for oi, (shape, dtype, lead, alias) in enumerate(outs):
        out_shapes.append(jax.ShapeDtypeStruct(shape, dtype))
        if lead == "rowsum":
            out_specs.append(pl.BlockSpec((None, 1, tn), lambda i, j, k: (i, 0, j)))
        elif lead is None:
            out_specs.append(pl.BlockSpec((tm, tn), lambda i, j, k: (i, j)))
        else:
            out_specs.append(pl.BlockSpec((None, tm, tn), lambda i, j, k, lead=lead: (lead, i, j)))
        if alias is not None:
            operands.append(alias)
            specs.append(pl.BlockSpec(memory_space=pl.ANY))
            aliases[len(operands) - 1] = oi
    n_all_in = len(operands)
    n_out = len(outs)

    def body(*refs):
        ins = refs[:n_in]
        o_refs = refs[n_all_in:n_all_in + n_out]
        accs = refs[n_all_in + n_out:]

        def dots():
            sums = [None] * n_acc
            for ai, bi, acc_i in term_idx:
                a = ins[ai][...]
                b = ins[bi][...]
                if a.dtype != BF16:
                    a = a.astype(BF16)
                if b.dtype != BF16:
                    b = b.astype(BF16)
                d = lax.dot_general(a, b, dn, preferred_element_type=F32)
                sums[acc_i] = d if sums[acc_i] is None else sums[acc_i] + d
            return sums

        def finish(vals):
            res = epilogue(vals, [ins[e][...] for e in extra_idx])
            for o, r in zip(o_refs, res):
                o[...] = r.astype(o.dtype)

        if nk == 1:
            finish(dots())
            return
        k = pl.program_id(2)

        @pl.when(k == 0)
        def _():
            for acc, d in zip(accs, dots()):
                acc[...] = d

        if nk > 2:
            @pl.when((k > 0) & (k < nk - 1))
            def _():
                for acc, d in zip(accs, dots()):
                    acc[...] += d

        @pl.when(k == nk - 1)
        def _():
            finish([acc[...] + d for acc, d in zip(accs, dots())])

    res = pl.pallas_call(
        body, name=name, grid=(M // tm, N // tn, nk), in_specs=specs, out_specs=out_specs, out_shape=out_shapes,
        scratch_shapes=[pltpu.VMEM((tm, tn), F32) for _ in range(n_acc if nk > 1 else 0)], input_output_aliases=aliases,
        compiler_params=_cparams(("parallel", "parallel", "arbitrary")),
    )(*operands)
    return res


def _rowk(name, T, tm, ins, outs, body, scratch=()):
    assert T % tm == 0, (name, T, tm)
    n = T // tm
    specs = []
    for arr, kind in ins:
        w = arr.shape[-1]
        if kind == "tile":
            specs.append(pl.BlockSpec((tm, w), lambda i: (i, 0)))
        elif kind == "prev":
            specs.append(pl.BlockSpec((tm, w), lambda i: (jnp.maximum(i - 1, 0), 0)))
        elif kind == "next":
            specs.append(pl.BlockSpec((tm, w), lambda i, n=n: (jnp.minimum(i + 1, n - 1), 0)))
        else:
            specs.append(pl.BlockSpec(arr.shape, lambda i, nd=arr.ndim: (0,) * nd))
    out_shapes, out_specs = [], []
    for shape, dtype, kind in outs:
        out_shapes.append(jax.ShapeDtypeStruct(shape, dtype))
        if kind == "tile":
            out_specs.append(pl.BlockSpec((tm, shape[-1]), lambda i: (i, 0)))
        else:
            out_specs.append(pl.BlockSpec(shape, lambda i, nd=len(shape): (0,) * nd))
    n_in, n_out = len(ins), len(outs)

    def kbody(*refs):
        body(pl.program_id(0), n, refs[:n_in], refs[n_in:n_in + n_out], refs[n_in + n_out:])

    return pl.pallas_call(
        kbody, name=name, grid=(n,), in_specs=specs, out_specs=out_specs, out_shape=out_shapes,
        scratch_shapes=list(scratch), compiler_params=_cparams(("arbitrary",)),
    )(*[a for a, _ in ins])


def _accum(ref, i, val):
    @pl.when(i == 0)
    def _():
        ref[...] = val

    @pl.when(i > 0)
    def _():
        ref[...] += val


def _colsum(v):
    return jnp.sum(v, axis=0, keepdims=True)


def _rms_r(x):
    return lax.rsqrt(jnp.mean(x * x, axis=-1, keepdims=True) + EPS)


def _rms_fwd(name, x, g, tm):
    T, D = x.shape

    def body(i, n, ins, outs, scr):
        xv = ins[0][...]
        outs[0][...] = (xv * _rms_r(xv) * ins[1][...]).astype(BF16)

    return _rowk(name, T, tm, [(x, "tile"), (g, "full")], [((T, D), BF16, "tile")], body)[0]


def _rms_bwd(name, dres, x, g, dh, tm, want_bf16=False, want_colsum=False):
    T, D = x.shape

    def body(i, n, ins, outs, scr):
        xv = ins[1][...]
        gv = ins[2][...]
        dhv = ins[3][...].astype(F32)
        r = _rms_r(xv)
        xh = xv * r
        dhg = dhv * gv
        dx = ins[0][...] + r * (dhg - xh * jnp.mean(dhg * xh, axis=-1, keepdims=True))
        outs[0][...] = dx
        _accum(outs[1], i, _colsum(dhv * xh))
        o = 2
        if want_bf16:
            outs[o][...] = dx.astype(BF16)
            o += 1
        if want_colsum:
            _accum(outs[o], i, _colsum(dx))

    outs = [((T, D), F32, "tile"), ((1, D), F32, "acc")]
    if want_bf16:
        outs.append(((T, D), BF16, "tile"))
    if want_colsum:
        outs.append(((1, D), F32, "acc"))
    return _rowk(name, T, tm, [(dres, "tile"), (x, "tile"), (g, "full"), (dh, "tile")], outs, body)


def _loss_head(y, target, tm):
    T, D = y.shape

    def body(i, n, ins, outs, scr):
        d = ins[0][...] - ins[1][...]
        outs[0][...] = d * (1.0 / D)
        _accum(outs[1], i, jnp.sum(_colsum(d * d), axis=1, keepdims=True) * (0.5 / D))

    return _rowk("loss_head", T, tm, [(y, "tile"), (target, "tile")], [((T, D), F32, "tile"), ((1, 1), F32, "acc")], body)


CONV_ROWS = 128
SUBLANES = 8
_PHASE_PAD = 32


def _phase_scratch(tm):
    return pltpu.VMEM((SUBLANES, tm + _PHASE_PAD, LANES), F32)


def _phase_copies(buf, shf, base, l0, tm):
    n = tm + _PHASE_PAD - SUBLANES
    for r in range(1, SUBLANES):
        shf[r, pl.ds(0, n), :] = buf[pl.ds(base + r, n), pl.ds(l0, LANES)]


def _phase_window(buf, shf, base, q, row0, rows, l0):
    r = q % SUBLANES
    a = q - r
    if r == 0:
        return buf[pl.ds(base + a + row0, rows), pl.ds(l0, LANES)]
    return shf[r, pl.ds(a + row0, rows), :]


def _dwconv_fwd(name, u1, w_dw, b_dw, tm):
    T, D = u1.shape
    rc_n = tm // CONV_ROWS if tm >= CONV_ROWS else 1
    rows = min(CONV_ROWS, tm)
    halo = CONV_WIDTH - 1

    base = tm - _PHASE_PAD

    def body(i, n, ins, outs, scr):
        buf, shf = scr
        buf[pl.ds(0, tm), :] = jnp.where(i > 0, ins[0][...], 0.0)
        buf[pl.ds(tm, tm), :] = ins[1][...]
        w_ref, b_ref, o_ref = ins[2], ins[3], outs[0]

        def chunk(lc, carry):
            l0 = pl.multiple_of(lc * LANES, LANES)
            _phase_copies(buf, shf, base, l0, tm)
            for rc in range(rc_n):
                acc = jnp.zeros((rows, LANES), F32) + b_ref[:, pl.ds(l0, LANES)]
                for k in range(CONV_WIDTH):
                    acc = acc + _phase_window(buf, shf, base, k + _PHASE_PAD - halo, rc * rows, rows, l0) * w_ref[pl.ds(k, 1), pl.ds(l0, LANES)]
                o_ref[pl.ds(rc * rows, rows), pl.ds(l0, LANES)] = acc
            return carry

        lax.fori_loop(0, D // LANES, chunk, 0)

    return _rowk(name, T, tm, [(u1, "prev"), (u1, "tile"), (w_dw, "full"), (b_dw, "full")], [((T, D), F32, "tile")], body,
                 scratch=[pltpu.VMEM((2 * tm, D), F32), _phase_scratch(tm)])[0]


def _ln_silu_fwd(name, u2, g, b, tm):
    T, D = u2.shape

    def body(i, n, ins, outs, scr):
        v = ins[0][...]
        mu = jnp.mean(v, axis=-1, keepdims=True)
        xc = v - mu
        y = xc * lax.rsqrt(jnp.mean(xc * xc, axis=-1, keepdims=True) + EPS) * ins[1][...] + ins[2][...]
        outs[0][...] = (y * _sig(y)).astype(BF16)

    return _rowk(name, T, tm, [(u2, "tile"), (g, "full"), (b, "full")], [((T, D), BF16, "tile")], body)[0]


def _ln_silu_bwd(name, du4, u2, g, b, tm):
    T, D = u2.shape

    def body(i, n, ins, outs, scr):
        v = ins[1][...]
        gv = ins[2][...]
        mu = jnp.mean(v, axis=-1, keepdims=True)
        xc = v - mu
        r = lax.rsqrt(jnp.mean(xc * xc, axis=-1, keepdims=True) + EPS)
        xh = xc * r
        y = xh * gv + ins[3][...]
        s = _sig(y)
        dy = ins[0][...] * (s * (1.0 + y * (1.0 - s)))
        dyg = dy * gv
        du2 = r * (dyg - jnp.mean(dyg, axis=-1, keepdims=True) - xh * jnp.mean(dyg * xh, axis=-1, keepdims=True))
        outs[0][...] = du2
        _accum(outs[1], i, _colsum(dy * xh))
        _accum(outs[2], i, _colsum(dy))
        _accum(outs[3], i, _colsum(du2))

    return _rowk(name, T, tm, [(du4, "tile"), (u2, "tile"), (g, "full"), (b, "full")],
                 [((T, D), F32, "tile"), ((1, D), F32, "acc"), ((1, D), F32, "acc"), ((1, D), F32, "acc")], body)


def _dwconv_glu_bwd(name, du2, u1, a_, gate, w_dw, tm):
    T, D = u1.shape
    rc_n = tm // CONV_ROWS if tm >= CONV_ROWS else 1
    rows = min(CONV_ROWS, tm)
    halo = CONV_WIDTH - 1

    base = tm - _PHASE_PAD

    def body(i, n, ins, outs, scr):
        bu, bd, shu, shd = scr
        bd[pl.ds(0, tm), :] = ins[0][...]
        bd[pl.ds(tm, tm), :] = jnp.where(i < n - 1, ins[1][...], 0.0)
        bu[pl.ds(0, tm), :] = jnp.where(i > 0, ins[2][...], 0.0)
        bu[pl.ds(tm, tm), :] = ins[3][...]
        a_ref, g_ref, w_ref = ins[4], ins[5], ins[6]
        dag_ref, dw_ref, db_ref = outs

        @pl.when(i == 0)
        def _():
            dw_ref[...] = jnp.zeros_like(dw_ref)
            db_ref[...] = jnp.zeros_like(db_ref)

        def chunk(lc, carry):
            l0 = pl.multiple_of(lc * LANES, LANES)
            l1 = pl.multiple_of(D + lc * LANES, LANES)
            _phase_copies(bd, shd, 0, l0, tm)
            _phase_copies(bu, shu, base, l0, tm)
            for rc in range(rc_n):
                r0 = rc * rows
                d_here = bd[pl.ds(r0, rows), pl.ds(l0, LANES)]
                acc = jnp.zeros((rows, LANES), F32)
                for k in range(CONV_WIDTH):
                    wk = w_ref[pl.ds(k, 1), pl.ds(l0, LANES)]
                    acc = acc + _phase_window(bd, shd, 0, halo - k, r0, rows, l0) * wk
                    dw_ref[pl.ds(k, 1), pl.ds(l0, LANES)] += _colsum(d_here * _phase_window(bu, shu, base, k + _PHASE_PAD - halo, r0, rows, l0))
                av = a_ref[pl.ds(r0, rows), pl.ds(l0, LANES)].astype(F32)
                sg = _sig(g_ref[pl.ds(r0, rows), pl.ds(l0, LANES)].astype(F32))
                da = acc * sg
                dg = acc * av * sg * (1.0 - sg)
                dag_ref[pl.ds(r0, rows), pl.ds(l0, LANES)] = da.astype(BF16)
                dag_ref[pl.ds(r0, rows), pl.ds(l1, LANES)] = dg.astype(BF16)
                db_ref[:, pl.ds(l0, LANES)] += _colsum(da)
                db_ref[:, pl.ds(l1, LANES)] += _colsum(dg)
            return carry

        lax.fori_loop(0, D // LANES, chunk, 0)

    return _rowk(name, T, tm, [(du2, "tile"), (du2, "next"), (u1, "prev"), (u1, "tile"), (a_, "tile"), (gate, "tile"), (w_dw, "full")],
                 [((T, 2 * D), BF16, "tile"), ((CONV_WIDTH, D), F32, "acc"), ((1, 2 * D), F32, "acc")], body,
                 scratch=[pltpu.VMEM((2 * tm, D), F32), pltpu.VMEM((2 * tm, D), F32), _phase_scratch(tm), _phase_scratch(tm)])


def _row_index(i, tm, r0, rows):
    return (i * tm + r0 + lax.broadcasted_iota(jnp.int32, (rows, 1), 0)).astype(F32)


def _pool_fwd(name, x, g, tm):
    T, D = x.shape
    gc = D // len(POOL_WINDOWS)
    rows = min(CONV_ROWS, tm)
    rc_n = tm // rows

    def body(i, n, ins, outs, scr):
        buf = scr[0]
        xp = ins[0][...]
        buf[pl.ds(0, tm), :] = jnp.where(i > 0, xp * _rms_r(xp) * ins[2][...], 0.0)
        xc = ins[1][...]
        buf[pl.ds(tm, tm), :] = xc * _rms_r(xc) * ins[2][...]
        o_ref = outs[0]
        for gi, w in enumerate(POOL_WINDOWS):
            def chunk(lc, carry, gi=gi, w=w):
                l0 = pl.multiple_of(gi * gc + lc * LANES, LANES)
                for rc in range(rc_n):
                    r0 = rc * rows
                    acc = buf[pl.ds(tm + r0, rows), pl.ds(l0, LANES)]
                    here = acc
                    for d in range(1, w):
                        acc = acc + buf[pl.ds(tm + r0 - d, rows), pl.ds(l0, LANES)]
                    cnt = jnp.minimum(_row_index(i, tm, r0, rows) + 1.0, float(w))
                    o_ref[pl.ds(r0, rows), pl.ds(l0, LANES)] = (acc / cnt - here).astype(BF16)
                return carry

            lax.fori_loop(0, gc // LANES, chunk, 0)

    return _rowk(name, T, tm, [(x, "prev"), (x, "tile"), (g, "full")], [((T, D), BF16, "tile")], body,
                 scratch=[pltpu.VMEM((2 * tm, D), F32)])[0]


def _pool_bwd(name, dmix, tm):
    T, D = dmix.shape
    gc = D // len(POOL_WINDOWS)
    rows = min(CONV_ROWS, tm)
    rc_n = tm // rows

    def body(i, n, ins, outs, scr):
        buf = scr[0]
        o_ref = outs[0]
        t_here = (i * tm + lax.broadcasted_iota(jnp.int32, (tm, 1), 0)).astype(F32) + 1.0
        for gi, w in enumerate(POOL_WINDOWS):
            cols = pl.ds(gi * gc, gc)
            buf[pl.ds(0, tm), cols] = ins[0][:, cols] / jnp.minimum(t_here, float(w))
            buf[pl.ds(tm, tm), cols] = jnp.where(i < n - 1, ins[1][:, cols] / float(w), 0.0)

            def chunk(lc, carry, gi=gi, w=w):
                l0 = pl.multiple_of(gi * gc + lc * LANES, LANES)
                for rc in range(rc_n):
                    r0 = rc * rows
                    acc = -ins[0][pl.ds(r0, rows), pl.ds(l0, LANES)]
                    for d in range(w):
                        acc = acc + buf[pl.ds(r0 + d, rows), pl.ds(l0, LANES)]
                    o_ref[pl.ds(r0, rows), pl.ds(l0, LANES)] = acc
                return carry

            lax.fori_loop(0, gc // LANES, chunk, 0)

    return _rowk(name, T, tm, [(dmix, "tile"), (dmix, "next")], [((T, D), F32, "tile")], body,
                 scratch=[pltpu.VMEM((2 * tm, D), F32)])[0]


def _pool_scale_bwd(name, dy, y0, scale, tm):
    T, D = dy.shape

    def body(i, n, ins, outs, scr):
        d = ins[0][...]
        outs[0][...] = (d * ins[2][...]).astype(BF16)
        _accum(outs[1], i, _colsum(d * ins[1][...]))

    return _rowk(name, T, tm, [(dy, "tile"), (y0, "tile"), (scale, "full")], [((T, D), BF16, "tile"), ((1, D), F32, "acc")], body)


def _t5_bucket_np():
    i = np.arange(QBLOCK)[:, None]
    j = np.arange(2 * QBLOCK)[None, :]
    rel = j - QBLOCK - i
    nb = NUM_BUCKETS // 2
    n = -rel
    ret = np.where(n < 0, nb, 0)
    n = np.abs(n)
    max_exact = nb // 2
    nf = np.maximum(n, 1).astype(np.float32)
    large = max_exact + (np.log(nf / np.float32(max_exact)) / np.float32(math.log(REL_MAX_DIST / max_exact))
                         * np.float32(nb - max_exact)).astype(np.int32)
    large = np.minimum(large, nb - 1)
    return (ret + np.where(n < max_exact, n, large)).astype(np.int32)


def _bias_fwd(rel_bias, bucket):
    nb, nh = rel_bias.shape

    def body(rb_ref, bk_ref, o_ref):
        h = pl.program_id(0)
        bk = bk_ref[...]
        acc = jnp.zeros(bk.shape, F32)
        for b in range(nb):
            acc = jnp.where(bk == b, rb_ref[b, h], acc)
        o_ref[...] = acc

    return pl.pallas_call(
        body, name="attn_bias_fwd", grid=(nh,),
        in_specs=[pl.BlockSpec(memory_space=pltpu.SMEM), pl.BlockSpec(bucket.shape, lambda h: (0, 0))],
        out_specs=pl.BlockSpec((None,) + bucket.shape, lambda h: (h, 0, 0)),
        out_shape=jax.ShapeDtypeStruct((nh,) + bucket.shape, F32), compiler_params=_cparams(("arbitrary",)),
    )(rel_bias, bucket)


def _bias_bwd(dbias, bucket, nb):
    nh = dbias.shape[0]

    def body(db_ref, bk_ref, o_ref):
        h = pl.program_id(0)
        bk = bk_ref[...]
        d = db_ref[...]
        for b in range(nb):
            o_ref[h, b] = jnp.sum(jnp.where(bk == b, d, 0.0))

    return pl.pallas_call(
        body, name="attn_bias_bwd", grid=(nh,),
        in_specs=[pl.BlockSpec((None,) + bucket.shape, lambda h: (h, 0, 0)), pl.BlockSpec(bucket.shape, lambda h: (0, 0))],
        out_specs=pl.BlockSpec(memory_space=pltpu.SMEM),
        out_shape=jax.ShapeDtypeStruct((nh, nb), F32), compiler_params=_cparams(("arbitrary",)),
    )(dbias, bucket)


def _head_norm_fwd(name, q2, g, tm):
    R, W = q2.shape

    def body(i, n, ins, outs, scr):
        v = ins[0][...]
        outs[0][...] = (v * _rms_r(v) * ins[1][...]).astype(BF16)

    return _rowk(name, R, tm, [(q2, "tile"), (g, "full")], [((R, W), BF16, "tile")], body)[0]


def _head_norm_bwd(name, dqn, q2, g, tm):
    R, W = q2.shape

    def body(i, n, ins, outs, scr):
        v = ins[1][...]
        d = ins[0][...]
        r = _rms_r(v)
        xh = v * r
        dg = d * ins[2][...]
        outs[0][...] = (r * (dg - xh * jnp.mean(dg * xh, axis=-1, keepdims=True))).astype(BF16)
        _accum(outs[1], i, _colsum(d * xh))

    return _rowk(name, R, tm, [(dqn, "tile"), (q2, "tile"), (g, "full")], [((R, W), BF16, "tile"), ((1, W), F32, "acc")], body)


def _band_merge(name, own, prev, k3=None, g=None):
    H, T, W = own.shape
    nblk = T // QBLOCK

    def body(*refs):
        o_ref, p_ref = refs[0], refs[1]
        out_ref = refs[4] if k3 is not None else refs[2]

        def blk(m, carry):
            r0 = pl.multiple_of(m * QBLOCK, QBLOCK)
            rn = pl.multiple_of(jnp.minimum(m + 1, nblk - 1) * QBLOCK, QBLOCK)
            d = o_ref[pl.ds(r0, QBLOCK), :] + jnp.where(m < nblk - 1, p_ref[pl.ds(rn, QBLOCK), :], 0.0)
            if k3 is None:
                out_ref[pl.ds(r0, QBLOCK), :] = d.astype(BF16)
                return carry
            v = refs[2][pl.ds(r0, QBLOCK), :]
            r = _rms_r(v)
            xh = v * r
            dg = d * refs[3][...]
            out_ref[pl.ds(r0, QBLOCK), :] = (r * (dg - xh * jnp.mean(dg * xh, axis=-1, keepdims=True))).astype(BF16)
            return carry + _colsum(d * xh)

        tot = lax.fori_loop(0, nblk, blk, jnp.zeros((1, W), F32))
        if k3 is not None:
            _accum(refs[5], pl.program_id(0), tot)

    head = pl.BlockSpec((None, T, W), lambda h: (h, 0, 0))
    ins, in_specs = [own, prev], [head, head]
    out_shape, out_specs = [jax.ShapeDtypeStruct((H, T, W), BF16)], [head]
    if k3 is not None:
        ins += [k3, g]
        in_specs += [head, pl.BlockSpec(g.shape, lambda h: (0, 0))]
        out_shape.append(jax.ShapeDtypeStruct((1, W), F32))
        out_specs.append(pl.BlockSpec((1, W), lambda h: (0, 0)))
    return pl.pallas_call(body, name=name, grid=(H,), in_specs=in_specs, out_specs=out_specs, out_shape=out_shape,
                          compiler_params=_cparams(("arbitrary",)))(*ins)


def _masked_bias(bias):
    rows = bias.shape[1]
    qc = (jnp.arange(rows)[:, None] % QBLOCK) // CHUNK
    j = jnp.arange(2 * QBLOCK)[None, :]
    kc = j // CHUNK - QBLOCK // CHUNK
    ok = (kc <= qc) & (kc >= qc - WINDOW_CHUNKS)
    hide = lambda visible: jnp.where(visible, 0.0, NEG_INF).astype(F32)[None]
    return jnp.stack([bias + hide(ok & (j >= QBLOCK)), bias + hide(ok)], axis=0)


def _attn_logits(q, kb, bias_masked, sink):
    s = lax.dot_general(q, kb, _DOT_DIMS["nt"], preferred_element_type=F32) * (HEAD_DIM ** -0.5) + bias_masked
    m = jnp.maximum(jnp.max(s, axis=-1, keepdims=True), sink)
    e = jnp.exp(s - m)
    es = jnp.exp(sink - m)
    den = jnp.sum(e, axis=-1, keepdims=True) + es
    return e / den, es / den


def _heads_per_step(n_kv):
    return 2 if n_kv % 2 == 0 else 1


def _attn_specs(group, hp, rows):
    blk = lambda hn, fn: pl.BlockSpec((hn, QBLOCK, HEAD_DIM), fn)
    cur = lambda h, n: (h, n, 0)
    prv = lambda h, n: (h, jnp.maximum(n - 1, 0), 0)
    bsp = pl.BlockSpec((hp, rows, 2 * QBLOCK), lambda h, n: (h, 0, 0))
    ssp = pl.BlockSpec((hp, rows, 1), lambda h, n: (h, 0, 0))
    bmsp = pl.BlockSpec((None, hp, rows, 2 * QBLOCK), lambda h, n: (jnp.minimum(n, 1), h, 0, 0))
    return blk(hp * group, cur), blk(hp, prv), blk(hp, cur), bsp, ssp, bmsp


def _attn_fwd(qn, kn, v, bias, sink_rows, n_kv, group, T):
    nblk = T // QBLOCK
    rows = group * QBLOCK
    hp = _heads_per_step(n_kv)

    def body(q_ref, kp_ref, kc_ref, vp_ref, vc_ref, b_ref, s_ref, o_ref):
        for hh in range(hp):
            q = q_ref[pl.ds(hh * group, group)].reshape(rows, HEAD_DIM)
            kb = jnp.concatenate([kp_ref[hh], kc_ref[hh]], axis=0)
            vb = jnp.concatenate([vp_ref[hh], vc_ref[hh]], axis=0)
            p, _ = _attn_logits(q, kb, b_ref[hh], s_ref[hh])
            o = lax.dot_general(p.astype(BF16), vb, _DOT_DIMS["nn"], preferred_element_type=F32)
            o_ref[pl.ds(hh * group, group)] = o.reshape(group, QBLOCK, HEAD_DIM).astype(BF16)

    qs, kp, kc, _, ssp, bmsp = _attn_specs(group, hp, rows)
    return pl.pallas_call(
        body, name="attn_fwd", grid=(n_kv // hp, nblk), in_specs=[qs, kp, kc, kp, kc, bmsp, ssp],
        out_specs=qs, out_shape=jax.ShapeDtypeStruct(qn.shape, BF16), compiler_params=_cparams(("arbitrary", "arbitrary")),
    )(qn, kn, kn, v, v, bias, sink_rows)


def _attn_bwd(qn, kn, v, bias, sink_rows, do, n_kv, group, T):
    nblk = T // QBLOCK
    rows = group * QBLOCK
    scale = HEAD_DIM ** -0.5
    hp = _heads_per_step(n_kv)

    def body(q_ref, kp_ref, kc_ref, vp_ref, vc_ref, b_ref, s_ref, do_ref, dq_ref, dko_ref, dkp_ref, dvo_ref, dvp_ref, db_ref, ds_ref):
        n = pl.program_id(1)
        for hh in range(hp):
            q = q_ref[pl.ds(hh * group, group)].reshape(rows, HEAD_DIM)
            dov = do_ref[pl.ds(hh * group, group)].reshape(rows, HEAD_DIM)
            kb = jnp.concatenate([kp_ref[hh], kc_ref[hh]], axis=0)
            vb = jnp.concatenate([vp_ref[hh], vc_ref[hh]], axis=0)
            p, ps = _attn_logits(q, kb, b_ref[hh], s_ref[hh])
            dp = lax.dot_general(dov, vb, _DOT_DIMS["nt"], preferred_element_type=F32)
            delta = jnp.sum(p * dp, axis=-1, keepdims=True)
            dl = p * (dp - delta)
            dlb = dl.astype(BF16)
            dq = lax.dot_general(dlb, kb, _DOT_DIMS["nn"], preferred_element_type=F32) * scale
            dkb = lax.dot_general(dlb, q, _DOT_DIMS["tn"], preferred_element_type=F32) * scale
            dvb = lax.dot_general(p.astype(BF16), dov, _DOT_DIMS["tn"], preferred_element_type=F32)
            dq_ref[pl.ds(hh * group, group)] = dq.reshape(group, QBLOCK, HEAD_DIM)
            dkp_ref[hh] = dkb[:QBLOCK]
            dko_ref[hh] = dkb[QBLOCK:]
            dvp_ref[hh] = dvb[:QBLOCK]
            dvo_ref[hh] = dvb[QBLOCK:]
            dsink = -ps * delta

            @pl.when(n == 0)
            def _(hh=hh, dl=dl, dsink=dsink):
                db_ref[hh] = dl
                ds_ref[hh] = dsink

            @pl.when(n > 0)
            def _(hh=hh, dl=dl, dsink=dsink):
                db_ref[hh] += dl
                ds_ref[hh] += dsink

    qs, kp, kc, bsp, ssp, bmsp = _attn_specs(group, hp, rows)
    kv_shape = jax.ShapeDtypeStruct(kn.shape, F32)
    return pl.pallas_call(
        body, name="attn_bwd", grid=(n_kv // hp, nblk), in_specs=[qs, kp, kc, kp, kc, bmsp, ssp, qs],
        out_specs=[qs, kc, kc, kc, kc, bsp, ssp],
        out_shape=[jax.ShapeDtypeStruct(qn.shape, F32), kv_shape, kv_shape, kv_shape, kv_shape,
                   jax.ShapeDtypeStruct(bias.shape[1:], F32), jax.ShapeDtypeStruct(sink_rows.shape, F32)],
        compiler_params=_cparams(("arbitrary", "arbitrary")),
    )(qn, kn, kn, v, v, bias, sink_rows, do)


def _adamw(name, w, g, m, v):
    shape = w.shape
    w2, g2, m2, v2 = (a.reshape(-1, shape[-1]) for a in (w, g, m, v))
    R, W = w2.shape
    tm = _tile(R, max(8, (1 << 19) // W), 8)
    d1 = 1.0 - ADAM_B1 ** ADAM_STEP
    d2 = 1.0 - ADAM_B2 ** ADAM_STEP

    def body(i, n, ins, outs, scr):
        wv, gv = ins[0][...], ins[1][...]
        mn = ADAM_B1 * ins[2][...] + (1.0 - ADAM_B1) * gv
        vn = ADAM_B2 * ins[3][...] + (1.0 - ADAM_B2) * (gv * gv)
        outs[0][...] = -ADAM_LR * ((mn / d1) / (jnp.sqrt(vn / d2) + ADAM_EPS) + ADAM_WD * wv)
        outs[1][...] = mn
        outs[2][...] = vn

    d, mn, vn = _rowk(name, R, tm, [(w2, "tile"), (g2, "tile"), (m2, "tile"), (v2, "tile")],
                      [((R, W), F32, "tile")] * 3, body)
    return d.reshape(shape), mn.reshape(shape), vn.reshape(shape)


def _first(accs, extras):
    return [accs[0]]


def _swiglu_fwd(accs, extras):
    a, b = accs
    s = _sig(a)
    t = a * s
    return [t, b * (s + t * (1.0 - s)), t * b]


def _ple_fwd(accs, extras):
    b, p, w_proj, x2 = extras
    g = _sig(accs[0] + b)
    q = lax.dot_general(p.astype(BF16), w_proj.astype(BF16), _DOT_DIMS["nn"], preferred_element_type=F32)
    return [g, x2 + g * q]


def _ple_bwd(accs, extras):
    d, g = extras
    dz = d * accs[0] * g * (1.0 - g)
    return [d * g, dz, _colsum(dz)]


def _swiglu_bwd(accs, extras):
    return [accs[0] * extras[1].astype(F32), accs[0] * extras[0].astype(F32)]


def _local_step(x, p, target, wts, small):
    T, D = x.shape
    L, _, PLE = p.shape
    gu, down, cin, sq, qkv_w = wts["gu"], wts["down"], wts["cin"], wts["sq"], wts["qkv"]
    FF = gu.shape[2]
    NA, NC = cin.shape[0], qkv_w.shape[0]
    QW = qkv_w.shape[2]
    KVD = (QW - D) // 2
    n_heads, n_kv = D // HEAD_DIM, KVD // HEAD_DIM
    group = n_heads // n_kv
    nblk = T // QBLOCK
    GC = D // len(POOL_WINDOWS)

    tr = _tile(T, 256, 8)
    tmm = _tile(T, 1024)
    tD = _tile(D, 1024)
    tDk = _tile(D, 2048)
    tD2 = _tile(D, 512)
    tF = _tile(FF, 512)
    tFk = _tile(FF, 2816)
    tFw = _tile(FF, 1408)
    tP = _tile(PLE, 512)
    tQ = _tile(QW, 768)
    tT = _tile(T, 1024)

    bucket = jnp.asarray(_t5_bucket_np())
    saved = []
    xs = x

    for i in range(L):
        kind, j = i % 3, i // 3
        sv = {"x": xs}
        h1 = _rms_fwd(f"rms_mix_{i}", xs, small["norm_mix"][i:i + 1], tr)
        if kind == 0:
            a_, gate, u1 = _mm(
                f"conv_in_{i}", "nn", (T, D, D), (tmm, tD2, tDk),
                [(_op(h1), _op(cin, j), 0), (_op(h1), _op(cin, j, 0, D // tD2), 1)], 2,
                lambda accs, ex: (lambda a, g: [a, g, a * _sig(g)])(accs[0] + ex[0], accs[1] + ex[1]),
                [((T, D), BF16, None, None), ((T, D), BF16, None, None), ((T, D), F32, None, None)],
                extras=[(small["conv_b_in"], "row", j, 0), (small["conv_b_in"], "row", j, D // tD2)])
            u2 = _dwconv_fwd(f"dwconv_{i}", u1, small["conv_w_dw"][j], small["conv_b_dw"][j:j + 1], tr)
            u4 = _ln_silu_fwd(f"ln_silu_{i}", u2, small["conv_ln_g"][j:j + 1], small["conv_ln_b"][j:j + 1], tr)
            x1, = _mm(f"conv_out_{i}", "nn", (T, D, D), (tmm, tD2, tDk), [(_op(u4), _op(sq, j), 0)], 1,
                      lambda accs, ex: [accs[0] + ex[0] + ex[1]], [((T, D), F32, None, None)],
                      extras=[(small["conv_b_out"], "row", j, 0), (xs, "tile", None, 0)])
            sv.update(h1=h1, a=a_, gate=gate, u1=u1, u2=u2, u4=u4)
        elif kind == 1:
            mix = _pool_fwd(f"pool_{i}", xs, small["norm_mix"][i:i + 1], tr)
            pw = small["pool_w"][j].reshape(len(POOL_WINDOWS) * GC, GC)
            kb = GC // _tile(GC, 512)
            tg = _tile(GC, 512)
            y0, x1 = _mm(f"pool_mm_{i}", "nn", (T, D, GC), (tmm, GC, tg),
                         [(_op(mix, fn=lambda i_, j_, k_, kb=kb: (i_, j_ * kb + k_)), _op(pw, fn=lambda i_, j_, k_, kb=kb: (j_ * kb + k_, 0)), 0)], 1,
                         lambda accs, ex: [accs[0], ex[1] + accs[0] * ex[0]],
                         [((T, D), F32, None, None), ((T, D), F32, None, None)],
                         extras=[(small["pool_scale"][j:j + 1], "row", None, 0), (xs, "tile", None, 0)])
            sv.update(mix=mix, y0=y0, pw=pw)
        else:
            qkv, = _mm(f"qkv_{i}", "nn", (T, QW, D), (tmm, tQ, tDk), [(_op(h1), _op(qkv_w, j), 0)], 1, _first,
                       [((T, QW), F32, None, None)])
            q_hm = qkv[:, :D].reshape(T, n_heads, HEAD_DIM).transpose(1, 0, 2).reshape(n_heads * T, HEAD_DIM)
            k_hm = qkv[:, D:D + KVD].reshape(T, n_kv, HEAD_DIM).transpose(1, 0, 2).reshape(n_kv * T, HEAD_DIM)
            v_hm = qkv[:, D + KVD:].reshape(T, n_kv, HEAD_DIM).transpose(1, 0, 2).astype(BF16)
            th = _tile(T, 2048, 8)
            qn = _head_norm_fwd(f"qnorm_{i}", q_hm, small["attn_q_norm"][j:j + 1], th).reshape(n_heads, T, HEAD_DIM)
            kn = _head_norm_fwd(f"knorm_{i}", k_hm, small["attn_k_norm"][j:j + 1], th).reshape(n_kv, T, HEAD_DIM)
            bias = _masked_bias(_bias_fwd(small["rel_bias"], bucket).reshape(n_kv, group * QBLOCK, 2 * QBLOCK))
            sink_rows = jnp.broadcast_to(small["attn_sinks"][j].reshape(n_kv, group, 1, 1), (n_kv, group, QBLOCK, 1)).reshape(n_kv, group * QBLOCK, 1)
            o_hm = _attn_fwd(qn, kn, v_hm, bias, sink_rows, n_kv, group, T)
            o = o_hm.transpose(1, 0, 2).reshape(T, D)
            x1, = _mm(f"attn_o_{i}", "nn", (T, D, D), (tmm, tD2, tDk), [(_op(o), _op(sq, NA + j), 0)], 1,
                      lambda accs, ex: [accs[0] + ex[0]], [((T, D), F32, None, None)], extras=[(xs, "tile", None, 0)])
            sv.update(h1=h1, q_hm=q_hm, k_hm=k_hm, v_hm=v_hm, qn=qn, kn=kn, bias=bias, sink_rows=sink_rows, o=o)
        h2 = _rms_fwd(f"rms_ffn_{i}", x1, small["norm_ffn"][i:i + 1], tr)
        a, b, f = _mm(f"ffn_up_{i}", "nn", (T, FF, D), (tmm, tF, tDk), [(_op(h2), _op(gu, i), 0), (_op(h2), _op(gu, L + i), 1)], 2,
                      _swiglu_fwd, [((T, FF), BF16, None, None)] * 3)
        x2, = _mm(f"ffn_down_{i}", "nn", (T, D, FF), (tmm, tD2, tFk), [(_op(f), _op(down, i), 0)], 1,
                  lambda accs, ex: [accs[0] + ex[0]], [((T, D), F32, None, None)], extras=[(x1, "tile", None, 0)])
        h3 = _rms_fwd(f"rms_ple_{i}", x2, small["norm_ple"][i:i + 1], tr)
        gt, x3 = _mm(f"ple_gate_{i}", "nn", (T, D, D), (tmm, tD2, tDk), [(_op(h3), _op(sq, NA + NC + i), 0)], 1, _ple_fwd,
                     [((T, D), F32, None, None), ((T, D), F32, None, None)],
                     extras=[(small["ple_b_gate"], "row", i, 0), (p, "rows", i, 0), (small["ple_w_proj"], "cols", i, 0), (x2, "tile", None, 0)])
        sv.update(x1=x1, h2=h2, a=sv.get("a"), fa=a, fb=b, f=f, x2=x2, h3=h3, gt=gt)
        saved.append(sv)
        xs = x3

    dx, loss = _loss_head(xs, target, tr)

    g_gu = g_down = g_cin = g_sq = g_qkv = None
    gs = {k: [None] * v.shape[0] for k, v in small.items() if k != "rel_bias"}
    gs["rel_bias"] = None
    gs["ple_w_proj"] = [None] * L

    for i in reversed(range(L)):
        kind, j = i % 3, i // 3
        sv = saved[i]
        dq, dz, dbg = _mm(f"ple_bwd_{i}", "nn", (T, D, PLE), (tmm, tD2, tP), [(_op(p, i), _op(small["ple_w_proj"], i), 0)], 1, _ple_bwd,
                          [((T, D), BF16, None, None), ((T, D), BF16, None, None), ((T // tmm, 1, D), F32, "rowsum", None)],
                          extras=[(dx, "tile", None, 0), (sv["gt"], "tile", None, 0)])
        gs["ple_b_gate"][i] = jnp.sum(dbg, axis=0)
        gs["ple_w_proj"][i], = _mm(f"d_ple_proj_{i}", "tn", (PLE, D, T), (tP, tD, tT), [(_op(p, i), _op(dq), 0)], 1, _first,
                                   [((PLE, D), F32, None, None)])
        g_sq, = _mm(f"d_ple_gate_{i}", "tn", (D, D, T), (tD, tD, tT), [(_op(sv["h3"]), _op(dz), 0)], 1, _first,
                    [(sq.shape, BF16, NA + NC + i, g_sq)])
        dh3, = _mm(f"dh_ple_{i}", "nt", (T, D, D), (tmm, tD2, tDk), [(_op(dz), _op(sq, NA + NC + i), 0)], 1, _first,
                   [((T, D), F32, None, None)])
        dx2, gs["norm_ple"][i], dx2b = _rms_bwd(f"rms_ple_bwd_{i}", dx, sv["x2"], small["norm_ple"][i:i + 1], dh3, tr, want_bf16=True)
        da, db = _mm(f"d_ffn_act_{i}", "nt", (T, FF, D), (tmm, tF, tDk), [(_op(dx2b), _op(down, i), 0)], 1, _swiglu_bwd,
                     [((T, FF), BF16, None, None)] * 2, extras=[(sv["fa"], "tile", None, 0), (sv["fb"], "tile", None, 0)])
        g_down, = _mm(f"d_ffn_down_{i}", "tn", (FF, D, T), (tFw, tD, tT), [(_op(sv["f"]), _op(dx2b), 0)], 1, _first,
                      [(down.shape, BF16, i, g_down)])
        g_gu, = _mm(f"d_ffn_gate_{i}", "tn", (D, FF, T), (tD, tFw, tT), [(_op(sv["h2"]), _op(da), 0)], 1, _first,
                    [(gu.shape, BF16, i, g_gu)])
        g_gu, = _mm(f"d_ffn_up_{i}", "tn", (D, FF, T), (tD, tFw, tT), [(_op(sv["h2"]), _op(db), 0)], 1, _first,
                    [(gu.shape, BF16, L + i, g_gu)])
        dh2, = _mm(f"dh_ffn_{i}", "nt", (T, D, FF), (tmm, tD2, tFk), [(_op(da), _op(gu, i), 0), (_op(db), _op(gu, L + i), 0)], 1, _first,
                   [((T, D), F32, None, None)])
        want_cs = kind == 0
        res = _rms_bwd(f"rms_ffn_bwd_{i}", dx2, sv["x1"], small["norm_ffn"][i:i + 1], dh2, tr, want_bf16=True, want_colsum=want_cs)
        dx1, gs["norm_ffn"][i], dx1b = res[:3]
        xin = sv["x"]
        if kind == 0:
            gs["conv_b_out"][j] = res[3]
            g_sq, = _mm(f"d_conv_out_{i}", "tn", (D, D, T), (tD, tD, tT), [(_op(sv["u4"]), _op(dx1b), 0)], 1, _first,
                        [(sq.shape, BF16, j, g_sq)])
            du4, = _mm(f"dh_conv_out_{i}", "nt", (T, D, D), (tmm, tD2, tDk), [(_op(dx1b), _op(sq, j), 0)], 1, _first,
                       [((T, D), F32, None, None)])
            du2, gs["conv_ln_g"][j], gs["conv_ln_b"][j], gs["conv_b_dw"][j] = _ln_silu_bwd(
                f"ln_silu_bwd_{i}", du4, sv["u2"], small["conv_ln_g"][j:j + 1], small["conv_ln_b"][j:j + 1], tr)
            dag, gs["conv_w_dw"][j], gs["conv_b_in"][j] = _dwconv_glu_bwd(
                f"dwconv_bwd_{i}", du2, sv["u1"], sv["a"], sv["gate"], small["conv_w_dw"][j], tr)
            g_cin, = _mm(f"d_conv_in_{i}", "tn", (D, 2 * D, T), (tD, tD, tT), [(_op(sv["h1"]), _op(dag), 0)], 1, _first,
                         [(cin.shape, BF16, j, g_cin)])
            dh1, = _mm(f"dh_conv_in_{i}", "nt", (T, D, 2 * D), (tmm, tD2, tDk), [(_op(dag), _op(cin, j), 0)], 1, _first,
                       [((T, D), F32, None, None)])
        elif kind == 1:
            dys, gs["pool_scale"][j] = _pool_scale_bwd(f"pool_scale_bwd_{i}", dx1, sv["y0"], small["pool_scale"][j:j + 1], tr)
            tg = _tile(GC, 512)
            kb = GC // tg
            dmix, = _mm(f"dh_pool_{i}", "nt", (T, D, GC), (tmm, GC, tg),
                        [(_op(dys, fn=lambda i_, j_, k_, kb=kb: (i_, j_ * kb + k_)), _op(sv["pw"]), 0)], 1, _first,
                        [((T, D), F32, None, None)])
            ng = len(POOL_WINDOWS)
            gs["pool_w"][j], = _mm(f"d_pool_w_{i}", "tn", (D, GC, T), (GC, GC, tT),
                                   [(_op(sv["mix"]), _op(dys, fn=lambda i_, j_, k_: (k_, i_)), 0)], 1, _first,
                                   [((D, GC), F32, None, None)])
            gs["pool_w"][j] = gs["pool_w"][j].reshape(ng, GC, GC)
            dh1 = _pool_bwd(f"pool_bwd_{i}", dmix, tr)
        else:
            g_sq, = _mm(f"d_attn_o_{i}", "tn", (D, D, T), (tD, tD, tT), [(_op(sv["o"]), _op(dx1b), 0)], 1, _first,
                        [(sq.shape, BF16, NA + j, g_sq)])
            do, = _mm(f"dh_attn_o_{i}", "nt", (T, D, D), (tmm, tD2, tDk), [(_op(dx1b), _op(sq, NA + j), 0)], 1, _first,
                      [((T, D), BF16, None, None)])
            do_hm = do.reshape(T, n_heads, HEAD_DIM).transpose(1, 0, 2)
            dqn, dko, dkp, dvo, dvp, dbias, dsink = _attn_bwd(sv["qn"], sv["kn"], sv["v_hm"], sv["bias"], sv["sink_rows"], do_hm, n_kv, group, T)
            th = _tile(T, 2048, 8)
            dq_hm, gs["attn_q_norm"][j] = _head_norm_bwd(f"qnorm_bwd_{i}", dqn.reshape(n_heads * T, HEAD_DIM), sv["q_hm"],
                                                         small["attn_q_norm"][j:j + 1], th)
            dk_hm, gs["attn_k_norm"][j] = _band_merge(f"knorm_bwd_{i}", dko, dkp, sv["k_hm"].reshape(n_kv, T, HEAD_DIM), small["attn_k_norm"][j:j + 1])
            dv_hm, = _band_merge(f"v_merge_{i}", dvo, dvp)
            gs["attn_sinks"][j] = jnp.sum(dsink.reshape(n_heads, QBLOCK), axis=1).reshape(1, n_heads)
            rb = _bias_bwd(dbias.reshape(n_heads, QBLOCK, 2 * QBLOCK), bucket, NUM_BUCKETS).T
            gs["rel_bias"] = rb if gs["rel_bias"] is None else gs["rel_bias"] + rb
            tok = lambda t_, nh: t_.reshape(nh, T, HEAD_DIM).transpose(1, 0, 2).reshape(T, nh * HEAD_DIM)
            dqkv = jnp.concatenate([tok(dq_hm, n_heads), tok(dk_hm, n_kv), tok(dv_hm, n_kv)], axis=1)
            g_qkv, = _mm(f"d_qkv_{i}", "tn", (D, QW, T), (tD, tQ, tT), [(_op(sv["h1"]), _op(dqkv), 0)], 1, _first,
                         [(qkv_w.shape, BF16, j, g_qkv)])
            dh1, = _mm(f"dh_qkv_{i}", "nt", (T, D, QW), (tmm, tD2, _tile(QW, 3072)), [(_op(dqkv), _op(qkv_w, j), 0)], 1, _first,
                       [((T, D), F32, None, None)])
        dx, gs["norm_mix"][i] = _rms_bwd(f"rms_mix_bwd_{i}", dx1, xin, small["norm_mix"][i:i + 1], dh1, tr)

    big = {"gu": g_gu, "down": g_down, "cin": g_cin, "sq": g_sq, "qkv": g_qkv}
    gsmall = {}
    for k, v in gs.items():
        if k == "rel_bias":
            gsmall[k] = v
        else:
            gsmall[k] = jnp.stack([t.reshape(small[k].shape[1:]) for t in v], axis=0)
    return loss, dx, big, gsmall


_ANY = pl.BlockSpec(memory_space=pl.ANY)


def _place():
    x, y, c = lax.axis_index("x"), lax.axis_index("y"), lax.axis_index("c")
    return x, y, c, [(1 - x, y), (x, 1 - y), (1 - x, 1 - y)]


def _lane_start(s, w):
    return pl.multiple_of(s * w, LANES) if w % LANES == 0 else s * w


def _slot(ref, kind, s):
    if kind == "col":
        w = ref.shape[2] // N_SLOTS
        return ref.at[:, :, pl.ds(_lane_start(s, w), w)]
    return ref.at[:, pl.ds(s, 1)]


def _rows_quarter(ref, h, quarter):
    n = ref.shape[-2] // 4
    if len(ref.shape) == 3:
        return ref.at[:, pl.ds((2 * h + quarter) * n, n), :]
    return ref.at[:, :, pl.ds((2 * h + quarter) * n, n), :]


def _rows_half(ref, h):
    n = ref.shape[-2] // 2
    if len(ref.shape) == 3:
        return ref.at[:, pl.ds(h * n, n), :]
    return ref.at[:, :, pl.ds(h * n, n), :]


def _place_own(name, shard, kind, s_arr):
    if kind == "col":
        lead, R, W = shard.shape
        full = (lead, R, N_SLOTS * W)
    else:
        lead, _, R, W = shard.shape
        full = (lead, N_SLOTS, R, W)
    tr = _tile(R, max(16, (1 << 19) // W), 16)
    if kind == "col":
        i_spec = pl.BlockSpec((None, tr, W), lambda l, i, s: (l, i, 0))
        o_spec = pl.BlockSpec((None, tr, W), lambda l, i, s: (l, i, s[0]))
    else:
        i_spec = pl.BlockSpec((None, None, tr, W), lambda l, i, s: (l, 0, i, 0))
        o_spec = pl.BlockSpec((None, None, tr, W), lambda l, i, s: (l, s[0], i, 0))

    def body(s_ref, i_ref, o_ref):
        o_ref[...] = i_ref[...]

    return pl.pallas_call(
        body, name=name,
        grid_spec=pltpu.PrefetchScalarGridSpec(num_scalar_prefetch=1, grid=(lead, R // tr), in_specs=[i_spec], out_specs=o_spec),
        out_shape=jax.ShapeDtypeStruct(full, shard.dtype), compiler_params=_cparams(("arbitrary", "arbitrary")),
    )(s_arr, shard)


def _gather(shards, fulls, kinds):
    ng = len(shards)
    n_sem = 8

    def body(*refs):
        sh, out = refs[:ng], refs[2 * ng:3 * ng]
        send, recv = refs[3 * ng:]
        x, y, c, _ = _place()
        s, s_x, s_y, s_d = 2 * x + y, 2 * (1 - x) + y, 2 * x + (1 - y), 2 * (1 - x) + (1 - y)
        to_x, to_y, sib = (1 - x, y, c), (x, 1 - y, c), (x, y, 1 - c)

        def rcopy(g, k, src, dst, dev):
            return pltpu.make_async_remote_copy(src_ref=src, dst_ref=dst, send_sem=send.at[g * n_sem + k], recv_sem=recv.at[g * n_sem + k],
                                                device_id=dev, device_id_type=MESH)

        def win(g, slot, h, quarter=None):
            w = _slot(out[g], kinds[g], slot)
            return _rows_half(w, h) if quarter is None else _rows_quarter(w, h, quarter)

        sent = []

        def go(cp):
            cp.start()
            sent.append(cp)

        for g in range(ng):
            go(rcopy(g, 0, _rows_half(sh[g], c), win(g, s, c), to_x))
            go(rcopy(g, 1, _rows_half(sh[g], c), win(g, s, c), to_y))
        for g in range(ng):
            rcopy(g, 0, win(g, s_x, c), win(g, s_x, c), to_x).wait_recv()
            go(rcopy(g, 3, win(g, s_x, c, 1), win(g, s_x, c, 1), to_y))
            go(rcopy(g, 4, win(g, s_x, c), win(g, s_x, c), sib))
            rcopy(g, 1, win(g, s_y, c), win(g, s_y, c), to_y).wait_recv()
            go(rcopy(g, 2, win(g, s_y, c, 0), win(g, s_y, c, 0), to_x))
            go(rcopy(g, 5, win(g, s_y, c), win(g, s_y, c), sib))
        for g in range(ng):
            for k, quarter in ((2, 0), (3, 1)):
                rcopy(g, k, win(g, s_d, c, quarter), win(g, s_d, c, quarter), sib).wait_recv()
                go(rcopy(g, 6 + quarter, win(g, s_d, c, quarter), win(g, s_d, c, quarter), sib))
        for g in range(ng):
            rcopy(g, 4, win(g, s_x, 1 - c), win(g, s_x, 1 - c), sib).wait_recv()
            rcopy(g, 5, win(g, s_y, 1 - c), win(g, s_y, 1 - c), sib).wait_recv()
            for quarter in (0, 1):
                rcopy(g, 6 + quarter, win(g, s_d, 1 - c, quarter), win(g, s_d, 1 - c, quarter), sib).wait_recv()
        for cp in sent:
            cp.wait_send()

    return pl.pallas_call(
        body, name="gather_weights", in_specs=[_ANY] * (2 * ng), out_specs=[_ANY] * ng,
        out_shape=[jax.ShapeDtypeStruct(a.shape, a.dtype) for a in fulls],
        input_output_aliases={ng + g: g for g in range(ng)},
        scratch_shapes=[pltpu.SemaphoreType.DMA((n_sem * ng,)), pltpu.SemaphoreType.DMA((n_sem * ng,))],
    )(*shards, *fulls)


def _pair_send(grads):
    ng = len(grads)

    def half_shape(a):
        s = list(a.shape)
        s[-2] //= 2
        return tuple(s)

    def body(*refs):
        gr, out = refs[:ng], refs[ng:2 * ng]
        send, recv = refs[2 * ng:]
        x, y, c, _ = _place()
        cps = [pltpu.make_async_remote_copy(src_ref=_rows_half(gr[g], 1 - c), dst_ref=out[g], send_sem=send.at[g], recv_sem=recv.at[g],
                                            device_id=(x, y, 1 - c), device_id_type=MESH) for g in range(ng)]
        for cp in cps:
            cp.start()
        for cp in cps:
            cp.wait()

    return pl.pallas_call(
        body, name="grad_pair_send", in_specs=[_ANY] * ng, out_specs=[_ANY] * ng,
        out_shape=[jax.ShapeDtypeStruct(half_shape(a), a.dtype) for a in grads],
        scratch_shapes=[pltpu.SemaphoreType.DMA((ng,)), pltpu.SemaphoreType.DMA((ng,))],
    )(*grads)


def _add_half(name, g3, pa3, c_arr):
    n, R, N = g3.shape
    rh = R // 2
    tr = _tile(rh, max(16, (1 << 19) // N), 16)
    nb = rh // tr

    def body(c_ref, g_ref, p_ref, o_ref):
        o_ref[...] = (g_ref[...].astype(F32) + p_ref[...].astype(F32)).astype(o_ref.dtype)

    return pl.pallas_call(
        body, name=name,
        grid_spec=pltpu.PrefetchScalarGridSpec(
            num_scalar_prefetch=1, grid=(n, nb),
            in_specs=[pl.BlockSpec((None, tr, N), lambda l, i, c, nb=nb: (l, c[0] * nb + i, 0)), pl.BlockSpec((None, tr, N), lambda l, i, c: (l, i, 0))],
            out_specs=pl.BlockSpec((None, tr, N), lambda l, i, c: (l, i, 0))),
        out_shape=jax.ShapeDtypeStruct(pa3.shape, g3.dtype), compiler_params=_cparams(("arbitrary", "arbitrary")),
    )(c_arr, g3, pa3)


def _ici_exchange(psums, kinds):
    ng = len(psums)

    def recv_shape(a, kind):
        s = a.shape
        return (3, s[0], s[1], s[2] // N_SLOTS) if kind == "col" else (3, s[0], 1, s[2], s[3])

    def body(*refs):
        ps, out = refs[:ng], refs[ng:2 * ng]
        send, recv = refs[2 * ng:]
        x, y, c, chips = _place()
        cps = []
        for j, (cx, cy) in enumerate(chips):
            for g in range(ng):
                cps.append(pltpu.make_async_remote_copy(src_ref=_slot(ps[g], kinds[g], 2 * cx + cy), dst_ref=out[g].at[j], send_sem=send.at[g * 3 + j],
                                                        recv_sem=recv.at[g * 3 + j], device_id=(cx, cy, c), device_id_type=MESH))
                cps[-1].start()
        for cp in cps:
            cp.wait()

    return pl.pallas_call(
        body, name="grad_ici_exchange", in_specs=[_ANY] * ng, out_specs=[_ANY] * ng,
        out_shape=[jax.ShapeDtypeStruct(recv_shape(a, k), a.dtype) for a, k in zip(psums, kinds)],
        scratch_shapes=[pltpu.SemaphoreType.DMA((3 * ng,)), pltpu.SemaphoreType.DMA((3 * ng,))],
    )(*psums)


def _sum4(name, p3, rc3, s_arr, lead, kind):
    R = p3.shape[1]
    W = rc3.shape[2]
    tr = _tile(R, max(16, (1 << 18) // W), 16)
    if kind == "col":
        p_spec = pl.BlockSpec((None, tr, W), lambda l, i, s: (l, i, s[0]))
    else:
        p_spec = pl.BlockSpec((None, tr, W), lambda l, i, s: (l * N_SLOTS + s[0], i, 0))
    r_specs = [pl.BlockSpec((None, tr, W), lambda l, i, s, j=j: (j * lead + l, i, 0)) for j in range(3)]

    def body(s_ref, p_ref, r0, r1, r2, o_ref):
        o_ref[...] = ((p_ref[...].astype(F32) + r0[...].astype(F32)) + r1[...].astype(F32)) + r2[...].astype(F32)

    return pl.pallas_call(
        body, name=name,
        grid_spec=pltpu.PrefetchScalarGridSpec(num_scalar_prefetch=1, grid=(lead, R // tr), in_specs=[p_spec] + r_specs,
                                               out_specs=pl.BlockSpec((None, tr, W), lambda l, i, s: (l, i, 0))),
        out_shape=jax.ShapeDtypeStruct((lead, R, W), F32), compiler_params=_cparams(("arbitrary", "arbitrary")),
    )(s_arr, p3, rc3, rc3, rc3)


def _pair_swap(halves):
    ng = len(halves)

    def body(*refs):
        hv, out = refs[:ng], refs[ng:2 * ng]
        send, recv = refs[2 * ng:]
        x, y, c, _ = _place()
        cps = [pltpu.make_async_remote_copy(src_ref=hv[g], dst_ref=out[g], send_sem=send.at[g], recv_sem=recv.at[g],
                                            device_id=(x, y, 1 - c), device_id_type=MESH) for g in range(ng)]
        for cp in cps:
            cp.start()
        for cp in cps:
            cp.wait()

    return pl.pallas_call(
        body, name="grad_pair_swap", in_specs=[_ANY] * ng, out_specs=[_ANY] * ng,
        out_shape=[jax.ShapeDtypeStruct(a.shape, a.dtype) for a in halves],
        scratch_shapes=[pltpu.SemaphoreType.DMA((ng,)), pltpu.SemaphoreType.DMA((ng,))],
    )(*halves)


N_DEVICES = 8


def _allreduce_small(v):
    rows, m = v.shape

    def body(v_ref, o_ref, buf, send, recv):
        x, y, c, _ = _place()
        me = 4 * x + 2 * y + c
        buf[me] = v_ref[...]
        cps = []
        for k in range(1, N_DEVICES):
            peer = me ^ k
            cps.append(pltpu.make_async_remote_copy(src_ref=v_ref, dst_ref=buf.at[me], send_sem=send.at[k - 1], recv_sem=recv.at[k - 1],
                                                    device_id=((peer >> 2) & 1, (peer >> 1) & 1, peer & 1), device_id_type=MESH))
            cps[-1].start()
        for k in range(1, N_DEVICES):
            theirs = buf.at[me ^ k]
            pltpu.make_async_remote_copy(src_ref=v_ref, dst_ref=theirs, send_sem=send.at[k - 1], recv_sem=recv.at[k - 1],
                                         device_id=(x, y, c), device_id_type=MESH).wait_recv()
        for cp in cps:
            cp.wait_send()
        acc = buf[0]
        for d in range(1, N_DEVICES):
            acc = acc + buf[d]
        o_ref[...] = acc

    vm = pl.BlockSpec(memory_space=pltpu.VMEM)
    return pl.pallas_call(
        body, name="allreduce_small", in_specs=[vm], out_specs=vm, out_shape=jax.ShapeDtypeStruct(v.shape, F32),
        scratch_shapes=[pltpu.VMEM((N_DEVICES, rows, m), F32), pltpu.SemaphoreType.DMA((N_DEVICES - 1,)), pltpu.SemaphoreType.DMA((N_DEVICES - 1,))],
    )(v)


def _pad_rows(a, mult):
    r = (-a.shape[0]) % mult
    return a if r == 0 else jnp.concatenate([a, jnp.zeros((r,) + a.shape[1:], a.dtype)], axis=0)


def _pack_rows(parts, width, mult=16):
    rows, offs, at = [], [], 0
    for a in parts:
        a2 = _pad_rows(a.reshape(-1, width), mult)
        offs.append((at, a.size // width))
        rows.append(a2)
        at += a2.shape[0]
    return _pad_rows(jnp.concatenate(rows, axis=0), 4 * SUBLANES), offs


SMALL_SHARDED = ("ple_w_proj", "pool_w", "conv_w_dw", "conv_b_dw", "conv_ln_g", "conv_ln_b", "conv_b_out", "conv_b_in")
SMALL_REPLICATED = ("norm_mix", "norm_ffn", "norm_ple", "pool_scale", "attn_q_norm", "attn_k_norm", "attn_sinks", "rel_bias", "ple_b_gate")


def _small_to_full(name, slots):
    if name == "pool_w":
        return jnp.moveaxis(slots, 0, 2).reshape(slots.shape[1], slots.shape[2], N_SLOTS * slots.shape[3], slots.shape[4])
    return jnp.moveaxis(slots, 0, -2).reshape(slots.shape[1:-1] + (N_SLOTS * slots.shape[-1],))


def _small_to_slots(name, full):
    if name == "pool_w":
        nb, ng, gc, _ = full.shape
        return jnp.moveaxis(full.reshape(nb, ng, N_SLOTS, gc // N_SLOTS, gc), 2, 0)
    w = full.shape[-1] // N_SLOTS
    return jnp.moveaxis(full.reshape(full.shape[:-1] + (N_SLOTS, w)), -2, 0)


W_NAMES = ("norm_mix", "norm_ffn", "norm_ple", "conv_w_in", "conv_b_in", "conv_w_dw", "conv_b_dw", "conv_ln_g", "conv_ln_b", "conv_w_out",
           "conv_b_out", "pool_w", "pool_scale", "attn_w_qkv", "attn_q_norm", "attn_k_norm", "attn_sinks", "attn_w_o", "rel_bias",
           "ffn_w_gate", "ffn_w_up", "ffn_w_down", "ple_w_proj", "ple_w_gate", "ple_b_gate")


def _step(x, p, target, w, m, v):
    T, D = x.shape[1], x.shape[2]
    L = p.shape[0]
    NA, NC = w["conv_w_in"].shape[0], w["attn_w_qkv"].shape[0]
    xi, yi, ci = lax.axis_index("x"), lax.axis_index("y"), lax.axis_index("c")
    c_arr = jnp.reshape(ci, (1,)).astype(jnp.int32)
    s_arr = jnp.reshape(2 * xi + yi, (1,)).astype(jnp.int32)

    wq = D // N_SLOTS
    sm_pack, sm_offs = _pack_rows([w[k] for k in SMALL_SHARDED], wq)
    shards = [
        jnp.concatenate([w["ffn_w_gate"], w["ffn_w_up"]], axis=0).astype(BF16),
        w["ffn_w_down"].astype(BF16)[:, None],
        w["conv_w_in"].astype(BF16),
        jnp.concatenate([w["conv_w_out"], w["attn_w_o"], w["ple_w_gate"]], axis=0).astype(BF16)[:, None],
        w["attn_w_qkv"].astype(BF16),
        sm_pack[None, None],
    ]
    kinds = ["col", "row", "col", "row", "col", "row"]
    full = _gather(shards, [_place_own(f"place_own_{g}", a, k, s_arr) for g, (a, k) in enumerate(zip(shards, kinds))], kinds)
    wts = {"gu": full[0], "down": full[1].reshape(L, -1, D), "cin": full[2], "sq": full[3].reshape(NA + NC + L, D, D), "qkv": full[4]}
    small = {k: w[k] for k in SMALL_REPLICATED}
    for k, (at, n) in zip(SMALL_SHARDED, sm_offs):
        small[k] = _small_to_full(k, full[5][0, :, at:at + n].reshape((N_SLOTS,) + w[k].shape))

    loss, dx, big, gsmall = _local_step(x[0], p[:, 0], target[0], wts, small)

    rep_parts = [gsmall[k] for k in SMALL_REPLICATED] + [loss]
    flat = jnp.concatenate([a.reshape(-1) for a in rep_parts])
    n_flat = flat.shape[0]
    m_cols = -(-n_flat // (8 * LANES)) * LANES
    flat = jnp.concatenate([flat, jnp.zeros((8 * m_cols - n_flat,), F32)]).reshape(8, m_cols)
    red = _allreduce_small(flat).reshape(-1)
    grads, at = {}, 0
    for k in SMALL_REPLICATED:
        grads[k] = red[at:at + w[k].size].reshape(w[k].shape)
        at += w[k].size
    loss_out = red[at]

    slots = {k: _small_to_slots(k, gsmall[k]) for k in SMALL_SHARDED}
    gsm = jnp.stack([_pack_rows([slots[k][s] for k in SMALL_SHARDED], wq)[0] for s in range(N_SLOTS)], axis=0)
    local = [big["gu"], big["down"].reshape(L, N_SLOTS, -1, D), big["cin"], big["sq"].reshape(NA + NC + L, N_SLOTS, -1, D), big["qkv"],
             gsm[None]]
    theirs = _pair_send(local)
    psums = []
    for g, (a, t) in enumerate(zip(local, theirs)):
        if kinds[g] == "col":
            psums.append(_add_half(f"pair_add_{g}", a, t, c_arr))
        else:
            n4 = a.shape[0] * N_SLOTS
            psums.append(_add_half(f"pair_add_{g}", a.reshape(n4, a.shape[2], a.shape[3]), t.reshape(n4, t.shape[2], t.shape[3]), c_arr).reshape(t.shape))
    got = _ici_exchange(psums, kinds)
    halves = []
    for g, (ps, rc) in enumerate(zip(psums, got)):
        lead = ps.shape[0]
        if kinds[g] == "col":
            halves.append(_sum4(f"slot_sum_{g}", ps, rc.reshape(3 * lead, rc.shape[2], rc.shape[3]), s_arr, lead, "col"))
        else:
            halves.append(_sum4(f"slot_sum_{g}", ps.reshape(lead * N_SLOTS, ps.shape[2], ps.shape[3]), rc.reshape(3 * lead, rc.shape[3], rc.shape[4]),
                                s_arr, lead, "row"))
    first = ci == 0
    gsh = [jnp.concatenate([jnp.where(first, a, b), jnp.where(first, b, a)], axis=1) for a, b in zip(halves, _pair_swap(halves))]
    grads["ffn_w_gate"], grads["ffn_w_up"] = gsh[0][:L], gsh[0][L:]
    grads["ffn_w_down"] = gsh[1]
    grads["conv_w_in"] = gsh[2]
    grads["conv_w_out"], grads["attn_w_o"], grads["ple_w_gate"] = gsh[3][:NA], gsh[3][NA:NA + NC], gsh[3][NA + NC:]
    grads["attn_w_qkv"] = gsh[4]
    for k, (at, n) in zip(SMALL_SHARDED, sm_offs):
        grads[k] = gsh[5][0, at:at + n].reshape(w[k].shape)

    outs_d, outs_m, outs_v = [], [], []
    for k in W_NAMES:
        d_, m_, v_ = _adamw(f"adamw_{k}", w[k], grads[k], m[k], v[k])
        outs_d.append(d_)
        outs_m.append(m_)
        outs_v.append(v_)
    return (loss_out, dx[None], *[grads[k] for k in W_NAMES], *outs_d, *outs_m, *outs_v)


def kernel(x, p, norm_mix, norm_ffn, norm_ple, conv_w_in, conv_b_in, conv_w_dw, conv_b_dw, conv_ln_g, conv_ln_b, conv_w_out, conv_b_out, pool_w, pool_scale, attn_w_qkv, attn_q_norm, attn_k_norm, attn_sinks, attn_w_o, rel_bias, ffn_w_gate, ffn_w_up, ffn_w_down, ple_w_proj, ple_w_gate, ple_b_gate, loss_target, m_norm_mix, m_norm_ffn, m_norm_ple, m_conv_w_in, m_conv_b_in, m_conv_w_dw, m_conv_b_dw, m_conv_ln_g, m_conv_ln_b, m_conv_w_out, m_conv_b_out, m_pool_w, m_pool_scale, m_attn_w_qkv, m_attn_q_norm, m_attn_k_norm, m_attn_sinks, m_attn_w_o, m_rel_bias, m_ffn_w_gate, m_ffn_w_up, m_ffn_w_down, m_ple_w_proj, m_ple_w_gate, m_ple_b_gate, v_norm_mix, v_norm_ffn, v_norm_ple, v_conv_w_in, v_conv_b_in, v_conv_w_dw, v_conv_b_dw, v_conv_ln_g, v_conv_ln_b, v_conv_w_out, v_conv_b_out, v_pool_w, v_pool_scale, v_attn_w_qkv, v_attn_q_norm, v_attn_k_norm, v_attn_sinks, v_attn_w_o, v_rel_bias, v_ffn_w_gate, v_ffn_w_up, v_ffn_w_down, v_ple_w_proj, v_ple_w_gate, v_ple_b_gate):
    ws_ = (norm_mix, norm_ffn, norm_ple, conv_w_in, conv_b_in, conv_w_dw, conv_b_dw, conv_ln_g, conv_ln_b, conv_w_out, conv_b_out, pool_w, pool_scale, attn_w_qkv, attn_q_norm, attn_k_norm, attn_sinks, attn_w_o, rel_bias, ffn_w_gate, ffn_w_up, ffn_w_down, ple_w_proj, ple_w_gate, ple_b_gate)
    ms_ = (m_norm_mix, m_norm_ffn, m_norm_ple, m_conv_w_in, m_conv_b_in, m_conv_w_dw, m_conv_b_dw, m_conv_ln_g, m_conv_ln_b, m_conv_w_out, m_conv_b_out, m_pool_w, m_pool_scale, m_attn_w_qkv, m_attn_q_norm, m_attn_k_norm, m_attn_sinks, m_attn_w_o, m_rel_bias, m_ffn_w_gate, m_ffn_w_up, m_ffn_w_down, m_ple_w_proj, m_ple_w_gate, m_ple_b_gate)
    vs_ = (v_norm_mix, v_norm_ffn, v_norm_ple, v_conv_w_in, v_conv_b_in, v_conv_w_dw, v_conv_b_dw, v_conv_ln_g, v_conv_ln_b, v_conv_w_out, v_conv_b_out, v_pool_w, v_pool_scale, v_attn_w_qkv, v_attn_q_norm, v_attn_k_norm, v_attn_sinks, v_attn_w_o, v_rel_bias, v_ffn_w_gate, v_ffn_w_up, v_ffn_w_down, v_ple_w_proj, v_ple_w_gate, v_ple_b_gate)
    return _step(x, p, loss_target, dict(zip(W_NAMES, ws_)), dict(zip(W_NAMES, ms_)), dict(zip(W_NAMES, vs_)))
```

```python
import functools
import math

import jax
import jax.numpy as jnp
import numpy as np
from jax import lax
from jax.experimental import pallas as pl
from jax.experimental.pallas import tpu as pltpu

F32 = jnp.float32
BF16 = jnp.bfloat16
MESH = pl.DeviceIdType.MESH

CHUNK = 64
CONV_WIDTH = 31
POOL_WINDOWS = (2, 4, 8, 16)
HEAD_DIM = 64
WINDOW_CHUNKS = 2
QBLOCK = 128
NUM_BUCKETS = 32
REL_MAX_DIST = 128
EPS = 1e-6
NEG_INF = -1e30
ADAM_LR, ADAM_B1, ADAM_B2, ADAM_EPS, ADAM_WD, ADAM_STEP = 0.001, 0.9, 0.999, 1e-08, 0.01, 10
N_SLOTS = 4
LANES = 128
VMEM_LIMIT_BYTES = 56 * 1024 * 1024


def _cparams(sem):
    return pltpu.CompilerParams(dimension_semantics=sem, vmem_limit_bytes=VMEM_LIMIT_BYTES)


def _tile(n, pref, mult=LANES):
    if n <= pref:
        return n
    t = (pref // mult) * mult
    while t >= mult:
        if n % t == 0:
            return t
        t -= mult
    return n


def _sig(z):
    return 1.0 / (1.0 + jnp.exp(-z))


def _op(arr, lead=None, ro=0, co=0, fn=None):
    return (arr, lead, ro, co, fn)


_DOT_DIMS = {"nn": (((1,), (0,)), ((), ())), "nt": (((1,), (1,)), ((), ())), "tn": (((0,), (0,)), ((), ()))}


def _mm(name, mode, dims, tiles, terms, n_acc, epilogue, outs, extras=()):
    M, N, K = dims
    tm, tn, tk = tiles
    assert M % tm == 0 and N % tn == 0 and K % tk == 0, (name, dims, tiles)
    nk = K // tk
    a_tile = (tk, tm) if mode == "tn" else (tm, tk)
    b_tile = (tn, tk) if mode == "nt" else (tk, tn)
    a_fn = (lambda i, j, k: (k, i)) if mode == "tn" else (lambda i, j, k: (i, k))
    b_fn = (lambda i, j, k: (j, k)) if mode == "nt" else (lambda i, j, k: (k, j))
    dn = _DOT_DIMS[mode]

    operands, specs, seen = [], [], {}

    def add(op, tshape, default_fn):
        arr, lead, ro, co, fn = op
        fn = fn or default_fn
        key = (id(arr), lead, ro, co, id(fn) if op[4] is not None else None, tshape)
        if key in seen:
            return seen[key]

        def imap(i, j, k, fn=fn, lead=lead, ro=ro, co=co):
            r, c = fn(i, j, k)
            return (r + ro, c + co) if lead is None else (lead, r + ro, c + co)

        operands.append(arr)
        specs.append(pl.BlockSpec(tshape if lead is None else (None,) + tshape, imap))
        seen[key] = len(operands) - 1
        return seen[key]

    term_idx = [(add(a, a_tile, a_fn), add(b, b_tile, b_fn), acc) for a, b, acc in terms]
    extra_idx = []
    for arr, kind, lead, co in extras:
        if kind == "tile":
            extra_idx.append(add(_op(arr, lead, 0, co), (tm, tn), lambda i, j, k: (i, j)))
        elif kind == "row":
            arr3 = arr.reshape(arr.shape[0], 1, arr.shape[1])
            extra_idx.append(add(_op(arr3, 0 if lead is None else lead, 0, co), (1, tn), lambda i, j, k: (0, j)))
        elif kind == "rows":
            extra_idx.append(add(_op(arr, lead, 0, 0), (tm, arr.shape[-1]), lambda i, j, k: (i, 0)))
        elif kind == "cols":
            extra_idx.append(add(_op(arr, lead, 0, co), (arr.shape[-2], tn), lambda i, j, k: (0, j)))
        else:
            extra_idx.append(add(_op(arr, None, 0, 0), (tm, 1), lambda i, j, k: (i, 0)))
    n_in = len(operands)
    out_shapes, out_specs, aliases = [], [], {}
    for oi, (shape, dtype, lead, alias) in enumerate(outs):
        out_shapes.append(jax.ShapeDtypeStruct(shape, dtype))
        if lead == "rowsum":
            out_specs.append(pl.BlockSpec((None, 1, tn), lambda i, j, k: (i, 0, j)))
        elif lead is None:
            out_specs.append(pl.BlockSpec((tm, tn), lambda i, j, k: (i, j)))
        else:
            out_specs.append(pl.BlockSpec((None, tm, tn), lambda i, j, k, lead=lead: (lead, i, j)))
        if alias is not None:
            operands.append(alias)
            specs.append(pl.BlockSpec(memory_space=pl.ANY))
            aliases[len(operands) - 1] = oi
    n_all_in = len(operands)
    n_out = len(outs)

    def body(*refs):
        ins = refs[:n_in]
        o_refs = refs[n_all_in:n_all_in + n_out]
        accs = refs[n_all_in + n_out:]

        def dots():
            sums = [None] * n_acc
            for ai, bi, acc_i in term_idx:
                a = ins[ai][...]
                b = ins[bi][...]
                if a.dtype != BF16:
                    a = a.astype(BF16)
                if b.dtype != BF16:
                    b = b.astype(BF16)
                d = lax.dot_general(a, b, dn, preferred_element_type=F32)
                sums[acc_i] = d if sums[acc_i] is None else sums[acc_i] + d
            return sums

        def finish(vals):
            res = epilogue(vals, [ins[e][...] for e in extra_idx])
            for o, r in zip(o_refs, res):
                o[...] = r.astype(o.dtype)

        if nk == 1:
            finish(dots())
            return
        k = pl.program_id(2)

        @pl.when(k == 0)
        def _():
            for acc, d in zip(accs, dots()):
                acc[...] = d

        if nk > 2:
            @pl.when((k > 0) & (k < nk - 1))
            def _():
                for acc, d in zip(accs, dots()):
                    acc[...] += d

        @pl.when(k == nk - 1)
        def _():
            finish([acc[...] + d for acc, d in zip(accs, dots())])

    res = pl.pallas_call(
        body, name=name, grid=(M // tm, N // tn, nk), in_specs=specs, out_specs=out_specs, out_shape=out_shapes,
        scratch_shapes=[pltpu.VMEM((tm, tn), F32) for _ in range(n_acc if nk > 1 else 0)], input_output_aliases=aliases,
        compiler_params=_cparams(("parallel", "parallel", "arbitrary")),
    )(*operands)
    return res


def _rowk(name, T, tm, ins, outs, body, scratch=()):
    assert T % tm == 0, (name, T, tm)
    n = T // tm
    specs = []
    for arr, kind in ins:
        w = arr.shape[-1]
        if kind == "tile":
            specs.append(pl.BlockSpec((tm, w), lambda i: (i, 0)))
        elif kind == "prev":
            specs.append(pl.BlockSpec((tm, w), lambda i: (jnp.maximum(i - 1, 0), 0)))
        elif kind == "next":
            specs.append(pl.BlockSpec((tm, w), lambda i, n=n: (jnp.minimum(i + 1, n - 1), 0)))
        else:
            specs.append(pl.BlockSpec(arr.shape, lambda i, nd=arr.ndim: (0,) * nd))
    out_shapes, out_specs = [], []
    for shape, dtype, kind in outs:
        out_shapes.append(jax.ShapeDtypeStruct(shape, dtype))
        if kind == "tile":
            out_specs.append(pl.BlockSpec((tm, shape[-1]), lambda i: (i, 0)))
        else:
            out_specs.append(pl.BlockSpec(shape, lambda i, nd=len(shape): (0,) * nd))
    n_in, n_out = len(ins), len(outs)

    def kbody(*refs):
        body(pl.program_id(0), n, refs[:n_in], refs[n_in:n_in + n_out], refs[n_in + n_out:])

    return pl.pallas_call(
        kbody, name=name, grid=(n,), in_specs=specs, out_specs=out_specs, out_shape=out_shapes,
        scratch_shapes=list(scratch), compiler_params=_cparams(("arbitrary",)),
    )(*[a for a, _ in ins])


def _accum(ref, i, val):
    @pl.when(i == 0)
    def _():
        ref[...] = val

    @pl.when(i > 0)
    def _():
        ref[...] += val


def _colsum(v):
    return jnp.sum(v, axis=0, keepdims=True)


def _rms_r(x):
    return lax.rsqrt(jnp.mean(x * x, axis=-1, keepdims=True) + EPS)


def _rms_fwd(name, x, g, tm):
    T, D = x.shape

    def body(i, n, ins, outs, scr):
        xv = ins[0][...]
        outs[0][...] = (xv * _rms_r(xv) * ins[1][...]).astype(BF16)

    return _rowk(name, T, tm, [(x, "tile"), (g, "full")], [((T, D), BF16, "tile")], body)[0]


def _rms_bwd(name, dres, x, g, dh, tm, want_bf16=False, want_colsum=False):
    T, D = x.shape

    def body(i, n, ins, outs, scr):
        xv = ins[1][...]
        gv = ins[2][...]
        dhv = ins[3][...].astype(F32)
        r = _rms_r(xv)
        xh = xv * r
        dhg = dhv * gv
        dx = ins[0][...] + r * (dhg - xh * jnp.mean(dhg * xh, axis=-1, keepdims=True))
        outs[0][...] = dx
        _accum(outs[1], i, _colsum(dhv * xh))
        o = 2
        if want_bf16:
            outs[o][...] = dx.astype(BF16)
            o += 1
        if want_colsum:
            _accum(outs[o], i, _colsum(dx))

    outs = [((T, D), F32, "tile"), ((1, D), F32, "acc")]
    if want_bf16:
        outs.append(((T, D), BF16, "tile"))
    if want_colsum:
        outs.append(((1, D), F32, "acc"))
    return _rowk(name, T, tm, [(dres, "tile"), (x, "tile"), (g, "full"), (dh, "tile")], outs, body)


def _loss_head(y, target, tm):
    T, D = y.shape

    def body(i, n, ins, outs, scr):
        d = ins[0][...] - ins[1][...]
        outs[0][...] = d * (1.0 / D)
        _accum(outs[1], i, jnp.sum(_colsum(d * d), axis=1, keepdims=True) * (0.5 / D))

    return _rowk("loss_head", T, tm, [(y, "tile"), (target, "tile")], [((T, D), F32, "tile"), ((1, 1), F32, "acc")], body)


CONV_ROWS = 128
SUBLANES = 8
_PHASE_PAD = 32


def _phase_scratch(tm):
    return pltpu.VMEM((SUBLANES, tm + _PHASE_PAD, LANES), F32)


def _phase_copies(buf, shf, base, l0, tm):
    n = tm + _PHASE_PAD - SUBLANES
    for r in range(1, SUBLANES):
        shf[r, pl.ds(0, n), :] = buf[pl.ds(base + r, n), pl.ds(l0, LANES)]


def _phase_window(buf, shf, base, q, row0, rows, l0):
    r = q % SUBLANES
    a = q - r
    if r == 0:
        return buf[pl.ds(base + a + row0, rows), pl.ds(l0, LANES)]
    return shf[r, pl.ds(a + row0, rows), :]


def _dwconv_fwd(name, u1, w_dw, b_dw, tm):
    T, D = u1.shape
    rc_n = tm // CONV_ROWS if tm >= CONV_ROWS else 1
    rows = min(CONV_ROWS, tm)
    halo = CONV_WIDTH - 1

    base = tm - _PHASE_PAD

    def body(i, n, ins, outs, scr):
        buf, shf = scr
        buf[pl.ds(0, tm), :] = jnp.where(i > 0, ins[0][...], 0.0)
        buf[pl.ds(tm, tm), :] = ins[1][...]
        w_ref, b_ref, o_ref = ins[2], ins[3], outs[0]

        def chunk(lc, carry):
            l0 = pl.multiple_of(lc * LANES, LANES)
            _phase_copies(buf, shf, base, l0, tm)
            for rc in range(rc_n):
                acc = jnp.zeros((rows, LANES), F32) + b_ref[:, pl.ds(l0, LANES)]
                for k in range(CONV_WIDTH):
                    acc = acc + _phase_window(buf, shf, base, k + _PHASE_PAD - halo, rc * rows, rows, l0) * w_ref[pl.ds(k, 1), pl.ds(l0, LANES)]
                o_ref[pl.ds(rc * rows, rows), pl.ds(l0, LANES)] = acc
            return carry

        lax.fori_loop(0, D // LANES, chunk, 0)

    return _rowk(name, T, tm, [(u1, "prev"), (u1, "tile"), (w_dw, "full"), (b_dw, "full")], [((T, D), F32, "tile")], body,
                 scratch=[pltpu.VMEM((2 * tm, D), F32), _phase_scratch(tm)])[0]


def _ln_silu_fwd(name, u2, g, b, tm):
    T, D = u2.shape

    def body(i, n, ins, outs, scr):
        v = ins[0][...]
        mu = jnp.mean(v, axis=-1, keepdims=True)
        xc = v - mu
        y = xc * lax.rsqrt(jnp.mean(xc * xc, axis=-1, keepdims=True) + EPS) * ins[1][...] + ins[2][...]
        outs[0][...] = (y * _sig(y)).astype(BF16)

    return _rowk(name, T, tm, [(u2, "tile"), (g, "full"), (b, "full")], [((T, D), BF16, "tile")], body)[0]


def _ln_silu_bwd(name, du4, u2, g, b, tm):
    T, D = u2.shape

    def body(i, n, ins, outs, scr):
        v = ins[1][...]
        gv = ins[2][...]
        mu = jnp.mean(v, axis=-1, keepdims=True)
        xc = v - mu
        r = lax.rsqrt(jnp.mean(xc * xc, axis=-1, keepdims=True) + EPS)
        xh = xc * r
        y = xh * gv + ins[3][...]
        s = _sig(y)
        dy = ins[0][...] * (s * (1.0 + y * (1.0 - s)))
        dyg = dy * gv
        du2 = r * (dyg - jnp.mean(dyg, axis=-1, keepdims=True) - xh * jnp.mean(dyg * xh, axis=-1, keepdims=True))
        outs[0][...] = du2
        _accum(outs[1], i, _colsum(dy * xh))
        _accum(outs[2], i, _colsum(dy))
        _accum(outs[3], i, _colsum(du2))

    return _rowk(name, T, tm, [(du4, "tile"), (u2, "tile"), (g, "full"), (b, "full")],
                 [((T, D), F32, "tile"), ((1, D), F32, "acc"), ((1, D), F32, "acc"), ((1, D), F32, "acc")], body)


def _dwconv_glu_bwd(name, du2, u1, a_, gate, w_dw, tm):
    T, D = u1.shape
    rc_n = tm // CONV_ROWS if tm >= CONV_ROWS else 1
    rows = min(CONV_ROWS, tm)
    halo = CONV_WIDTH - 1

    base = tm - _PHASE_PAD

    def body(i, n, ins, outs, scr):
        bu, bd, shu, shd = scr
        bd[pl.ds(0, tm), :] = ins[0][...]
        bd[pl.ds(tm, tm), :] = jnp.where(i < n - 1, ins[1][...], 0.0)
        bu[pl.ds(0, tm), :] = jnp.where(i > 0, ins[2][...], 0.0)
        bu[pl.ds(tm, tm), :] = ins[3][...]
        a_ref, g_ref, w_ref = ins[4], ins[5], ins[6]
        dag_ref, dw_ref, db_ref = outs

        @pl.when(i == 0)
        def _():
            dw_ref[...] = jnp.zeros_like(dw_ref)
            db_ref[...] = jnp.zeros_like(db_ref)

        def chunk(lc, carry):
            l0 = pl.multiple_of(lc * LANES, LANES)
            l1 = pl.multiple_of(D + lc * LANES, LANES)
            _phase_copies(bd, shd, 0, l0, tm)
            _phase_copies(bu, shu, base, l0, tm)
            for rc in range(rc_n):
                r0 = rc * rows
                d_here = bd[pl.ds(r0, rows), pl.ds(l0, LANES)]
                acc = jnp.zeros((rows, LANES), F32)
                for k in range(CONV_WIDTH):
                    wk = w_ref[pl.ds(k, 1), pl.ds(l0, LANES)]
                    acc = acc + _phase_window(bd, shd, 0, halo - k, r0, rows, l0) * wk
                    dw_ref[pl.ds(k, 1), pl.ds(l0, LANES)] += _colsum(d_here * _phase_window(bu, shu, base, k + _PHASE_PAD - halo, r0, rows, l0))
                av = a_ref[pl.ds(r0, rows), pl.ds(l0, LANES)].astype(F32)
                sg = _sig(g_ref[pl.ds(r0, rows), pl.ds(l0, LANES)].astype(F32))
                da = acc * sg
                dg = acc * av * sg * (1.0 - sg)
                dag_ref[pl.ds(r0, rows), pl.ds(l0, LANES)] = da.astype(BF16)
                dag_ref[pl.ds(r0, rows), pl.ds(l1, LANES)] = dg.astype(BF16)
                db_ref[:, pl.ds(l0, LANES)] += _colsum(da)
                db_ref[:, pl.ds(l1, LANES)] += _colsum(dg)
            return carry

        lax.fori_loop(0, D // LANES, chunk, 0)

    return _rowk(name, T, tm, [(du2, "tile"), (du2, "next"), (u1, "prev"), (u1, "tile"), (a_, "tile"), (gate, "tile"), (w_dw, "full")],
                 [((T, 2 * D), BF16, "tile"), ((CONV_WIDTH, D), F32, "acc"), ((1, 2 * D), F32, "acc")], body,
                 scratch=[pltpu.VMEM((2 * tm, D), F32), pltpu.VMEM((2 * tm, D), F32), _phase_scratch(tm), _phase_scratch(tm)])


def _row_index(i, tm, r0, rows):
    return (i * tm + r0 + lax.broadcasted_iota(jnp.int32, (rows, 1), 0)).astype(F32)


def _pool_fwd(name, x, g, tm):
    T, D = x.shape
    gc = D // len(POOL_WINDOWS)
    rows = min(CONV_ROWS, tm)
    rc_n = tm // rows

    def body(i, n, ins, outs, scr):
        buf = scr[0]
        xp = ins[0][...]
        buf[pl.ds(0, tm), :] = jnp.where(i > 0, xp * _rms_r(xp) * ins[2][...], 0.0)
        xc = ins[1][...]
        buf[pl.ds(tm, tm), :] = xc * _rms_r(xc) * ins[2][...]
        o_ref = outs[0]
        for gi, w in enumerate(POOL_WINDOWS):
            def chunk(lc, carry, gi=gi, w=w):
                l0 = pl.multiple_of(gi * gc + lc * LANES, LANES)
                for rc in range(rc_n):
                    r0 = rc * rows
                    acc = buf[pl.ds(tm + r0, rows), pl.ds(l0, LANES)]
                    here = acc
                    for d in range(1, w):
                        acc = acc + buf[pl.ds(tm + r0 - d, rows), pl.ds(l0, LANES)]
                    cnt = jnp.minimum(_row_index(i, tm, r0, rows) + 1.0, float(w))
                    o_ref[pl.ds(r0, rows), pl.ds(l0, LANES)] = (acc / cnt - here).astype(BF16)
                return carry

            lax.fori_loop(0, gc // LANES, chunk, 0)

    return _rowk(name, T, tm, [(x, "prev"), (x, "tile"), (g, "full")], [((T, D), BF16, "tile")], body,
                 scratch=[pltpu.VMEM((2 * tm, D), F32)])[0]


def _pool_bwd(name, dmix, tm):
    T, D = dmix.shape
    gc = D // len(POOL_WINDOWS)
    rows = min(CONV_ROWS, tm)
    rc_n = tm // rows

    def body(i, n, ins, outs, scr):
        buf = scr[0]
        o_ref = outs[0]
        t_here = (i * tm + lax.broadcasted_iota(jnp.int32, (tm, 1), 0)).astype(F32) + 1.0
        for gi, w in enumerate(POOL_WINDOWS):
            cols = pl.ds(gi * gc, gc)
            buf[pl.ds(0, tm), cols] = ins[0][:, cols] / jnp.minimum(t_here, float(w))
            buf[pl.ds(tm, tm), cols] = jnp.where(i < n - 1, ins[1][:, cols] / float(w), 0.0)

            def chunk(lc, carry, gi=gi, w=w):
                l0 = pl.multiple_of(gi * gc + lc * LANES, LANES)
                for rc in range(rc_n):
                    r0 = rc * rows
                    acc = -ins[0][pl.ds(r0, rows), pl.ds(l0, LANES)]
                    for d in range(w):
                        acc = acc + buf[pl.ds(r0 + d, rows), pl.ds(l0, LANES)]
                    o_ref[pl.ds(r0, rows), pl.ds(l0, LANES)] = acc
                return carry

            lax.fori_loop(0, gc // LANES, chunk, 0)

    return _rowk(name, T, tm, [(dmix, "tile"), (dmix, "next")], [((T, D), F32, "tile")], body,
                 scratch=[pltpu.VMEM((2 * tm, D), F32)])[0]


def _pool_scale_bwd(name, dy, y0, scale, tm):
    T, D = dy.shape

    def body(i, n, ins, outs, scr):
        d = ins[0][...]
        outs[0][...] = (d * ins[2][...]).astype(BF16)
        _accum(outs[1], i, _colsum(d * ins[1][...]))

    return _rowk(name, T, tm, [(dy, "tile"), (y0, "tile"), (scale, "full")], [((T, D), BF16, "tile"), ((1, D), F32, "acc")], body)


def _t5_bucket_np():
    i = np.arange(QBLOCK)[:, None]
    j = np.arange(2 * QBLOCK)[None, :]
    rel = j - QBLOCK - i
    nb = NUM_BUCKETS // 2
    n = -rel
    ret = np.where(n < 0, nb, 0)
    n = np.abs(n)
    max_exact = nb // 2
    nf = np.maximum(n, 1).astype(np.float32)
    large = max_exact + (np.log(nf / np.float32(max_exact)) / np.float32(math.log(REL_MAX_DIST / max_exact))
                         * np.float32(nb - max_exact)).astype(np.int32)
    large = np.minimum(large, nb - 1)
    return (ret + np.where(n < max_exact, n, large)).astype(np.int32)


def _bias_fwd(rel_bias, bucket):
    nb, nh = rel_bias.shape

    def body(rb_ref, bk_ref, o_ref):
        h = pl.program_id(0)
        bk = bk_ref[...]
        acc = jnp.zeros(bk.shape, F32)
        for b in range(nb):
            acc = jnp.where(bk == b, rb_ref[b, h], acc)
        o_ref[...] = acc

    return pl.pallas_call(
        body, name="attn_bias_fwd", grid=(nh,),
        in_specs=[pl.BlockSpec(memory_space=pltpu.SMEM), pl.BlockSpec(bucket.shape, lambda h: (0, 0))],
        out_specs=pl.BlockSpec((None,) + bucket.shape, lambda h: (h, 0, 0)),
        out_shape=jax.ShapeDtypeStruct((nh,) + bucket.shape, F32), compiler_params=_cparams(("arbitrary",)),
    )(rel_bias, bucket)


def _bias_bwd(dbias, bucket, nb):
    nh = dbias.shape[0]

    def body(db_ref, bk_ref, o_ref):
        h = pl.program_id(0)
        bk = bk_ref[...]
        d = db_ref[...]
        for b in range(nb):
            o_ref[h, b] = jnp.sum(jnp.where(bk == b, d, 0.0))

    return pl.pallas_call(
        body, name="attn_bias_bwd", grid=(nh,),
        in_specs=[pl.BlockSpec((None,) + bucket.shape, lambda h: (h, 0, 0)), pl.BlockSpec(bucket.shape, lambda h: (0, 0))],
        out_specs=pl.BlockSpec(memory_space=pltpu.SMEM),
        out_shape=jax.ShapeDtypeStruct((nh, nb), F32), compiler_params=_cparams(("arbitrary",)),
    )(dbias, bucket)


def _head_norm_fwd(name, q2, g, tm):
    R, W = q2.shape

    def body(i, n, ins, outs, scr):
        v = ins[0][...]
        outs[0][...] = (v * _rms_r(v) * ins[1][...]).astype(BF16)

    return _rowk(name, R, tm, [(q2, "tile"), (g, "full")], [((R, W), BF16, "tile")], body)[0]


def _head_norm_bwd(name, dqn, q2, g, tm):
    R, W = q2.shape

    def body(i, n, ins, outs, scr):
        v = ins[1][...]
        d = ins[0][...]
        r = _rms_r(v)
        xh = v * r
        dg = d * ins[2][...]
        outs[0][...] = (r * (dg - xh * jnp.mean(dg * xh, axis=-1, keepdims=True))).astype(BF16)
        _accum(outs[1], i, _colsum(d * xh))

    return _rowk(name, R, tm, [(dqn, "tile"), (q2, "tile"), (g, "full")], [((R, W), BF16, "tile"), ((1, W), F32, "acc")], body)


def _band_merge(name, own, prev, k3=None, g=None):
    H, T, W = own.shape
    nblk = T // QBLOCK

    def body(*refs):
        o_ref, p_ref = refs[0], refs[1]
        out_ref = refs[4] if k3 is not None else refs[2]

        def blk(m, carry):
            r0 = pl.multiple_of(m * QBLOCK, QBLOCK)
            rn = pl.multiple_of(jnp.minimum(m + 1, nblk - 1) * QBLOCK, QBLOCK)
            d = o_ref[pl.ds(r0, QBLOCK), :] + jnp.where(m < nblk - 1, p_ref[pl.ds(rn, QBLOCK), :], 0.0)
            if k3 is None:
                out_ref[pl.ds(r0, QBLOCK), :] = d.astype(BF16)
                return carry
            v = refs[2][pl.ds(r0, QBLOCK), :]
            r = _rms_r(v)
            xh = v * r
            dg = d * refs[3][...]
            out_ref[pl.ds(r0, QBLOCK), :] = (r * (dg - xh * jnp.mean(dg * xh, axis=-1, keepdims=True))).astype(BF16)
            return carry + _colsum(d * xh)

        tot = lax.fori_loop(0, nblk, blk, jnp.zeros((1, W), F32))
        if k3 is not None:
            _accum(refs[5], pl.program_id(0), tot)

    head = pl.BlockSpec((None, T, W), lambda h: (h, 0, 0))
    ins, in_specs = [own, prev], [head, head]
    out_shape, out_specs = [jax.ShapeDtypeStruct((H, T, W), BF16)], [head]
    if k3 is not None:
        ins += [k3, g]
        in_specs += [head, pl.BlockSpec(g.shape, lambda h: (0, 0))]
        out_shape.append(jax.ShapeDtypeStruct((1, W), F32))
        out_specs.append(pl.BlockSpec((1, W), lambda h: (0, 0)))
    return pl.pallas_call(body, name=name, grid=(H,), in_specs=in_specs, out_specs=out_specs, out_shape=out_shape,
                          compiler_params=_cparams(("arbitrary",)))(*ins)


def _masked_bias(bias):
    rows = bias.shape[1]
    qc = (jnp.arange(rows)[:, None] % QBLOCK) // CHUNK
    j = jnp.arange(2 * QBLOCK)[None, :]
    kc = j // CHUNK - QBLOCK // CHUNK
    ok = (kc <= qc) & (kc >= qc - WINDOW_CHUNKS)
    hide = lambda visible: jnp.where(visible, 0.0, NEG_INF).astype(F32)[None]
    return jnp.stack([bias + hide(ok & (j >= QBLOCK)), bias + hide(ok)], axis=0)


def _attn_logits(q, kb, bias_masked, sink):
    s = lax.dot_general(q, kb, _DOT_DIMS["nt"], preferred_element_type=F32) * (HEAD_DIM ** -0.5) + bias_masked
    m = jnp.maximum(jnp.max(s, axis=-1, keepdims=True), sink)
    e = jnp.exp(s - m)
    es = jnp.exp(sink - m)
    den = jnp.sum(e, axis=-1, keepdims=True) + es
    return e / den, es / den


def _heads_per_step(n_kv):
    return 2 if n_kv % 2 == 0 else 1


def _attn_specs(group, hp, rows):
    blk = lambda hn, fn: pl.BlockSpec((hn, QBLOCK, HEAD_DIM), fn)
    cur = lambda h, n: (h, n, 0)
    prv = lambda h, n: (h, jnp.maximum(n - 1, 0), 0)
    bsp = pl.BlockSpec((hp, rows, 2 * QBLOCK), lambda h, n: (h, 0, 0))
    ssp = pl.BlockSpec((hp, rows, 1), lambda h, n: (h, 0, 0))
    bmsp = pl.BlockSpec((None, hp, rows, 2 * QBLOCK), lambda h, n: (jnp.minimum(n, 1), h, 0, 0))
    return blk(hp * group, cur), blk(hp, prv), blk(hp, cur), bsp, ssp, bmsp


def _attn_fwd(qn, kn, v, bias, sink_rows, n_kv, group, T):
    nblk = T // QBLOCK
    rows = group * QBLOCK
    hp = _heads_per_step(n_kv)

    def body(q_ref, kp_ref, kc_ref, vp_ref, vc_ref, b_ref, s_ref, o_ref):
        for hh in range(hp):
            q = q_ref[pl.ds(hh * group, group)].reshape(rows, HEAD_DIM)
            kb = jnp.concatenate([kp_ref[hh], kc_ref[hh]], axis=0)
            vb = jnp.concatenate([vp_ref[hh], vc_ref[hh]], axis=0)
            p, _ = _attn_logits(q, kb, b_ref[hh], s_ref[hh])
            o = lax.dot_general(p.astype(BF16), vb, _DOT_DIMS["nn"], preferred_element_type=F32)
            o_ref[pl.ds(hh * group, group)] = o.reshape(group, QBLOCK, HEAD_DIM).astype(BF16)

    qs, kp, kc, _, ssp, bmsp = _attn_specs(group, hp, rows)
    return pl.pallas_call(
        body, name="attn_fwd", grid=(n_kv // hp, nblk), in_specs=[qs, kp, kc, kp, kc, bmsp, ssp],
        out_specs=qs, out_shape=jax.ShapeDtypeStruct(qn.shape, BF16), compiler_params=_cparams(("arbitrary", "arbitrary")),
    )(qn, kn, kn, v, v, bias, sink_rows)


def _attn_bwd(qn, kn, v, bias, sink_rows, do, n_kv, group, T):
    nblk = T // QBLOCK
    rows = group * QBLOCK
    scale = HEAD_DIM ** -0.5
    hp = _heads_per_step(n_kv)

    def body(q_ref, kp_ref, kc_ref, vp_ref, vc_ref, b_ref, s_ref, do_ref, dq_ref, dko_ref, dkp_ref, dvo_ref, dvp_ref, db_ref, ds_ref):
        n = pl.program_id(1)
        for hh in range(hp):
            q = q_ref[pl.ds(hh * group, group)].reshape(rows, HEAD_DIM)
            dov = do_ref[pl.ds(hh * group, group)].reshape(rows, HEAD_DIM)
            kb = jnp.concatenate([kp_ref[hh], kc_ref[hh]], axis=0)
            vb = jnp.concatenate([vp_ref[hh], vc_ref[hh]], axis=0)
            p, ps = _attn_logits(q, kb, b_ref[hh], s_ref[hh])
            dp = lax.dot_general(dov, vb, _DOT_DIMS["nt"], preferred_element_type=F32)
            delta = jnp.sum(p * dp, axis=-1, keepdims=True)
            dl = p * (dp - delta)
            dlb = dl.astype(BF16)
            dq = lax.dot_general(dlb, kb, _DOT_DIMS["nn"], preferred_element_type=F32) * scale
            dkb = lax.dot_general(dlb, q, _DOT_DIMS["tn"], preferred_element_type=F32) * scale
            dvb = lax.dot_general(p.astype(BF16), dov, _DOT_DIMS["tn"], preferred_element_type=F32)
            dq_ref[pl.ds(hh * group, group)] = dq.reshape(group, QBLOCK, HEAD_DIM)
            dkp_ref[hh] = dkb[:QBLOCK]
            dko_ref[hh] = dkb[QBLOCK:]
            dvp_ref[hh] = dvb[:QBLOCK]
            dvo_ref[hh] = dvb[QBLOCK:]
            dsink = -ps * delta

            @pl.when(n == 0)
            def _(hh=hh, dl=dl, dsink=dsink):
                db_ref[hh] = dl
                ds_ref[hh] = dsink

            @pl.when(n > 0)
            def _(hh=hh, dl=dl, dsink=dsink):
                db_ref[hh] += dl
                ds_ref[hh] += dsink

    qs, kp, kc, bsp, ssp, bmsp = _attn_specs(group, hp, rows)
    kv_shape = jax.ShapeDtypeStruct(kn.shape, F32)
    return pl.pallas_call(
        body, name="attn_bwd", grid=(n_kv // hp, nblk), in_specs=[qs, kp, kc, kp, kc, bmsp, ssp, qs],
        out_specs=[qs, kc, kc, kc, kc, bsp, ssp],
        out_shape=[jax.ShapeDtypeStruct(qn.shape, F32), kv_shape, kv_shape, kv_shape, kv_shape,
                   jax.ShapeDtypeStruct(bias.shape[1:], F32), jax.ShapeDtypeStruct(sink_rows.shape, F32)],
        compiler_params=_cparams(("arbitrary", "arbitrary")),
    )(qn, kn, kn, v, v, bias, sink_rows, do)


def _adamw(name, w, g, m, v):
    shape = w.shape
    w2, g2, m2, v2 = (a.reshape(-1, shape[-1]) for a in (w, g, m, v))
    R, W = w2.shape
    tm = _tile(R, max(8, (1 << 19) // W), 8)
    d1 = 1.0 - ADAM_B1 ** ADAM_STEP
    d2 = 1.0 - ADAM_B2 ** ADAM_STEP

    def body(i, n, ins, outs, scr):
        wv, gv = ins[0][...], ins[1][...]
        mn = ADAM_B1 * ins[2][...] + (1.0 - ADAM_B1) * gv
        vn = ADAM_B2 * ins[3][...] + (1.0 - ADAM_B2) * (gv * gv)
        outs[0][...] = -ADAM_LR * ((mn / d1) / (jnp.sqrt(vn / d2) + ADAM_EPS) + ADAM_WD * wv)
        outs[1][...] = mn
        outs[2][...] = vn

    d, mn, vn = _rowk(name, R, tm, [(w2, "tile"), (g2, "tile"), (m2, "tile"), (v2, "tile")],
                      [((R, W), F32, "tile")] * 3, body)
    return d.reshape(shape), mn.reshape(shape), vn.reshape(shape)


def _first(accs, extras):
    return [accs[0]]


def _swiglu_fwd(accs, extras):
    a, b = accs
    s = _sig(a)
    t = a * s
    return [t, b * (s + t * (1.0 - s)), t * b]


def _ple_fwd(accs, extras):
    b, p, w_proj, x2 = extras
    g = _sig(accs[0] + b)
    q = lax.dot_general(p.astype(BF16), w_proj.astype(BF16), _DOT_DIMS["nn"], preferred_element_type=F32)
    return [g, x2 + g * q]


def _ple_bwd(accs, extras):
    d, g = extras
    dz = d * accs[0] * g * (1.0 - g)
    return [d * g, dz, _colsum(dz)]


def _swiglu_bwd(accs, extras):
    return [accs[0] * extras[1].astype(F32), accs[0] * extras[0].astype(F32)]


def _local_step(x, p, target, wts, small):
    T, D = x.shape
    L, _, PLE = p.shape
    gu, down, cin, sq, qkv_w = wts["gu"], wts["down"], wts["cin"], wts["sq"], wts["qkv"]
    FF = gu.shape[2]
    NA, NC = cin.shape[0], qkv_w.shape[0]
    QW = qkv_w.shape[2]
    KVD = (QW - D) // 2
    n_heads, n_kv = D // HEAD_DIM, KVD // HEAD_DIM
    group = n_heads // n_kv
    nblk = T // QBLOCK
    GC = D // len(POOL_WINDOWS)

    tr = _tile(T, 256, 8)
    tmm = _tile(T, 1024)
    tD = _tile(D, 1024)
    tDk = _tile(D, 2048)
    tD2 = _tile(D, 512)
    tF = _tile(FF, 512)
    tFk = _tile(FF, 2816)
    tFw = _tile(FF, 1408)
    tP = _tile(PLE, 512)
    tQ = _tile(QW, 768)
    tT = _tile(T, 1024)

    bucket = jnp.asarray(_t5_bucket_np())
    saved = []
    xs = x

    for i in range(L):
        kind, j = i % 3, i // 3
        sv = {"x": xs}
        h1 = _rms_fwd(f"rms_mix_{i}", xs, small["norm_mix"][i:i + 1], tr)
        if kind == 0:
            a_, gate, u1 = _mm(
                f"conv_in_{i}", "nn", (T, D, D), (tmm, tD2, tDk),
                [(_op(h1), _op(cin, j), 0), (_op(h1), _op(cin, j, 0, D // tD2), 1)], 2,
                lambda accs, ex: (lambda a, g: [a, g, a * _sig(g)])(accs[0] + ex[0], accs[1] + ex[1]),
                [((T, D), BF16, None, None), ((T, D), BF16, None, None), ((T, D), F32, None, None)],
                extras=[(small["conv_b_in"], "row", j, 0), (small["conv_b_in"], "row", j, D // tD2)])
            u2 = _dwconv_fwd(f"dwconv_{i}", u1, small["conv_w_dw"][j], small["conv_b_dw"][j:j + 1], tr)
            u4 = _ln_silu_fwd(f"ln_silu_{i}", u2, small["conv_ln_g"][j:j + 1], small["conv_ln_b"][j:j + 1], tr)
            x1, = _mm(f"conv_out_{i}", "nn", (T, D, D), (tmm, tD2, tDk), [(_op(u4), _op(sq, j), 0)], 1,
                      lambda accs, ex: [accs[0] + ex[0] + ex[1]], [((T, D), F32, None, None)],
                      extras=[(small["conv_b_out"], "row", j, 0), (xs, "tile", None, 0)])
            sv.update(h1=h1, a=a_, gate=gate, u1=u1, u2=u2, u4=u4)
        elif kind == 1:
            mix = _pool_fwd(f"pool_{i}", xs, small["norm_mix"][i:i + 1], tr)
            pw = small["pool_w"][j].reshape(len(POOL_WINDOWS) * GC, GC)
            kb = GC // _tile(GC, 512)
            tg = _tile(GC, 512)
            y0, x1 = _mm(f"pool_mm_{i}", "nn", (T, D, GC), (tmm, GC, tg),
                         [(_op(mix, fn=lambda i_, j_, k_, kb=kb: (i_, j_ * kb + k_)), _op(pw, fn=lambda i_, j_, k_, kb=kb: (j_ * kb + k_, 0)), 0)], 1,
                         lambda accs, ex: [accs[0], ex[1] + accs[0] * ex[0]],
                         [((T, D), F32, None, None), ((T, D), F32, None, None)],
                         extras=[(small["pool_scale"][j:j + 1], "row", None, 0), (xs, "tile", None, 0)])
            sv.update(mix=mix, y0=y0, pw=pw)
        else:
            qkv, = _mm(f"qkv_{i}", "nn", (T, QW, D), (tmm, tQ, tDk), [(_op(h1), _op(qkv_w, j), 0)], 1, _first,
                       [((T, QW), F32, None, None)])
            q_hm = qkv[:, :D].reshape(T, n_heads, HEAD_DIM).transpose(1, 0, 2).reshape(n_heads * T, HEAD_DIM)
            k_hm = qkv[:, D:D + KVD].reshape(T, n_kv, HEAD_DIM).transpose(1, 0, 2).reshape(n_kv * T, HEAD_DIM)
            v_hm = qkv[:, D + KVD:].reshape(T, n_kv, HEAD_DIM).transpose(1, 0, 2).astype(BF16)
            th = _tile(T, 2048, 8)
            qn = _head_norm_fwd(f"qnorm_{i}", q_hm, small["attn_q_norm"][j:j + 1], th).reshape(n_heads, T, HEAD_DIM)
            kn = _head_norm_fwd(f"knorm_{i}", k_hm, small["attn_k_norm"][j:j + 1], th).reshape(n_kv, T, HEAD_DIM)
            bias = _masked_bias(_bias_fwd(small["rel_bias"], bucket).reshape(n_kv, group * QBLOCK, 2 * QBLOCK))
            sink_rows = jnp.broadcast_to(small["attn_sinks"][j].reshape(n_kv, group, 1, 1), (n_kv, group, QBLOCK, 1)).reshape(n_kv, group * QBLOCK, 1)
            o_hm = _attn_fwd(qn, kn, v_hm, bias, sink_rows, n_kv, group, T)
            o = o_hm.transpose(1, 0, 2).reshape(T, D)
            x1, = _mm(f"attn_o_{i}", "nn", (T, D, D), (tmm, tD2, tDk), [(_op(o), _op(sq, NA + j), 0)], 1,
                      lambda accs, ex: [accs[0] + ex[0]], [((T, D), F32, None, None)], extras=[(xs, "tile", None, 0)])
            sv.update(h1=h1, q_hm=q_hm, k_hm=k_hm, v_hm=v_hm, qn=qn, kn=kn, bias=bias, sink_rows=sink_rows, o=o)
        h2 = _rms_fwd(f"rms_ffn_{i}", x1, small["norm_ffn"][i:i + 1], tr)
        a, b, f = _mm(f"ffn_up_{i}", "nn", (T, FF, D), (tmm, tF, tDk), [(_op(h2), _op(gu, i), 0), (_op(h2), _op(gu, L + i), 1)], 2,
                      _swiglu_fwd, [((T, FF), BF16, None, None)] * 3)
        x2, = _mm(f"ffn_down_{i}", "nn", (T, D, FF), (tmm, tD2, tFk), [(_op(f), _op(down, i), 0)], 1,
                  lambda accs, ex: [accs[0] + ex[0]], [((T, D), F32, None, None)], extras=[(x1, "tile", None, 0)])
        h3 = _rms_fwd(f"rms_ple_{i}", x2, small["norm_ple"][i:i + 1], tr)
        gt, x3 = _mm(f"ple_gate_{i}", "nn", (T, D, D), (tmm, tD2, tDk), [(_op(h3), _op(sq, NA + NC + i), 0)], 1, _ple_fwd,
                     [((T, D), F32, None, None), ((T, D), F32, None, None)],
                     extras=[(small["ple_b_gate"], "row", i, 0), (p, "rows", i, 0), (small["ple_w_proj"], "cols", i, 0), (x2, "tile", None, 0)])
        sv.update(x1=x1, h2=h2, a=sv.get("a"), fa=a, fb=b, f=f, x2=x2, h3=h3, gt=gt)
        saved.append(sv)
        xs = x3

    dx, loss = _loss_head(xs, target, tr)

    g_gu = g_down = g_cin = g_sq = g_qkv = None
    gs = {k: [None] * v.shape[0] for k, v in small.items() if k != "rel_bias"}
    gs["rel_bias"] = None
    gs["ple_w_proj"] = [None] * L

    for i in reversed(range(L)):
        kind, j = i % 3, i // 3
        sv = saved[i]
        dq, dz, dbg = _mm(f"ple_bwd_{i}", "nn", (T, D, PLE), (tmm, tD2, tP), [(_op(p, i), _op(small["ple_w_proj"], i), 0)], 1, _ple_bwd,
                          [((T, D), BF16, None, None), ((T, D), BF16, None, None), ((T // tmm, 1, D), F32, "rowsum", None)],
                          extras=[(dx, "tile", None, 0), (sv["gt"], "tile", None, 0)])
        gs["ple_b_gate"][i] = jnp.sum(dbg, axis=0)
        gs["ple_w_proj"][i], = _mm(f"d_ple_proj_{i}", "tn", (PLE, D, T), (tP, tD, tT), [(_op(p, i), _op(dq), 0)], 1, _first,
                                   [((PLE, D), F32, None, None)])
        g_sq, = _mm(f"d_ple_gate_{i}", "tn", (D, D, T), (tD, tD, tT), [(_op(sv["h3"]), _op(dz), 0)], 1, _first,
                    [(sq.shape, BF16, NA + NC + i, g_sq)])
        dh3, = _mm(f"dh_ple_{i}", "nt", (T, D, D), (tmm, tD2, tDk), [(_op(dz), _op(sq, NA + NC + i), 0)], 1, _first,
                   [((T, D), F32, None, None)])
        dx2, gs["norm_ple"][i], dx2b = _rms_bwd(f"rms_ple_bwd_{i}", dx, sv["x2"], small["norm_ple"][i:i + 1], dh3, tr, want_bf16=True)
        da, db = _mm(f"d_ffn_act_{i}", "nt", (T, FF, D), (tmm, tF, tDk), [(_op(dx2b), _op(down, i), 0)], 1, _swiglu_bwd,
                     [((T, FF), BF16, None, None)] * 2, extras=[(sv["fa"], "tile", None, 0), (sv["fb"], "tile", None, 0)])
        g_down, = _mm(f"d_ffn_down_{i}", "tn", (FF, D, T), (tFw, tD, tT), [(_op(sv["f"]), _op(dx2b), 0)], 1, _first,
                      [(down.shape, BF16, i, g_down)])
        g_gu, = _mm(f"d_ffn_gate_{i}", "tn", (D, FF, T), (tD, tFw, tT), [(_op(sv["h2"]), _op(da), 0)], 1, _first,
                    [(gu.shape, BF16, i, g_gu)])
        g_gu, = _mm(f"d_ffn_up_{i}", "tn", (D, FF, T), (tD, tFw, tT), [(_op(sv["h2"]), _op(db), 0)], 1, _first,
                    [(gu.shape, BF16, L + i, g_gu)])
        dh2, = _mm(f"dh_ffn_{i}", "nt", (T, D, FF), (tmm, tD2, tFk), [(_op(da), _op(gu, i), 0), (_op(db), _op(gu, L + i), 0)], 1, _first,
                   [((T, D), F32, None, None)])
        want_cs = kind == 0
        res = _rms_bwd(f"rms_ffn_bwd_{i}", dx2, sv["x1"], small["norm_ffn"][i:i + 1], dh2, tr, want_bf16=True, want_colsum=want_cs)
        dx1, gs["norm_ffn"][i], dx1b = res[:3]
        xin = sv["x"]
        if kind == 0:
            gs["conv_b_out"][j] = res[3]
            g_sq, = _mm(f"d_conv_out_{i}", "tn", (D, D, T), (tD, tD, tT), [(_op(sv["u4"]), _op(dx1b), 0)], 1, _first,
                        [(sq.shape, BF16, j, g_sq)])
            du4, = _mm(f"dh_conv_out_{i}", "nt", (T, D, D), (tmm, tD2, tDk), [(_op(dx1b), _op(sq, j), 0)], 1, _first,
                       [((T, D), F32, None, None)])
            du2, gs["conv_ln_g"][j], gs["conv_ln_b"][j], gs["conv_b_dw"][j] = _ln_silu_bwd(
                f"ln_silu_bwd_{i}", du4, sv["u2"], small["conv_ln_g"][j:j + 1], small["conv_ln_b"][j:j + 1], tr)
            dag, gs["conv_w_dw"][j], gs["conv_b_in"][j] = _dwconv_glu_bwd(
                f"dwconv_bwd_{i}", du2, sv["u1"], sv["a"], sv["gate"], small["conv_w_dw"][j], tr)
            g_cin, = _mm(f"d_conv_in_{i}", "tn", (D, 2 * D, T), (tD, tD, tT), [(_op(sv["h1"]), _op(dag), 0)], 1, _first,
                         [(cin.shape, BF16, j, g_cin)])
            dh1, = _mm(f"dh_conv_in_{i}", "nt", (T, D, 2 * D), (tmm, tD2, tDk), [(_op(dag), _op(cin, j), 0)], 1, _first,
                       [((T, D), F32, None, None)])
        elif kind == 1:
            dys, gs["pool_scale"][j] = _pool_scale_bwd(f"pool_scale_bwd_{i}", dx1, sv["y0"], small["pool_scale"][j:j + 1], tr)
            tg = _tile(GC, 512)
            kb = GC // tg
            dmix, = _mm(f"dh_pool_{i}", "nt", (T, D, GC), (tmm, GC, tg),
                        [(_op(dys, fn=lambda i_, j_, k_, kb=kb: (i_, j_ * kb + k_)), _op(sv["pw"]), 0)], 1, _first,
                        [((T, D), F32, None, None)])
            ng = len(POOL_WINDOWS)
            gs["pool_w"][j], = _mm(f"d_pool_w_{i}", "tn", (D, GC, T), (GC, GC, tT),
                                   [(_op(sv["mix"]), _op(dys, fn=lambda i_, j_, k_: (k_, i_)), 0)], 1, _first,
                                   [((D, GC), F32, None, None)])
            gs["pool_w"][j] = gs["pool_w"][j].reshape(ng, GC, GC)
            dh1 = _pool_bwd(f"pool_bwd_{i}", dmix, tr)
        else:
            g_sq, = _mm(f"d_attn_o_{i}", "tn", (D, D, T), (tD, tD, tT), [(_op(sv["o"]), _op(dx1b), 0)], 1, _first,
                        [(sq.shape, BF16, NA + j, g_sq)])
            do, = _mm(f"dh_attn_o_{i}", "nt", (T, D, D), (tmm, tD2, tDk), [(_op(dx1b), _op(sq, NA + j), 0)], 1, _first,
                      [((T, D), BF16, None, None)])
            do_hm = do.reshape(T, n_heads, HEAD_DIM).transpose(1, 0, 2)
            dqn, dko, dkp, dvo, dvp, dbias, dsink = _attn_bwd(sv["qn"], sv["kn"], sv["v_hm"], sv["bias"], sv["sink_rows"], do_hm, n_kv, group, T)
            th = _tile(T, 2048, 8)
            dq_hm, gs["attn_q_norm"][j] = _head_norm_bwd(f"qnorm_bwd_{i}", dqn.reshape(n_heads * T, HEAD_DIM), sv["q_hm"],
                                                         small["attn_q_norm"][j:j + 1], th)
            dk_hm, gs["attn_k_norm"][j] = _band_merge(f"knorm_bwd_{i}", dko, dkp, sv["k_hm"].reshape(n_kv, T, HEAD_DIM), small["attn_k_norm"][j:j + 1])
            dv_hm, = _band_merge(f"v_merge_{i}", dvo, dvp)
            gs["attn_sinks"][j] = jnp.sum(dsink.reshape(n_heads, QBLOCK), axis=1).reshape(1, n_heads)
            rb = _bias_bwd(dbias.reshape(n_heads, QBLOCK, 2 * QBLOCK), bucket, NUM_BUCKETS).T
            gs["rel_bias"] = rb if gs["rel_bias"] is None else gs["rel_bias"] + rb
            tok = lambda t_, nh: t_.reshape(nh, T, HEAD_DIM).transpose(1, 0, 2).reshape(T, nh * HEAD_DIM)
            dqkv = jnp.concatenate([tok(dq_hm, n_heads), tok(dk_hm, n_kv), tok(dv_hm, n_kv)], axis=1)
            g_qkv, = _mm(f"d_qkv_{i}", "tn", (D, QW, T), (tD, tQ, tT), [(_op(sv["h1"]), _op(dqkv), 0)], 1, _first,
                         [(qkv_w.shape, BF16, j, g_qkv)])
            dh1, = _mm(f"dh_qkv_{i}", "nt", (T, D, QW), (tmm, tD2, _tile(QW, 3072)), [(_op(dqkv), _op(qkv_w, j), 0)], 1, _first,
                       [((T, D), F32, None, None)])
        dx, gs["norm_mix"][i] = _rms_bwd(f"rms_mix_bwd_{i}", dx1, xin, small["norm_mix"][i:i + 1], dh1, tr)

    big = {"gu": g_gu, "down": g_down, "cin": g_cin, "sq": g_sq, "qkv": g_qkv}
    gsmall = {}
    for k, v in gs.items():
        if k == "rel_bias":
            gsmall[k] = v
        else:
            gsmall[k] = jnp.stack([t.reshape(small[k].shape[1:]) for t in v], axis=0)
    return loss, dx, big, gsmall


_ANY = pl.BlockSpec(memory_space=pl.ANY)


def _place():
    x, y, c = lax.axis_index("x"), lax.axis_index("y"), lax.axis_index("c")
    return x, y, c, [(1 - x, y), (x, 1 - y), (1 - x, 1 - y)]


def _lane_start(s, w):
    return pl.multiple_of(s * w, LANES) if w % LANES == 0 else s * w


def _slot(ref, kind, s):
    if kind == "col":
        w = ref.shape[2] // N_SLOTS
        return ref.at[:, :, pl.ds(_lane_start(s, w), w)]
    return ref.at[:, pl.ds(s, 1)]


def _rows_quarter(ref, h, quarter):
    n = ref.shape[-2] // 4
    if len(ref.shape) == 3:
        return ref.at[:, pl.ds((2 * h + quarter) * n, n), :]
    return ref.at[:, :, pl.ds((2 * h + quarter) * n, n), :]


def _rows_half(ref, h):
    n = ref.shape[-2] // 2
    if len(ref.shape) == 3:
        return ref.at[:, pl.ds(h * n, n), :]
    return ref.at[:, :, pl.ds(h * n, n), :]


def _place_own(name, shard, kind, s_arr):
    if kind == "col":
        lead, R, W = shard.shape
        full = (lead, R, N_SLOTS * W)
    else:
        lead, _, R, W = shard.shape
        full = (lead, N_SLOTS, R, W)
    tr = _tile(R, max(16, (1 << 19) // W), 16)
    if kind == "col":
        i_spec = pl.BlockSpec((None, tr, W), lambda l, i, s: (l, i, 0))
        o_spec = pl.BlockSpec((None, tr, W), lambda l, i, s: (l, i, s[0]))
    else:
        i_spec = pl.BlockSpec((None, None, tr, W), lambda l, i, s: (l, 0, i, 0))
        o_spec = pl.BlockSpec((None, None, tr, W), lambda l, i, s: (l, s[0], i, 0))

    def body(s_ref, i_ref, o_ref):
        o_ref[...] = i_ref[...]

    return pl.pallas_call(
        body, name=name,
        grid_spec=pltpu.PrefetchScalarGridSpec(num_scalar_prefetch=1, grid=(lead, R // tr), in_specs=[i_spec], out_specs=o_spec),
        out_shape=jax.ShapeDtypeStruct(full, shard.dtype), compiler_params=_cparams(("arbitrary", "arbitrary")),
    )(s_arr, shard)


def _gather(shards, fulls, kinds):
    ng = len(shards)
    n_sem = 8

    def body(*refs):
        sh, out = refs[:ng], refs[2 * ng:3 * ng]
        send, recv = refs[3 * ng:]
        x, y, c, _ = _place()
        s, s_x, s_y, s_d = 2 * x + y, 2 * (1 - x) + y, 2 * x + (1 - y), 2 * (1 - x) + (1 - y)
        to_x, to_y, sib = (1 - x, y, c), (x, 1 - y, c), (x, y, 1 - c)

        def rcopy(g, k, src, dst, dev):
            return pltpu.make_async_remote_copy(src_ref=src, dst_ref=dst, send_sem=send.at[g * n_sem + k], recv_sem=recv.at[g * n_sem + k],
                                                device_id=dev, device_id_type=MESH)

        def win(g, slot, h, quarter=None):
            w = _slot(out[g], kinds[g], slot)
            return _rows_half(w, h) if quarter is None else _rows_quarter(w, h, quarter)

        sent = []

        def go(cp):
            cp.start()
            sent.append(cp)

        for g in range(ng):
            go(rcopy(g, 0, _rows_half(sh[g], c), win(g, s, c), to_x))
            go(rcopy(g, 1, _rows_half(sh[g], c), win(g, s, c), to_y))
        for g in range(ng):
            rcopy(g, 0, win(g, s_x, c), win(g, s_x, c), to_x).wait_recv()
            go(rcopy(g, 3, win(g, s_x, c, 1), win(g, s_x, c, 1), to_y))
            go(rcopy(g, 4, win(g, s_x, c), win(g, s_x, c), sib))
            rcopy(g, 1, win(g, s_y, c), win(g, s_y, c), to_y).wait_recv()
            go(rcopy(g, 2, win(g, s_y, c, 0), win(g, s_y, c, 0), to_x))
            go(rcopy(g, 5, win(g, s_y, c), win(g, s_y, c), sib))
        for g in range(ng):
            for k, quarter in ((2, 0), (3, 1)):
                rcopy(g, k, win(g, s_d, c, quarter), win(g, s_d, c, quarter), sib).wait_recv()
                go(rcopy(g, 6 + quarter, win(g, s_d, c, quarter), win(g, s_d, c, quarter), sib))
        for g in range(ng):
            rcopy(g, 4, win(g, s_x, 1 - c), win(g, s_x, 1 - c), sib).wait_recv()
            rcopy(g, 5, win(g, s_y, 1 - c), win(g, s_y, 1 - c), sib).wait_recv()
            for quarter in (0, 1):
                rcopy(g, 6 + quarter, win(g, s_d, 1 - c, quarter), win(g, s_d, 1 - c, quarter), sib).wait_recv()
        for cp in sent:
            cp.wait_send()

    return pl.pallas_call(
        body, name="gather_weights", in_specs=[_ANY] * (2 * ng), out_specs=[_ANY] * ng,
        out_shape=[jax.ShapeDtypeStruct(a.shape, a.dtype) for a in fulls],
        input_output_aliases={ng + g: g for g in range(ng)},
        scratch_shapes=[pltpu.SemaphoreType.DMA((n_sem * ng,)), pltpu.SemaphoreType.DMA((n_sem * ng,))],
    )(*shards, *fulls)


def _pair_send(grads):
    ng = len(grads)

    def half_shape(a):
        s = list(a.shape)
        s[-2] //= 2
        return tuple(s)

    def body(*refs):
        gr, out = refs[:ng], refs[ng:2 * ng]
        send, recv = refs[2 * ng:]
        x, y, c, _ = _place()
        cps = [pltpu.make_async_remote_copy(src_ref=_rows_half(gr[g], 1 - c), dst_ref=out[g], send_sem=send.at[g], recv_sem=recv.at[g],
                                            device_id=(x, y, 1 - c), device_id_type=MESH) for g in range(ng)]
        for cp in cps:
            cp.start()
        for cp in cps:
            cp.wait()

    return pl.pallas_call(
        body, name="grad_pair_send", in_specs=[_ANY] * ng, out_specs=[_ANY] * ng,
        out_shape=[jax.ShapeDtypeStruct(half_shape(a), a.dtype) for a in grads],
        scratch_shapes=[pltpu.SemaphoreType.DMA((ng,)), pltpu.SemaphoreType.DMA((ng,))],
    )(*grads)


def _add_half(name, g3, pa3, c_arr):
    n, R, N = g3.shape
    rh = R // 2
    tr = _tile(rh, max(16, (1 << 19) // N), 16)
    nb = rh // tr

    def body(c_ref, g_ref, p_ref, o_ref):
        o_ref[...] = (g_ref[...].astype(F32) + p_ref[...].astype(F32)).astype(o_ref.dtype)

    return pl.pallas_call(
        body, name=name,
        grid_spec=pltpu.PrefetchScalarGridSpec(
            num_scalar_prefetch=1, grid=(n, nb),
            in_specs=[pl.BlockSpec((None, tr, N), lambda l, i, c, nb=nb: (l, c[0] * nb + i, 0)), pl.BlockSpec((None, tr, N), lambda l, i, c: (l, i, 0))],
            out_specs=pl.BlockSpec((None, tr, N), lambda l, i, c: (l, i, 0))),
        out_shape=jax.ShapeDtypeStruct(pa3.shape, g3.dtype), compiler_params=_cparams(("arbitrary", "arbitrary")),
    )(c_arr, g3, pa3)


def _quarter_shape(a, kind):
    s = a.shape
    return (s[0], s[1] // 2, s[2] // N_SLOTS) if kind == "col" else (s[0], 1, s[2] // 2, s[3])


def _ici_exchange_direct(psums, kinds):
    ng = len(psums)

    def body(*refs):
        ps, direct, relay = refs[:ng], refs[ng:2 * ng], refs[2 * ng:3 * ng]
        send, recv = refs[3 * ng:]
        x, y, c, _ = _place()
        s_x, s_y, s_d = 2 * (1 - x) + y, 2 * x + (1 - y), 2 * (1 - x) + (1 - y)
        to_x, to_y = (1 - x, y, c), (x, 1 - y, c)
        cps = []
        for g in range(ng):
            quarter = lambda slot, q, g=g: _rows_half(_slot(ps[g], kinds[g], slot), q)
            plan = ((quarter(s_x, 1), direct[g].at[0], to_x), (quarter(s_y, 0), direct[g].at[1], to_y),
                    (quarter(s_d, 1), relay[g].at[0], to_x), (quarter(s_d, 0), relay[g].at[1], to_y))
            for k, (src, dst, dev) in enumerate(plan):
                cps.append(pltpu.make_async_remote_copy(src_ref=src, dst_ref=dst, send_sem=send.at[g * 4 + k], recv_sem=recv.at[g * 4 + k],
                                                        device_id=dev, device_id_type=MESH))
                cps[-1].start()
        for cp in cps:
            cp.wait()

    shapes = [jax.ShapeDtypeStruct((2,) + _quarter_shape(a, k), a.dtype) for a, k in zip(psums, kinds)]
    res = pl.pallas_call(
        body, name="grad_ici_direct", in_specs=[_ANY] * ng, out_specs=[_ANY] * (2 * ng), out_shape=shapes + shapes,
        scratch_shapes=[pltpu.SemaphoreType.DMA((4 * ng,)), pltpu.SemaphoreType.DMA((4 * ng,))],
    )(*psums)
    return res[:ng], res[ng:]


def _ici_exchange_relayed(sums):
    ng = len(sums)

    def body(*refs):
        sm, out = refs[:ng], refs[ng:2 * ng]
        send, recv = refs[2 * ng:]
        x, y, c, _ = _place()
        cps = []
        for g in range(ng):
            for k, (t, dev) in enumerate(((1, (1 - x, y, c)), (0, (x, 1 - y, c)))):
                cps.append(pltpu.make_async_remote_copy(src_ref=sm[g].at[t], dst_ref=out[g].at[k], send_sem=send.at[g * 2 + k],
                                                        recv_sem=recv.at[g * 2 + k], device_id=dev, device_id_type=MESH))
                cps[-1].start()
        for cp in cps:
            cp.wait()

    return pl.pallas_call(
        body, name="grad_ici_relayed", in_specs=[_ANY] * ng, out_specs=[_ANY] * ng,
        out_shape=[jax.ShapeDtypeStruct(a.shape, a.dtype) for a in sums],
        scratch_shapes=[pltpu.SemaphoreType.DMA((2 * ng,)), pltpu.SemaphoreType.DMA((2 * ng,))],
    )(*sums)


def _quarter_tiles(p3, w):
    rq = p3.shape[1] // 2
    tr = _tile(rq, max(16, (1 << 18) // w), 16)
    return rq, tr, rq // tr


def _p_spec(kind, tr, w, row_block, slot):
    if kind == "col":
        return pl.BlockSpec((None, tr, w), lambda t, l, i, s: (l, row_block(t, i), slot(t, s)))
    return pl.BlockSpec((None, tr, w), lambda t, l, i, s: (l * N_SLOTS + slot(t, s), row_block(t, i), 0))


def _relay_add(name, p3, relay3, yx_arr, lead, kind):
    w = relay3.shape[2]
    rq, tr, nbq = _quarter_tiles(p3, w)
    two = pl.BlockSpec((None, tr, w), lambda t, l, i, s: (t * lead + l, i, 0))

    def body(s_ref, p_ref, r_ref, o_ref):
        o_ref[...] = (p_ref[...].astype(F32) + r_ref[...].astype(F32)).astype(o_ref.dtype)

    return pl.pallas_call(
        body, name=name,
        grid_spec=pltpu.PrefetchScalarGridSpec(
            num_scalar_prefetch=1, grid=(2, lead, nbq),
            in_specs=[_p_spec(kind, tr, w, lambda t, i: (1 - t) * nbq + i, lambda t, s: s[t]), two], out_specs=two),
        out_shape=jax.ShapeDtypeStruct(relay3.shape, relay3.dtype), compiler_params=_cparams(("arbitrary",) * 3),
    )(yx_arr, p3, relay3)


def _slot_sum(name, p3, direct3, relayed3, s_arr, lead, kind):
    w = direct3.shape[2]
    rq, tr, nbq = _quarter_tiles(p3, w)
    d_spec = pl.BlockSpec((None, tr, w), lambda t, l, i, s: ((1 - t) * lead + l, i, 0))
    r_spec = pl.BlockSpec((None, tr, w), lambda t, l, i, s: (t * lead + l, i, 0))

    def body(s_ref, p_ref, d_ref, r_ref, o_ref):
        o_ref[...] = (p_ref[...].astype(F32) + d_ref[...].astype(F32)) + r_ref[...].astype(F32)

    return pl.pallas_call(
        body, name=name,
        grid_spec=pltpu.PrefetchScalarGridSpec(
            num_scalar_prefetch=1, grid=(2, lead, nbq),
            in_specs=[_p_spec(kind, tr, w, lambda t, i: t * nbq + i, lambda t, s: s[0]), d_spec, r_spec],
            out_specs=pl.BlockSpec((None, tr, w), lambda t, l, i, s: (l, t * nbq + i, 0))),
        out_shape=jax.ShapeDtypeStruct((lead, 2 * rq, w), F32), compiler_params=_cparams(("arbitrary",) * 3),
    )(s_arr, p3, direct3, relayed3)


def _pair_swap(halves):
    ng = len(halves)

    def body(*refs):
        hv, out = refs[:ng], refs[ng:2 * ng]
        send, recv = refs[2 * ng:]
        x, y, c, _ = _place()
        cps = [pltpu.make_async_remote_copy(src_ref=hv[g], dst_ref=out[g], send_sem=send.at[g], recv_sem=recv.at[g],
                                            device_id=(x, y, 1 - c), device_id_type=MESH) for g in range(ng)]
        for cp in cps:
            cp.start()
        for cp in cps:
            cp.wait()

    return pl.pallas_call(
        body, name="grad_pair_swap", in_specs=[_ANY] * ng, out_specs=[_ANY] * ng,
        out_shape=[jax.ShapeDtypeStruct(a.shape, a.dtype) for a in halves],
        scratch_shapes=[pltpu.SemaphoreType.DMA((ng,)), pltpu.SemaphoreType.DMA((ng,))],
    )(*halves)


N_DEVICES = 8


def _allreduce_small(v):
    rows, m = v.shape

    def body(v_ref, o_ref, buf, send, recv):
        x, y, c, _ = _place()
        me = 4 * x + 2 * y + c
        buf[me] = v_ref[...]
        cps = []
        for k in range(1, N_DEVICES):
            peer = me ^ k
            cps.append(pltpu.make_async_remote_copy(src_ref=v_ref, dst_ref=buf.at[me], send_sem=send.at[k - 1], recv_sem=recv.at[k - 1],
                                                    device_id=((peer >> 2) & 1, (peer >> 1) & 1, peer & 1), device_id_type=MESH))
            cps[-1].start()
        for k in range(1, N_DEVICES):
            theirs = buf.at[me ^ k]
            pltpu.make_async_remote_copy(src_ref=v_ref, dst_ref=theirs, send_sem=send.at[k - 1], recv_sem=recv.at[k - 1],
                                         device_id=(x, y, c), device_id_type=MESH).wait_recv()
        for cp in cps:
            cp.wait_send()
        acc = buf[0]
        for d in range(1, N_DEVICES):
            acc = acc + buf[d]
        o_ref[...] = acc

    vm = pl.BlockSpec(memory_space=pltpu.VMEM)
    return pl.pallas_call(
        body, name="allreduce_small", in_specs=[vm], out_specs=vm, out_shape=jax.ShapeDtypeStruct(v.shape, F32),
        scratch_shapes=[pltpu.VMEM((N_DEVICES, rows, m), F32), pltpu.SemaphoreType.DMA((N_DEVICES - 1,)), pltpu.SemaphoreType.DMA((N_DEVICES - 1,))],
    )(v)


def _pad_rows(a, mult):
    r = (-a.shape[0]) % mult
    return a if r == 0 else jnp.concatenate([a, jnp.zeros((r,) + a.shape[1:], a.dtype)], axis=0)


def _pack_rows(parts, width, mult=16):
    rows, offs, at = [], [], 0
    for a in parts:
        a2 = _pad_rows(a.reshape(-1, width), mult)
        offs.append((at, a.size // width))
        rows.append(a2)
        at += a2.shape[0]
    return _pad_rows(jnp.concatenate(rows, axis=0), 4 * SUBLANES), offs


SMALL_SHARDED = ("ple_w_proj", "pool_w", "conv_w_dw", "conv_b_dw", "conv_ln_g", "conv_ln_b", "conv_b_out", "conv_b_in")
SMALL_REPLICATED = ("norm_mix", "norm_ffn", "norm_ple", "pool_scale", "attn_q_norm", "attn_k_norm", "attn_sinks", "rel_bias", "ple_b_gate")


def _small_to_full(name, slots):
    if name == "pool_w":
        return jnp.moveaxis(slots, 0, 2).reshape(slots.shape[1], slots.shape[2], N_SLOTS * slots.shape[3], slots.shape[4])
    return jnp.moveaxis(slots, 0, -2).reshape(slots.shape[1:-1] + (N_SLOTS * slots.shape[-1],))


def _small_to_slots(name, full):
    if name == "pool_w":
        nb, ng, gc, _ = full.shape
        return jnp.moveaxis(full.reshape(nb, ng, N_SLOTS, gc // N_SLOTS, gc), 2, 0)
    w = full.shape[-1] // N_SLOTS
    return jnp.moveaxis(full.reshape(full.shape[:-1] + (N_SLOTS, w)), -2, 0)


W_NAMES = ("norm_mix", "norm_ffn", "norm_ple", "conv_w_in", "conv_b_in", "conv_w_dw", "conv_b_dw", "conv_ln_g", "conv_ln_b", "conv_w_out",
           "conv_b_out", "pool_w", "pool_scale", "attn_w_qkv", "attn_q_norm", "attn_k_norm", "attn_sinks", "attn_w_o", "rel_bias",
           "ffn_w_gate", "ffn_w_up", "ffn_w_down", "ple_w_proj", "ple_w_gate", "ple_b_gate")


def _step(x, p, target, w, m, v):
    T, D = x.shape[1], x.shape[2]
    L = p.shape[0]
    NA, NC = w["conv_w_in"].shape[0], w["attn_w_qkv"].shape[0]
    xi, yi, ci = lax.axis_index("x"), lax.axis_index("y"), lax.axis_index("c")
    c_arr = jnp.reshape(ci, (1,)).astype(jnp.int32)
    s_arr = jnp.reshape(2 * xi + yi, (1,)).astype(jnp.int32)

    wq = D // N_SLOTS
    sm_pack, sm_offs = _pack_rows([w[k] for k in SMALL_SHARDED], wq)
    shards = [
        jnp.concatenate([w["ffn_w_gate"], w["ffn_w_up"]], axis=0).astype(BF16),
        w["ffn_w_down"].astype(BF16)[:, None],
        w["conv_w_in"].astype(BF16),
        jnp.concatenate([w["conv_w_out"], w["attn_w_o"], w["ple_w_gate"]], axis=0).astype(BF16)[:, None],
        w["attn_w_qkv"].astype(BF16),
        sm_pack[None, None],
    ]
    kinds = ["col", "row", "col", "row", "col", "row"]
    full = _gather(shards, [_place_own(f"place_own_{g}", a, k, s_arr) for g, (a, k) in enumerate(zip(shards, kinds))], kinds)
    wts = {"gu": full[0], "down": full[1].reshape(L, -1, D), "cin": full[2], "sq": full[3].reshape(NA + NC + L, D, D), "qkv": full[4]}
    small = {k: w[k] for k in SMALL_REPLICATED}
    for k, (at, n) in zip(SMALL_SHARDED, sm_offs):
        small[k] = _small_to_full(k, full[5][0, :, at:at + n].reshape((N_SLOTS,) + w[k].shape))

    loss, dx, big, gsmall = _local_step(x[0], p[:, 0], target[0], wts, small)

    rep_parts = [gsmall[k] for k in SMALL_REPLICATED] + [loss]
    flat = jnp.concatenate([a.reshape(-1) for a in rep_parts])
    n_flat = flat.shape[0]
    m_cols = -(-n_flat // (8 * LANES)) * LANES
    flat = jnp.concatenate([flat, jnp.zeros((8 * m_cols - n_flat,), F32)]).reshape(8, m_cols)
    red = _allreduce_small(flat).reshape(-1)
    grads, at = {}, 0
    for k in SMALL_REPLICATED:
        grads[k] = red[at:at + w[k].size].reshape(w[k].shape)
        at += w[k].size
    loss_out = red[at]

    slots = {k: _small_to_slots(k, gsmall[k]) for k in SMALL_SHARDED}
    gsm = jnp.stack([_pack_rows([slots[k][s] for k in SMALL_SHARDED], wq)[0] for s in range(N_SLOTS)], axis=0)
    local = [big["gu"], big["down"].reshape(L, N_SLOTS, -1, D), big["cin"], big["sq"].reshape(NA + NC + L, N_SLOTS, -1, D), big["qkv"],
             gsm[None]]
    theirs = _pair_send(local)
    psums = []
    for g, (a, t) in enumerate(zip(local, theirs)):
        if kinds[g] == "col":
            psums.append(_add_half(f"pair_add_{g}", a, t, c_arr))
        else:
            n4 = a.shape[0] * N_SLOTS
            psums.append(_add_half(f"pair_add_{g}", a.reshape(n4, a.shape[2], a.shape[3]), t.reshape(n4, t.shape[2], t.shape[3]), c_arr).reshape(t.shape))
    yx_arr = jnp.stack([2 * xi + (1 - yi), 2 * (1 - xi) + yi]).astype(jnp.int32)
    direct, relay = _ici_exchange_direct(psums, kinds)
    p3s = [ps if k == "col" else ps.reshape(ps.shape[0] * N_SLOTS, ps.shape[2], ps.shape[3]) for ps, k in zip(psums, kinds)]
    flat3 = lambda a: a.reshape(2 * a.shape[1], a.shape[-2], a.shape[-1])
    sums = [_relay_add(f"relay_add_{g}", p3, flat3(rl), yx_arr, rl.shape[1], kinds[g]).reshape(rl.shape) for g, (p3, rl) in enumerate(zip(p3s, relay))]
    relayed = _ici_exchange_relayed(sums)
    halves = [_slot_sum(f"slot_sum_{g}", p3, flat3(d), flat3(r), s_arr, d.shape[1], kinds[g]) for g, (p3, d, r) in enumerate(zip(p3s, direct, relayed))]
    first = ci == 0
    gsh = [jnp.concatenate([jnp.where(first, a, b), jnp.where(first, b, a)], axis=1) for a, b in zip(halves, _pair_swap(halves))]
    grads["ffn_w_gate"], grads["ffn_w_up"] = gsh[0][:L], gsh[0][L:]
    grads["ffn_w_down"] = gsh[1]
    grads["conv_w_in"] = gsh[2]
    grads["conv_w_out"], grads["attn_w_o"], grads["ple_w_gate"] = gsh[3][:NA], gsh[3][NA:NA + NC], gsh[3][NA + NC:]
    grads["attn_w_qkv"] = gsh[4]
    for k, (at, n) in zip(SMALL_SHARDED, sm_offs):
        grads[k] = gsh[5][0, at:at + n].reshape(w[k].shape)

    outs_d, outs_m, outs_v = [], [], []
    for k in W_NAMES:
        d_, m_, v_ = _adamw(f"adamw_{k}", w[k], grads[k], m[k], v[k])
        outs_d.append(d_)
        outs_m.append(m_)
        outs_v.append(v_)
    return (loss_out, dx[None], *[grads[k] for k in W_NAMES], *outs_d, *outs_m, *outs_v)


def kernel(x, p, norm_mix, norm_ffn, norm_ple, conv_w_in, conv_b_in, conv_w_dw, conv_b_dw, conv_ln_g, conv_ln_b, conv_w_out, conv_b_out, pool_w, pool_scale, attn_w_qkv, attn_q_norm, attn_k_norm, attn_sinks, attn_w_o, rel_bias, ffn_w_gate, ffn_w_up, ffn_w_down, ple_w_proj, ple_w_gate, ple_b_gate, loss_target, m_norm_mix, m_norm_ffn, m_norm_ple, m_conv_w_in, m_conv_b_in, m_conv_w_dw, m_conv_b_dw, m_conv_ln_g, m_conv_ln_b, m_conv_w_out, m_conv_b_out, m_pool_w, m_pool_scale, m_attn_w_qkv, m_attn_q_norm, m_attn_k_norm, m_attn_sinks, m_attn_w_o, m_rel_bias, m_ffn_w_gate, m_ffn_w_up, m_ffn_w_down, m_ple_w_proj, m_ple_w_gate, m_ple_b_gate, v_norm_mix, v_norm_ffn, v_norm_ple, v_conv_w_in, v_conv_b_in, v_conv_w_dw, v_conv_b_dw, v_conv_ln_g, v_conv_ln_b, v_conv_w_out, v_conv_b_out, v_pool_w, v_pool_scale, v_attn_w_qkv, v_attn_q_norm, v_attn_k_norm, v_attn_sinks, v_attn_w_o, v_rel_bias, v_ffn_w_gate, v_ffn_w_up, v_ffn_w_down, v_ple_w_proj, v_ple_w_gate, v_ple_b_gate):
    ws_ = (norm_mix, norm_ffn, norm_ple, conv_w_in, conv_b_in, conv_w_dw, conv_b_dw, conv_ln_g, conv_ln_b, conv_w_out, conv_b_out, pool_w, pool_scale, attn_w_qkv, attn_q_norm, attn_k_norm, attn_sinks, attn_w_o, rel_bias, ffn_w_gate, ffn_w_up, ffn_w_down, ple_w_proj, ple_w_gate, ple_b_gate)
    ms_ = (m_norm_mix, m_norm_ffn, m_norm_ple, m_conv_w_in, m_conv_b_in, m_conv_w_dw, m_conv_b_dw, m_conv_ln_g, m_conv_ln_b, m_conv_w_out, m_conv_b_out, m_pool_w, m_pool_scale, m_attn_w_qkv, m_attn_q_norm, m_attn_k_norm, m_attn_sinks, m_attn_w_o, m_rel_bias, m_ffn_w_gate, m_ffn_w_up, m_ffn_w_down, m_ple_w_proj, m_ple_w_gate, m_ple_b_gate)
    vs_ = (v_norm_mix, v_norm_ffn, v_norm_ple, v_conv_w_in, v_conv_b_in, v_conv_w_dw, v_conv_b_dw, v_conv_ln_g, v_conv_ln_b, v_conv_w_out, v_conv_b_out, v_pool_w, v_pool_scale, v_attn_w_qkv, v_attn_q_norm, v_attn_k_norm, v_attn_sinks, v_attn_w_o, v_rel_bias, v_ffn_w_gate, v_ffn_w_up, v_ffn_w_down, v_ple_w_proj, v_ple_w_gate, v_ple_b_gate)
    return _step(x, p, loss_target, dict(zip(W_NAMES, ws_)), dict(zip(W_NAMES, ms_)), dict(zip(W_NAMES, vs_)))
```

```python
import functools
import math

import jax
import jax.numpy as jnp
import numpy as np
from jax import lax
from jax.experimental import pallas as pl
from jax.experimental.pallas import tpu as pltpu

F32 = jnp.float32
BF16 = jnp.bfloat16
MESH = pl.DeviceIdType.MESH

CHUNK = 64
CONV_WIDTH = 31
POOL_WINDOWS = (2, 4, 8, 16)
HEAD_DIM = 64
WINDOW_CHUNKS = 2
QBLOCK = 128
NUM_BUCKETS = 32
REL_MAX_DIST = 128
EPS = 1e-6
NEG_INF = -1e30
ADAM_LR, ADAM_B1, ADAM_B2, ADAM_EPS, ADAM_WD, ADAM_STEP = 0.001, 0.9, 0.999, 1e-08, 0.01, 10
N_SLOTS = 4
LANES = 128
VMEM_LIMIT_BYTES = 56 * 1024 * 1024


def _cparams(sem):
    return pltpu.CompilerParams(dimension_semantics=sem, vmem_limit_bytes=VMEM_LIMIT_BYTES)


def _tile(n, pref, mult=LANES):
    if n <= pref:
        return n
    t = (pref // mult) * mult
    while t >= mult:
        if n % t == 0:
            return t
        t -= mult
    return n


def _sig(z):
    return 1.0 / (1.0 + jnp.exp(-z))


def _op(arr, lead=None, ro=0, co=0, fn=None):
    return (arr, lead, ro, co, fn)


_DOT_DIMS = {"nn": (((1,), (0,)), ((), ())), "nt": (((1,), (1,)), ((), ())), "tn": (((0,), (0,)), ((), ()))}


def _mm(name, mode, dims, tiles, terms, n_acc, epilogue, outs, extras=()):
    M, N, K = dims
    tm, tn, tk = tiles
    assert M % tm == 0 and N % tn == 0 and K % tk == 0, (name, dims, tiles)
    nk = K // tk
    a_tile = (tk, tm) if mode == "tn" else (tm, tk)
    b_tile = (tn, tk) if mode == "nt" else (tk, tn)
    a_fn = (lambda i, j, k: (k, i)) if mode == "tn" else (lambda i, j, k: (i, k))
    b_fn = (lambda i, j, k: (j, k)) if mode == "nt" else (lambda i, j, k: (k, j))
    dn = _DOT_DIMS[mode]

    operands, specs, seen = [], [], {}

    def add(op, tshape, default_fn):
        arr, lead, ro, co, fn = op
        fn = fn or default_fn
        key = (id(arr), lead, ro, co, id(fn) if op[4] is not None else None, tshape)
        if key in seen:
            return seen[key]

        def imap(i, j, k, fn=fn, lead=lead, ro=ro, co=co):
            r, c = fn(i, j, k)
            return (r + ro, c + co) if lead is None else (lead, r + ro, c + co)

        operands.append(arr)
        specs.append(pl.BlockSpec(tshape if lead is None else (None,) + tshape, imap))
        seen[key] = len(operands) - 1
        return seen[key]

    term_idx = [(add(a, a_tile, a_fn), add(b, b_tile, b_fn), acc) for a, b, acc in terms]
    extra_idx = []
    for arr, kind, lead, co in extras:
        if kind == "tile":
            extra_idx.append(add(_op(arr, lead, 0, co), (tm, tn), lambda i, j, k: (i, j)))
        elif kind == "row":
            arr3 = arr.reshape(arr.shape[0], 1, arr.shape[1])
            extra_idx.append(add(_op(arr3, 0 if lead is None else lead, 0, co), (1, tn), lambda i, j, k: (0, j)))
        elif kind == "rows":
            extra_idx.append(add(_op(arr, lead, 0, 0), (tm, arr.shape[-1]), lambda i, j, k: (i, 0)))
        elif kind == "cols":
            extra_idx.append(add(_op(arr, lead, 0, co), (arr.shape[-2], tn), lambda i, j, k: (0, j)))
        else:
            extra_idx.append(add(_op(arr, None, 0, 0), (tm, 1), lambda i, j, k: (i, 0)))
    n_in = len(operands)
    out_shapes, out_specs, aliases = [], [], {}
    for oi, (shape, dtype, lead, alias) in enumerate(outs):
        out_shapes.append(jax.ShapeDtypeStruct(shape, dtype))
        if lead == "rowsum":
            out_specs.append(pl.BlockSpec((None, 1, tn), lambda i, j, k: (i, 0, j)))
        elif lead is None:
            out_specs.append(pl.BlockSpec((tm, tn), lambda i, j, k: (i, j)))
        else:
            out_specs.append(pl.BlockSpec((None, tm, tn), lambda i, j, k, lead=lead: (lead, i, j)))
        if alias is not None:
            operands.append(alias)
            specs.append(pl.BlockSpec(memory_space=pl.ANY))
            aliases[len(operands) - 1] = oi
    n_all_in = len(operands)
    n_out = len(outs)

    def body(*refs):
        ins = refs[:n_in]
        o_refs = refs[n_all_in:n_all_in + n_out]
        accs = refs[n_all_in + n_out:]

        def dots():
            sums = [None] * n_acc
            for ai, bi, acc_i in term_idx:
                a = ins[ai][...]
                b = ins[bi][...]
                if a.dtype != BF16:
                    a = a.astype(BF16)
                if b.dtype != BF16:
                    b = b.astype(BF16)
                d = lax.dot_general(a, b, dn, preferred_element_type=F32)
                sums[acc_i] = d if sums[acc_i] is None else sums[acc_i] + d
            return sums

        def finish(vals):
            res = epilogue(vals, [ins[e][...] for e in extra_idx])
            for o, r in zip(o_refs, res):
                o[...] = r.astype(o.dtype)

        if nk == 1:
            finish(dots())
            return
        k = pl.program_id(2)

        @pl.when(k == 0)
        def _():
            for acc, d in zip(accs, dots()):
                acc[...] = d

        if nk > 2:
            @pl.when((k > 0) & (k < nk - 1))
            def _():
                for acc, d in zip(accs, dots()):
                    acc[...] += d

        @pl.when(k == nk - 1)
        def _():
            finish([acc[...] + d for acc, d in zip(accs, dots())])

    res = pl.pallas_call(
        body, name=name, grid=(M // tm, N // tn, nk), in_specs=specs, out_specs=out_specs, out_shape=out_shapes,
        scratch_shapes=[pltpu.VMEM((tm, tn), F32) for _ in range(n_acc if nk > 1 else 0)], input_output_aliases=aliases,
        compiler_params=_cparams(("parallel", "parallel", "arbitrary")),
    )(*operands)
    return res


def _rowk(name, T, tm, ins, outs, body, scratch=()):
    assert T % tm == 0, (name, T, tm)
    n = T // tm
    specs = []
    for arr, kind in ins:
        w = arr.shape[-1]
        if kind == "tile":
            specs.append(pl.BlockSpec((tm, w), lambda i: (i, 0)))
        elif kind == "prev":
            specs.append(pl.BlockSpec((tm, w), lambda i: (jnp.maximum(i - 1, 0), 0)))
        elif kind == "next":
            specs.append(pl.BlockSpec((tm, w), lambda i, n=n: (jnp.minimum(i + 1, n - 1), 0)))
        else:
            specs.append(pl.BlockSpec(arr.shape, lambda i, nd=arr.ndim: (0,) * nd))
    out_shapes, out_specs = [], []
    for shape, dtype, kind in outs:
        out_shapes.append(jax.ShapeDtypeStruct(shape, dtype))
        if kind == "tile":
            out_specs.append(pl.BlockSpec((tm, shape[-1]), lambda i: (i, 0)))
        else:
            out_specs.append(pl.BlockSpec(shape, lambda i, nd=len(shape): (0,) * nd))
    n_in, n_out = len(ins), len(outs)

    def kbody(*refs):
        body(pl.program_id(0), n, refs[:n_in], refs[n_in:n_in + n_out], refs[n_in + n_out:])

    return pl.pallas_call(
        kbody, name=name, grid=(n,), in_specs=specs, out_specs=out_specs, out_shape=out_shapes,
        scratch_shapes=list(scratch), compiler_params=_cparams(("arbitrary",)),
    )(*[a for a, _ in ins])


def _accum(ref, i, val):
    @pl.when(i == 0)
    def _():
        ref[...] = val

    @pl.when(i > 0)
    def _():
        ref[...] += val


def _colsum(v):
    return jnp.sum(v, axis=0, keepdims=True)


def _rms_r(x):
    return lax.rsqrt(jnp.mean(x * x, axis=-1, keepdims=True) + EPS)


def _rms_fwd(name, x, g, tm):
    T, D = x.shape

    def body(i, n, ins, outs, scr):
        xv = ins[0][...]
        outs[0][...] = (xv * _rms_r(xv) * ins[1][...]).astype(BF16)

    return _rowk(name, T, tm, [(x, "tile"), (g, "full")], [((T, D), BF16, "tile")], body)[0]


def _rms_bwd(name, dres, x, g, dh, tm, want_bf16=False, want_colsum=False):
    T, D = x.shape

    def body(i, n, ins, outs, scr):
        xv = ins[1][...]
        gv = ins[2][...]
        dhv = ins[3][...].astype(F32)
        r = _rms_r(xv)
        xh = xv * r
        dhg = dhv * gv
        dx = ins[0][...] + r * (dhg - xh * jnp.mean(dhg * xh, axis=-1, keepdims=True))
        outs[0][...] = dx
        _accum(outs[1], i, _colsum(dhv * xh))
        o = 2
        if want_bf16:
            outs[o][...] = dx.astype(BF16)
            o += 1
        if want_colsum:
            _accum(outs[o], i, _colsum(dx))

    outs = [((T, D), F32, "tile"), ((1, D), F32, "acc")]
    if want_bf16:
        outs.append(((T, D), BF16, "tile"))
    if want_colsum:
        outs.append(((1, D), F32, "acc"))
    return _rowk(name, T, tm, [(dres, "tile"), (x, "tile"), (g, "full"), (dh, "tile")], outs, body)


def _loss_head(y, target, tm):
    T, D = y.shape

    def body(i, n, ins, outs, scr):
        d = ins[0][...] - ins[1][...]
        outs[0][...] = d * (1.0 / D)
        _accum(outs[1], i, jnp.sum(_colsum(d * d), axis=1, keepdims=True) * (0.5 / D))

    return _rowk("loss_head", T, tm, [(y, "tile"), (target, "tile")], [((T, D), F32, "tile"), ((1, 1), F32, "acc")], body)


CONV_ROWS = 128
SUBLANES = 8
_PHASE_PAD = 32


def _phase_scratch(tm):
    return pltpu.VMEM((SUBLANES, tm + _PHASE_PAD, LANES), F32)


def _phase_copies(buf, shf, base, l0, tm):
    n = tm + _PHASE_PAD - SUBLANES
    for r in range(1, SUBLANES):
        shf[r, pl.ds(0, n), :] = buf[pl.ds(base + r, n), pl.ds(l0, LANES)]


def _phase_window(buf, shf, base, q, row0, rows, l0):
    r = q % SUBLANES
    a = q - r
    if r == 0:
        return buf[pl.ds(base + a + row0, rows), pl.ds(l0, LANES)]
    return shf[r, pl.ds(a + row0, rows), :]


def _dwconv_fwd(name, u1, w_dw, b_dw, tm):
    T, D = u1.shape
    rc_n = tm // CONV_ROWS if tm >= CONV_ROWS else 1
    rows = min(CONV_ROWS, tm)
    halo = CONV_WIDTH - 1

    base = tm - _PHASE_PAD

    def body(i, n, ins, outs, scr):
        buf, shf = scr
        buf[pl.ds(0, tm), :] = jnp.where(i > 0, ins[0][...], 0.0)
        buf[pl.ds(tm, tm), :] = ins[1][...]
        w_ref, b_ref, o_ref = ins[2], ins[3], outs[0]

        def chunk(lc, carry):
            l0 = pl.multiple_of(lc * LANES, LANES)
            _phase_copies(buf, shf, base, l0, tm)
            for rc in range(rc_n):
                acc = jnp.zeros((rows, LANES), F32) + b_ref[:, pl.ds(l0, LANES)]
                for k in range(CONV_WIDTH):
                    acc = acc + _phase_window(buf, shf, base, k + _PHASE_PAD - halo, rc * rows, rows, l0) * w_ref[pl.ds(k, 1), pl.ds(l0, LANES)]
                o_ref[pl.ds(rc * rows, rows), pl.ds(l0, LANES)] = acc
            return carry

        lax.fori_loop(0, D // LANES, chunk, 0)

    return _rowk(name, T, tm, [(u1, "prev"), (u1, "tile"), (w_dw, "full"), (b_dw, "full")], [((T, D), F32, "tile")], body,
                 scratch=[pltpu.VMEM((2 * tm, D), F32), _phase_scratch(tm)])[0]


def _ln_silu_fwd(name, u2, g, b, tm):
    T, D = u2.shape

    def body(i, n, ins, outs, scr):
        v = ins[0][...]
        mu = jnp.mean(v, axis=-1, keepdims=True)
        xc = v - mu
        y = xc * lax.rsqrt(jnp.mean(xc * xc, axis=-1, keepdims=True) + EPS) * ins[1][...] + ins[2][...]
        outs[0][...] = (y * _sig(y)).astype(BF16)

    return _rowk(name, T, tm, [(u2, "tile"), (g, "full"), (b, "full")], [((T, D), BF16, "tile")], body)[0]


def _ln_silu_bwd(name, du4, u2, g, b, tm):
    T, D = u2.shape

    def body(i, n, ins, outs, scr):
        v = ins[1][...]
        gv = ins[2][...]
        mu = jnp.mean(v, axis=-1, keepdims=True)
        xc = v - mu
        r = lax.rsqrt(jnp.mean(xc * xc, axis=-1, keepdims=True) + EPS)
        xh = xc * r
        y = xh * gv + ins[3][...]
        s = _sig(y)
        dy = ins[0][...] * (s * (1.0 + y * (1.0 - s)))
        dyg = dy * gv
        du2 = r * (dyg - jnp.mean(dyg, axis=-1, keepdims=True) - xh * jnp.mean(dyg * xh, axis=-1, keepdims=True))
        outs[0][...] = du2
        _accum(outs[1], i, _colsum(dy * xh))
        _accum(outs[2], i, _colsum(dy))
        _accum(outs[3], i, _colsum(du2))

    return _rowk(name, T, tm, [(du4, "tile"), (u2, "tile"), (g, "full"), (b, "full")],
                 [((T, D), F32, "tile"), ((1, D), F32, "acc"), ((1, D), F32, "acc"), ((1, D), F32, "acc")], body)


def _dwconv_glu_bwd(name, du2, u1, a_, gate, w_dw, tm):
    T, D = u1.shape
    rc_n = tm // CONV_ROWS if tm >= CONV_ROWS else 1
    rows = min(CONV_ROWS, tm)
    halo = CONV_WIDTH - 1

    base = tm - _PHASE_PAD

    def body(i, n, ins, outs, scr):
        bu, bd, shu, shd = scr
        bd[pl.ds(0, tm), :] = ins[0][...]
        bd[pl.ds(tm, tm), :] = jnp.where(i < n - 1, ins[1][...], 0.0)
        bu[pl.ds(0, tm), :] = jnp.where(i > 0, ins[2][...], 0.0)
        bu[pl.ds(tm, tm), :] = ins[3][...]
        a_ref, g_ref, w_ref = ins[4], ins[5], ins[6]
        dag_ref, dw_ref, db_ref = outs

        @pl.when(i == 0)
        def _():
            dw_ref[...] = jnp.zeros_like(dw_ref)
            db_ref[...] = jnp.zeros_like(db_ref)

        def chunk(lc, carry):
            l0 = pl.multiple_of(lc * LANES, LANES)
            l1 = pl.multiple_of(D + lc * LANES, LANES)
            _phase_copies(bd, shd, 0, l0, tm)
            _phase_copies(bu, shu, base, l0, tm)
            for rc in range(rc_n):
                r0 = rc * rows
                d_here = bd[pl.ds(r0, rows), pl.ds(l0, LANES)]
                acc = jnp.zeros((rows, LANES), F32)
                for k in range(CONV_WIDTH):
                    wk = w_ref[pl.ds(k, 1), pl.ds(l0, LANES)]
                    acc = acc + _phase_window(bd, shd, 0, halo - k, r0, rows, l0) * wk
                    dw_ref[pl.ds(k, 1), pl.ds(l0, LANES)] += _colsum(d_here * _phase_window(bu, shu, base, k + _PHASE_PAD - halo, r0, rows, l0))
                av = a_ref[pl.ds(r0, rows), pl.ds(l0, LANES)].astype(F32)
                sg = _sig(g_ref[pl.ds(r0, rows), pl.ds(l0, LANES)].astype(F32))
                da = acc * sg
                dg = acc * av * sg * (1.0 - sg)
                dag_ref[pl.ds(r0, rows), pl.ds(l0, LANES)] = da.astype(BF16)
                dag_ref[pl.ds(r0, rows), pl.ds(l1, LANES)] = dg.astype(BF16)
                db_ref[:, pl.ds(l0, LANES)] += _colsum(da)
                db_ref[:, pl.ds(l1, LANES)] += _colsum(dg)
            return carry

        lax.fori_loop(0, D // LANES, chunk, 0)

    return _rowk(name, T, tm, [(du2, "tile"), (du2, "next"), (u1, "prev"), (u1, "tile"), (a_, "tile"), (gate, "tile"), (w_dw, "full")],
                 [((T, 2 * D), BF16, "tile"), ((CONV_WIDTH, D), F32, "acc"), ((1, 2 * D), F32, "acc")], body,
                 scratch=[pltpu.VMEM((2 * tm, D), F32), pltpu.VMEM((2 * tm, D), F32), _phase_scratch(tm), _phase_scratch(tm)])


def _row_index(i, tm, r0, rows):
    return (i * tm + r0 + lax.broadcasted_iota(jnp.int32, (rows, 1), 0)).astype(F32)


def _pool_fwd(name, x, g, tm):
    T, D = x.shape
    gc = D // len(POOL_WINDOWS)
    rows = min(CONV_ROWS, tm)
    rc_n = tm // rows

    def body(i, n, ins, outs, scr):
        buf = scr[0]
        xp = ins[0][...]
        buf[pl.ds(0, tm), :] = jnp.where(i > 0, xp * _rms_r(xp) * ins[2][...], 0.0)
        xc = ins[1][...]
        buf[pl.ds(tm, tm), :] = xc * _rms_r(xc) * ins[2][...]
        o_ref = outs[0]
        for gi, w in enumerate(POOL_WINDOWS):
            def chunk(lc, carry, gi=gi, w=w):
                l0 = pl.multiple_of(gi * gc + lc * LANES, LANES)
                for rc in range(rc_n):
                    r0 = rc * rows
                    acc = buf[pl.ds(tm + r0, rows), pl.ds(l0, LANES)]
                    here = acc
                    for d in range(1, w):
                        acc = acc + buf[pl.ds(tm + r0 - d, rows), pl.ds(l0, LANES)]
                    cnt = jnp.minimum(_row_index(i, tm, r0, rows) + 1.0, float(w))
                    o_ref[pl.ds(r0, rows), pl.ds(l0, LANES)] = (acc / cnt - here).astype(BF16)
                return carry

            lax.fori_loop(0, gc // LANES, chunk, 0)

    return _rowk(name, T, tm, [(x, "prev"), (x, "tile"), (g, "full")], [((T, D), BF16, "tile")], body,
                 scratch=[pltpu.VMEM((2 * tm, D), F32)])[0]


def _pool_bwd(name, dmix, tm):
    T, D = dmix.shape
    gc = D // len(POOL_WINDOWS)
    rows = min(CONV_ROWS, tm)
    rc_n = tm // rows

    def body(i, n, ins, outs, scr):
        buf = scr[0]
        o_ref = outs[0]
        t_here = (i * tm + lax.broadcasted_iota(jnp.int32, (tm, 1), 0)).astype(F32) + 1.0
        for gi, w in enumerate(POOL_WINDOWS):
            cols = pl.ds(gi * gc, gc)
            buf[pl.ds(0, tm), cols] = ins[0][:, cols] / jnp.minimum(t_here, float(w))
            buf[pl.ds(tm, tm), cols] = jnp.where(i < n - 1, ins[1][:, cols] / float(w), 0.0)

            def chunk(lc, carry, gi=gi, w=w):
                l0 = pl.multiple_of(gi * gc + lc * LANES, LANES)
                for rc in range(rc_n):
                    r0 = rc * rows
                    acc = -ins[0][pl.ds(r0, rows), pl.ds(l0, LANES)]
                    for d in range(w):
                        acc = acc + buf[pl.ds(r0 + d, rows), pl.ds(l0, LANES)]
                    o_ref[pl.ds(r0, rows), pl.ds(l0, LANES)] = acc
                return carry

            lax.fori_loop(0, gc // LANES, chunk, 0)

    return _rowk(name, T, tm, [(dmix, "tile"), (dmix, "next")], [((T, D), F32, "tile")], body,
                 scratch=[pltpu.VMEM((2 * tm, D), F32)])[0]


def _pool_scale_bwd(name, dy, y0, scale, tm):
    T, D = dy.shape

    def body(i, n, ins, outs, scr):
        d = ins[0][...]
        outs[0][...] = (d * ins[2][...]).astype(BF16)
        _accum(outs[1], i, _colsum(d * ins[1][...]))

    return _rowk(name, T, tm, [(dy, "tile"), (y0, "tile"), (scale, "full")], [((T, D), BF16, "tile"), ((1, D), F32, "acc")], body)


def _t5_bucket_np():
    i = np.arange(QBLOCK)[:, None]
    j = np.arange(2 * QBLOCK)[None, :]
    rel = j - QBLOCK - i
    nb = NUM_BUCKETS // 2
    n = -rel
    ret = np.where(n < 0, nb, 0)
    n = np.abs(n)
    max_exact = nb // 2
    nf = np.maximum(n, 1).astype(np.float32)
    large = max_exact + (np.log(nf / np.float32(max_exact)) / np.float32(math.log(REL_MAX_DIST / max_exact))
                         * np.float32(nb - max_exact)).astype(np.int32)
    large = np.minimum(large, nb - 1)
    return (ret + np.where(n < max_exact, n, large)).astype(np.int32)


def _bias_fwd(rel_bias, bucket):
    nb, nh = rel_bias.shape

    def body(rb_ref, bk_ref, o_ref):
        h = pl.program_id(0)
        bk = bk_ref[...]
        acc = jnp.zeros(bk.shape, F32)
        for b in range(nb):
            acc = jnp.where(bk == b, rb_ref[b, h], acc)
        o_ref[...] = acc

    return pl.pallas_call(
        body, name="attn_bias_fwd", grid=(nh,),
        in_specs=[pl.BlockSpec(memory_space=pltpu.SMEM), pl.BlockSpec(bucket.shape, lambda h: (0, 0))],
        out_specs=pl.BlockSpec((None,) + bucket.shape, lambda h: (h, 0, 0)),
        out_shape=jax.ShapeDtypeStruct((nh,) + bucket.shape, F32), compiler_params=_cparams(("arbitrary",)),
    )(rel_bias, bucket)


def _bias_bwd(dbias, bucket, nb):
    nh = dbias.shape[0]

    def body(db_ref, bk_ref, o_ref):
        h = pl.program_id(0)
        bk = bk_ref[...]
        d = db_ref[...]
        for b in range(nb):
            o_ref[h, b] = jnp.sum(jnp.where(bk == b, d, 0.0))

    return pl.pallas_call(
        body, name="attn_bias_bwd", grid=(nh,),
        in_specs=[pl.BlockSpec((None,) + bucket.shape, lambda h: (h, 0, 0)), pl.BlockSpec(bucket.shape, lambda h: (0, 0))],
        out_specs=pl.BlockSpec(memory_space=pltpu.SMEM),
        out_shape=jax.ShapeDtypeStruct((nh, nb), F32), compiler_params=_cparams(("arbitrary",)),
    )(dbias, bucket)


def _head_norm_fwd(name, q2, g, tm):
    R, W = q2.shape

    def body(i, n, ins, outs, scr):
        v = ins[0][...]
        outs[0][...] = (v * _rms_r(v) * ins[1][...]).astype(BF16)

    return _rowk(name, R, tm, [(q2, "tile"), (g, "full")], [((R, W), BF16, "tile")], body)[0]


def _head_norm_bwd(name, dqn, q2, g, tm):
    R, W = q2.shape

    def body(i, n, ins, outs, scr):
        v = ins[1][...]
        d = ins[0][...]
        r = _rms_r(v)
        xh = v * r
        dg = d * ins[2][...]
        outs[0][...] = (r * (dg - xh * jnp.mean(dg * xh, axis=-1, keepdims=True))).astype(BF16)
        _accum(outs[1], i, _colsum(d * xh))

    return _rowk(name, R, tm, [(dqn, "tile"), (q2, "tile"), (g, "full")], [((R, W), BF16, "tile"), ((1, W), F32, "acc")], body)


def _band_merge(name, own, prev, k3=None, g=None):
    H, T, W = own.shape
    nblk = T // QBLOCK

    def body(*refs):
        o_ref, p_ref = refs[0], refs[1]
        out_ref = refs[4] if k3 is not None else refs[2]

        def blk(m, carry):
            r0 = pl.multiple_of(m * QBLOCK, QBLOCK)
            rn = pl.multiple_of(jnp.minimum(m + 1, nblk - 1) * QBLOCK, QBLOCK)
            d = o_ref[pl.ds(r0, QBLOCK), :] + jnp.where(m < nblk - 1, p_ref[pl.ds(rn, QBLOCK), :], 0.0)
            if k3 is None:
                out_ref[pl.ds(r0, QBLOCK), :] = d.astype(BF16)
                return carry
            v = refs[2][pl.ds(r0, QBLOCK), :]
            r = _rms_r(v)
            xh = v * r
            dg = d * refs[3][...]
            out_ref[pl.ds(r0, QBLOCK), :] = (r * (dg - xh * jnp.mean(dg * xh, axis=-1, keepdims=True))).astype(BF16)
            return carry + _colsum(d * xh)

        tot = lax.fori_loop(0, nblk, blk, jnp.zeros((1, W), F32))
        if k3 is not None:
            _accum(refs[5], pl.program_id(0), tot)

    head = pl.BlockSpec((None, T, W), lambda h: (h, 0, 0))
    ins, in_specs = [own, prev], [head, head]
    out_shape, out_specs = [jax.ShapeDtypeStruct((H, T, W), BF16)], [head]
    if k3 is not None:
        ins += [k3, g]
        in_specs += [head, pl.BlockSpec(g.shape, lambda h: (0, 0))]
        out_shape.append(jax.ShapeDtypeStruct((1, W), F32))
        out_specs.append(pl.BlockSpec((1, W), lambda h: (0, 0)))
    return pl.pallas_call(body, name=name, grid=(H,), in_specs=in_specs, out_specs=out_specs, out_shape=out_shape,
                          compiler_params=_cparams(("arbitrary",)))(*ins)


def _masked_bias(bias):
    rows = bias.shape[1]
    qc = (jnp.arange(rows)[:, None] % QBLOCK) // CHUNK
    j = jnp.arange(2 * QBLOCK)[None, :]
    kc = j // CHUNK - QBLOCK // CHUNK
    ok = (kc <= qc) & (kc >= qc - WINDOW_CHUNKS)
    hide = lambda visible: jnp.where(visible, 0.0, NEG_INF).astype(F32)[None]
    return jnp.stack([bias + hide(ok & (j >= QBLOCK)), bias + hide(ok)], axis=0)


def _attn_logits(q, kb, bias_masked, sink):
    s = lax.dot_general(q, kb, _DOT_DIMS["nt"], preferred_element_type=F32) * (HEAD_DIM ** -0.5) + bias_masked
    m = jnp.maximum(jnp.max(s, axis=-1, keepdims=True), sink)
    e = jnp.exp(s - m)
    es = jnp.exp(sink - m)
    den = jnp.sum(e, axis=-1, keepdims=True) + es
    return e / den, es / den


def _heads_per_step(n_kv):
    return 4 if n_kv % 4 == 0 else (2 if n_kv % 2 == 0 else 1)


def _attn_specs(group, hp, rows):
    blk = lambda hn, fn: pl.BlockSpec((hn, QBLOCK, HEAD_DIM), fn)
    cur = lambda h, n: (h, n, 0)
    prv = lambda h, n: (h, jnp.maximum(n - 1, 0), 0)
    bsp = pl.BlockSpec((hp, rows, 2 * QBLOCK), lambda h, n: (h, 0, 0))
    ssp = pl.BlockSpec((hp, rows, 1), lambda h, n: (h, 0, 0))
    bmsp = pl.BlockSpec((None, hp, rows, 2 * QBLOCK), lambda h, n: (jnp.minimum(n, 1), h, 0, 0))
    return blk(hp * group, cur), blk(hp, prv), blk(hp, cur), bsp, ssp, bmsp


def _attn_fwd(qn, kn, v, bias, sink_rows, n_kv, group, T):
    nblk = T // QBLOCK
    rows = group * QBLOCK
    hp = _heads_per_step(n_kv)

    def body(q_ref, kp_ref, kc_ref, vp_ref, vc_ref, b_ref, s_ref, o_ref):
        for hh in range(hp):
            q = q_ref[pl.ds(hh * group, group)].reshape(rows, HEAD_DIM)
            kb = jnp.concatenate([kp_ref[hh], kc_ref[hh]], axis=0)
            vb = jnp.concatenate([vp_ref[hh], vc_ref[hh]], axis=0)
            p, _ = _attn_logits(q, kb, b_ref[hh], s_ref[hh])
            o = lax.dot_general(p.astype(BF16), vb, _DOT_DIMS["nn"], preferred_element_type=F32)
            o_ref[pl.ds(hh * group, group)] = o.reshape(group, QBLOCK, HEAD_DIM).astype(BF16)

    qs, kp, kc, _, ssp, bmsp = _attn_specs(group, hp, rows)
    return pl.pallas_call(
        body, name="attn_fwd", grid=(n_kv // hp, nblk), in_specs=[qs, kp, kc, kp, kc, bmsp, ssp],
        out_specs=qs, out_shape=jax.ShapeDtypeStruct(qn.shape, BF16), compiler_params=_cparams(("arbitrary", "arbitrary")),
    )(qn, kn, kn, v, v, bias, sink_rows)


def _attn_bwd(qn, kn, v, bias, sink_rows, do, n_kv, group, T):
    nblk = T // QBLOCK
    rows = group * QBLOCK
    scale = HEAD_DIM ** -0.5
    hp = _heads_per_step(n_kv)

    def body(q_ref, kp_ref, kc_ref, vp_ref, vc_ref, b_ref, s_ref, do_ref, dq_ref, dko_ref, dkp_ref, dvo_ref, dvp_ref, db_ref, ds_ref):
        n = pl.program_id(1)
        for hh in range(hp):
            q = q_ref[pl.ds(hh * group, group)].reshape(rows, HEAD_DIM)
            dov = do_ref[pl.ds(hh * group, group)].reshape(rows, HEAD_DIM)
            kb = jnp.concatenate([kp_ref[hh], kc_ref[hh]], axis=0)
            vb = jnp.concatenate([vp_ref[hh], vc_ref[hh]], axis=0)
            p, ps = _attn_logits(q, kb, b_ref[hh], s_ref[hh])
            dp = lax.dot_general(dov, vb, _DOT_DIMS["nt"], preferred_element_type=F32)
            delta = jnp.sum(p * dp, axis=-1, keepdims=True)
            dl = p * (dp - delta)
            dlb = dl.astype(BF16)
            dq = lax.dot_general(dlb, kb, _DOT_DIMS["nn"], preferred_element_type=F32) * scale
            dkb = lax.dot_general(dlb, q, _DOT_DIMS["tn"], preferred_element_type=F32) * scale
            dvb = lax.dot_general(p.astype(BF16), dov, _DOT_DIMS["tn"], preferred_element_type=F32)
            dq_ref[pl.ds(hh * group, group)] = dq.reshape(group, QBLOCK, HEAD_DIM)
            dkp_ref[hh] = dkb[:QBLOCK]
            dko_ref[hh] = dkb[QBLOCK:]
            dvp_ref[hh] = dvb[:QBLOCK]
            dvo_ref[hh] = dvb[QBLOCK:]
            dsink = -ps * delta

            @pl.when(n == 0)
            def _(hh=hh, dl=dl, dsink=dsink):
                db_ref[hh] = dl
                ds_ref[hh] = dsink

            @pl.when(n > 0)
            def _(hh=hh, dl=dl, dsink=dsink):
                db_ref[hh] += dl
                ds_ref[hh] += dsink

    qs, kp, kc, bsp, ssp, bmsp = _attn_specs(group, hp, rows)
    kv_shape = jax.ShapeDtypeStruct(kn.shape, F32)
    return pl.pallas_call(
        body, name="attn_bwd", grid=(n_kv // hp, nblk), in_specs=[qs, kp, kc, kp, kc, bmsp, ssp, qs],
        out_specs=[qs, kc, kc, kc, kc, bsp, ssp],
        out_shape=[jax.ShapeDtypeStruct(qn.shape, F32), kv_shape, kv_shape, kv_shape, kv_shape,
                   jax.ShapeDtypeStruct(bias.shape[1:], F32), jax.ShapeDtypeStruct(sink_rows.shape, F32)],
        compiler_params=_cparams(("arbitrary", "arbitrary")),
    )(qn, kn, kn, v, v, bias, sink_rows, do)


def _adamw(name, w, g, m, v):
    shape = w.shape
    w2, g2, m2, v2 = (a.reshape(-1, shape[-1]) for a in (w, g, m, v))
    R, W = w2.shape
    tm = _tile(R, max(8, (1 << 19) // W), 8)
    d1 = 1.0 - ADAM_B1 ** ADAM_STEP
    d2 = 1.0 - ADAM_B2 ** ADAM_STEP

    def body(i, n, ins, outs, scr):
        wv, gv = ins[0][...], ins[1][...]
        mn = ADAM_B1 * ins[2][...] + (1.0 - ADAM_B1) * gv
        vn = ADAM_B2 * ins[3][...] + (1.0 - ADAM_B2) * (gv * gv)
        outs[0][...] = -ADAM_LR * ((mn / d1) / (jnp.sqrt(vn / d2) + ADAM_EPS) + ADAM_WD * wv)
        outs[1][...] = mn
        outs[2][...] = vn

    d, mn, vn = _rowk(name, R, tm, [(w2, "tile"), (g2, "tile"), (m2, "tile"), (v2, "tile")],
                      [((R, W), F32, "tile")] * 3, body)
    return d.reshape(shape), mn.reshape(shape), vn.reshape(shape)


def _first(accs, extras):
    return [accs[0]]


def _swiglu_fwd(accs, extras):
    a, b = accs
    s = _sig(a)
    t = a * s
    return [t, b * (s + t * (1.0 - s)), t * b]


def _ple_fwd(accs, extras):
    b, p, w_proj, x2 = extras
    g = _sig(accs[0] + b)
    q = lax.dot_general(p.astype(BF16), w_proj.astype(BF16), _DOT_DIMS["nn"], preferred_element_type=F32)
    return [g, x2 + g * q]


def _ple_bwd(accs, extras):
    d, g = extras
    dz = d * accs[0] * g * (1.0 - g)
    return [d * g, dz, _colsum(dz)]


def _swiglu_bwd(accs, extras):
    return [accs[0] * extras[1].astype(F32), accs[0] * extras[0].astype(F32)]


def _local_step(x, p, target, wts, small):
    T, D = x.shape
    L, _, PLE = p.shape
    gu, down, cin, sq, qkv_w = wts["gu"], wts["down"], wts["cin"], wts["sq"], wts["qkv"]
    FF = gu.shape[2]
    NA, NC = cin.shape[0], qkv_w.shape[0]
    QW = qkv_w.shape[2]
    KVD = (QW - D) // 2
    n_heads, n_kv = D // HEAD_DIM, KVD // HEAD_DIM
    group = n_heads // n_kv
    nblk = T // QBLOCK
    GC = D // len(POOL_WINDOWS)

    tr = _tile(T, 256, 8)
    tmm = _tile(T, 1024)
    tD = _tile(D, 1024)
    tDk = _tile(D, 2048)
    tD2 = _tile(D, 512)
    tF = _tile(FF, 512)
    tFk = _tile(FF, 2816)
    tFw = _tile(FF, 1408)
    tP = _tile(PLE, 512)
    tQ = _tile(QW, 768)
    tT = _tile(T, 1024)

    bucket = jnp.asarray(_t5_bucket_np())
    saved = []
    xs = x

    for i in range(L):
        kind, j = i % 3, i // 3
        sv = {"x": xs}
        h1 = _rms_fwd(f"rms_mix_{i}", xs, small["norm_mix"][i:i + 1], tr)
        if kind == 0:
            a_, gate, u1 = _mm(
                f"conv_in_{i}", "nn", (T, D, D), (tmm, tD2, tDk),
                [(_op(h1), _op(cin, j), 0), (_op(h1), _op(cin, j, 0, D // tD2), 1)], 2,
                lambda accs, ex: (lambda a, g: [a, g, a * _sig(g)])(accs[0] + ex[0], accs[1] + ex[1]),
                [((T, D), BF16, None, None), ((T, D), BF16, None, None), ((T, D), F32, None, None)],
                extras=[(small["conv_b_in"], "row", j, 0), (small["conv_b_in"], "row", j, D // tD2)])
            u2 = _dwconv_fwd(f"dwconv_{i}", u1, small["conv_w_dw"][j], small["conv_b_dw"][j:j + 1], tr)
            u4 = _ln_silu_fwd(f"ln_silu_{i}", u2, small["conv_ln_g"][j:j + 1], small["conv_ln_b"][j:j + 1], tr)
            x1, = _mm(f"conv_out_{i}", "nn", (T, D, D), (tmm, tD2, tDk), [(_op(u4), _op(sq, j), 0)], 1,
                      lambda accs, ex: [accs[0] + ex[0] + ex[1]], [((T, D), F32, None, None)],
                      extras=[(small["conv_b_out"], "row", j, 0), (xs, "tile", None, 0)])
            sv.update(h1=h1, a=a_, gate=gate, u1=u1, u2=u2, u4=u4)
        elif kind == 1:
            mix = _pool_fwd(f"pool_{i}", xs, small["norm_mix"][i:i + 1], tr)
            pw = small["pool_w"][j].reshape(len(POOL_WINDOWS) * GC, GC)
            kb = GC // _tile(GC, 512)
            tg = _tile(GC, 512)
            y0, x1 = _mm(f"pool_mm_{i}", "nn", (T, D, GC), (tmm, GC, tg),
                         [(_op(mix, fn=lambda i_, j_, k_, kb=kb: (i_, j_ * kb + k_)), _op(pw, fn=lambda i_, j_, k_, kb=kb: (j_ * kb + k_, 0)), 0)], 1,
                         lambda accs, ex: [accs[0], ex[1] + accs[0] * ex[0]],
                         [((T, D), F32, None, None), ((T, D), F32, None, None)],
                         extras=[(small["pool_scale"][j:j + 1], "row", None, 0), (xs, "tile", None, 0)])
            sv.update(mix=mix, y0=y0, pw=pw)
        else:
            qkv, = _mm(f"qkv_{i}", "nn", (T, QW, D), (tmm, tQ, tDk), [(_op(h1), _op(qkv_w, j), 0)], 1, _first,
                       [((T, QW), F32, None, None)])
            q_hm = qkv[:, :D].reshape(T, n_heads, HEAD_DIM).transpose(1, 0, 2).reshape(n_heads * T, HEAD_DIM)
            k_hm = qkv[:, D:D + KVD].reshape(T, n_kv, HEAD_DIM).transpose(1, 0, 2).reshape(n_kv * T, HEAD_DIM)
            v_hm = qkv[:, D + KVD:].reshape(T, n_kv, HEAD_DIM).transpose(1, 0, 2).astype(BF16)
            th = _tile(T, 2048, 8)
            qn = _head_norm_fwd(f"qnorm_{i}", q_hm, small["attn_q_norm"][j:j + 1], th).reshape(n_heads, T, HEAD_DIM)
            kn = _head_norm_fwd(f"knorm_{i}", k_hm, small["attn_k_norm"][j:j + 1], th).reshape(n_kv, T, HEAD_DIM)
            bias = _masked_bias(_bias_fwd(small["rel_bias"], bucket).reshape(n_kv, group * QBLOCK, 2 * QBLOCK))
            sink_rows = jnp.broadcast_to(small["attn_sinks"][j].reshape(n_kv, group, 1, 1), (n_kv, group, QBLOCK, 1)).reshape(n_kv, group * QBLOCK, 1)
            o_hm = _attn_fwd(qn, kn, v_hm, bias, sink_rows, n_kv, group, T)
            o = o_hm.transpose(1, 0, 2).reshape(T, D)
            x1, = _mm(f"attn_o_{i}", "nn", (T, D, D), (tmm, tD2, tDk), [(_op(o), _op(sq, NA + j), 0)], 1,
                      lambda accs, ex: [accs[0] + ex[0]], [((T, D), F32, None, None)], extras=[(xs, "tile", None, 0)])
            sv.update(h1=h1, q_hm=q_hm, k_hm=k_hm, v_hm=v_hm, qn=qn, kn=kn, bias=bias, sink_rows=sink_rows, o=o)
        h2 = _rms_fwd(f"rms_ffn_{i}", x1, small["norm_ffn"][i:i + 1], tr)
        a, b, f = _mm(f"ffn_up_{i}", "nn", (T, FF, D), (tmm, tF, tDk), [(_op(h2), _op(gu, i), 0), (_op(h2), _op(gu, L + i), 1)], 2,
                      _swiglu_fwd, [((T, FF), BF16, None, None)] * 3)
        x2, = _mm(f"ffn_down_{i}", "nn", (T, D, FF), (tmm, tD2, tFk), [(_op(f), _op(down, i), 0)], 1,
                  lambda accs, ex: [accs[0] + ex[0]], [((T, D), F32, None, None)], extras=[(x1, "tile", None, 0)])
        h3 = _rms_fwd(f"rms_ple_{i}", x2, small["norm_ple"][i:i + 1], tr)
        gt, x3 = _mm(f"ple_gate_{i}", "nn", (T, D, D), (tmm, tD2, tDk), [(_op(h3), _op(sq, NA + NC + i), 0)], 1, _ple_fwd,
                     [((T, D), F32, None, None), ((T, D), F32, None, None)],
                     extras=[(small["ple_b_gate"], "row", i, 0), (p, "rows", i, 0), (small["ple_w_proj"], "cols", i, 0), (x2, "tile", None, 0)])
        sv.update(x1=x1, h2=h2, a=sv.get("a"), fa=a, fb=b, f=f, x2=x2, h3=h3, gt=gt)
        saved.append(sv)
        xs = x3

    dx, loss = _loss_head(xs, target, tr)

    g_gu = g_down = g_cin = g_sq = g_qkv = None
    gs = {k: [None] * v.shape[0] for k, v in small.items() if k != "rel_bias"}
    gs["rel_bias"] = None
    gs["ple_w_proj"] = [None] * L

    for i in reversed(range(L)):
        kind, j = i % 3, i // 3
        sv = saved[i]
        dq, dz, dbg = _mm(f"ple_bwd_{i}", "nn", (T, D, PLE), (tmm, tD2, tP), [(_op(p, i), _op(small["ple_w_proj"], i), 0)], 1, _ple_bwd,
                          [((T, D), BF16, None, None), ((T, D), BF16, None, None), ((T // tmm, 1, D), F32, "rowsum", None)],
                          extras=[(dx, "tile", None, 0), (sv["gt"], "tile", None, 0)])
        gs["ple_b_gate"][i] = jnp.sum(dbg, axis=0)
        gs["ple_w_proj"][i], = _mm(f"d_ple_proj_{i}", "tn", (PLE, D, T), (tP, tD, tT), [(_op(p, i), _op(dq), 0)], 1, _first,
                                   [((PLE, D), F32, None, None)])
        g_sq, = _mm(f"d_ple_gate_{i}", "tn", (D, D, T), (tD, tD, tT), [(_op(sv["h3"]), _op(dz), 0)], 1, _first,
                    [(sq.shape, BF16, NA + NC + i, g_sq)])
        dh3, = _mm(f"dh_ple_{i}", "nt", (T, D, D), (tmm, tD2, tDk), [(_op(dz), _op(sq, NA + NC + i), 0)], 1, _first,
                   [((T, D), F32, None, None)])
        dx2, gs["norm_ple"][i], dx2b = _rms_bwd(f"rms_ple_bwd_{i}", dx, sv["x2"], small["norm_ple"][i:i + 1], dh3, tr, want_bf16=True)
        da, db = _mm(f"d_ffn_act_{i}", "nt", (T, FF, D), (tmm, tF, tDk), [(_op(dx2b), _op(down, i), 0)], 1, _swiglu_bwd,
                     [((T, FF), BF16, None, None)] * 2, extras=[(sv["fa"], "tile", None, 0), (sv["fb"], "tile", None, 0)])
        g_down, = _mm(f"d_ffn_down_{i}", "tn", (FF, D, T), (tFw, tD, tT), [(_op(sv["f"]), _op(dx2b), 0)], 1, _first,
                      [(down.shape, BF16, i, g_down)])
        g_gu, = _mm(f"d_ffn_gate_{i}", "tn", (D, FF, T), (tD, tFw, tT), [(_op(sv["h2"]), _op(da), 0)], 1, _first,
                    [(gu.shape, BF16, i, g_gu)])
        g_gu, = _mm(f"d_ffn_up_{i}", "tn", (D, FF, T), (tD, tFw, tT), [(_op(sv["h2"]), _op(db), 0)], 1, _first,
                    [(gu.shape, BF16, L + i, g_gu)])
        dh2, = _mm(f"dh_ffn_{i}", "nt", (T, D, FF), (tmm, tD2, tFk), [(_op(da), _op(gu, i), 0), (_op(db), _op(gu, L + i), 0)], 1, _first,
                   [((T, D), F32, None, None)])
        want_cs = kind == 0
        res = _rms_bwd(f"rms_ffn_bwd_{i}", dx2, sv["x1"], small["norm_ffn"][i:i + 1], dh2, tr, want_bf16=True, want_colsum=want_cs)
        dx1, gs["norm_ffn"][i], dx1b = res[:3]
        xin = sv["x"]
        if kind == 0:
            gs["conv_b_out"][j] = res[3]
            g_sq, = _mm(f"d_conv_out_{i}", "tn", (D, D, T), (tD, tD, tT), [(_op(sv["u4"]), _op(dx1b), 0)], 1, _first,
                        [(sq.shape, BF16, j, g_sq)])
            du4, = _mm(f"dh_conv_out_{i}", "nt", (T, D, D), (tmm, tD2, tDk), [(_op(dx1b), _op(sq, j), 0)], 1, _first,
                       [((T, D), F32, None, None)])
            du2, gs["conv_ln_g"][j], gs["conv_ln_b"][j], gs["conv_b_dw"][j] = _ln_silu_bwd(
                f"ln_silu_bwd_{i}", du4, sv["u2"], small["conv_ln_g"][j:j + 1], small["conv_ln_b"][j:j + 1], tr)
            dag, gs["conv_w_dw"][j], gs["conv_b_in"][j] = _dwconv_glu_bwd(
                f"dwconv_bwd_{i}", du2, sv["u1"], sv["a"], sv["gate"], small["conv_w_dw"][j], tr)
            g_cin, = _mm(f"d_conv_in_{i}", "tn", (D, 2 * D, T), (tD, tD, tT), [(_op(sv["h1"]), _op(dag), 0)], 1, _first,
                         [(cin.shape, BF16, j, g_cin)])
            dh1, = _mm(f"dh_conv_in_{i}", "nt", (T, D, 2 * D), (tmm, tD2, tDk), [(_op(dag), _op(cin, j), 0)], 1, _first,
                       [((T, D), F32, None, None)])
        elif kind == 1:
            dys, gs["pool_scale"][j] = _pool_scale_bwd(f"pool_scale_bwd_{i}", dx1, sv["y0"], small["pool_scale"][j:j + 1], tr)
            tg = _tile(GC, 512)
            kb = GC // tg
            dmix, = _mm(f"dh_pool_{i}", "nt", (T, D, GC), (tmm, GC, tg),
                        [(_op(dys, fn=lambda i_, j_, k_, kb=kb: (i_, j_ * kb + k_)), _op(sv["pw"]), 0)], 1, _first,
                        [((T, D), F32, None, None)])
            ng = len(POOL_WINDOWS)
            gs["pool_w"][j], = _mm(f"d_pool_w_{i}", "tn", (D, GC, T), (GC, GC, tT),
                                   [(_op(sv["mix"]), _op(dys, fn=lambda i_, j_, k_: (k_, i_)), 0)], 1, _first,
                                   [((D, GC), F32, None, None)])
            gs["pool_w"][j] = gs["pool_w"][j].reshape(ng, GC, GC)
            dh1 = _pool_bwd(f"pool_bwd_{i}", dmix, tr)
        else:
            g_sq, = _mm(f"d_attn_o_{i}", "tn", (D, D, T), (tD, tD, tT), [(_op(sv["o"]), _op(dx1b), 0)], 1, _first,
                        [(sq.shape, BF16, NA + j, g_sq)])
            do, = _mm(f"dh_attn_o_{i}", "nt", (T, D, D), (tmm, tD2, tDk), [(_op(dx1b), _op(sq, NA + j), 0)], 1, _first,
                      [((T, D), BF16, None, None)])
            do_hm = do.reshape(T, n_heads, HEAD_DIM).transpose(1, 0, 2)
            dqn, dko, dkp, dvo, dvp, dbias, dsink = _attn_bwd(sv["qn"], sv["kn"], sv["v_hm"], sv["bias"], sv["sink_rows"], do_hm, n_kv, group, T)
            th = _tile(T, 2048, 8)
            dq_hm, gs["attn_q_norm"][j] = _head_norm_bwd(f"qnorm_bwd_{i}", dqn.reshape(n_heads * T, HEAD_DIM), sv["q_hm"],
                                                         small["attn_q_norm"][j:j + 1], th)
            dk_hm, gs["attn_k_norm"][j] = _band_merge(f"knorm_bwd_{i}", dko, dkp, sv["k_hm"].reshape(n_kv, T, HEAD_DIM), small["attn_k_norm"][j:j + 1])
            dv_hm, = _band_merge(f"v_merge_{i}", dvo, dvp)
            gs["attn_sinks"][j] = jnp.sum(dsink.reshape(n_heads, QBLOCK), axis=1).reshape(1, n_heads)
            rb = _bias_bwd(dbias.reshape(n_heads, QBLOCK, 2 * QBLOCK), bucket, NUM_BUCKETS).T
            gs["rel_bias"] = rb if gs["rel_bias"] is None else gs["rel_bias"] + rb
            tok = lambda t_, nh: t_.reshape(nh, T, HEAD_DIM).transpose(1, 0, 2).reshape(T, nh * HEAD_DIM)
            dqkv = jnp.concatenate([tok(dq_hm, n_heads), tok(dk_hm, n_kv), tok(dv_hm, n_kv)], axis=1)
            g_qkv, = _mm(f"d_qkv_{i}", "tn", (D, QW, T), (tD, tQ, tT), [(_op(sv["h1"]), _op(dqkv), 0)], 1, _first,
                         [(qkv_w.shape, BF16, j, g_qkv)])
            dh1, = _mm(f"dh_qkv_{i}", "nt", (T, D, QW), (tmm, tD2, _tile(QW, 3072)), [(_op(dqkv), _op(qkv_w, j), 0)], 1, _first,
                       [((T, D), F32, None, None)])
        dx, gs["norm_mix"][i] = _rms_bwd(f"rms_mix_bwd_{i}", dx1, xin, small["norm_mix"][i:i + 1], dh1, tr)

    big = {"gu": g_gu, "down": g_down, "cin": g_cin, "sq": g_sq, "qkv": g_qkv}
    gsmall = {}
    for k, v in gs.items():
        if k == "rel_bias":
            gsmall[k] = v
        else:
            gsmall[k] = jnp.stack([t.reshape(small[k].shape[1:]) for t in v], axis=0)
    return loss, dx, big, gsmall


_ANY = pl.BlockSpec(memory_space=pl.ANY)


def _place():
    x, y, c = lax.axis_index("x"), lax.axis_index("y"), lax.axis_index("c")
    return x, y, c, [(1 - x, y), (x, 1 - y), (1 - x, 1 - y)]


def _lane_start(s, w):
    return pl.multiple_of(s * w, LANES) if w % LANES == 0 else s * w


def _slot(ref, kind, s):
    if kind == "col":
        w = ref.shape[2] // N_SLOTS
        return ref.at[:, :, pl.ds(_lane_start(s, w), w)]
    return ref.at[:, pl.ds(s, 1)]


def _rows_quarter(ref, h, quarter):
    n = ref.shape[-2] // 4
    if len(ref.shape) == 3:
        return ref.at[:, pl.ds((2 * h + quarter) * n, n), :]
    return ref.at[:, :, pl.ds((2 * h + quarter) * n, n), :]


def _rows_half(ref, h):
    n = ref.shape[-2] // 2
    if len(ref.shape) == 3:
        return ref.at[:, pl.ds(h * n, n), :]
    return ref.at[:, :, pl.ds(h * n, n), :]


def _place_own(name, shard, kind, s_arr):
    if kind == "col":
        lead, R, W = shard.shape
        full = (lead, R, N_SLOTS * W)
    else:
        lead, _, R, W = shard.shape
        full = (lead, N_SLOTS, R, W)
    tr = _tile(R, max(16, (1 << 19) // W), 16)
    if kind == "col":
        i_spec = pl.BlockSpec((None, tr, W), lambda l, i, s: (l, i, 0))
        o_spec = pl.BlockSpec((None, tr, W), lambda l, i, s: (l, i, s[0]))
    else:
        i_spec = pl.BlockSpec((None, None, tr, W), lambda l, i, s: (l, 0, i, 0))
        o_spec = pl.BlockSpec((None, None, tr, W), lambda l, i, s: (l, s[0], i, 0))

    def body(s_ref, i_ref, o_ref):
        o_ref[...] = i_ref[...]

    return pl.pallas_call(
        body, name=name,
        grid_spec=pltpu.PrefetchScalarGridSpec(num_scalar_prefetch=1, grid=(lead, R // tr), in_specs=[i_spec], out_specs=o_spec),
        out_shape=jax.ShapeDtypeStruct(full, shard.dtype), compiler_params=_cparams(("arbitrary", "arbitrary")),
    )(s_arr, shard)


def _gather(shards, fulls, kinds):
    ng = len(shards)
    n_sem = 8

    def body(*refs):
        sh, out = refs[:ng], refs[2 * ng:3 * ng]
        send, recv = refs[3 * ng:]
        x, y, c, _ = _place()
        s, s_x, s_y, s_d = 2 * x + y, 2 * (1 - x) + y, 2 * x + (1 - y), 2 * (1 - x) + (1 - y)
        to_x, to_y, sib = (1 - x, y, c), (x, 1 - y, c), (x, y, 1 - c)

        def rcopy(g, k, src, dst, dev):
            return pltpu.make_async_remote_copy(src_ref=src, dst_ref=dst, send_sem=send.at[g * n_sem + k], recv_sem=recv.at[g * n_sem + k],
                                                device_id=dev, device_id_type=MESH)

        def win(g, slot, h, quarter=None):
            w = _slot(out[g], kinds[g], slot)
            return _rows_half(w, h) if quarter is None else _rows_quarter(w, h, quarter)

        sent = []

        def go(cp):
            cp.start()
            sent.append(cp)

        for g in range(ng):
            go(rcopy(g, 0, _rows_half(sh[g], c), win(g, s, c), to_x))
            go(rcopy(g, 1, _rows_half(sh[g], c), win(g, s, c), to_y))
        for g in range(ng):
            rcopy(g, 0, win(g, s_x, c), win(g, s_x, c), to_x).wait_recv()
            go(rcopy(g, 3, win(g, s_x, c, 1), win(g, s_x, c, 1), to_y))
            go(rcopy(g, 4, win(g, s_x, c), win(g, s_x, c), sib))
            rcopy(g, 1, win(g, s_y, c), win(g, s_y, c), to_y).wait_recv()
            go(rcopy(g, 2, win(g, s_y, c, 0), win(g, s_y, c, 0), to_x))
            go(rcopy(g, 5, win(g, s_y, c), win(g, s_y, c), sib))
        for g in range(ng):
            for k, quarter in ((2, 0), (3, 1)):
                rcopy(g, k, win(g, s_d, c, quarter), win(g, s_d, c, quarter), sib).wait_recv()
                go(rcopy(g, 6 + quarter, win(g, s_d, c, quarter), win(g, s_d, c, quarter), sib))
        for g in range(ng):
            rcopy(g, 4, win(g, s_x, 1 - c), win(g, s_x, 1 - c), sib).wait_recv()
            rcopy(g, 5, win(g, s_y, 1 - c), win(g, s_y, 1 - c), sib).wait_recv()
            for quarter in (0, 1):
                rcopy(g, 6 + quarter, win(g, s_d, 1 - c, quarter), win(g, s_d, 1 - c, quarter), sib).wait_recv()
        for cp in sent:
            cp.wait_send()

    return pl.pallas_call(
        body, name="gather_weights", in_specs=[_ANY] * (2 * ng), out_specs=[_ANY] * ng,
        out_shape=[jax.ShapeDtypeStruct(a.shape, a.dtype) for a in fulls],
        input_output_aliases={ng + g: g for g in range(ng)},
        scratch_shapes=[pltpu.SemaphoreType.DMA((n_sem * ng,)), pltpu.SemaphoreType.DMA((n_sem * ng,))],
    )(*shards, *fulls)


def _pair_send(grads):
    ng = len(grads)

    def half_shape(a):
        s = list(a.shape)
        s[-2] //= 2
        return tuple(s)

    def body(*refs):
        gr, out = refs[:ng], refs[ng:2 * ng]
        send, recv = refs[2 * ng:]
        x, y, c, _ = _place()
        cps = [pltpu.make_async_remote_copy(src_ref=_rows_half(gr[g], 1 - c), dst_ref=out[g], send_sem=send.at[g], recv_sem=recv.at[g],
                                            device_id=(x, y, 1 - c), device_id_type=MESH) for g in range(ng)]
        for cp in cps:
            cp.start()
        for cp in cps:
            cp.wait()

    return pl.pallas_call(
        body, name="grad_pair_send", in_specs=[_ANY] * ng, out_specs=[_ANY] * ng,
        out_shape=[jax.ShapeDtypeStruct(half_shape(a), a.dtype) for a in grads],
        scratch_shapes=[pltpu.SemaphoreType.DMA((ng,)), pltpu.SemaphoreType.DMA((ng,))],
    )(*grads)


def _add_half(name, g3, pa3, c_arr):
    n, R, N = g3.shape
    rh = R // 2
    tr = _tile(rh, max(16, (1 << 19) // N), 16)
    nb = rh // tr

    def body(c_ref, g_ref, p_ref, o_ref):
        o_ref[...] = (g_ref[...].astype(F32) + p_ref[...].astype(F32)).astype(o_ref.dtype)

    return pl.pallas_call(
        body, name=name,
        grid_spec=pltpu.PrefetchScalarGridSpec(
            num_scalar_prefetch=1, grid=(n, nb),
            in_specs=[pl.BlockSpec((None, tr, N), lambda l, i, c, nb=nb: (l, c[0] * nb + i, 0)), pl.BlockSpec((None, tr, N), lambda l, i, c: (l, i, 0))],
            out_specs=pl.BlockSpec((None, tr, N), lambda l, i, c: (l, i, 0))),
        out_shape=jax.ShapeDtypeStruct(pa3.shape, g3.dtype), compiler_params=_cparams(("arbitrary", "arbitrary")),
    )(c_arr, g3, pa3)


def _quarter_shape(a, kind):
    s = a.shape
    return (s[0], s[1] // 2, s[2] // N_SLOTS) if kind == "col" else (s[0], 1, s[2] // 2, s[3])


def _ici_exchange_direct(psums, kinds):
    ng = len(psums)

    def body(*refs):
        ps, direct, relay = refs[:ng], refs[ng:2 * ng], refs[2 * ng:3 * ng]
        send, recv = refs[3 * ng:]
        x, y, c, _ = _place()
        s_x, s_y, s_d = 2 * (1 - x) + y, 2 * x + (1 - y), 2 * (1 - x) + (1 - y)
        to_x, to_y = (1 - x, y, c), (x, 1 - y, c)
        cps = []
        for g in range(ng):
            quarter = lambda slot, q, g=g: _rows_half(_slot(ps[g], kinds[g], slot), q)
            plan = ((quarter(s_x, 1), direct[g].at[0], to_x), (quarter(s_y, 0), direct[g].at[1], to_y),
                    (quarter(s_d, 1), relay[g].at[0], to_x), (quarter(s_d, 0), relay[g].at[1], to_y))
            for k, (src, dst, dev) in enumerate(plan):
                cps.append(pltpu.make_async_remote_copy(src_ref=src, dst_ref=dst, send_sem=send.at[g * 4 + k], recv_sem=recv.at[g * 4 + k],
                                                        device_id=dev, device_id_type=MESH))
                cps[-1].start()
        for cp in cps:
            cp.wait()

    shapes = [jax.ShapeDtypeStruct((2,) + _quarter_shape(a, k), a.dtype) for a, k in zip(psums, kinds)]
    res = pl.pallas_call(
        body, name="grad_ici_direct", in_specs=[_ANY] * ng, out_specs=[_ANY] * (2 * ng), out_shape=shapes + shapes,
        scratch_shapes=[pltpu.SemaphoreType.DMA((4 * ng,)), pltpu.SemaphoreType.DMA((4 * ng,))],
    )(*psums)
    return res[:ng], res[ng:]


def _ici_exchange_relayed(sums):
    ng = len(sums)

    def body(*refs):
        sm, out = refs[:ng], refs[ng:2 * ng]
        send, recv = refs[2 * ng:]
        x, y, c, _ = _place()
        cps = []
        for g in range(ng):
            for k, (t, dev) in enumerate(((1, (1 - x, y, c)), (0, (x, 1 - y, c)))):
                cps.append(pltpu.make_async_remote_copy(src_ref=sm[g].at[t], dst_ref=out[g].at[k], send_sem=send.at[g * 2 + k],
                                                        recv_sem=recv.at[g * 2 + k], device_id=dev, device_id_type=MESH))
                cps[-1].start()
        for cp in cps:
            cp.wait()

    return pl.pallas_call(
        body, name="grad_ici_relayed", in_specs=[_ANY] * ng, out_specs=[_ANY] * ng,
        out_shape=[jax.ShapeDtypeStruct(a.shape, a.dtype) for a in sums],
        scratch_shapes=[pltpu.SemaphoreType.DMA((2 * ng,)), pltpu.SemaphoreType.DMA((2 * ng,))],
    )(*sums)


def _quarter_tiles(p3, w):
    rq = p3.shape[1] // 2
    tr = _tile(rq, max(16, (1 << 18) // w), 16)
    return rq, tr, rq // tr


def _p_spec(kind, tr, w, row_block, slot):
    if kind == "col":
        return pl.BlockSpec((None, tr, w), lambda t, l, i, s: (l, row_block(t, i), slot(t, s)))
    return pl.BlockSpec((None, tr, w), lambda t, l, i, s: (l * N_SLOTS + slot(t, s), row_block(t, i), 0))


def _relay_add(name, p3, relay3, yx_arr, lead, kind):
    w = relay3.shape[2]
    rq, tr, nbq = _quarter_tiles(p3, w)
    two = pl.BlockSpec((None, tr, w), lambda t, l, i, s: (t * lead + l, i, 0))

    def body(s_ref, p_ref, r_ref, o_ref):
        o_ref[...] = (p_ref[...].astype(F32) + r_ref[...].astype(F32)).astype(o_ref.dtype)

    return pl.pallas_call(
        body, name=name,
        grid_spec=pltpu.PrefetchScalarGridSpec(
            num_scalar_prefetch=1, grid=(2, lead, nbq),
            in_specs=[_p_spec(kind, tr, w, lambda t, i: (1 - t) * nbq + i, lambda t, s: jnp.where(t == 0, s[0], s[1])), two], out_specs=two),
        out_shape=jax.ShapeDtypeStruct(relay3.shape, relay3.dtype), compiler_params=_cparams(("arbitrary",) * 3),
    )(yx_arr, p3, relay3)


def _slot_sum(name, p3, direct3, relayed3, s_arr, lead, kind):
    w = direct3.shape[2]
    rq, tr, nbq = _quarter_tiles(p3, w)
    d_spec = pl.BlockSpec((None, tr, w), lambda t, l, i, s: ((1 - t) * lead + l, i, 0))
    r_spec = pl.BlockSpec((None, tr, w), lambda t, l, i, s: (t * lead + l, i, 0))

    def body(s_ref, p_ref, d_ref, r_ref, o_ref):
        o_ref[...] = (p_ref[...].astype(F32) + d_ref[...].astype(F32)) + r_ref[...].astype(F32)

    return pl.pallas_call(
        body, name=name,
        grid_spec=pltpu.PrefetchScalarGridSpec(
            num_scalar_prefetch=1, grid=(2, lead, nbq),
            in_specs=[_p_spec(kind, tr, w, lambda t, i: t * nbq + i, lambda t, s: s[0]), d_spec, r_spec],
            out_specs=pl.BlockSpec((None, tr, w), lambda t, l, i, s: (l, t * nbq + i, 0))),
        out_shape=jax.ShapeDtypeStruct((lead, 2 * rq, w), F32), compiler_params=_cparams(("arbitrary",) * 3),
    )(s_arr, p3, direct3, relayed3)


def _pair_swap(halves):
    ng = len(halves)

    def body(*refs):
        hv, out = refs[:ng], refs[ng:2 * ng]
        send, recv = refs[2 * ng:]
        x, y, c, _ = _place()
        cps = [pltpu.make_async_remote_copy(src_ref=hv[g], dst_ref=out[g], send_sem=send.at[g], recv_sem=recv.at[g],
                                            device_id=(x, y, 1 - c), device_id_type=MESH) for g in range(ng)]
        for cp in cps:
            cp.start()
        for cp in cps:
            cp.wait()

    return pl.pallas_call(
        body, name="grad_pair_swap", in_specs=[_ANY] * ng, out_specs=[_ANY] * ng,
        out_shape=[jax.ShapeDtypeStruct(a.shape, a.dtype) for a in halves],
        scratch_shapes=[pltpu.SemaphoreType.DMA((ng,)), pltpu.SemaphoreType.DMA((ng,))],
    )(*halves)


N_DEVICES = 8


def _allreduce_small(v):
    rows, m = v.shape

    def body(v_ref, o_ref, buf, send, recv):
        x, y, c, _ = _place()
        me = 4 * x + 2 * y + c
        buf[me] = v_ref[...]
        cps = []
        for k in range(1, N_DEVICES):
            peer = me ^ k
            cps.append(pltpu.make_async_remote_copy(src_ref=v_ref, dst_ref=buf.at[me], send_sem=send.at[k - 1], recv_sem=recv.at[k - 1],
                                                    device_id=((peer >> 2) & 1, (peer >> 1) & 1, peer & 1), device_id_type=MESH))
            cps[-1].start()
        for k in range(1, N_DEVICES):
            theirs = buf.at[me ^ k]
            pltpu.make_async_remote_copy(src_ref=v_ref, dst_ref=theirs, send_sem=send.at[k - 1], recv_sem=recv.at[k - 1],
                                         device_id=(x, y, c), device_id_type=MESH).wait_recv()
        for cp in cps:
            cp.wait_send()
        acc = buf[0]
        for d in range(1, N_DEVICES):
            acc = acc + buf[d]
        o_ref[...] = acc

    vm = pl.BlockSpec(memory_space=pltpu.VMEM)
    return pl.pallas_call(
        body, name="allreduce_small", in_specs=[vm], out_specs=vm, out_shape=jax.ShapeDtypeStruct(v.shape, F32),
        scratch_shapes=[pltpu.VMEM((N_DEVICES, rows, m), F32), pltpu.SemaphoreType.DMA((N_DEVICES - 1,)), pltpu.SemaphoreType.DMA((N_DEVICES - 1,))],
    )(v)


def _pad_rows(a, mult):
    r = (-a.shape[0]) % mult
    return a if r == 0 else jnp.concatenate([a, jnp.zeros((r,) + a.shape[1:], a.dtype)], axis=0)


def _pack_rows(parts, width, mult=16):
    rows, offs, at = [], [], 0
    for a in parts:
        a2 = _pad_rows(a.reshape(-1, width), mult)
        offs.append((at, a.size // width))
        rows.append(a2)
        at += a2.shape[0]
    return _pad_rows(jnp.concatenate(rows, axis=0), 4 * SUBLANES), offs


SMALL_SHARDED = ("ple_w_proj", "pool_w", "conv_w_dw", "conv_b_dw", "conv_ln_g", "conv_ln_b", "conv_b_out", "conv_b_in")
SMALL_REPLICATED = ("norm_mix", "norm_ffn", "norm_ple", "pool_scale", "attn_q_norm", "attn_k_norm", "attn_sinks", "rel_bias", "ple_b_gate")


def _small_to_full(name, slots):
    if name == "pool_w":
        return jnp.moveaxis(slots, 0, 2).reshape(slots.shape[1], slots.shape[2], N_SLOTS * slots.shape[3], slots.shape[4])
    return jnp.moveaxis(slots, 0, -2).reshape(slots.shape[1:-1] + (N_SLOTS * slots.shape[-1],))


def _small_to_slots(name, full):
    if name == "pool_w":
        nb, ng, gc, _ = full.shape
        return jnp.moveaxis(full.reshape(nb, ng, N_SLOTS, gc // N_SLOTS, gc), 2, 0)
    w = full.shape[-1] // N_SLOTS
    return jnp.moveaxis(full.reshape(full.shape[:-1] + (N_SLOTS, w)), -2, 0)


W_NAMES = ("norm_mix", "norm_ffn", "norm_ple", "conv_w_in", "conv_b_in", "conv_w_dw", "conv_b_dw", "conv_ln_g", "conv_ln_b", "conv_w_out",
           "conv_b_out", "pool_w", "pool_scale", "attn_w_qkv", "attn_q_norm", "attn_k_norm", "attn_sinks", "attn_w_o", "rel_bias",
           "ffn_w_gate", "ffn_w_up", "ffn_w_down", "ple_w_proj", "ple_w_gate", "ple_b_gate")


def _step(x, p, target, w, m, v):
    T, D = x.shape[1], x.shape[2]
    L = p.shape[0]
    NA, NC = w["conv_w_in"].shape[0], w["attn_w_qkv"].shape[0]
    xi, yi, ci = lax.axis_index("x"), lax.axis_index("y"), lax.axis_index("c")
    c_arr = jnp.reshape(ci, (1,)).astype(jnp.int32)
    s_arr = jnp.reshape(2 * xi + yi, (1,)).astype(jnp.int32)

    wq = D // N_SLOTS
    sm_pack, sm_offs = _pack_rows([w[k] for k in SMALL_SHARDED], wq)
    shards = [
        jnp.concatenate([w["ffn_w_gate"], w["ffn_w_up"]], axis=0).astype(BF16),
        w["ffn_w_down"].astype(BF16)[:, None],
        w["conv_w_in"].astype(BF16),
        jnp.concatenate([w["conv_w_out"], w["attn_w_o"], w["ple_w_gate"]], axis=0).astype(BF16)[:, None],
        w["attn_w_qkv"].astype(BF16),
        sm_pack[None, None],
    ]
    kinds = ["col", "row", "col", "row", "col", "row"]
    full = _gather(shards, [_place_own(f"place_own_{g}", a, k, s_arr) for g, (a, k) in enumerate(zip(shards, kinds))], kinds)
    wts = {"gu": full[0], "down": full[1].reshape(L, -1, D), "cin": full[2], "sq": full[3].reshape(NA + NC + L, D, D), "qkv": full[4]}
    small = {k: w[k] for k in SMALL_REPLICATED}
    for k, (at, n) in zip(SMALL_SHARDED, sm_offs):
        small[k] = _small_to_full(k, full[5][0, :, at:at + n].reshape((N_SLOTS,) + w[k].shape))

    loss, dx, big, gsmall = _local_step(x[0], p[:, 0], target[0], wts, small)

    rep_parts = [gsmall[k] for k in SMALL_REPLICATED] + [loss]
    flat = jnp.concatenate([a.reshape(-1) for a in rep_parts])
    n_flat = flat.shape[0]
    m_cols = -(-n_flat // (8 * LANES)) * LANES
    flat = jnp.concatenate([flat, jnp.zeros((8 * m_cols - n_flat,), F32)]).reshape(8, m_cols)
    red = _allreduce_small(flat).reshape(-1)
    grads, at = {}, 0
    for k in SMALL_REPLICATED:
        grads[k] = red[at:at + w[k].size].reshape(w[k].shape)
        at += w[k].size
    loss_out = red[at]

    slots = {k: _small_to_slots(k, gsmall[k]) for k in SMALL_SHARDED}
    gsm = jnp.stack([_pack_rows([slots[k][s] for k in SMALL_SHARDED], wq)[0] for s in range(N_SLOTS)], axis=0)
    local = [big["gu"], big["down"].reshape(L, N_SLOTS, -1, D), big["cin"], big["sq"].reshape(NA + NC + L, N_SLOTS, -1, D), big["qkv"],
             gsm[None]]
    theirs = _pair_send(local)
    psums = []
    for g, (a, t) in enumerate(zip(local, theirs)):
        if kinds[g] == "col":
            psums.append(_add_half(f"pair_add_{g}", a, t, c_arr))
        else:
            n4 = a.shape[0] * N_SLOTS
            psums.append(_add_half(f"pair_add_{g}", a.reshape(n4, a.shape[2], a.shape[3]), t.reshape(n4, t.shape[2], t.shape[3]), c_arr).reshape(t.shape))
    yx_arr = jnp.stack([2 * xi + (1 - yi), 2 * (1 - xi) + yi]).astype(jnp.int32)
    direct, relay = _ici_exchange_direct(psums, kinds)
    p3s = [ps if k == "col" else ps.reshape(ps.shape[0] * N_SLOTS, ps.shape[2], ps.shape[3]) for ps, k in zip(psums, kinds)]
    flat3 = lambda a: a.reshape(2 * a.shape[1], a.shape[-2], a.shape[-1])
    sums = [_relay_add(f"relay_add_{g}", p3, flat3(rl), yx_arr, rl.shape[1], kinds[g]).reshape(rl.shape) for g, (p3, rl) in enumerate(zip(p3s, relay))]
    relayed = _ici_exchange_relayed(sums)
    halves = [_slot_sum(f"slot_sum_{g}", p3, flat3(d), flat3(r), s_arr, d.shape[1], kinds[g]) for g, (p3, d, r) in enumerate(zip(p3s, direct, relayed))]
    first = ci == 0
    gsh = [jnp.concatenate([jnp.where(first, a, b), jnp.where(first, b, a)], axis=1) for a, b in zip(halves, _pair_swap(halves))]
    grads["ffn_w_gate"], grads["ffn_w_up"] = gsh[0][:L], gsh[0][L:]
    grads["ffn_w_down"] = gsh[1]
    grads["conv_w_in"] = gsh[2]
    grads["conv_w_out"], grads["attn_w_o"], grads["ple_w_gate"] = gsh[3][:NA], gsh[3][NA:NA + NC], gsh[3][NA + NC:]
    grads["attn_w_qkv"] = gsh[4]
    for k, (at, n) in zip(SMALL_SHARDED, sm_offs):
        grads[k] = gsh[5][0, at:at + n].reshape(w[k].shape)

    outs_d, outs_m, outs_v = [], [], []
    for k in W_NAMES:
        d_, m_, v_ = _adamw(f"adamw_{k}", w[k], grads[k], m[k], v[k])
        outs_d.append(d_)
        outs_m.append(m_)
        outs_v.append(v_)
    return (loss_out, dx[None], *[grads[k] for k in W_NAMES], *outs_d, *outs_m, *outs_v)


def kernel(x, p, norm_mix, norm_ffn, norm_ple, conv_w_in, conv_b_in, conv_w_dw, conv_b_dw, conv_ln_g, conv_ln_b, conv_w_out, conv_b_out, pool_w, pool_scale, attn_w_qkv, attn_q_norm, attn_k_norm, attn_sinks, attn_w_o, rel_bias, ffn_w_gate, ffn_w_up, ffn_w_down, ple_w_proj, ple_w_gate, ple_b_gate, loss_target, m_norm_mix, m_norm_ffn, m_norm_ple, m_conv_w_in, m_conv_b_in, m_conv_w_dw, m_conv_b_dw, m_conv_ln_g, m_conv_ln_b, m_conv_w_out, m_conv_b_out, m_pool_w, m_pool_scale, m_attn_w_qkv, m_attn_q_norm, m_attn_k_norm, m_attn_sinks, m_attn_w_o, m_rel_bias, m_ffn_w_gate, m_ffn_w_up, m_ffn_w_down, m_ple_w_proj, m_ple_w_gate, m_ple_b_gate, v_norm_mix, v_norm_ffn, v_norm_ple, v_conv_w_in, v_conv_b_in, v_conv_w_dw, v_conv_b_dw, v_conv_ln_g, v_conv_ln_b, v_conv_w_out, v_conv_b_out, v_pool_w, v_pool_scale, v_attn_w_qkv, v_attn_q_norm, v_attn_k_norm, v_attn_sinks, v_attn_w_o, v_rel_bias, v_ffn_w_gate, v_ffn_w_up, v_ffn_w_down, v_ple_w_proj, v_ple_w_gate, v_ple_b_gate):
    ws_ = (norm_mix, norm_ffn, norm_ple, conv_w_in, conv_b_in, conv_w_dw, conv_b_dw, conv_ln_g, conv_ln_b, conv_w_out, conv_b_out, pool_w, pool_scale, attn_w_qkv, attn_q_norm, attn_k_norm, attn_sinks, attn_w_o, rel_bias, ffn_w_gate, ffn_w_up, ffn_w_down, ple_w_proj, ple_w_gate, ple_b_gate)
    ms_ = (m_norm_mix, m_norm_ffn, m_norm_ple, m_conv_w_in, m_conv_b_in, m_conv_w_dw, m_conv_b_dw, m_conv_ln_g, m_conv_ln_b, m_conv_w_out, m_conv_b_out, m_pool_w, m_pool_scale, m_attn_w_qkv, m_attn_q_norm, m_attn_k_norm, m_attn_sinks, m_attn_w_o, m_rel_bias, m_ffn_w_gate, m_ffn_w_up, m_ffn_w_down, m_ple_w_proj, m_ple_w_gate, m_ple_b_gate)
    vs_ = (v_norm_mix, v_norm_ffn, v_norm_ple, v_conv_w_in, v_conv_b_in, v_conv_w_dw, v_conv_b_dw, v_conv_ln_g, v_conv_ln_b, v_conv_w_out, v_conv_b_out, v_pool_w, v_pool_scale, v_attn_w_qkv, v_attn_q_norm, v_attn_k_norm, v_attn_sinks, v_attn_w_o, v_rel_bias, v_ffn_w_gate, v_ffn_w_up, v_ffn_w_down, v_ple_w_proj, v_ple_w_gate, v_ple_b_gate)
    return _step(x, p, loss_target, dict(zip(W_NAMES, ws_)), dict(zip(W_NAMES, ms_)), dict(zip(W_NAMES, vs_)))
```

```python
import functools
import math

import jax
import jax.numpy as jnp
import numpy as np
from jax import lax
from jax.experimental import pallas as pl
from jax.experimental.pallas import tpu as pltpu

F32 = jnp.float32
BF16 = jnp.bfloat16
MESH = pl.DeviceIdType.MESH

CHUNK = 64
CONV_WIDTH = 31
POOL_WINDOWS = (2, 4, 8, 16)
HEAD_DIM = 64
WINDOW_CHUNKS = 2
QBLOCK = 128
NUM_BUCKETS = 32
REL_MAX_DIST = 128
EPS = 1e-6
NEG_INF = -1e30
ADAM_LR, ADAM_B1, ADAM_B2, ADAM_EPS, ADAM_WD, ADAM_STEP = 0.001, 0.9, 0.999, 1e-08, 0.01, 10
N_SLOTS = 4
LANES = 128
VMEM_LIMIT_BYTES = 56 * 1024 * 1024


def _cparams(sem):
    return pltpu.CompilerParams(dimension_semantics=sem, vmem_limit_bytes=VMEM_LIMIT_BYTES)


def _tile(n, pref, mult=LANES):
    if n <= pref:
        return n
    t = (pref // mult) * mult
    while t >= mult:
        if n % t == 0:
            return t
        t -= mult
    return n


def _sig(z):
    return 1.0 / (1.0 + jnp.exp(-z))


def _op(arr, lead=None, ro=0, co=0, fn=None):
    return (arr, lead, ro, co, fn)


_DOT_DIMS = {"nn": (((1,), (0,)), ((), ())), "nt": (((1,), (1,)), ((), ())), "tn": (((0,), (0,)), ((), ()))}


def _mm(name, mode, dims, tiles, terms, n_acc, epilogue, outs, extras=()):
    M, N, K = dims
    tm, tn, tk = tiles
    assert M % tm == 0 and N % tn == 0 and K % tk == 0, (name, dims, tiles)
    nk = K // tk
    a_tile = (tk, tm) if mode == "tn" else (tm, tk)
    b_tile = (tn, tk) if mode == "nt" else (tk, tn)
    a_fn = (lambda i, j, k: (k, i)) if mode == "tn" else (lambda i, j, k: (i, k))
    b_fn = (lambda i, j, k: (j, k)) if mode == "nt" else (lambda i, j, k: (k, j))
    dn = _DOT_DIMS[mode]

    operands, specs, seen = [], [], {}

    def add(op, tshape, default_fn):
        arr, lead, ro, co, fn = op
        fn = fn or default_fn
        key = (id(arr), lead, ro, co, id(fn) if op[4] is not None else None, tshape)
        if key in seen:
            return seen[key]

        def imap(i, j, k, fn=fn, lead=lead, ro=ro, co=co):
            r, c = fn(i, j, k)
            return (r + ro, c + co) if lead is None else (lead, r + ro, c + co)

        operands.append(arr)
        specs.append(pl.BlockSpec(tshape if lead is None else (None,) + tshape, imap))
        seen[key] = len(operands) - 1
        return seen[key]

    term_idx = [(add(a, a_tile, a_fn), add(b, b_tile, b_fn), acc) for a, b, acc in terms]
    extra_idx = []
    for arr, kind, lead, co in extras:
        if kind == "tile":
            extra_idx.append(add(_op(arr, lead, 0, co), (tm, tn), lambda i, j, k: (i, j)))
        elif kind == "row":
            arr3 = arr.reshape(arr.shape[0], 1, arr.shape[1])
            extra_idx.append(add(_op(arr3, 0 if lead is None else lead, 0, co), (1, tn), lambda i, j, k: (0, j)))
        elif kind == "rows":
            extra_idx.append(add(_op(arr, lead, 0, 0), (tm, arr.shape[-1]), lambda i, j, k: (i, 0)))
        elif kind == "cols":
            extra_idx.append(add(_op(arr, lead, 0, co), (arr.shape[-2], tn), lambda i, j, k: (0, j)))
        else:
            extra_idx.append(add(_op(arr, None, 0, 0), (tm, 1), lambda i, j, k: (i, 0)))
    n_in = len(operands)
    out_shapes, out_specs, aliases = [], [], {}
    for oi, (shape, dtype, lead, alias) in enumerate(outs):
        out_shapes.append(jax.ShapeDtypeStruct(shape, dtype))
        if lead == "rowsum":
            out_specs.append(pl.BlockSpec((None, 1, tn), lambda i, j, k: (i, 0, j)))
        elif lead is None:
            out_specs.append(pl.BlockSpec((tm, tn), lambda i, j, k: (i, j)))
        else:
            out_specs.append(pl.BlockSpec((None, tm, tn), lambda i, j, k, lead=lead: (lead, i, j)))
        if alias is not None:
            operands.append(alias)
            specs.append(pl.BlockSpec(memory_space=pl.ANY))
            aliases[len(operands) - 1] = oi
    n_all_in = len(operands)
    n_out = len(outs)

    def body(*refs):
        ins = refs[:n_in]
        o_refs = refs[n_all_in:n_all_in + n_out]
        accs = refs[n_all_in + n_out:]

        def dots():
            sums = [None] * n_acc
            for ai, bi, acc_i in term_idx:
                a = ins[ai][...]
                b = ins[bi][...]
                if a.dtype != BF16:
                    a = a.astype(BF16)
                if b.dtype != BF16:
                    b = b.astype(BF16)
                d = lax.dot_general(a, b, dn, preferred_element_type=F32)
                sums[acc_i] = d if sums[acc_i] is None else sums[acc_i] + d
            return sums

        def finish(vals):
            res = epilogue(vals, [ins[e][...] for e in extra_idx])
            for o, r in zip(o_refs, res):
                o[...] = r.astype(o.dtype)

        if nk == 1:
            finish(dots())
            return
        k = pl.program_id(2)

        @pl.when(k == 0)
        def _():
            for acc, d in zip(accs, dots()):
                acc[...] = d

        if nk > 2:
            @pl.when((k > 0) & (k < nk - 1))
            def _():
                for acc, d in zip(accs, dots()):
                    acc[...] += d

        @pl.when(k == nk - 1)
        def _():
            finish([acc[...] + d for acc, d in zip(accs, dots())])

    res = pl.pallas_call(
        body, name=name, grid=(M // tm, N // tn, nk), in_specs=specs, out_specs=out_specs, out_shape=out_shapes,
        scratch_shapes=[pltpu.VMEM((tm, tn), F32) for _ in range(n_acc if nk > 1 else 0)], input_output_aliases=aliases,
        compiler_params=_cparams(("parallel", "parallel", "arbitrary")),
    )(*operands)
    return res


def _rowk(name, T, tm, ins, outs, body, scratch=()):
    assert T % tm == 0, (name, T, tm)
    n = T // tm
    specs = []
    for arr, kind in ins:
        w = arr.shape[-1]
        if kind == "tile":
            specs.append(pl.BlockSpec((tm, w), lambda i: (i, 0)))
        elif kind == "prev":
            specs.append(pl.BlockSpec((tm, w), lambda i: (jnp.maximum(i - 1, 0), 0)))
        elif kind == "next":
            specs.append(pl.BlockSpec((tm, w), lambda i, n=n: (jnp.minimum(i + 1, n - 1), 0)))
        else:
            specs.append(pl.BlockSpec(arr.shape, lambda i, nd=arr.ndim: (0,) * nd))
    out_shapes, out_specs = [], []
    for shape, dtype, kind in outs:
        out_shapes.append(jax.ShapeDtypeStruct(shape, dtype))
        if kind == "tile":
            out_specs.append(pl.BlockSpec((tm, shape[-1]), lambda i: (i, 0)))
        else:
            out_specs.append(pl.BlockSpec(shape, lambda i, nd=len(shape): (0,) * nd))
    n_in, n_out = len(ins), len(outs)

    def kbody(*refs):
        body(pl.program_id(0), n, refs[:n_in], refs[n_in:n_in + n_out], refs[n_in + n_out:])

    return pl.pallas_call(
        kbody, name=name, grid=(n,), in_specs=specs, out_specs=out_specs, out_shape=out_shapes,
        scratch_shapes=list(scratch), compiler_params=_cparams(("arbitrary",)),
    )(*[a for a, _ in ins])


def _accum(ref, i, val):
    @pl.when(i == 0)
    def _():
        ref[...] = val

    @pl.when(i > 0)
    def _():
        ref[...] += val


def _colsum(v):
    return jnp.sum(v, axis=0, keepdims=True)


def _rms_r(x):
    return lax.rsqrt(jnp.mean(x * x, axis=-1, keepdims=True) + EPS)


def _rms_fwd(name, x, g, tm):
    T, D = x.shape

    def body(i, n, ins, outs, scr):
        xv = ins[0][...]
        outs[0][...] = (xv * _rms_r(xv) * ins[1][...]).astype(BF16)

    return _rowk(name, T, tm, [(x, "tile"), (g, "full")], [((T, D), BF16, "tile")], body)[0]


def _rms_bwd(name, dres, x, g, dh, tm, want_bf16=False, want_colsum=False):
    T, D = x.shape

    def body(i, n, ins, outs, scr):
        xv = ins[1][...]
        gv = ins[2][...]
        dhv = ins[3][...].astype(F32)
        r = _rms_r(xv)
        xh = xv * r
        dhg = dhv * gv
        dx = ins[0][...] + r * (dhg - xh * jnp.mean(dhg * xh, axis=-1, keepdims=True))
        outs[0][...] = dx
        _accum(outs[1], i, _colsum(dhv * xh))
        o = 2
        if want_bf16:
            outs[o][...] = dx.astype(BF16)
            o += 1
        if want_colsum:
            _accum(outs[o], i, _colsum(dx))

    outs = [((T, D), F32, "tile"), ((1, D), F32, "acc")]
    if want_bf16:
        outs.append(((T, D), BF16, "tile"))
    if want_colsum:
        outs.append(((1, D), F32, "acc"))
    return _rowk(name, T, tm, [(dres, "tile"), (x, "tile"), (g, "full"), (dh, "tile")], outs, body)


def _loss_head(y, target, tm):
    T, D = y.shape

    def body(i, n, ins, outs, scr):
        d = ins[0][...] - ins[1][...]
        outs[0][...] = d * (1.0 / D)
        _accum(outs[1], i, jnp.sum(_colsum(d * d), axis=1, keepdims=True) * (0.5 / D))

    return _rowk("loss_head", T, tm, [(y, "tile"), (target, "tile")], [((T, D), F32, "tile"), ((1, 1), F32, "acc")], body)


CONV_ROWS = 128
SUBLANES = 8
_PHASE_PAD = 32


def _phase_scratch(tm):
    return pltpu.VMEM((SUBLANES, tm + _PHASE_PAD, LANES), F32)


def _phase_copies(buf, shf, base, l0, tm):
    n = tm + _PHASE_PAD - SUBLANES
    for r in range(1, SUBLANES):
        shf[r, pl.ds(0, n), :] = buf[pl.ds(base + r, n), pl.ds(l0, LANES)]


def _phase_window(buf, shf, base, q, row0, rows, l0):
    r = q % SUBLANES
    a = q - r
    if r == 0:
        return buf[pl.ds(base + a + row0, rows), pl.ds(l0, LANES)]
    return shf[r, pl.ds(a + row0, rows), :]


def _dwconv_fwd(name, u1, w_dw, b_dw, tm):
    T, D = u1.shape
    rc_n = tm // CONV_ROWS if tm >= CONV_ROWS else 1
    rows = min(CONV_ROWS, tm)
    halo = CONV_WIDTH - 1

    base = tm - _PHASE_PAD

    def body(i, n, ins, outs, scr):
        buf, shf = scr
        buf[pl.ds(0, tm), :] = jnp.where(i > 0, ins[0][...], 0.0)
        buf[pl.ds(tm, tm), :] = ins[1][...]
        w_ref, b_ref, o_ref = ins[2], ins[3], outs[0]

        def chunk(lc, carry):
            l0 = pl.multiple_of(lc * LANES, LANES)
            _phase_copies(buf, shf, base, l0, tm)
            for rc in range(rc_n):
                acc = jnp.zeros((rows, LANES), F32) + b_ref[:, pl.ds(l0, LANES)]
                for k in range(CONV_WIDTH):
                    acc = acc + _phase_window(buf, shf, base, k + _PHASE_PAD - halo, rc * rows, rows, l0) * w_ref[pl.ds(k, 1), pl.ds(l0, LANES)]
                o_ref[pl.ds(rc * rows, rows), pl.ds(l0, LANES)] = acc
            return carry

        lax.fori_loop(0, D // LANES, chunk, 0)

    return _rowk(name, T, tm, [(u1, "prev"), (u1, "tile"), (w_dw, "full"), (b_dw, "full")], [((T, D), F32, "tile")], body,
                 scratch=[pltpu.VMEM((2 * tm, D), F32), _phase_scratch(tm)])[0]


def _ln_silu_fwd(name, u2, g, b, tm):
    T, D = u2.shape

    def body(i, n, ins, outs, scr):
        v = ins[0][...]
        mu = jnp.mean(v, axis=-1, keepdims=True)
        xc = v - mu
        y = xc * lax.rsqrt(jnp.mean(xc * xc, axis=-1, keepdims=True) + EPS) * ins[1][...] + ins[2][...]
        outs[0][...] = (y * _sig(y)).astype(BF16)

    return _rowk(name, T, tm, [(u2, "tile"), (g, "full"), (b, "full")], [((T, D), BF16, "tile")], body)[0]


def _ln_silu_bwd(name, du4, u2, g, b, tm):
    T, D = u2.shape

    def body(i, n, ins, outs, scr):
        v = ins[1][...]
        gv = ins[2][...]
        mu = jnp.mean(v, axis=-1, keepdims=True)
        xc = v - mu
        r = lax.rsqrt(jnp.mean(xc * xc, axis=-1, keepdims=True) + EPS)
        xh = xc * r
        y = xh * gv + ins[3][...]
        s = _sig(y)
        dy = ins[0][...] * (s * (1.0 + y * (1.0 - s)))
        dyg = dy * gv
        du2 = r * (dyg - jnp.mean(dyg, axis=-1, keepdims=True) - xh * jnp.mean(dyg * xh, axis=-1, keepdims=True))
        outs[0][...] = du2
        _accum(outs[1], i, _colsum(dy * xh))
        _accum(outs[2], i, _colsum(dy))
        _accum(outs[3], i, _colsum(du2))

    return _rowk(name, T, tm, [(du4, "tile"), (u2, "tile"), (g, "full"), (b, "full")],
                 [((T, D), F32, "tile"), ((1, D), F32, "acc"), ((1, D), F32, "acc"), ((1, D), F32, "acc")], body)


def _dwconv_glu_bwd(name, du2, u1, a_, gate, w_dw, tm):
    T, D = u1.shape
    rc_n = tm // CONV_ROWS if tm >= CONV_ROWS else 1
    rows = min(CONV_ROWS, tm)
    halo = CONV_WIDTH - 1

    base = tm - _PHASE_PAD

    def body(i, n, ins, outs, scr):
        bu, bd, shu, shd = scr
        bd[pl.ds(0, tm), :] = ins[0][...]
        bd[pl.ds(tm, tm), :] = jnp.where(i < n - 1, ins[1][...], 0.0)
        bu[pl.ds(0, tm), :] = jnp.where(i > 0, ins[2][...], 0.0)
        bu[pl.ds(tm, tm), :] = ins[3][...]
        a_ref, g_ref, w_ref = ins[4], ins[5], ins[6]
        dag_ref, dw_ref, db_ref = outs

        @pl.when(i == 0)
        def _():
            dw_ref[...] = jnp.zeros_like(dw_ref)
            db_ref[...] = jnp.zeros_like(db_ref)

        def chunk(lc, carry):
            l0 = pl.multiple_of(lc * LANES, LANES)
            l1 = pl.multiple_of(D + lc * LANES, LANES)
            _phase_copies(bd, shd, 0, l0, tm)
            _phase_copies(bu, shu, base, l0, tm)
            for rc in range(rc_n):
                r0 = rc * rows
                d_here = bd[pl.ds(r0, rows), pl.ds(l0, LANES)]
                acc = jnp.zeros((rows, LANES), F32)
                for k in range(CONV_WIDTH):
                    wk = w_ref[pl.ds(k, 1), pl.ds(l0, LANES)]
                    acc = acc + _phase_window(bd, shd, 0, halo - k, r0, rows, l0) * wk
                    dw_ref[pl.ds(k, 1), pl.ds(l0, LANES)] += _colsum(d_here * _phase_window(bu, shu, base, k + _PHASE_PAD - halo, r0, rows, l0))
                av = a_ref[pl.ds(r0, rows), pl.ds(l0, LANES)].astype(F32)
                sg = _sig(g_ref[pl.ds(r0, rows), pl.ds(l0, LANES)].astype(F32))
                da = acc * sg
                dg = acc * av * sg * (1.0 - sg)
                dag_ref[pl.ds(r0, rows), pl.ds(l0, LANES)] = da.astype(BF16)
                dag_ref[pl.ds(r0, rows), pl.ds(l1, LANES)] = dg.astype(BF16)
                db_ref[:, pl.ds(l0, LANES)] += _colsum(da)
                db_ref[:, pl.ds(l1, LANES)] += _colsum(dg)
            return carry

        lax.fori_loop(0, D // LANES, chunk, 0)

    return _rowk(name, T, tm, [(du2, "tile"), (du2, "next"), (u1, "prev"), (u1, "tile"), (a_, "tile"), (gate, "tile"), (w_dw, "full")],
                 [((T, 2 * D), BF16, "tile"), ((CONV_WIDTH, D), F32, "acc"), ((1, 2 * D), F32, "acc")], body,
                 scratch=[pltpu.VMEM((2 * tm, D), F32), pltpu.VMEM((2 * tm, D), F32), _phase_scratch(tm), _phase_scratch(tm)])


def _row_index(i, tm, r0, rows):
    return (i * tm + r0 + lax.broadcasted_iota(jnp.int32, (rows, 1), 0)).astype(F32)


def _pool_fwd(name, x, g, tm):
    T, D = x.shape
    gc = D // len(POOL_WINDOWS)
    rows = min(CONV_ROWS, tm)
    rc_n = tm // rows

    def body(i, n, ins, outs, scr):
        buf = scr[0]
        xp = ins[0][...]
        buf[pl.ds(0, tm), :] = jnp.where(i > 0, xp * _rms_r(xp) * ins[2][...], 0.0)
        xc = ins[1][...]
        buf[pl.ds(tm, tm), :] = xc * _rms_r(xc) * ins[2][...]
        o_ref = outs[0]
        for gi, w in enumerate(POOL_WINDOWS):
            def chunk(lc, carry, gi=gi, w=w):
                l0 = pl.multiple_of(gi * gc + lc * LANES, LANES)
                for rc in range(rc_n):
                    r0 = rc * rows
                    acc = buf[pl.ds(tm + r0, rows), pl.ds(l0, LANES)]
                    here = acc
                    for d in range(1, w):
                        acc = acc + buf[pl.ds(tm + r0 - d, rows), pl.ds(l0, LANES)]
                    cnt = jnp.minimum(_row_index(i, tm, r0, rows) + 1.0, float(w))
                    o_ref[pl.ds(r0, rows), pl.ds(l0, LANES)] = (acc / cnt - here).astype(BF16)
                return carry

            lax.fori_loop(0, gc // LANES, chunk, 0)

    return _rowk(name, T, tm, [(x, "prev"), (x, "tile"), (g, "full")], [((T, D), BF16, "tile")], body,
                 scratch=[pltpu.VMEM((2 * tm, D), F32)])[0]


def _pool_bwd(name, dmix, tm):
    T, D = dmix.shape
    gc = D // len(POOL_WINDOWS)
    rows = min(CONV_ROWS, tm)
    rc_n = tm // rows

    def body(i, n, ins, outs, scr):
        buf = scr[0]
        o_ref = outs[0]
        t_here = (i * tm + lax.broadcasted_iota(jnp.int32, (tm, 1), 0)).astype(F32) + 1.0
        for gi, w in enumerate(POOL_WINDOWS):
            cols = pl.ds(gi * gc, gc)
            buf[pl.ds(0, tm), cols] = ins[0][:, cols] / jnp.minimum(t_here, float(w))
            buf[pl.ds(tm, tm), cols] = jnp.where(i < n - 1, ins[1][:, cols] / float(w), 0.0)

            def chunk(lc, carry, gi=gi, w=w):
                l0 = pl.multiple_of(gi * gc + lc * LANES, LANES)
                for rc in range(rc_n):
                    r0 = rc * rows
                    acc = -ins[0][pl.ds(r0, rows), pl.ds(l0, LANES)]
                    for d in range(w):
                        acc = acc + buf[pl.ds(r0 + d, rows), pl.ds(l0, LANES)]
                    o_ref[pl.ds(r0, rows), pl.ds(l0, LANES)] = acc
                return carry

            lax.fori_loop(0, gc // LANES, chunk, 0)

    return _rowk(name, T, tm, [(dmix, "tile"), (dmix, "next")], [((T, D), F32, "tile")], body,
                 scratch=[pltpu.VMEM((2 * tm, D), F32)])[0]


def _pool_scale_bwd(name, dy, y0, scale, tm):
    T, D = dy.shape

    def body(i, n, ins, outs, scr):
        d = ins[0][...]
        outs[0][...] = (d * ins[2][...]).astype(BF16)
        _accum(outs[1], i, _colsum(d * ins[1][...]))

    return _rowk(name, T, tm, [(dy, "tile"), (y0, "tile"), (scale, "full")], [((T, D), BF16, "tile"), ((1, D), F32, "acc")], body)


def _t5_bucket_np():
    i = np.arange(QBLOCK)[:, None]
    j = np.arange(2 * QBLOCK)[None, :]
    rel = j - QBLOCK - i
    nb = NUM_BUCKETS // 2
    n = -rel
    ret = np.where(n < 0, nb, 0)
    n = np.abs(n)
    max_exact = nb // 2
    nf = np.maximum(n, 1).astype(np.float32)
    large = max_exact + (np.log(nf / np.float32(max_exact)) / np.float32(math.log(REL_MAX_DIST / max_exact))
                         * np.float32(nb - max_exact)).astype(np.int32)
    large = np.minimum(large, nb - 1)
    return (ret + np.where(n < max_exact, n, large)).astype(np.int32)


def _bias_fwd(rel_bias, bucket):
    nb, nh = rel_bias.shape

    def body(rb_ref, bk_ref, o_ref):
        h = pl.program_id(0)
        bk = bk_ref[...]
        acc = jnp.zeros(bk.shape, F32)
        for b in range(nb):
            acc = jnp.where(bk == b, rb_ref[b, h], acc)
        o_ref[...] = acc

    return pl.pallas_call(
        body, name="attn_bias_fwd", grid=(nh,),
        in_specs=[pl.BlockSpec(memory_space=pltpu.SMEM), pl.BlockSpec(bucket.shape, lambda h: (0, 0))],
        out_specs=pl.BlockSpec((None,) + bucket.shape, lambda h: (h, 0, 0)),
        out_shape=jax.ShapeDtypeStruct((nh,) + bucket.shape, F32), compiler_params=_cparams(("arbitrary",)),
    )(rel_bias, bucket)


def _bias_bwd(dbias, bucket, nb):
    nh = dbias.shape[0]

    def body(db_ref, bk_ref, o_ref):
        h = pl.program_id(0)
        bk = bk_ref[...]
        d = db_ref[...]
        for b in range(nb):
            o_ref[h, b] = jnp.sum(jnp.where(bk == b, d, 0.0))

    return pl.pallas_call(
        body, name="attn_bias_bwd", grid=(nh,),
        in_specs=[pl.BlockSpec((None,) + bucket.shape, lambda h: (h, 0, 0)), pl.BlockSpec(bucket.shape, lambda h: (0, 0))],
        out_specs=pl.BlockSpec(memory_space=pltpu.SMEM),
        out_shape=jax.ShapeDtypeStruct((nh, nb), F32), compiler_params=_cparams(("arbitrary",)),
    )(dbias, bucket)


def _head_norm_fwd(name, q2, g, tm):
    R, W = q2.shape

    def body(i, n, ins, outs, scr):
        v = ins[0][...]
        outs[0][...] = (v * _rms_r(v) * ins[1][...]).astype(BF16)

    return _rowk(name, R, tm, [(q2, "tile"), (g, "full")], [((R, W), BF16, "tile")], body)[0]


def _head_norm_bwd(name, dqn, q2, g, tm):
    R, W = q2.shape

    def body(i, n, ins, outs, scr):
        v = ins[1][...]
        d = ins[0][...]
        r = _rms_r(v)
        xh = v * r
        dg = d * ins[2][...]
        outs[0][...] = (r * (dg - xh * jnp.mean(dg * xh, axis=-1, keepdims=True))).astype(BF16)
        _accum(outs[1], i, _colsum(d * xh))

    return _rowk(name, R, tm, [(dqn, "tile"), (q2, "tile"), (g, "full")], [((R, W), BF16, "tile"), ((1, W), F32, "acc")], body)


def _band_merge(name, own, prev, k3=None, g=None):
    H, T, W = own.shape
    nblk = T // QBLOCK

    def body(*refs):
        o_ref, p_ref = refs[0], refs[1]
        out_ref = refs[4] if k3 is not None else refs[2]

        def blk(m, carry):
            r0 = pl.multiple_of(m * QBLOCK, QBLOCK)
            rn = pl.multiple_of(jnp.minimum(m + 1, nblk - 1) * QBLOCK, QBLOCK)
            d = o_ref[pl.ds(r0, QBLOCK), :] + jnp.where(m < nblk - 1, p_ref[pl.ds(rn, QBLOCK), :], 0.0)
            if k3 is None:
                out_ref[pl.ds(r0, QBLOCK), :] = d.astype(BF16)
                return carry
            v = refs[2][pl.ds(r0, QBLOCK), :]
            r = _rms_r(v)
            xh = v * r
            dg = d * refs[3][...]
            out_ref[pl.ds(r0, QBLOCK), :] = (r * (dg - xh * jnp.mean(dg * xh, axis=-1, keepdims=True))).astype(BF16)
            return carry + _colsum(d * xh)

        tot = lax.fori_loop(0, nblk, blk, jnp.zeros((1, W), F32))
        if k3 is not None:
            _accum(refs[5], pl.program_id(0), tot)

    head = pl.BlockSpec((None, T, W), lambda h: (h, 0, 0))
    ins, in_specs = [own, prev], [head, head]
    out_shape, out_specs = [jax.ShapeDtypeStruct((H, T, W), BF16)], [head]
    if k3 is not None:
        ins += [k3, g]
        in_specs += [head, pl.BlockSpec(g.shape, lambda h: (0, 0))]
        out_shape.append(jax.ShapeDtypeStruct((1, W), F32))
        out_specs.append(pl.BlockSpec((1, W), lambda h: (0, 0)))
    return pl.pallas_call(body, name=name, grid=(H,), in_specs=in_specs, out_specs=out_specs, out_shape=out_shape,
                          compiler_params=_cparams(("arbitrary",)))(*ins)


def _masked_bias(bias):
    rows = bias.shape[1]
    qc = (jnp.arange(rows)[:, None] % QBLOCK) // CHUNK
    j = jnp.arange(2 * QBLOCK)[None, :]
    kc = j // CHUNK - QBLOCK // CHUNK
    ok = (kc <= qc) & (kc >= qc - WINDOW_CHUNKS)
    hide = lambda visible: jnp.where(visible, 0.0, NEG_INF).astype(F32)[None]
    return jnp.stack([bias + hide(ok & (j >= QBLOCK)), bias + hide(ok)], axis=0)


def _attn_logits(q, kb, bias_masked, sink):
    s = lax.dot_general(q, kb, _DOT_DIMS["nt"], preferred_element_type=F32) * (HEAD_DIM ** -0.5) + bias_masked
    m = jnp.maximum(jnp.max(s, axis=-1, keepdims=True), sink)
    e = jnp.exp(s - m)
    es = jnp.exp(sink - m)
    den = jnp.sum(e, axis=-1, keepdims=True) + es
    return e / den, es / den


def _heads_per_step(n_kv):
    return max(h for h in (8, 4, 2, 1) if n_kv % h == 0)


def _attn_specs(group, hp, rows):
    blk = lambda hn, fn: pl.BlockSpec((hn, QBLOCK, HEAD_DIM), fn)
    cur = lambda h, n: (h, n, 0)
    prv = lambda h, n: (h, jnp.maximum(n - 1, 0), 0)
    bsp = pl.BlockSpec((hp, rows, 2 * QBLOCK), lambda h, n: (h, 0, 0))
    ssp = pl.BlockSpec((hp, rows, 1), lambda h, n: (h, 0, 0))
    bmsp = pl.BlockSpec((None, hp, rows, 2 * QBLOCK), lambda h, n: (jnp.minimum(n, 1), h, 0, 0))
    return blk(hp * group, cur), blk(hp, prv), blk(hp, cur), bsp, ssp, bmsp


def _attn_fwd(qn, kn, v, bias, sink_rows, n_kv, group, T):
    nblk = T // QBLOCK
    rows = group * QBLOCK
    hp = _heads_per_step(n_kv)

    def body(q_ref, kp_ref, kc_ref, vp_ref, vc_ref, b_ref, s_ref, o_ref):
        for hh in range(hp):
            q = q_ref[pl.ds(hh * group, group)].reshape(rows, HEAD_DIM)
            kb = jnp.concatenate([kp_ref[hh], kc_ref[hh]], axis=0)
            vb = jnp.concatenate([vp_ref[hh], vc_ref[hh]], axis=0)
            p, _ = _attn_logits(q, kb, b_ref[hh], s_ref[hh])
            o = lax.dot_general(p.astype(BF16), vb, _DOT_DIMS["nn"], preferred_element_type=F32)
            o_ref[pl.ds(hh * group, group)] = o.reshape(group, QBLOCK, HEAD_DIM).astype(BF16)

    qs, kp, kc, _, ssp, bmsp = _attn_specs(group, hp, rows)
    return pl.pallas_call(
        body, name="attn_fwd", grid=(n_kv // hp, nblk), in_specs=[qs, kp, kc, kp, kc, bmsp, ssp],
        out_specs=qs, out_shape=jax.ShapeDtypeStruct(qn.shape, BF16), compiler_params=_cparams(("arbitrary", "arbitrary")),
    )(qn, kn, kn, v, v, bias, sink_rows)


def _attn_bwd(qn, kn, v, bias, sink_rows, do, n_kv, group, T):
    nblk = T // QBLOCK
    rows = group * QBLOCK
    scale = HEAD_DIM ** -0.5
    hp = _heads_per_step(n_kv)

    def body(q_ref, kp_ref, kc_ref, vp_ref, vc_ref, b_ref, s_ref, do_ref, dq_ref, dko_ref, dkp_ref, dvo_ref, dvp_ref, db_ref, ds_ref):
        n = pl.program_id(1)
        for hh in range(hp):
            q = q_ref[pl.ds(hh * group, group)].reshape(rows, HEAD_DIM)
            dov = do_ref[pl.ds(hh * group, group)].reshape(rows, HEAD_DIM)
            kb = jnp.concatenate([kp_ref[hh], kc_ref[hh]], axis=0)
            vb = jnp.concatenate([vp_ref[hh], vc_ref[hh]], axis=0)
            p, ps = _attn_logits(q, kb, b_ref[hh], s_ref[hh])
            dp = lax.dot_general(dov, vb, _DOT_DIMS["nt"], preferred_element_type=F32)
            delta = jnp.sum(p * dp, axis=-1, keepdims=True)
            dl = p * (dp - delta)
            dlb = dl.astype(BF16)
            dq = lax.dot_general(dlb, kb, _DOT_DIMS["nn"], preferred_element_type=F32) * scale
            dkb = lax.dot_general(dlb, q, _DOT_DIMS["tn"], preferred_element_type=F32) * scale
            dvb = lax.dot_general(p.astype(BF16), dov, _DOT_DIMS["tn"], preferred_element_type=F32)
            dq_ref[pl.ds(hh * group, group)] = dq.reshape(group, QBLOCK, HEAD_DIM)
            dkp_ref[hh] = dkb[:QBLOCK]
            dko_ref[hh] = dkb[QBLOCK:]
            dvp_ref[hh] = dvb[:QBLOCK]
            dvo_ref[hh] = dvb[QBLOCK:]
            dsink = -ps * delta

            @pl.when(n == 0)
            def _(hh=hh, dl=dl, dsink=dsink):
                db_ref[hh] = dl
                ds_ref[hh] = dsink

            @pl.when(n > 0)
            def _(hh=hh, dl=dl, dsink=dsink):
                db_ref[hh] += dl
                ds_ref[hh] += dsink

    qs, kp, kc, bsp, ssp, bmsp = _attn_specs(group, hp, rows)
    kv_shape = jax.ShapeDtypeStruct(kn.shape, F32)
    return pl.pallas_call(
        body, name="attn_bwd", grid=(n_kv // hp, nblk), in_specs=[qs, kp, kc, kp, kc, bmsp, ssp, qs],
        out_specs=[qs, kc, kc, kc, kc, bsp, ssp],
        out_shape=[jax.ShapeDtypeStruct(qn.shape, F32), kv_shape, kv_shape, kv_shape, kv_shape,
                   jax.ShapeDtypeStruct(bias.shape[1:], F32), jax.ShapeDtypeStruct(sink_rows.shape, F32)],
        compiler_params=_cparams(("arbitrary", "arbitrary")),
    )(qn, kn, kn, v, v, bias, sink_rows, do)


def _adamw(name, w, g, m, v):
    shape = w.shape
    w2, g2, m2, v2 = (a.reshape(-1, shape[-1]) for a in (w, g, m, v))
    R, W = w2.shape
    tm = _tile(R, max(8, (1 << 19) // W), 8)
    d1 = 1.0 - ADAM_B1 ** ADAM_STEP
    d2 = 1.0 - ADAM_B2 ** ADAM_STEP

    def body(i, n, ins, outs, scr):
        wv, gv = ins[0][...], ins[1][...]
        mn = ADAM_B1 * ins[2][...] + (1.0 - ADAM_B1) * gv
        vn = ADAM_B2 * ins[3][...] + (1.0 - ADAM_B2) * (gv * gv)
        outs[0][...] = -ADAM_LR * ((mn / d1) / (jnp.sqrt(vn / d2) + ADAM_EPS) + ADAM_WD * wv)
        outs[1][...] = mn
        outs[2][...] = vn

    d, mn, vn = _rowk(name, R, tm, [(w2, "tile"), (g2, "tile"), (m2, "tile"), (v2, "tile")],
                      [((R, W), F32, "tile")] * 3, body)
    return d.reshape(shape), mn.reshape(shape), vn.reshape(shape)


def _first(accs, extras):
    return [accs[0]]


def _swiglu_fwd(accs, extras):
    a, b = accs
    s = _sig(a)
    t = a * s
    return [t, b * (s + t * (1.0 - s)), t * b]


def _ple_fwd(accs, extras):
    b, p, w_proj, x2 = extras
    g = _sig(accs[0] + b)
    q = lax.dot_general(p.astype(BF16), w_proj.astype(BF16), _DOT_DIMS["nn"], preferred_element_type=F32)
    return [g, x2 + g * q]


def _ple_bwd(accs, extras):
    d, g = extras
    dz = d * accs[0] * g * (1.0 - g)
    return [d * g, dz, _colsum(dz)]


def _swiglu_bwd(accs, extras):
    return [accs[0] * extras[1].astype(F32), accs[0] * extras[0].astype(F32)]


def _local_step(x, p, target, wts, small):
    T, D = x.shape
    L, _, PLE = p.shape
    gu, down, cin, sq, qkv_w = wts["gu"], wts["down"], wts["cin"], wts["sq"], wts["qkv"]
    FF = gu.shape[2]
    NA, NC = cin.shape[0], qkv_w.shape[0]
    QW = qkv_w.shape[2]
    KVD = (QW - D) // 2
    n_heads, n_kv = D // HEAD_DIM, KVD // HEAD_DIM
    group = n_heads // n_kv
    nblk = T // QBLOCK
    GC = D // len(POOL_WINDOWS)

    tr = _tile(T, 256, 8)
    tmm = _tile(T, 1024)
    tD = _tile(D, 1024)
    tDk = _tile(D, 2048)
    tD2 = _tile(D, 512)
    tF = _tile(FF, 512)
    tFk = _tile(FF, 2816)
    tFw = _tile(FF, 1408)
    tP = _tile(PLE, 512)
    tQ = _tile(QW, 768)
    tT = _tile(T, 1024)

    bucket = jnp.asarray(_t5_bucket_np())
    saved = []
    xs = x

    for i in range(L):
        kind, j = i % 3, i // 3
        sv = {"x": xs}
        h1 = _rms_fwd(f"rms_mix_{i}", xs, small["norm_mix"][i:i + 1], tr)
        if kind == 0:
            a_, gate, u1 = _mm(
                f"conv_in_{i}", "nn", (T, D, D), (tmm, tD2, tDk),
                [(_op(h1), _op(cin, j), 0), (_op(h1), _op(cin, j, 0, D // tD2), 1)], 2,
                lambda accs, ex: (lambda a, g: [a, g, a * _sig(g)])(accs[0] + ex[0], accs[1] + ex[1]),
                [((T, D), BF16, None, None), ((T, D), BF16, None, None), ((T, D), F32, None, None)],
                extras=[(small["conv_b_in"], "row", j, 0), (small["conv_b_in"], "row", j, D // tD2)])
            u2 = _dwconv_fwd(f"dwconv_{i}", u1, small["conv_w_dw"][j], small["conv_b_dw"][j:j + 1], tr)
            u4 = _ln_silu_fwd(f"ln_silu_{i}", u2, small["conv_ln_g"][j:j + 1], small["conv_ln_b"][j:j + 1], tr)
            x1, = _mm(f"conv_out_{i}", "nn", (T, D, D), (tmm, tD2, tDk), [(_op(u4), _op(sq, j), 0)], 1,
                      lambda accs, ex: [accs[0] + ex[0] + ex[1]], [((T, D), F32, None, None)],
                      extras=[(small["conv_b_out"], "row", j, 0), (xs, "tile", None, 0)])
            sv.update(h1=h1, a=a_, gate=gate, u1=u1, u2=u2, u4=u4)
        elif kind == 1:
            mix = _pool_fwd(f"pool_{i}", xs, small["norm_mix"][i:i + 1], tr)
            pw = small["pool_w"][j].reshape(len(POOL_WINDOWS) * GC, GC)
            kb = GC // _tile(GC, 512)
            tg = _tile(GC, 512)
            y0, x1 = _mm(f"pool_mm_{i}", "nn", (T, D, GC), (tmm, GC, tg),
                         [(_op(mix, fn=lambda i_, j_, k_, kb=kb: (i_, j_ * kb + k_)), _op(pw, fn=lambda i_, j_, k_, kb=kb: (j_ * kb + k_, 0)), 0)], 1,
                         lambda accs, ex: [accs[0], ex[1] + accs[0] * ex[0]],
                         [((T, D), F32, None, None), ((T, D), F32, None, None)],
                         extras=[(small["pool_scale"][j:j + 1], "row", None, 0), (xs, "tile", None, 0)])
            sv.update(mix=mix, y0=y0, pw=pw)
        else:
            qkv, = _mm(f"qkv_{i}", "nn", (T, QW, D), (tmm, tQ, tDk), [(_op(h1), _op(qkv_w, j), 0)], 1, _first,
                       [((T, QW), F32, None, None)])
            q_hm = qkv[:, :D].reshape(T, n_heads, HEAD_DIM).transpose(1, 0, 2).reshape(n_heads * T, HEAD_DIM)
            k_hm = qkv[:, D:D + KVD].reshape(T, n_kv, HEAD_DIM).transpose(1, 0, 2).reshape(n_kv * T, HEAD_DIM)
            v_hm = qkv[:, D + KVD:].reshape(T, n_kv, HEAD_DIM).transpose(1, 0, 2).astype(BF16)
            th = _tile(T, 2048, 8)
            qn = _head_norm_fwd(f"qnorm_{i}", q_hm, small["attn_q_norm"][j:j + 1], th).reshape(n_heads, T, HEAD_DIM)
            kn = _head_norm_fwd(f"knorm_{i}", k_hm, small["attn_k_norm"][j:j + 1], th).reshape(n_kv, T, HEAD_DIM)
            bias = _masked_bias(_bias_fwd(small["rel_bias"], bucket).reshape(n_kv, group * QBLOCK, 2 * QBLOCK))
            sink_rows = jnp.broadcast_to(small["attn_sinks"][j].reshape(n_kv, group, 1, 1), (n_kv, group, QBLOCK, 1)).reshape(n_kv, group * QBLOCK, 1)
            o_hm = _attn_fwd(qn, kn, v_hm, bias, sink_rows, n_kv, group, T)
            o = o_hm.transpose(1, 0, 2).reshape(T, D)
            x1, = _mm(f"attn_o_{i}", "nn", (T, D, D), (tmm, tD2, tDk), [(_op(o), _op(sq, NA + j), 0)], 1,
                      lambda accs, ex: [accs[0] + ex[0]], [((T, D), F32, None, None)], extras=[(xs, "tile", None, 0)])
            sv.update(h1=h1, q_hm=q_hm, k_hm=k_hm, v_hm=v_hm, qn=qn, kn=kn, bias=bias, sink_rows=sink_rows, o=o)
        h2 = _rms_fwd(f"rms_ffn_{i}", x1, small["norm_ffn"][i:i + 1], tr)
        a, b, f = _mm(f"ffn_up_{i}", "nn", (T, FF, D), (tmm, tF, tDk), [(_op(h2), _op(gu, i), 0), (_op(h2), _op(gu, L + i), 1)], 2,
                      _swiglu_fwd, [((T, FF), BF16, None, None)] * 3)
        x2, = _mm(f"ffn_down_{i}", "nn", (T, D, FF), (tmm, tD2, tFk), [(_op(f), _op(down, i), 0)], 1,
                  lambda accs, ex: [accs[0] + ex[0]], [((T, D), F32, None, None)], extras=[(x1, "tile", None, 0)])
        h3 = _rms_fwd(f"rms_ple_{i}", x2, small["norm_ple"][i:i + 1], tr)
        gt, x3 = _mm(f"ple_gate_{i}", "nn", (T, D, D), (tmm, tD2, tDk), [(_op(h3), _op(sq, NA + NC + i), 0)], 1, _ple_fwd,
                     [((T, D), F32, None, None), ((T, D), F32, None, None)],
                     extras=[(small["ple_b_gate"], "row", i, 0), (p, "rows", i, 0), (small["ple_w_proj"], "cols", i, 0), (x2, "tile", None, 0)])
        sv.update(x1=x1, h2=h2, a=sv.get("a"), fa=a, fb=b, f=f, x2=x2, h3=h3, gt=gt)
        saved.append(sv)
        xs = x3

    dx, loss = _loss_head(xs, target, tr)

    g_gu = g_down = g_cin = g_sq = g_qkv = None
    gs = {k: [None] * v.shape[0] for k, v in small.items() if k != "rel_bias"}
    gs["rel_bias"] = None
    gs["ple_w_proj"] = [None] * L

    for i in reversed(range(L)):
        kind, j = i % 3, i // 3
        sv = saved[i]
        dq, dz, dbg = _mm(f"ple_bwd_{i}", "nn", (T, D, PLE), (tmm, tD2, tP), [(_op(p, i), _op(small["ple_w_proj"], i), 0)], 1, _ple_bwd,
                          [((T, D), BF16, None, None), ((T, D), BF16, None, None), ((T // tmm, 1, D), F32, "rowsum", None)],
                          extras=[(dx, "tile", None, 0), (sv["gt"], "tile", None, 0)])
        gs["ple_b_gate"][i] = jnp.sum(dbg, axis=0)
        gs["ple_w_proj"][i], = _mm(f"d_ple_proj_{i}", "tn", (PLE, D, T), (tP, tD, tT), [(_op(p, i), _op(dq), 0)], 1, _first,
                                   [((PLE, D), F32, None, None)])
        g_sq, = _mm(f"d_ple_gate_{i}", "tn", (D, D, T), (tD, tD, tT), [(_op(sv["h3"]), _op(dz), 0)], 1, _first,
                    [(sq.shape, BF16, NA + NC + i, g_sq)])
        dh3, = _mm(f"dh_ple_{i}", "nt", (T, D, D), (tmm, tD2, tDk), [(_op(dz), _op(sq, NA + NC + i), 0)], 1, _first,
                   [((T, D), F32, None, None)])
        dx2, gs["norm_ple"][i], dx2b = _rms_bwd(f"rms_ple_bwd_{i}", dx, sv["x2"], small["norm_ple"][i:i + 1], dh3, tr, want_bf16=True)
        da, db = _mm(f"d_ffn_act_{i}", "nt", (T, FF, D), (tmm, tF, tDk), [(_op(dx2b), _op(down, i), 0)], 1, _swiglu_bwd,
                     [((T, FF), BF16, None, None)] * 2, extras=[(sv["fa"], "tile", None, 0), (sv["fb"], "tile", None, 0)])
        g_down, = _mm(f"d_ffn_down_{i}", "tn", (FF, D, T), (tFw, tD, tT), [(_op(sv["f"]), _op(dx2b), 0)], 1, _first,
                      [(down.shape, BF16, i, g_down)])
        g_gu, = _mm(f"d_ffn_gate_{i}", "tn", (D, FF, T), (tD, tFw, tT), [(_op(sv["h2"]), _op(da), 0)], 1, _first,
                    [(gu.shape, BF16, i, g_gu)])
        g_gu, = _mm(f"d_ffn_up_{i}", "tn", (D, FF, T), (tD, tFw, tT), [(_op(sv["h2"]), _op(db), 0)], 1, _first,
                    [(gu.shape, BF16, L + i, g_gu)])
        dh2, = _mm(f"dh_ffn_{i}", "nt", (T, D, FF), (tmm, tD2, tFk), [(_op(da), _op(gu, i), 0), (_op(db), _op(gu, L + i), 0)], 1, _first,
                   [((T, D), F32, None, None)])
        want_cs = kind == 0
        res = _rms_bwd(f"rms_ffn_bwd_{i}", dx2, sv["x1"], small["norm_ffn"][i:i + 1], dh2, tr, want_bf16=True, want_colsum=want_cs)
        dx1, gs["norm_ffn"][i], dx1b = res[:3]
        xin = sv["x"]
        if kind == 0:
            gs["conv_b_out"][j] = res[3]
            g_sq, = _mm(f"d_conv_out_{i}", "tn", (D, D, T), (tD, tD, tT), [(_op(sv["u4"]), _op(dx1b), 0)], 1, _first,
                        [(sq.shape, BF16, j, g_sq)])
            du4, = _mm(f"dh_conv_out_{i}", "nt", (T, D, D), (tmm, tD2, tDk), [(_op(dx1b), _op(sq, j), 0)], 1, _first,
                       [((T, D), F32, None, None)])
            du2, gs["conv_ln_g"][j], gs["conv_ln_b"][j], gs["conv_b_dw"][j] = _ln_silu_bwd(
                f"ln_silu_bwd_{i}", du4, sv["u2"], small["conv_ln_g"][j:j + 1], small["conv_ln_b"][j:j + 1], tr)
            dag, gs["conv_w_dw"][j], gs["conv_b_in"][j] = _dwconv_glu_bwd(
                f"dwconv_bwd_{i}", du2, sv["u1"], sv["a"], sv["gate"], small["conv_w_dw"][j], tr)
            g_cin, = _mm(f"d_conv_in_{i}", "tn", (D, 2 * D, T), (tD, tD, tT), [(_op(sv["h1"]), _op(dag), 0)], 1, _first,
                         [(cin.shape, BF16, j, g_cin)])
            dh1, = _mm(f"dh_conv_in_{i}", "nt", (T, D, 2 * D), (tmm, tD2, tDk), [(_op(dag), _op(cin, j), 0)], 1, _first,
                       [((T, D), F32, None, None)])
        elif kind == 1:
            dys, gs["pool_scale"][j] = _pool_scale_bwd(f"pool_scale_bwd_{i}", dx1, sv["y0"], small["pool_scale"][j:j + 1], tr)
            tg = _tile(GC, 512)
            kb = GC // tg
            dmix, = _mm(f"dh_pool_{i}", "nt", (T, D, GC), (tmm, GC, tg),
                        [(_op(dys, fn=lambda i_, j_, k_, kb=kb: (i_, j_ * kb + k_)), _op(sv["pw"]), 0)], 1, _first,
                        [((T, D), F32, None, None)])
            ng = len(POOL_WINDOWS)
            gs["pool_w"][j], = _mm(f"d_pool_w_{i}", "tn", (D, GC, T), (GC, GC, tT),
                                   [(_op(sv["mix"]), _op(dys, fn=lambda i_, j_, k_: (k_, i_)), 0)], 1, _first,
                                   [((D, GC), F32, None, None)])
            gs["pool_w"][j] = gs["pool_w"][j].reshape(ng, GC, GC)
            dh1 = _pool_bwd(f"pool_bwd_{i}", dmix, tr)
        else:
            g_sq, = _mm(f"d_attn_o_{i}", "tn", (D, D, T), (tD, tD, tT), [(_op(sv["o"]), _op(dx1b), 0)], 1, _first,
                        [(sq.shape, BF16, NA + j, g_sq)])
            do, = _mm(f"dh_attn_o_{i}", "nt", (T, D, D), (tmm, tD2, tDk), [(_op(dx1b), _op(sq, NA + j), 0)], 1, _first,
                      [((T, D), BF16, None, None)])
            do_hm = do.reshape(T, n_heads, HEAD_DIM).transpose(1, 0, 2)
            dqn, dko, dkp, dvo, dvp, dbias, dsink = _attn_bwd(sv["qn"], sv["kn"], sv["v_hm"], sv["bias"], sv["sink_rows"], do_hm, n_kv, group, T)
            th = _tile(T, 2048, 8)
            dq_hm, gs["attn_q_norm"][j] = _head_norm_bwd(f"qnorm_bwd_{i}", dqn.reshape(n_heads * T, HEAD_DIM), sv["q_hm"],
                                                         small["attn_q_norm"][j:j + 1], th)
            dk_hm, gs["attn_k_norm"][j] = _band_merge(f"knorm_bwd_{i}", dko, dkp, sv["k_hm"].reshape(n_kv, T, HEAD_DIM), small["attn_k_norm"][j:j + 1])
            dv_hm, = _band_merge(f"v_merge_{i}", dvo, dvp)
            gs["attn_sinks"][j] = jnp.sum(dsink.reshape(n_heads, QBLOCK), axis=1).reshape(1, n_heads)
            rb = _bias_bwd(dbias.reshape(n_heads, QBLOCK, 2 * QBLOCK), bucket, NUM_BUCKETS).T
            gs["rel_bias"] = rb if gs["rel_bias"] is None else gs["rel_bias"] + rb
            tok = lambda t_, nh: t_.reshape(nh, T, HEAD_DIM).transpose(1, 0, 2).reshape(T, nh * HEAD_DIM)
            dqkv = jnp.concatenate([tok(dq_hm, n_heads), tok(dk_hm, n_kv), tok(dv_hm, n_kv)], axis=1)
            g_qkv, = _mm(f"d_qkv_{i}", "tn", (D, QW, T), (tD, tQ, tT), [(_op(sv["h1"]), _op(dqkv), 0)], 1, _first,
                         [(qkv_w.shape, BF16, j, g_qkv)])
            dh1, = _mm(f"dh_qkv_{i}", "nt", (T, D, QW), (tmm, tD2, _tile(QW, 3072)), [(_op(dqkv), _op(qkv_w, j), 0)], 1, _first,
                       [((T, D), F32, None, None)])
        dx, gs["norm_mix"][i] = _rms_bwd(f"rms_mix_bwd_{i}", dx1, xin, small["norm_mix"][i:i + 1], dh1, tr)

    big = {"gu": g_gu, "down": g_down, "cin": g_cin, "sq": g_sq, "qkv": g_qkv}
    gsmall = {}
    for k, v in gs.items():
        if k == "rel_bias":
            gsmall[k] = v
        else:
            gsmall[k] = jnp.stack([t.reshape(small[k].shape[1:]) for t in v], axis=0)
    return loss, dx, big, gsmall


_ANY = pl.BlockSpec(memory_space=pl.ANY)


def _place():
    x, y, c = lax.axis_index("x"), lax.axis_index("y"), lax.axis_index("c")
    return x, y, c, [(1 - x, y), (x, 1 - y), (1 - x, 1 - y)]


def _lane_start(s, w):
    return pl.multiple_of(s * w, LANES) if w % LANES == 0 else s * w


def _slot(ref, kind, s):
    if kind == "col":
        w = ref.shape[2] // N_SLOTS
        return ref.at[:, :, pl.ds(_lane_start(s, w), w)]
    return ref.at[:, pl.ds(s, 1)]


def _rows_quarter(ref, h, quarter):
    n = ref.shape[-2] // 4
    if len(ref.shape) == 3:
        return ref.at[:, pl.ds((2 * h + quarter) * n, n), :]
    return ref.at[:, :, pl.ds((2 * h + quarter) * n, n), :]


def _rows_half(ref, h):
    n = ref.shape[-2] // 2
    if len(ref.shape) == 3:
        return ref.at[:, pl.ds(h * n, n), :]
    return ref.at[:, :, pl.ds(h * n, n), :]


def _place_own(name, shard, kind, s_arr):
    if kind == "col":
        lead, R, W = shard.shape
        full = (lead, R, N_SLOTS * W)
    else:
        lead, _, R, W = shard.shape
        full = (lead, N_SLOTS, R, W)
    tr = _tile(R, max(16, (1 << 19) // W), 16)
    if kind == "col":
        i_spec = pl.BlockSpec((None, tr, W), lambda l, i, s: (l, i, 0))
        o_spec = pl.BlockSpec((None, tr, W), lambda l, i, s: (l, i, s[0]))
    else:
        i_spec = pl.BlockSpec((None, None, tr, W), lambda l, i, s: (l, 0, i, 0))
        o_spec = pl.BlockSpec((None, None, tr, W), lambda l, i, s: (l, s[0], i, 0))

    def body(s_ref, i_ref, o_ref):
        o_ref[...] = i_ref[...]

    return pl.pallas_call(
        body, name=name,
        grid_spec=pltpu.PrefetchScalarGridSpec(num_scalar_prefetch=1, grid=(lead, R // tr), in_specs=[i_spec], out_specs=o_spec),
        out_shape=jax.ShapeDtypeStruct(full, shard.dtype), compiler_params=_cparams(("arbitrary", "arbitrary")),
    )(s_arr, shard)


def _gather(shards, fulls, kinds):
    ng = len(shards)
    n_sem = 8

    def body(*refs):
        sh, out = refs[:ng], refs[2 * ng:3 * ng]
        send, recv = refs[3 * ng:]
        x, y, c, _ = _place()
        s, s_x, s_y, s_d = 2 * x + y, 2 * (1 - x) + y, 2 * x + (1 - y), 2 * (1 - x) + (1 - y)
        to_x, to_y, sib = (1 - x, y, c), (x, 1 - y, c), (x, y, 1 - c)

        def rcopy(g, k, src, dst, dev):
            return pltpu.make_async_remote_copy(src_ref=src, dst_ref=dst, send_sem=send.at[g * n_sem + k], recv_sem=recv.at[g * n_sem + k],
                                                device_id=dev, device_id_type=MESH)

        def win(g, slot, h, quarter=None):
            w = _slot(out[g], kinds[g], slot)
            return _rows_half(w, h) if quarter is None else _rows_quarter(w, h, quarter)

        sent = []

        def go(cp):
            cp.start()
            sent.append(cp)

        for g in range(ng):
            go(rcopy(g, 0, _rows_half(sh[g], c), win(g, s, c), to_x))
            go(rcopy(g, 1, _rows_half(sh[g], c), win(g, s, c), to_y))
        for g in range(ng):
            rcopy(g, 0, win(g, s_x, c), win(g, s_x, c), to_x).wait_recv()
            go(rcopy(g, 3, win(g, s_x, c, 1), win(g, s_x, c, 1), to_y))
            go(rcopy(g, 4, win(g, s_x, c), win(g, s_x, c), sib))
            rcopy(g, 1, win(g, s_y, c), win(g, s_y, c), to_y).wait_recv()
            go(rcopy(g, 2, win(g, s_y, c, 0), win(g, s_y, c, 0), to_x))
            go(rcopy(g, 5, win(g, s_y, c), win(g, s_y, c), sib))
        for g in range(ng):
            for k, quarter in ((2, 0), (3, 1)):
                rcopy(g, k, win(g, s_d, c, quarter), win(g, s_d, c, quarter), sib).wait_recv()
                go(rcopy(g, 6 + quarter, win(g, s_d, c, quarter), win(g, s_d, c, quarter), sib))
        for g in range(ng):
            rcopy(g, 4, win(g, s_x, 1 - c), win(g, s_x, 1 - c), sib).wait_recv()
            rcopy(g, 5, win(g, s_y, 1 - c), win(g, s_y, 1 - c), sib).wait_recv()
            for quarter in (0, 1):
                rcopy(g, 6 + quarter, win(g, s_d, 1 - c, quarter), win(g, s_d, 1 - c, quarter), sib).wait_recv()
        for cp in sent:
            cp.wait_send()

    return pl.pallas_call(
        body, name="gather_weights", in_specs=[_ANY] * (2 * ng), out_specs=[_ANY] * ng,
        out_shape=[jax.ShapeDtypeStruct(a.shape, a.dtype) for a in fulls],
        input_output_aliases={ng + g: g for g in range(ng)},
        scratch_shapes=[pltpu.SemaphoreType.DMA((n_sem * ng,)), pltpu.SemaphoreType.DMA((n_sem * ng,))],
    )(*shards, *fulls)


def _pair_send(grads):
    ng = len(grads)

    def half_shape(a):
        s = list(a.shape)
        s[-2] //= 2
        return tuple(s)

    def body(*refs):
        gr, out = refs[:ng], refs[ng:2 * ng]
        send, recv = refs[2 * ng:]
        x, y, c, _ = _place()
        cps = [pltpu.make_async_remote_copy(src_ref=_rows_half(gr[g], 1 - c), dst_ref=out[g], send_sem=send.at[g], recv_sem=recv.at[g],
                                            device_id=(x, y, 1 - c), device_id_type=MESH) for g in range(ng)]
        for cp in cps:
            cp.start()
        for cp in cps:
            cp.wait()

    return pl.pallas_call(
        body, name="grad_pair_send", in_specs=[_ANY] * ng, out_specs=[_ANY] * ng,
        out_shape=[jax.ShapeDtypeStruct(half_shape(a), a.dtype) for a in grads],
        scratch_shapes=[pltpu.SemaphoreType.DMA((ng,)), pltpu.SemaphoreType.DMA((ng,))],
    )(*grads)


def _add_half(name, g3, pa3, c_arr):
    n, R, N = g3.shape
    rh = R // 2
    tr = _tile(rh, max(16, (1 << 19) // N), 16)
    nb = rh // tr

    def body(c_ref, g_ref, p_ref, o_ref):
        o_ref[...] = (g_ref[...].astype(F32) + p_ref[...].astype(F32)).astype(o_ref.dtype)

    return pl.pallas_call(
        body, name=name,
        grid_spec=pltpu.PrefetchScalarGridSpec(
            num_scalar_prefetch=1, grid=(n, nb),
            in_specs=[pl.BlockSpec((None, tr, N), lambda l, i, c, nb=nb: (l, c[0] * nb + i, 0)), pl.BlockSpec((None, tr, N), lambda l, i, c: (l, i, 0))],
            out_specs=pl.BlockSpec((None, tr, N), lambda l, i, c: (l, i, 0))),
        out_shape=jax.ShapeDtypeStruct(pa3.shape, g3.dtype), compiler_params=_cparams(("arbitrary", "arbitrary")),
    )(c_arr, g3, pa3)


def _quarter_shape(a, kind):
    s = a.shape
    return (s[0], s[1] // 2, s[2] // N_SLOTS) if kind == "col" else (s[0], 1, s[2] // 2, s[3])


def _ici_exchange_direct(psums, kinds):
    ng = len(psums)

    def body(*refs):
        ps, direct, relay = refs[:ng], refs[ng:2 * ng], refs[2 * ng:3 * ng]
        send, recv = refs[3 * ng:]
        x, y, c, _ = _place()
        s_x, s_y, s_d = 2 * (1 - x) + y, 2 * x + (1 - y), 2 * (1 - x) + (1 - y)
        to_x, to_y = (1 - x, y, c), (x, 1 - y, c)
        cps = []
        for g in range(ng):
            quarter = lambda slot, q, g=g: _rows_half(_slot(ps[g], kinds[g], slot), q)
            plan = ((quarter(s_x, 1), direct[g].at[0], to_x), (quarter(s_y, 0), direct[g].at[1], to_y),
                    (quarter(s_d, 1), relay[g].at[0], to_x), (quarter(s_d, 0), relay[g].at[1], to_y))
            for k, (src, dst, dev) in enumerate(plan):
                cps.append(pltpu.make_async_remote_copy(src_ref=src, dst_ref=dst, send_sem=send.at[g * 4 + k], recv_sem=recv.at[g * 4 + k],
                                                        device_id=dev, device_id_type=MESH))
                cps[-1].start()
        for cp in cps:
            cp.wait()

    shapes = [jax.ShapeDtypeStruct((2,) + _quarter_shape(a, k), a.dtype) for a, k in zip(psums, kinds)]
    res = pl.pallas_call(
        body, name="grad_ici_direct", in_specs=[_ANY] * ng, out_specs=[_ANY] * (2 * ng), out_shape=shapes + shapes,
        scratch_shapes=[pltpu.SemaphoreType.DMA((4 * ng,)), pltpu.SemaphoreType.DMA((4 * ng,))],
    )(*psums)
    return res[:ng], res[ng:]


def _ici_exchange_relayed(sums):
    ng = len(sums)

    def body(*refs):
        sm, out = refs[:ng], refs[ng:2 * ng]
        send, recv = refs[2 * ng:]
        x, y, c, _ = _place()
        cps = []
        for g in range(ng):
            for k, (t, dev) in enumerate(((1, (1 - x, y, c)), (0, (x, 1 - y, c)))):
                cps.append(pltpu.make_async_remote_copy(src_ref=sm[g].at[t], dst_ref=out[g].at[k], send_sem=send.at[g * 2 + k],
                                                        recv_sem=recv.at[g * 2 + k], device_id=dev, device_id_type=MESH))
                cps[-1].start()
        for cp in cps:
            cp.wait()

    return pl.pallas_call(
        body, name="grad_ici_relayed", in_specs=[_ANY] * ng, out_specs=[_ANY] * ng,
        out_shape=[jax.ShapeDtypeStruct(a.shape, a.dtype) for a in sums],
        scratch_shapes=[pltpu.SemaphoreType.DMA((2 * ng,)), pltpu.SemaphoreType.DMA((2 * ng,))],
    )(*sums)


def _quarter_tiles(p3, w):
    rq = p3.shape[1] // 2
    tr = _tile(rq, max(16, (1 << 18) // w), 16)
    return rq, tr, rq // tr


def _p_spec(kind, tr, w, row_block, slot):
    if kind == "col":
        return pl.BlockSpec((None, tr, w), lambda t, l, i, s: (l, row_block(t, i), slot(t, s)))
    return pl.BlockSpec((None, tr, w), lambda t, l, i, s: (l * N_SLOTS + slot(t, s), row_block(t, i), 0))


def _relay_add(name, p3, relay3, yx_arr, lead, kind):
    w = relay3.shape[2]
    rq, tr, nbq = _quarter_tiles(p3, w)
    two = pl.BlockSpec((None, tr, w), lambda t, l, i, s: (t * lead + l, i, 0))

    def body(s_ref, p_ref, r_ref, o_ref):
        o_ref[...] = (p_ref[...].astype(F32) + r_ref[...].astype(F32)).astype(o_ref.dtype)

    return pl.pallas_call(
        body, name=name,
        grid_spec=pltpu.PrefetchScalarGridSpec(
            num_scalar_prefetch=1, grid=(2, lead, nbq),
            in_specs=[_p_spec(kind, tr, w, lambda t, i: (1 - t) * nbq + i, lambda t, s: jnp.where(t == 0, s[0], s[1])), two], out_specs=two),
        out_shape=jax.ShapeDtypeStruct(relay3.shape, relay3.dtype), compiler_params=_cparams(("arbitrary",) * 3),
    )(yx_arr, p3, relay3)


def _slot_sum(name, p3, direct3, relayed3, s_arr, lead, kind):
    w = direct3.shape[2]
    rq, tr, nbq = _quarter_tiles(p3, w)
    d_spec = pl.BlockSpec((None, tr, w), lambda t, l, i, s: ((1 - t) * lead + l, i, 0))
    r_spec = pl.BlockSpec((None, tr, w), lambda t, l, i, s: (t * lead + l, i, 0))

    def body(s_ref, p_ref, d_ref, r_ref, o_ref):
        o_ref[...] = (p_ref[...].astype(F32) + d_ref[...].astype(F32)) + r_ref[...].astype(F32)

    return pl.pallas_call(
        body, name=name,
        grid_spec=pltpu.PrefetchScalarGridSpec(
            num_scalar_prefetch=1, grid=(2, lead, nbq),
            in_specs=[_p_spec(kind, tr, w, lambda t, i: t * nbq + i, lambda t, s: s[0]), d_spec, r_spec],
            out_specs=pl.BlockSpec((None, tr, w), lambda t, l, i, s: (l, t * nbq + i, 0))),
        out_shape=jax.ShapeDtypeStruct((lead, 2 * rq, w), F32), compiler_params=_cparams(("arbitrary",) * 3),
    )(s_arr, p3, direct3, relayed3)


def _pair_swap(halves):
    ng = len(halves)

    def body(*refs):
        hv, out = refs[:ng], refs[ng:2 * ng]
        send, recv = refs[2 * ng:]
        x, y, c, _ = _place()
        cps = [pltpu.make_async_remote_copy(src_ref=hv[g], dst_ref=out[g], send_sem=send.at[g], recv_sem=recv.at[g],
                                            device_id=(x, y, 1 - c), device_id_type=MESH) for g in range(ng)]
        for cp in cps:
            cp.start()
        for cp in cps:
            cp.wait()

    return pl.pallas_call(
        body, name="grad_pair_swap", in_specs=[_ANY] * ng, out_specs=[_ANY] * ng,
        out_shape=[jax.ShapeDtypeStruct(a.shape, a.dtype) for a in halves],
        scratch_shapes=[pltpu.SemaphoreType.DMA((ng,)), pltpu.SemaphoreType.DMA((ng,))],
    )(*halves)


N_DEVICES = 8


def _allreduce_small(v):
    rows, m = v.shape

    def body(v_ref, o_ref, buf, send, recv):
        x, y, c, _ = _place()
        me = 4 * x + 2 * y + c
        buf[me] = v_ref[...]
        cps = []
        for k in range(1, N_DEVICES):
            peer = me ^ k
            cps.append(pltpu.make_async_remote_copy(src_ref=v_ref, dst_ref=buf.at[me], send_sem=send.at[k - 1], recv_sem=recv.at[k - 1],
                                                    device_id=((peer >> 2) & 1, (peer >> 1) & 1, peer & 1), device_id_type=MESH))
            cps[-1].start()
        for k in range(1, N_DEVICES):
            theirs = buf.at[me ^ k]
            pltpu.make_async_remote_copy(src_ref=v_ref, dst_ref=theirs, send_sem=send.at[k - 1], recv_sem=recv.at[k - 1],
                                         device_id=(x, y, c), device_id_type=MESH).wait_recv()
        for cp in cps:
            cp.wait_send()
        acc = buf[0]
        for d in range(1, N_DEVICES):
            acc = acc + buf[d]
        o_ref[...] = acc

    vm = pl.BlockSpec(memory_space=pltpu.VMEM)
    return pl.pallas_call(
        body, name="allreduce_small", in_specs=[vm], out_specs=vm, out_shape=jax.ShapeDtypeStruct(v.shape, F32),
        scratch_shapes=[pltpu.VMEM((N_DEVICES, rows, m), F32), pltpu.SemaphoreType.DMA((N_DEVICES - 1,)), pltpu.SemaphoreType.DMA((N_DEVICES - 1,))],
    )(v)


def _pad_rows(a, mult):
    r = (-a.shape[0]) % mult
    return a if r == 0 else jnp.concatenate([a, jnp.zeros((r,) + a.shape[1:], a.dtype)], axis=0)


def _pack_rows(parts, width, mult=16):
    rows, offs, at = [], [], 0
    for a in parts:
        a2 = _pad_rows(a.reshape(-1, width), mult)
        offs.append((at, a.size // width))
        rows.append(a2)
        at += a2.shape[0]
    return _pad_rows(jnp.concatenate(rows, axis=0), 4 * SUBLANES), offs


SMALL_SHARDED = ("ple_w_proj", "pool_w", "conv_w_dw", "conv_b_dw", "conv_ln_g", "conv_ln_b", "conv_b_out", "conv_b_in")
SMALL_REPLICATED = ("norm_mix", "norm_ffn", "norm_ple", "pool_scale", "attn_q_norm", "attn_k_norm", "attn_sinks", "rel_bias", "ple_b_gate")


def _small_to_full(name, slots):
    if name == "pool_w":
        return jnp.moveaxis(slots, 0, 2).reshape(slots.shape[1], slots.shape[2], N_SLOTS * slots.shape[3], slots.shape[4])
    return jnp.moveaxis(slots, 0, -2).reshape(slots.shape[1:-1] + (N_SLOTS * slots.shape[-1],))


def _small_to_slots(name, full):
    if name == "pool_w":
        nb, ng, gc, _ = full.shape
        return jnp.moveaxis(full.reshape(nb, ng, N_SLOTS, gc // N_SLOTS, gc), 2, 0)
    w = full.shape[-1] // N_SLOTS
    return jnp.moveaxis(full.reshape(full.shape[:-1] + (N_SLOTS, w)), -2, 0)


W_NAMES = ("norm_mix", "norm_ffn", "norm_ple", "conv_w_in", "conv_b_in", "conv_w_dw", "conv_b_dw", "conv_ln_g", "conv_ln_b", "conv_w_out",
           "conv_b_out", "pool_w", "pool_scale", "attn_w_qkv", "attn_q_norm", "attn_k_norm", "attn_sinks", "attn_w_o", "rel_bias",
           "ffn_w_gate", "ffn_w_up", "ffn_w_down", "ple_w_proj", "ple_w_gate", "ple_b_gate")


def _step(x, p, target, w, m, v):
    T, D = x.shape[1], x.shape[2]
    L = p.shape[0]
    NA, NC = w["conv_w_in"].shape[0], w["attn_w_qkv"].shape[0]
    xi, yi, ci = lax.axis_index("x"), lax.axis_index("y"), lax.axis_index("c")
    c_arr = jnp.reshape(ci, (1,)).astype(jnp.int32)
    s_arr = jnp.reshape(2 * xi + yi, (1,)).astype(jnp.int32)

    wq = D // N_SLOTS
    sm_pack, sm_offs = _pack_rows([w[k] for k in SMALL_SHARDED], wq)
    shards = [
        jnp.concatenate([w["ffn_w_gate"], w["ffn_w_up"]], axis=0).astype(BF16),
        w["ffn_w_down"].astype(BF16)[:, None],
        w["conv_w_in"].astype(BF16),
        jnp.concatenate([w["conv_w_out"], w["attn_w_o"], w["ple_w_gate"]], axis=0).astype(BF16)[:, None],
        w["attn_w_qkv"].astype(BF16),
        sm_pack[None, None],
    ]
    kinds = ["col", "row", "col", "row", "col", "row"]
    full = _gather(shards, [_place_own(f"place_own_{g}", a, k, s_arr) for g, (a, k) in enumerate(zip(shards, kinds))], kinds)
    wts = {"gu": full[0], "down": full[1].reshape(L, -1, D), "cin": full[2], "sq": full[3].reshape(NA + NC + L, D, D), "qkv": full[4]}
    small = {k: w[k] for k in SMALL_REPLICATED}
    for k, (at, n) in zip(SMALL_SHARDED, sm_offs):
        small[k] = _small_to_full(k, full[5][0, :, at:at + n].reshape((N_SLOTS,) + w[k].shape))

    loss, dx, big, gsmall = _local_step(x[0], p[:, 0], target[0], wts, small)

    rep_parts = [gsmall[k] for k in SMALL_REPLICATED] + [loss]
    flat = jnp.concatenate([a.reshape(-1) for a in rep_parts])
    n_flat = flat.shape[0]
    m_cols = -(-n_flat // (8 * LANES)) * LANES
    flat = jnp.concatenate([flat, jnp.zeros((8 * m_cols - n_flat,), F32)]).reshape(8, m_cols)
    red = _allreduce_small(flat).reshape(-1)
    grads, at = {}, 0
    for k in SMALL_REPLICATED:
        grads[k] = red[at:at + w[k].size].reshape(w[k].shape)
        at += w[k].size
    loss_out = red[at]

    slots = {k: _small_to_slots(k, gsmall[k]) for k in SMALL_SHARDED}
    gsm = jnp.stack([_pack_rows([slots[k][s] for k in SMALL_SHARDED], wq)[0] for s in range(N_SLOTS)], axis=0)
    local = [big["gu"], big["down"].reshape(L, N_SLOTS, -1, D), big["cin"], big["sq"].reshape(NA + NC + L, N_SLOTS, -1, D), big["qkv"],
             gsm[None]]
    theirs = _pair_send(local)
    psums = []
    for g, (a, t) in enumerate(zip(local, theirs)):
        if kinds[g] == "col":
            psums.append(_add_half(f"pair_add_{g}", a, t, c_arr))
        else:
            n4 = a.shape[0] * N_SLOTS
            psums.append(_add_half(f"pair_add_{g}", a.reshape(n4, a.shape[2], a.shape[3]), t.reshape(n4, t.shape[2], t.shape[3]), c_arr).reshape(t.shape))
    yx_arr = jnp.stack([2 * xi + (1 - yi), 2 * (1 - xi) + yi]).astype(jnp.int32)
    direct, relay = _ici_exchange_direct(psums, kinds)
    p3s = [ps if k == "col" else ps.reshape(ps.shape[0] * N_SLOTS, ps.shape[2], ps.shape[3]) for ps, k in zip(psums, kinds)]
    flat3 = lambda a: a.reshape(2 * a.shape[1], a.shape[-2], a.shape[-1])
    sums = [_relay_add(f"relay_add_{g}", p3, flat3(rl), yx_arr, rl.shape[1], kinds[g]).reshape(rl.shape) for g, (p3, rl) in enumerate(zip(p3s, relay))]
    relayed = _ici_exchange_relayed(sums)
    halves = [_slot_sum(f"slot_sum_{g}", p3, flat3(d), flat3(r), s_arr, d.shape[1], kinds[g]) for g, (p3, d, r) in enumerate(zip(p3s, direct, relayed))]
    first = ci == 0
    gsh = [jnp.concatenate([jnp.where(first, a, b), jnp.where(first, b, a)], axis=1) for a, b in zip(halves, _pair_swap(halves))]
    grads["ffn_w_gate"], grads["ffn_w_up"] = gsh[0][:L], gsh[0][L:]
    grads["ffn_w_down"] = gsh[1]
    grads["conv_w_in"] = gsh[2]
    grads["conv_w_out"], grads["attn_w_o"], grads["ple_w_gate"] = gsh[3][:NA], gsh[3][NA:NA + NC], gsh[3][NA + NC:]
    grads["attn_w_qkv"] = gsh[4]
    for k, (at, n) in zip(SMALL_SHARDED, sm_offs):
        grads[k] = gsh[5][0, at:at + n].reshape(w[k].shape)

    outs_d, outs_m, outs_v = [], [], []
    for k in W_NAMES:
        d_, m_, v_ = _adamw(f"adamw_{k}", w[k], grads[k], m[k], v[k])
        outs_d.append(d_)
        outs_m.append(m_)
        outs_v.append(v_)
    return (loss_out, dx[None], *[grads[k] for k in W_NAMES], *outs_d, *outs_m, *outs_v)


def kernel(x, p, norm_mix, norm_ffn, norm_ple, conv_w_in, conv_b_in, conv_w_dw, conv_b_dw, conv_ln_g, conv_ln_b, conv_w_out, conv_b_out, pool_w, pool_scale, attn_w_qkv, attn_q_norm, attn_k_norm, attn_sinks, attn_w_o, rel_bias, ffn_w_gate, ffn_w_up, ffn_w_down, ple_w_proj, ple_w_gate, ple_b_gate, loss_target, m_norm_mix, m_norm_ffn, m_norm_ple, m_conv_w_in, m_conv_b_in, m_conv_w_dw, m_conv_b_dw, m_conv_ln_g, m_conv_ln_b, m_conv_w_out, m_conv_b_out, m_pool_w, m_pool_scale, m_attn_w_qkv, m_attn_q_norm, m_attn_k_norm, m_attn_sinks, m_attn_w_o, m_rel_bias, m_ffn_w_gate, m_ffn_w_up, m_ffn_w_down, m_ple_w_proj, m_ple_w_gate, m_ple_b_gate, v_norm_mix, v_norm_ffn, v_norm_ple, v_conv_w_in, v_conv_b_in, v_conv_w_dw, v_conv_b_dw, v_conv_ln_g, v_conv_ln_b, v_conv_w_out, v_conv_b_out, v_pool_w, v_pool_scale, v_attn_w_qkv, v_attn_q_norm, v_attn_k_norm, v_attn_sinks, v_attn_w_o, v_rel_bias, v_ffn_w_gate, v_ffn_w_up, v_ffn_w_down, v_ple_w_proj, v_ple_w_gate, v_ple_b_gate):
    ws_ = (norm_mix, norm_ffn, norm_ple, conv_w_in, conv_b_in, conv_w_dw, conv_b_dw, conv_ln_g, conv_ln_b, conv_w_out, conv_b_out, pool_w, pool_scale, attn_w_qkv, attn_q_norm, attn_k_norm, attn_sinks, attn_w_o, rel_bias, ffn_w_gate, ffn_w_up, ffn_w_down, ple_w_proj, ple_w_gate, ple_b_gate)
    ms_ = (m_norm_mix, m_norm_ffn, m_norm_ple, m_conv_w_in, m_conv_b_in, m_conv_w_dw, m_conv_b_dw, m_conv_ln_g, m_conv_ln_b, m_conv_w_out, m_conv_b_out, m_pool_w, m_pool_scale, m_attn_w_qkv, m_attn_q_norm, m_attn_k_norm, m_attn_sinks, m_attn_w_o, m_rel_bias, m_ffn_w_gate, m_ffn_w_up, m_ffn_w_down, m_ple_w_proj, m_ple_w_gate, m_ple_b_gate)
    vs_ = (v_norm_mix, v_norm_ffn, v_norm_ple, v_conv_w_in, v_conv_b_in, v_conv_w_dw, v_conv_b_dw, v_conv_ln_g, v_conv_ln_b, v_conv_w_out, v_conv_b_out, v_pool_w, v_pool_scale, v_attn_w_qkv, v_attn_q_norm, v_attn_k_norm, v_attn_sinks, v_attn_w_o, v_rel_bias, v_ffn_w_gate, v_ffn_w_up, v_ffn_w_down, v_ple_w_proj, v_ple_w_gate, v_ple_b_gate)
    return _step(x, p, loss_target, dict(zip(W_NAMES, ws_)), dict(zip(W_NAMES, ms_)), dict(zip(W_NAMES, vs_)))
```

```python
import functools
import math

import jax
import jax.numpy as jnp
import numpy as np
from jax import lax
from jax.experimental import pallas as pl
from jax.experimental.pallas import tpu as pltpu

F32 = jnp.float32
BF16 = jnp.bfloat16
MESH = pl.DeviceIdType.MESH

CHUNK = 64
CONV_WIDTH = 31
POOL_WINDOWS = (2, 4, 8, 16)
HEAD_DIM = 64
WINDOW_CHUNKS = 2
QBLOCK = 128
NUM_BUCKETS = 32
REL_MAX_DIST = 128
EPS = 1e-6
NEG_INF = -1e30
ADAM_LR, ADAM_B1, ADAM_B2, ADAM_EPS, ADAM_WD, ADAM_STEP = 0.001, 0.9, 0.999, 1e-08, 0.01, 10
N_SLOTS = 4
LANES = 128
VMEM_LIMIT_BYTES = 56 * 1024 * 1024


def _cparams(sem):
    return pltpu.CompilerParams(dimension_semantics=sem, vmem_limit_bytes=VMEM_LIMIT_BYTES)


def _tile(n, pref, mult=LANES):
    if n <= pref:
        return n
    t = (pref // mult) * mult
    while t >= mult:
        if n % t == 0:
            return t
        t -= mult
    return n


def _sig(z):
    return 1.0 / (1.0 + jnp.exp(-z))


def _op(arr, lead=None, ro=0, co=0, fn=None):
    return (arr, lead, ro, co, fn)


_DOT_DIMS = {"nn": (((1,), (0,)), ((), ())), "nt": (((1,), (1,)), ((), ())), "tn": (((0,), (0,)), ((), ()))}


def _mm(name, mode, dims, tiles, terms, n_acc, epilogue, outs, extras=()):
    M, N, K = dims
    tm, tn, tk = tiles
    assert M % tm == 0 and N % tn == 0 and K % tk == 0, (name, dims, tiles)
    nk = K // tk
    a_tile = (tk, tm) if mode == "tn" else (tm, tk)
    b_tile = (tn, tk) if mode == "nt" else (tk, tn)
    a_fn = (lambda i, j, k: (k, i)) if mode == "tn" else (lambda i, j, k: (i, k))
    b_fn = (lambda i, j, k: (j, k)) if mode == "nt" else (lambda i, j, k: (k, j))
    dn = _DOT_DIMS[mode]

    operands, specs, seen = [], [], {}

    def add(op, tshape, default_fn):
        arr, lead, ro, co, fn = op
        fn = fn or default_fn
        key = (id(arr), lead, ro, co, id(fn) if op[4] is not None else None, tshape)
        if key in seen:
            return seen[key]

        def imap(i, j, k, fn=fn, lead=lead, ro=ro, co=co):
            r, c = fn(i, j, k)
            return (r + ro, c + co) if lead is None else (lead, r + ro, c + co)

        operands.append(arr)
        specs.append(pl.BlockSpec(tshape if lead is None else (None,) + tshape, imap))
        seen[key] = len(operands) - 1
        return seen[key]

    term_idx = [(add(a, a_tile, a_fn), add(b, b_tile, b_fn), acc) for a, b, acc in terms]
    extra_idx = []
    for arr, kind, lead, co in extras:
        if kind == "tile":
            extra_idx.append(add(_op(arr, lead, 0, co), (tm, tn), lambda i, j, k: (i, j)))
        elif kind == "row":
            arr3 = arr.reshape(arr.shape[0], 1, arr.shape[1])
            extra_idx.append(add(_op(arr3, 0 if lead is None else lead, 0, co), (1, tn), lambda i, j, k: (0, j)))
        elif kind == "rows":
            extra_idx.append(add(_op(arr, lead, 0, 0), (tm, arr.shape[-1]), lambda i, j, k: (i, 0)))
        elif kind == "cols":
            extra_idx.append(add(_op(arr, lead, 0, co), (arr.shape[-2], tn), lambda i, j, k: (0, j)))
        else:
            extra_idx.append(add(_op(arr, None, 0, 0), (tm, 1), lambda i, j, k: (i, 0)))
    n_in = len(operands)
    out_shapes, out_specs, aliases = [], [], {}
    for oi, (shape, dtype, lead, alias) in enumerate(outs):
        out_shapes.append(jax.ShapeDtypeStruct(shape, dtype))
        if lead == "rowsum":
            out_specs.append(pl.BlockSpec((None, 1, tn), lambda i, j, k: (i, 0, j)))
        elif lead is None:
            out_specs.append(pl.BlockSpec((tm, tn), lambda i, j, k: (i, j)))
        else:
            out_specs.append(pl.BlockSpec((None, tm, tn), lambda i, j, k, lead=lead: (lead, i, j)))
        if alias is not None:
            operands.append(alias)
            specs.append(pl.BlockSpec(memory_space=pl.ANY))
            aliases[len(operands) - 1] = oi
    n_all_in = len(operands)
    n_out = len(outs)

    def body(*refs):
        ins = refs[:n_in]
        o_refs = refs[n_all_in:n_all_in + n_out]
        accs = refs[n_all_in + n_out:]

        def dots():
            sums = [None] * n_acc
            for ai, bi, acc_i in term_idx:
                a = ins[ai][...]
                b = ins[bi][...]
                if a.dtype != BF16:
                    a = a.astype(BF16)
                if b.dtype != BF16:
                    b = b.astype(BF16)
                d = lax.dot_general(a, b, dn, preferred_element_type=F32)
                sums[acc_i] = d if sums[acc_i] is None else sums[acc_i] + d
            return sums

        def finish(vals):
            res = epilogue(vals, [ins[e][...] for e in extra_idx])
            for o, r in zip(o_refs, res):
                o[...] = r.astype(o.dtype)

        if nk == 1:
            finish(dots())
            return
        k = pl.program_id(2)

        @pl.when(k == 0)
        def _():
            for acc, d in zip(accs, dots()):
                acc[...] = d

        if nk > 2:
            @pl.when((k > 0) & (k < nk - 1))
            def _():
                for acc, d in zip(accs, dots()):
                    acc[...] += d

        @pl.when(k == nk - 1)
        def _():
            finish([acc[...] + d for acc, d in zip(accs, dots())])

    res = pl.pallas_call(
        body, name=name, grid=(M // tm, N // tn, nk), in_specs=specs, out_specs=out_specs, out_shape=out_shapes,
        scratch_shapes=[pltpu.VMEM((tm, tn), F32) for _ in range(n_acc if nk > 1 else 0)], input_output_aliases=aliases,
        compiler_params=_cparams(("parallel", "parallel", "arbitrary")),
    )(*operands)
    return res


def _rowk(name, T, tm, ins, outs, body, scratch=()):
    assert T % tm == 0, (name, T, tm)
    n = T // tm
    specs = []
    for arr, kind in ins:
        w = arr.shape[-1]
        if kind == "tile":
            specs.append(pl.BlockSpec((tm, w), lambda i: (i, 0)))
        elif kind == "prev":
            specs.append(pl.BlockSpec((tm, w), lambda i: (jnp.maximum(i - 1, 0), 0)))
        elif kind == "next":
            specs.append(pl.BlockSpec((tm, w), lambda i, n=n: (jnp.minimum(i + 1, n - 1), 0)))
        else:
            specs.append(pl.BlockSpec(arr.shape, lambda i, nd=arr.ndim: (0,) * nd))
    out_shapes, out_specs = [], []
    for shape, dtype, kind in outs:
        out_shapes.append(jax.ShapeDtypeStruct(shape, dtype))
        if kind == "tile":
            out_specs.append(pl.BlockSpec((tm, shape[-1]), lambda i: (i, 0)))
        else:
            out_specs.append(pl.BlockSpec(shape, lambda i, nd=len(shape): (0,) * nd))
    n_in, n_out = len(ins), len(outs)

    def kbody(*refs):
        body(pl.program_id(0), n, refs[:n_in], refs[n_in:n_in + n_out], refs[n_in + n_out:])

    return pl.pallas_call(
        kbody, name=name, grid=(n,), in_specs=specs, out_specs=out_specs, out_shape=out_shapes,
        scratch_shapes=list(scratch), compiler_params=_cparams(("arbitrary",)),
    )(*[a for a, _ in ins])


def _accum(ref, i, val):
    @pl.when(i == 0)
    def _():
        ref[...] = val

    @pl.when(i > 0)
    def _():
        ref[...] += val


def _colsum(v):
    return jnp.sum(v, axis=0, keepdims=True)


def _rms_r(x):
    return lax.rsqrt(jnp.mean(x * x, axis=-1, keepdims=True) + EPS)


def _rms_fwd(name, x, g, tm):
    T, D = x.shape

    def body(i, n, ins, outs, scr):
        xv = ins[0][...]
        outs[0][...] = (xv * _rms_r(xv) * ins[1][...]).astype(BF16)

    return _rowk(name, T, tm, [(x, "tile"), (g, "full")], [((T, D), BF16, "tile")], body)[0]


def _rms_bwd(name, dres, x, g, dh, tm, want_bf16=False, want_colsum=False):
    T, D = x.shape

    def body(i, n, ins, outs, scr):
        xv = ins[1][...]
        gv = ins[2][...]
        dhv = ins[3][...].astype(F32)
        r = _rms_r(xv)
        xh = xv * r
        dhg = dhv * gv
        dx = ins[0][...] + r * (dhg - xh * jnp.mean(dhg * xh, axis=-1, keepdims=True))
        outs[0][...] = dx
        _accum(outs[1], i, _colsum(dhv * xh))
        o = 2
        if want_bf16:
            outs[o][...] = dx.astype(BF16)
            o += 1
        if want_colsum:
            _accum(outs[o], i, _colsum(dx))

    outs = [((T, D), F32, "tile"), ((1, D), F32, "acc")]
    if want_bf16:
        outs.append(((T, D), BF16, "tile"))
    if want_colsum:
        outs.append(((1, D), F32, "acc"))
    return _rowk(name, T, tm, [(dres, "tile"), (x, "tile"), (g, "full"), (dh, "tile")], outs, body)


def _loss_head(y, target, tm):
    T, D = y.shape

    def body(i, n, ins, outs, scr):
        d = ins[0][...] - ins[1][...]
        outs[0][...] = d * (1.0 / D)
        _accum(outs[1], i, jnp.sum(_colsum(d * d), axis=1, keepdims=True) * (0.5 / D))

    return _rowk("loss_head", T, tm, [(y, "tile"), (target, "tile")], [((T, D), F32, "tile"), ((1, 1), F32, "acc")], body)


CONV_ROWS = 128
SUBLANES = 8
_PHASE_PAD = 32


def _phase_scratch(tm):
    return pltpu.VMEM((SUBLANES, tm + _PHASE_PAD, LANES), F32)


def _phase_copies(buf, shf, base, l0, tm):
    n = tm + _PHASE_PAD - SUBLANES
    for r in range(1, SUBLANES):
        shf[r, pl.ds(0, n), :] = buf[pl.ds(base + r, n), pl.ds(l0, LANES)]


def _phase_window(buf, shf, base, q, row0, rows, l0):
    r = q % SUBLANES
    a = q - r
    if r == 0:
        return buf[pl.ds(base + a + row0, rows), pl.ds(l0, LANES)]
    return shf[r, pl.ds(a + row0, rows), :]


def _dwconv_fwd(name, u1, w_dw, b_dw, tm):
    T, D = u1.shape
    rc_n = tm // CONV_ROWS if tm >= CONV_ROWS else 1
    rows = min(CONV_ROWS, tm)
    halo = CONV_WIDTH - 1

    base = tm - _PHASE_PAD

    def body(i, n, ins, outs, scr):
        buf, shf = scr
        buf[pl.ds(0, tm), :] = jnp.where(i > 0, ins[0][...], 0.0)
        buf[pl.ds(tm, tm), :] = ins[1][...]
        w_ref, b_ref, o_ref = ins[2], ins[3], outs[0]

        def chunk(lc, carry):
            l0 = pl.multiple_of(lc * LANES, LANES)
            _phase_copies(buf, shf, base, l0, tm)
            for rc in range(rc_n):
                acc = jnp.zeros((rows, LANES), F32) + b_ref[:, pl.ds(l0, LANES)]
                for k in range(CONV_WIDTH):
                    acc = acc + _phase_window(buf, shf, base, k + _PHASE_PAD - halo, rc * rows, rows, l0) * w_ref[pl.ds(k, 1), pl.ds(l0, LANES)]
                o_ref[pl.ds(rc * rows, rows), pl.ds(l0, LANES)] = acc
            return carry

        lax.fori_loop(0, D // LANES, chunk, 0)

    return _rowk(name, T, tm, [(u1, "prev"), (u1, "tile"), (w_dw, "full"), (b_dw, "full")], [((T, D), F32, "tile")], body,
                 scratch=[pltpu.VMEM((2 * tm, D), F32), _phase_scratch(tm)])[0]


def _ln_silu_fwd(name, u2, g, b, tm):
    T, D = u2.shape

    def body(i, n, ins, outs, scr):
        v = ins[0][...]
        mu = jnp.mean(v, axis=-1, keepdims=True)
        xc = v - mu
        y = xc * lax.rsqrt(jnp.mean(xc * xc, axis=-1, keepdims=True) + EPS) * ins[1][...] + ins[2][...]
        outs[0][...] = (y * _sig(y)).astype(BF16)

    return _rowk(name, T, tm, [(u2, "tile"), (g, "full"), (b, "full")], [((T, D), BF16, "tile")], body)[0]


def _ln_silu_bwd(name, du4, u2, g, b, tm):
    T, D = u2.shape

    def body(i, n, ins, outs, scr):
        v = ins[1][...]
        gv = ins[2][...]
        mu = jnp.mean(v, axis=-1, keepdims=True)
        xc = v - mu
        r = lax.rsqrt(jnp.mean(xc * xc, axis=-1, keepdims=True) + EPS)
        xh = xc * r
        y = xh * gv + ins[3][...]
        s = _sig(y)
        dy = ins[0][...] * (s * (1.0 + y * (1.0 - s)))
        dyg = dy * gv
        du2 = r * (dyg - jnp.mean(dyg, axis=-1, keepdims=True) - xh * jnp.mean(dyg * xh, axis=-1, keepdims=True))
        outs[0][...] = du2
        _accum(outs[1], i, _colsum(dy * xh))
        _accum(outs[2], i, _colsum(dy))
        _accum(outs[3], i, _colsum(du2))

    return _rowk(name, T, tm, [(du4, "tile"), (u2, "tile"), (g, "full"), (b, "full")],
                 [((T, D), F32, "tile"), ((1, D), F32, "acc"), ((1, D), F32, "acc"), ((1, D), F32, "acc")], body)


def _dwconv_glu_bwd(name, du2, u1, a_, gate, w_dw, tm):
    T, D = u1.shape
    rc_n = tm // CONV_ROWS if tm >= CONV_ROWS else 1
    rows = min(CONV_ROWS, tm)
    halo = CONV_WIDTH - 1

    base = tm - _PHASE_PAD

    def body(i, n, ins, outs, scr):
        bu, bd, shu, shd = scr
        bd[pl.ds(0, tm), :] = ins[0][...]
        bd[pl.ds(tm, tm), :] = jnp.where(i < n - 1, ins[1][...], 0.0)
        bu[pl.ds(0, tm), :] = jnp.where(i > 0, ins[2][...], 0.0)
        bu[pl.ds(tm, tm), :] = ins[3][...]
        a_ref, g_ref, w_ref = ins[4], ins[5], ins[6]
        dag_ref, dw_ref, db_ref = outs

        @pl.when(i == 0)
        def _():
            dw_ref[...] = jnp.zeros_like(dw_ref)
            db_ref[...] = jnp.zeros_like(db_ref)

        def chunk(lc, carry):
            l0 = pl.multiple_of(lc * LANES, LANES)
            l1 = pl.multiple_of(D + lc * LANES, LANES)
            _phase_copies(bd, shd, 0, l0, tm)
            _phase_copies(bu, shu, base, l0, tm)
            for rc in range(rc_n):
                r0 = rc * rows
                d_here = bd[pl.ds(r0, rows), pl.ds(l0, LANES)]
                acc = jnp.zeros((rows, LANES), F32)
                for k in range(CONV_WIDTH):
                    wk = w_ref[pl.ds(k, 1), pl.ds(l0, LANES)]
                    acc = acc + _phase_window(bd, shd, 0, halo - k, r0, rows, l0) * wk
                    dw_ref[pl.ds(k, 1), pl.ds(l0, LANES)] += _colsum(d_here * _phase_window(bu, shu, base, k + _PHASE_PAD - halo, r0, rows, l0))
                av = a_ref[pl.ds(r0, rows), pl.ds(l0, LANES)].astype(F32)
                sg = _sig(g_ref[pl.ds(r0, rows), pl.ds(l0, LANES)].astype(F32))
                da = acc * sg
                dg = acc * av * sg * (1.0 - sg)
                dag_ref[pl.ds(r0, rows), pl.ds(l0, LANES)] = da.astype(BF16)
                dag_ref[pl.ds(r0, rows), pl.ds(l1, LANES)] = dg.astype(BF16)
                db_ref[:, pl.ds(l0, LANES)] += _colsum(da)
                db_ref[:, pl.ds(l1, LANES)] += _colsum(dg)
            return carry

        lax.fori_loop(0, D // LANES, chunk, 0)

    return _rowk(name, T, tm, [(du2, "tile"), (du2, "next"), (u1, "prev"), (u1, "tile"), (a_, "tile"), (gate, "tile"), (w_dw, "full")],
                 [((T, 2 * D), BF16, "tile"), ((CONV_WIDTH, D), F32, "acc"), ((1, 2 * D), F32, "acc")], body,
                 scratch=[pltpu.VMEM((2 * tm, D), F32), pltpu.VMEM((2 * tm, D), F32), _phase_scratch(tm), _phase_scratch(tm)])


def _row_index(i, tm, r0, rows):
    return (i * tm + r0 + lax.broadcasted_iota(jnp.int32, (rows, 1), 0)).astype(F32)


def _pool_fwd(name, x, g, tm):
    T, D = x.shape
    gc = D // len(POOL_WINDOWS)
    rows = min(CONV_ROWS, tm)
    rc_n = tm // rows

    def body(i, n, ins, outs, scr):
        buf = scr[0]
        xp = ins[0][...]
        buf[pl.ds(0, tm), :] = jnp.where(i > 0, xp * _rms_r(xp) * ins[2][...], 0.0)
        xc = ins[1][...]
        buf[pl.ds(tm, tm), :] = xc * _rms_r(xc) * ins[2][...]
        o_ref = outs[0]
        for gi, w in enumerate(POOL_WINDOWS):
            def chunk(lc, carry, gi=gi, w=w):
                l0 = pl.multiple_of(gi * gc + lc * LANES, LANES)
                for rc in range(rc_n):
                    r0 = rc * rows
                    acc = buf[pl.ds(tm + r0, rows), pl.ds(l0, LANES)]
                    here = acc
                    for d in range(1, w):
                        acc = acc + buf[pl.ds(tm + r0 - d, rows), pl.ds(l0, LANES)]
                    cnt = jnp.minimum(_row_index(i, tm, r0, rows) + 1.0, float(w))
                    o_ref[pl.ds(r0, rows), pl.ds(l0, LANES)] = (acc / cnt - here).astype(BF16)
                return carry

            lax.fori_loop(0, gc // LANES, chunk, 0)

    return _rowk(name, T, tm, [(x, "prev"), (x, "tile"), (g, "full")], [((T, D), BF16, "tile")], body,
                 scratch=[pltpu.VMEM((2 * tm, D), F32)])[0]


def _pool_bwd(name, dmix, tm):
    T, D = dmix.shape
    gc = D // len(POOL_WINDOWS)
    rows = min(CONV_ROWS, tm)
    rc_n = tm // rows

    def body(i, n, ins, outs, scr):
        buf = scr[0]
        o_ref = outs[0]
        t_here = (i * tm + lax.broadcasted_iota(jnp.int32, (tm, 1), 0)).astype(F32) + 1.0
        for gi, w in enumerate(POOL_WINDOWS):
            cols = pl.ds(gi * gc, gc)
            buf[pl.ds(0, tm), cols] = ins[0][:, cols] / jnp.minimum(t_here, float(w))
            buf[pl.ds(tm, tm), cols] = jnp.where(i < n - 1, ins[1][:, cols] / float(w), 0.0)

            def chunk(lc, carry, gi=gi, w=w):
                l0 = pl.multiple_of(gi * gc + lc * LANES, LANES)
                for rc in range(rc_n):
                    r0 = rc * rows
                    acc = -ins[0][pl.ds(r0, rows), pl.ds(l0, LANES)]
                    for d in range(w):
                        acc = acc + buf[pl.ds(r0 + d, rows), pl.ds(l0, LANES)]
                    o_ref[pl.ds(r0, rows), pl.ds(l0, LANES)] = acc
                return carry

            lax.fori_loop(0, gc // LANES, chunk, 0)

    return _rowk(name, T, tm, [(dmix, "tile"), (dmix, "next")], [((T, D), F32, "tile")], body,
                 scratch=[pltpu.VMEM((2 * tm, D), F32)])[0]


def _pool_scale_bwd(name, dy, y0, scale, tm):
    T, D = dy.shape

    def body(i, n, ins, outs, scr):
        d = ins[0][...]
        outs[0][...] = (d * ins[2][...]).astype(BF16)
        _accum(outs[1], i, _colsum(d * ins[1][...]))

    return _rowk(name, T, tm, [(dy, "tile"), (y0, "tile"), (scale, "full")], [((T, D), BF16, "tile"), ((1, D), F32, "acc")], body)


def _t5_bucket_np():
    i = np.arange(QBLOCK)[:, None]
    j = np.arange(2 * QBLOCK)[None, :]
    rel = j - QBLOCK - i
    nb = NUM_BUCKETS // 2
    n = -rel
    ret = np.where(n < 0, nb, 0)
    n = np.abs(n)
    max_exact = nb // 2
    nf = np.maximum(n, 1).astype(np.float32)
    large = max_exact + (np.log(nf / np.float32(max_exact)) / np.float32(math.log(REL_MAX_DIST / max_exact))
                         * np.float32(nb - max_exact)).astype(np.int32)
    large = np.minimum(large, nb - 1)
    return (ret + np.where(n < max_exact, n, large)).astype(np.int32)


def _bias_fwd(rel_bias, bucket):
    nb, nh = rel_bias.shape

    def body(rb_ref, bk_ref, o_ref):
        h = pl.program_id(0)
        bk = bk_ref[...]
        acc = jnp.zeros(bk.shape, F32)
        for b in range(nb):
            acc = jnp.where(bk == b, rb_ref[b, h], acc)
        o_ref[...] = acc

    return pl.pallas_call(
        body, name="attn_bias_fwd", grid=(nh,),
        in_specs=[pl.BlockSpec(memory_space=pltpu.SMEM), pl.BlockSpec(bucket.shape, lambda h: (0, 0))],
        out_specs=pl.BlockSpec((None,) + bucket.shape, lambda h: (h, 0, 0)),
        out_shape=jax.ShapeDtypeStruct((nh,) + bucket.shape, F32), compiler_params=_cparams(("arbitrary",)),
    )(rel_bias, bucket)


def _bias_bwd(dbias, bucket, nb):
    nh = dbias.shape[0]

    def body(db_ref, bk_ref, o_ref):
        h = pl.program_id(0)
        bk = bk_ref[...]
        d = db_ref[...]
        for b in range(nb):
            o_ref[h, b] = jnp.sum(jnp.where(bk == b, d, 0.0))

    return pl.pallas_call(
        body, name="attn_bias_bwd", grid=(nh,),
        in_specs=[pl.BlockSpec((None,) + bucket.shape, lambda h: (h, 0, 0)), pl.BlockSpec(bucket.shape, lambda h: (0, 0))],
        out_specs=pl.BlockSpec(memory_space=pltpu.SMEM),
        out_shape=jax.ShapeDtypeStruct((nh, nb), F32), compiler_params=_cparams(("arbitrary",)),
    )(dbias, bucket)


def _head_norm_fwd(name, q2, g, tm):
    R, W = q2.shape

    def body(i, n, ins, outs, scr):
        v = ins[0][...]
        outs[0][...] = (v * _rms_r(v) * ins[1][...]).astype(BF16)

    return _rowk(name, R, tm, [(q2, "tile"), (g, "full")], [((R, W), BF16, "tile")], body)[0]


def _head_norm_bwd(name, dqn, q2, g, tm):
    R, W = q2.shape

    def body(i, n, ins, outs, scr):
        v = ins[1][...]
        d = ins[0][...]
        r = _rms_r(v)
        xh = v * r
        dg = d * ins[2][...]
        outs[0][...] = (r * (dg - xh * jnp.mean(dg * xh, axis=-1, keepdims=True))).astype(BF16)
        _accum(outs[1], i, _colsum(d * xh))

    return _rowk(name, R, tm, [(dqn, "tile"), (q2, "tile"), (g, "full")], [((R, W), BF16, "tile"), ((1, W), F32, "acc")], body)


def _band_merge(name, own, prev, k3=None, g=None):
    H, T, W = own.shape
    nblk = T // QBLOCK

    def body(*refs):
        o_ref, p_ref = refs[0], refs[1]
        out_ref = refs[4] if k3 is not None else refs[2]

        def blk(m, carry):
            r0 = pl.multiple_of(m * QBLOCK, QBLOCK)
            rn = pl.multiple_of(jnp.minimum(m + 1, nblk - 1) * QBLOCK, QBLOCK)
            d = o_ref[pl.ds(r0, QBLOCK), :] + jnp.where(m < nblk - 1, p_ref[pl.ds(rn, QBLOCK), :], 0.0)
            if k3 is None:
                out_ref[pl.ds(r0, QBLOCK), :] = d.astype(BF16)
                return carry
            v = refs[2][pl.ds(r0, QBLOCK), :]
            r = _rms_r(v)
            xh = v * r
            dg = d * refs[3][...]
            out_ref[pl.ds(r0, QBLOCK), :] = (r * (dg - xh * jnp.mean(dg * xh, axis=-1, keepdims=True))).astype(BF16)
            return carry + _colsum(d * xh)

        tot = lax.fori_loop(0, nblk, blk, jnp.zeros((1, W), F32))
        if k3 is not None:
            _accum(refs[5], pl.program_id(0), tot)

    head = pl.BlockSpec((None, T, W), lambda h: (h, 0, 0))
    ins, in_specs = [own, prev], [head, head]
    out_shape, out_specs = [jax.ShapeDtypeStruct((H, T, W), BF16)], [head]
    if k3 is not None:
        ins += [k3, g]
        in_specs += [head, pl.BlockSpec(g.shape, lambda h: (0, 0))]
        out_shape.append(jax.ShapeDtypeStruct((1, W), F32))
        out_specs.append(pl.BlockSpec((1, W), lambda h: (0, 0)))
    return pl.pallas_call(body, name=name, grid=(H,), in_specs=in_specs, out_specs=out_specs, out_shape=out_shape,
                          compiler_params=_cparams(("arbitrary",)))(*ins)


def _masked_bias(bias):
    rows = bias.shape[1]
    qc = (jnp.arange(rows)[:, None] % QBLOCK) // CHUNK
    j = jnp.arange(2 * QBLOCK)[None, :]
    kc = j // CHUNK - QBLOCK // CHUNK
    ok = (kc <= qc) & (kc >= qc - WINDOW_CHUNKS)
    hide = lambda visible: jnp.where(visible, 0.0, NEG_INF).astype(F32)[None]
    return jnp.stack([bias + hide(ok & (j >= QBLOCK)), bias + hide(ok)], axis=0)


def _attn_logits(q, kb, bias_masked, sink):
    s = lax.dot_general(q, kb, _DOT_DIMS["nt"], preferred_element_type=F32) * (HEAD_DIM ** -0.5) + bias_masked
    m = jnp.maximum(jnp.max(s, axis=-1, keepdims=True), sink)
    e = jnp.exp(s - m)
    es = jnp.exp(sink - m)
    den = jnp.sum(e, axis=-1, keepdims=True) + es
    return e / den, es / den


def _heads_per_step(n_kv):
    return max(h for h in (8, 4, 2, 1) if n_kv % h == 0)


def _attn_specs(group, hp, rows):
    blk = lambda hn, fn: pl.BlockSpec((hn, QBLOCK, HEAD_DIM), fn)
    cur = lambda h, n: (h, n, 0)
    prv = lambda h, n: (h, jnp.maximum(n - 1, 0), 0)
    bsp = pl.BlockSpec((hp, rows, 2 * QBLOCK), lambda h, n: (h, 0, 0))
    ssp = pl.BlockSpec((hp, rows, 1), lambda h, n: (h, 0, 0))
    bmsp = pl.BlockSpec((None, hp, rows, 2 * QBLOCK), lambda h, n: (jnp.minimum(n, 1), h, 0, 0))
    return blk(hp * group, cur), blk(hp, prv), blk(hp, cur), bsp, ssp, bmsp


def _attn_fwd(qn, kn, v, bias, sink_rows, n_kv, group, T):
    nblk = T // QBLOCK
    rows = group * QBLOCK
    hp = _heads_per_step(n_kv)

    def body(q_ref, kp_ref, kc_ref, vp_ref, vc_ref, b_ref, s_ref, o_ref):
        for hh in range(hp):
            q = q_ref[pl.ds(hh * group, group)].reshape(rows, HEAD_DIM)
            kb = jnp.concatenate([kp_ref[hh], kc_ref[hh]], axis=0)
            vb = jnp.concatenate([vp_ref[hh], vc_ref[hh]], axis=0)
            p, _ = _attn_logits(q, kb, b_ref[hh], s_ref[hh])
            o = lax.dot_general(p.astype(BF16), vb, _DOT_DIMS["nn"], preferred_element_type=F32)
            o_ref[pl.ds(hh * group, group)] = o.reshape(group, QBLOCK, HEAD_DIM).astype(BF16)

    qs, kp, kc, _, ssp, bmsp = _attn_specs(group, hp, rows)
    return pl.pallas_call(
        body, name="attn_fwd", grid=(n_kv // hp, nblk), in_specs=[qs, kp, kc, kp, kc, bmsp, ssp],
        out_specs=qs, out_shape=jax.ShapeDtypeStruct(qn.shape, BF16), compiler_params=_cparams(("arbitrary", "arbitrary")),
    )(qn, kn, kn, v, v, bias, sink_rows)


def _attn_bwd(qn, kn, v, bias, sink_rows, do, n_kv, group, T):
    nblk = T // QBLOCK
    rows = group * QBLOCK
    scale = HEAD_DIM ** -0.5
    hp = _heads_per_step(n_kv)

    def body(q_ref, kp_ref, kc_ref, vp_ref, vc_ref, b_ref, s_ref, do_ref, dq_ref, dko_ref, dkp_ref, dvo_ref, dvp_ref, db_ref, ds_ref):
        n = pl.program_id(1)
        for hh in range(hp):
            q = q_ref[pl.ds(hh * group, group)].reshape(rows, HEAD_DIM)
            dov = do_ref[pl.ds(hh * group, group)].reshape(rows, HEAD_DIM)
            kb = jnp.concatenate([kp_ref[hh], kc_ref[hh]], axis=0)
            vb = jnp.concatenate([vp_ref[hh], vc_ref[hh]], axis=0)
            p, ps = _attn_logits(q, kb, b_ref[hh], s_ref[hh])
            dp = lax.dot_general(dov, vb, _DOT_DIMS["nt"], preferred_element_type=F32)
            delta = jnp.sum(p * dp, axis=-1, keepdims=True)
            dl = p * (dp - delta)
            dlb = dl.astype(BF16)
            dq = lax.dot_general(dlb, kb, _DOT_DIMS["nn"], preferred_element_type=F32) * scale
            dkb = lax.dot_general(dlb, q, _DOT_DIMS["tn"], preferred_element_type=F32) * scale
            dvb = lax.dot_general(p.astype(BF16), dov, _DOT_DIMS["tn"], preferred_element_type=F32)
            dq_ref[pl.ds(hh * group, group)] = dq.reshape(group, QBLOCK, HEAD_DIM)
            dkp_ref[hh] = dkb[:QBLOCK]
            dko_ref[hh] = dkb[QBLOCK:]
            dvp_ref[hh] = dvb[:QBLOCK]
            dvo_ref[hh] = dvb[QBLOCK:]
            dsink = -ps * delta

            @pl.when(n == 0)
            def _(hh=hh, dl=dl, dsink=dsink):
                db_ref[hh] = dl
                ds_ref[hh] = dsink

            @pl.when(n > 0)
            def _(hh=hh, dl=dl, dsink=dsink):
                db_ref[hh] += dl
                ds_ref[hh] += dsink

    qs, kp, kc, bsp, ssp, bmsp = _attn_specs(group, hp, rows)
    kv_shape = jax.ShapeDtypeStruct(kn.shape, F32)
    return pl.pallas_call(
        body, name="attn_bwd", grid=(n_kv // hp, nblk), in_specs=[qs, kp, kc, kp, kc, bmsp, ssp, qs],
        out_specs=[qs, kc, kc, kc, kc, bsp, ssp],
        out_shape=[jax.ShapeDtypeStruct(qn.shape, F32), kv_shape, kv_shape, kv_shape, kv_shape,
                   jax.ShapeDtypeStruct(bias.shape[1:], F32), jax.ShapeDtypeStruct(sink_rows.shape, F32)],
        compiler_params=_cparams(("arbitrary", "arbitrary")),
    )(qn, kn, kn, v, v, bias, sink_rows, do)


def _adamw(name, w, g, m, v):
    shape = w.shape
    w2, g2, m2, v2 = (a.reshape(-1, shape[-1]) for a in (w, g, m, v))
    R, W = w2.shape
    tm = _tile(R, max(8, (1 << 19) // W), 8)
    d1 = 1.0 - ADAM_B1 ** ADAM_STEP
    d2 = 1.0 - ADAM_B2 ** ADAM_STEP

    def body(i, n, ins, outs, scr):
        wv, gv = ins[0][...], ins[1][...]
        mn = ADAM_B1 * ins[2][...] + (1.0 - ADAM_B1) * gv
        vn = ADAM_B2 * ins[3][...] + (1.0 - ADAM_B2) * (gv * gv)
        outs[0][...] = -ADAM_LR * ((mn / d1) / (jnp.sqrt(vn / d2) + ADAM_EPS) + ADAM_WD * wv)
        outs[1][...] = mn
        outs[2][...] = vn

    d, mn, vn = _rowk(name, R, tm, [(w2, "tile"), (g2, "tile"), (m2, "tile"), (v2, "tile")],
                      [((R, W), F32, "tile")] * 3, body)
    return d.reshape(shape), mn.reshape(shape), vn.reshape(shape)


def _first(accs, extras):
    return [accs[0]]


def _swiglu_fwd(accs, extras):
    a, b = accs
    s = _sig(a)
    t = a * s
    return [t, b * (s + t * (1.0 - s)), t * b]


def _ple_fwd(accs, extras):
    b, p, w_proj, x2 = extras
    g = _sig(accs[0] + b)
    q = lax.dot_general(p.astype(BF16), w_proj.astype(BF16), _DOT_DIMS["nn"], preferred_element_type=F32)
    return [g, x2 + g * q]


def _ple_bwd(accs, extras):
    d, g = extras
    dz = d * accs[0] * g * (1.0 - g)
    return [d * g, dz, _colsum(dz)]


def _swiglu_bwd(accs, extras):
    return [accs[0] * extras[1].astype(F32), accs[0] * extras[0].astype(F32)]


def _local_step(x, p, target, wts, small):
    T, D = x.shape
    L, _, PLE = p.shape
    gu, down, cin, sq, qkv_w = wts["gu"], wts["down"], wts["cin"], wts["sq"], wts["qkv"]
    FF = gu.shape[2]
    NA, NC = cin.shape[0], qkv_w.shape[0]
    QW = qkv_w.shape[2]
    KVD = (QW - D) // 2
    n_heads, n_kv = D // HEAD_DIM, KVD // HEAD_DIM
    group = n_heads // n_kv
    nblk = T // QBLOCK
    GC = D // len(POOL_WINDOWS)

    tr = _tile(T, 512, 8)
    trc = _tile(T, 256, 8)
    tmm = _tile(T, 1024)
    tD = _tile(D, 1024)
    tDk = _tile(D, 2048)
    tD2 = _tile(D, 512)
    tF = _tile(FF, 512)
    tFk = _tile(FF, 2816)
    tFw = _tile(FF, 1408)
    tP = _tile(PLE, 512)
    tQ = _tile(QW, 768)
    tT = _tile(T, 1024)

    bucket = jnp.asarray(_t5_bucket_np())
    saved = []
    xs = x

    for i in range(L):
        kind, j = i % 3, i // 3
        sv = {"x": xs}
        h1 = _rms_fwd(f"rms_mix_{i}", xs, small["norm_mix"][i:i + 1], tr)
        if kind == 0:
            a_, gate, u1 = _mm(
                f"conv_in_{i}", "nn", (T, D, D), (tmm, tD2, tDk),
                [(_op(h1), _op(cin, j), 0), (_op(h1), _op(cin, j, 0, D // tD2), 1)], 2,
                lambda accs, ex: (lambda a, g: [a, g, a * _sig(g)])(accs[0] + ex[0], accs[1] + ex[1]),
                [((T, D), BF16, None, None), ((T, D), BF16, None, None), ((T, D), F32, None, None)],
                extras=[(small["conv_b_in"], "row", j, 0), (small["conv_b_in"], "row", j, D // tD2)])
            u2 = _dwconv_fwd(f"dwconv_{i}", u1, small["conv_w_dw"][j], small["conv_b_dw"][j:j + 1], trc)
            u4 = _ln_silu_fwd(f"ln_silu_{i}", u2, small["conv_ln_g"][j:j + 1], small["conv_ln_b"][j:j + 1], tr)
            x1, = _mm(f"conv_out_{i}", "nn", (T, D, D), (tmm, tD2, tDk), [(_op(u4), _op(sq, j), 0)], 1,
                      lambda accs, ex: [accs[0] + ex[0] + ex[1]], [((T, D), F32, None, None)],
                      extras=[(small["conv_b_out"], "row", j, 0), (xs, "tile", None, 0)])
            sv.update(h1=h1, a=a_, gate=gate, u1=u1, u2=u2, u4=u4)
        elif kind == 1:
            mix = _pool_fwd(f"pool_{i}", xs, small["norm_mix"][i:i + 1], trc)
            pw = small["pool_w"][j].reshape(len(POOL_WINDOWS) * GC, GC)
            kb = GC // _tile(GC, 512)
            tg = _tile(GC, 512)
            y0, x1 = _mm(f"pool_mm_{i}", "nn", (T, D, GC), (tmm, GC, tg),
                         [(_op(mix, fn=lambda i_, j_, k_, kb=kb: (i_, j_ * kb + k_)), _op(pw, fn=lambda i_, j_, k_, kb=kb: (j_ * kb + k_, 0)), 0)], 1,
                         lambda accs, ex: [accs[0], ex[1] + accs[0] * ex[0]],
                         [((T, D), F32, None, None), ((T, D), F32, None, None)],
                         extras=[(small["pool_scale"][j:j + 1], "row", None, 0), (xs, "tile", None, 0)])
            sv.update(mix=mix, y0=y0, pw=pw)
        else:
            qkv, = _mm(f"qkv_{i}", "nn", (T, QW, D), (tmm, tQ, tDk), [(_op(h1), _op(qkv_w, j), 0)], 1, _first,
                       [((T, QW), F32, None, None)])
            q_hm = qkv[:, :D].reshape(T, n_heads, HEAD_DIM).transpose(1, 0, 2).reshape(n_heads * T, HEAD_DIM)
            k_hm = qkv[:, D:D + KVD].reshape(T, n_kv, HEAD_DIM).transpose(1, 0, 2).reshape(n_kv * T, HEAD_DIM)
            v_hm = qkv[:, D + KVD:].reshape(T, n_kv, HEAD_DIM).transpose(1, 0, 2).astype(BF16)
            th = _tile(T, 2048, 8)
            qn = _head_norm_fwd(f"qnorm_{i}", q_hm, small["attn_q_norm"][j:j + 1], th).reshape(n_heads, T, HEAD_DIM)
            kn = _head_norm_fwd(f"knorm_{i}", k_hm, small["attn_k_norm"][j:j + 1], th).reshape(n_kv, T, HEAD_DIM)
            bias = _masked_bias(_bias_fwd(small["rel_bias"], bucket).reshape(n_kv, group * QBLOCK, 2 * QBLOCK))
            sink_rows = jnp.broadcast_to(small["attn_sinks"][j].reshape(n_kv, group, 1, 1), (n_kv, group, QBLOCK, 1)).reshape(n_kv, group * QBLOCK, 1)
            o_hm = _attn_fwd(qn, kn, v_hm, bias, sink_rows, n_kv, group, T)
            o = o_hm.transpose(1, 0, 2).reshape(T, D)
            x1, = _mm(f"attn_o_{i}", "nn", (T, D, D), (tmm, tD2, tDk), [(_op(o), _op(sq, NA + j), 0)], 1,
                      lambda accs, ex: [accs[0] + ex[0]], [((T, D), F32, None, None)], extras=[(xs, "tile", None, 0)])
            sv.update(h1=h1, q_hm=q_hm, k_hm=k_hm, v_hm=v_hm, qn=qn, kn=kn, bias=bias, sink_rows=sink_rows, o=o)
        h2 = _rms_fwd(f"rms_ffn_{i}", x1, small["norm_ffn"][i:i + 1], tr)
        a, b, f = _mm(f"ffn_up_{i}", "nn", (T, FF, D), (tmm, tF, tDk), [(_op(h2), _op(gu, i), 0), (_op(h2), _op(gu, L + i), 1)], 2,
                      _swiglu_fwd, [((T, FF), BF16, None, None)] * 3)
        x2, = _mm(f"ffn_down_{i}", "nn", (T, D, FF), (tmm, tD2, tFk), [(_op(f), _op(down, i), 0)], 1,
                  lambda accs, ex: [accs[0] + ex[0]], [((T, D), F32, None, None)], extras=[(x1, "tile", None, 0)])
        h3 = _rms_fwd(f"rms_ple_{i}", x2, small["norm_ple"][i:i + 1], tr)
        gt, x3 = _mm(f"ple_gate_{i}", "nn", (T, D, D), (tmm, tD2, tDk), [(_op(h3), _op(sq, NA + NC + i), 0)], 1, _ple_fwd,
                     [((T, D), F32, None, None), ((T, D), F32, None, None)],
                     extras=[(small["ple_b_gate"], "row", i, 0), (p, "rows", i, 0), (small["ple_w_proj"], "cols", i, 0), (x2, "tile", None, 0)])
        sv.update(x1=x1, h2=h2, a=sv.get("a"), fa=a, fb=b, f=f, x2=x2, h3=h3, gt=gt)
        saved.append(sv)
        xs = x3

    dx, loss = _loss_head(xs, target, tr)

    g_gu = g_down = g_cin = g_sq = g_qkv = None
    gs = {k: [None] * v.shape[0] for k, v in small.items() if k != "rel_bias"}
    gs["rel_bias"] = None
    gs["ple_w_proj"] = [None] * L

    for i in reversed(range(L)):
        kind, j = i % 3, i // 3
        sv = saved[i]
        dq, dz, dbg = _mm(f"ple_bwd_{i}", "nn", (T, D, PLE), (tmm, tD2, tP), [(_op(p, i), _op(small["ple_w_proj"], i), 0)], 1, _ple_bwd,
                          [((T, D), BF16, None, None), ((T, D), BF16, None, None), ((T // tmm, 1, D), F32, "rowsum", None)],
                          extras=[(dx, "tile", None, 0), (sv["gt"], "tile", None, 0)])
        gs["ple_b_gate"][i] = jnp.sum(dbg, axis=0)
        gs["ple_w_proj"][i], = _mm(f"d_ple_proj_{i}", "tn", (PLE, D, T), (tP, tD, tT), [(_op(p, i), _op(dq), 0)], 1, _first,
                                   [((PLE, D), F32, None, None)])
        g_sq, = _mm(f"d_ple_gate_{i}", "tn", (D, D, T), (tD, tD, tT), [(_op(sv["h3"]), _op(dz), 0)], 1, _first,
                    [(sq.shape, BF16, NA + NC + i, g_sq)])
        dh3, = _mm(f"dh_ple_{i}", "nt", (T, D, D), (tmm, tD2, tDk), [(_op(dz), _op(sq, NA + NC + i), 0)], 1, _first,
                   [((T, D), F32, None, None)])
        dx2, gs["norm_ple"][i], dx2b = _rms_bwd(f"rms_ple_bwd_{i}", dx, sv["x2"], small["norm_ple"][i:i + 1], dh3, tr, want_bf16=True)
        da, db = _mm(f"d_ffn_act_{i}", "nt", (T, FF, D), (tmm, tF, tDk), [(_op(dx2b), _op(down, i), 0)], 1, _swiglu_bwd,
                     [((T, FF), BF16, None, None)] * 2, extras=[(sv["fa"], "tile", None, 0), (sv["fb"], "tile", None, 0)])
        g_down, = _mm(f"d_ffn_down_{i}", "tn", (FF, D, T), (tFw, tD, tT), [(_op(sv["f"]), _op(dx2b), 0)], 1, _first,
                      [(down.shape, BF16, i, g_down)])
        g_gu, = _mm(f"d_ffn_gate_{i}", "tn", (D, FF, T), (tD, tFw, tT), [(_op(sv["h2"]), _op(da), 0)], 1, _first,
                    [(gu.shape, BF16, i, g_gu)])
        g_gu, = _mm(f"d_ffn_up_{i}", "tn", (D, FF, T), (tD, tFw, tT), [(_op(sv["h2"]), _op(db), 0)], 1, _first,
                    [(gu.shape, BF16, L + i, g_gu)])
        dh2, = _mm(f"dh_ffn_{i}", "nt", (T, D, FF), (tmm, tD2, tFk), [(_op(da), _op(gu, i), 0), (_op(db), _op(gu, L + i), 0)], 1, _first,
                   [((T, D), F32, None, None)])
        want_cs = kind == 0
        res = _rms_bwd(f"rms_ffn_bwd_{i}", dx2, sv["x1"], small["norm_ffn"][i:i + 1], dh2, tr, want_bf16=True, want_colsum=want_cs)
        dx1, gs["norm_ffn"][i], dx1b = res[:3]
        xin = sv["x"]
        if kind == 0:
            gs["conv_b_out"][j] = res[3]
            g_sq, = _mm(f"d_conv_out_{i}", "tn", (D, D, T), (tD, tD, tT), [(_op(sv["u4"]), _op(dx1b), 0)], 1, _first,
                        [(sq.shape, BF16, j, g_sq)])
            du4, = _mm(f"dh_conv_out_{i}", "nt", (T, D, D), (tmm, tD2, tDk), [(_op(dx1b), _op(sq, j), 0)], 1, _first,
                       [((T, D), F32, None, None)])
            du2, gs["conv_ln_g"][j], gs["conv_ln_b"][j], gs["conv_b_dw"][j] = _ln_silu_bwd(
                f"ln_silu_bwd_{i}", du4, sv["u2"], small["conv_ln_g"][j:j + 1], small["conv_ln_b"][j:j + 1], tr)
            dag, gs["conv_w_dw"][j], gs["conv_b_in"][j] = _dwconv_glu_bwd(
                f"dwconv_bwd_{i}", du2, sv["u1"], sv["a"], sv["gate"], small["conv_w_dw"][j], trc)
            g_cin, = _mm(f"d_conv_in_{i}", "tn", (D, 2 * D, T), (tD, tD, tT), [(_op(sv["h1"]), _op(dag), 0)], 1, _first,
                         [(cin.shape, BF16, j, g_cin)])
            dh1, = _mm(f"dh_conv_in_{i}", "nt", (T, D, 2 * D), (tmm, tD2, tDk), [(_op(dag), _op(cin, j), 0)], 1, _first,
                       [((T, D), F32, None, None)])
        elif kind == 1:
            dys, gs["pool_scale"][j] = _pool_scale_bwd(f"pool_scale_bwd_{i}", dx1, sv["y0"], small["pool_scale"][j:j + 1], tr)
            tg = _tile(GC, 512)
            kb = GC // tg
            dmix, = _mm(f"dh_pool_{i}", "nt", (T, D, GC), (tmm, GC, tg),
                        [(_op(dys, fn=lambda i_, j_, k_, kb=kb: (i_, j_ * kb + k_)), _op(sv["pw"]), 0)], 1, _first,
                        [((T, D), F32, None, None)])
            ng = len(POOL_WINDOWS)
            gs["pool_w"][j], = _mm(f"d_pool_w_{i}", "tn", (D, GC, T), (GC, GC, tT),
                                   [(_op(sv["mix"]), _op(dys, fn=lambda i_, j_, k_: (k_, i_)), 0)], 1, _first,
                                   [((D, GC), F32, None, None)])
            gs["pool_w"][j] = gs["pool_w"][j].reshape(ng, GC, GC)
            dh1 = _pool_bwd(f"pool_bwd_{i}", dmix, trc)
        else:
            g_sq, = _mm(f"d_attn_o_{i}", "tn", (D, D, T), (tD, tD, tT), [(_op(sv["o"]), _op(dx1b), 0)], 1, _first,
                        [(sq.shape, BF16, NA + j, g_sq)])
            do, = _mm(f"dh_attn_o_{i}", "nt", (T, D, D), (tmm, tD2, tDk), [(_op(dx1b), _op(sq, NA + j), 0)], 1, _first,
                      [((T, D), BF16, None, None)])
            do_hm = do.reshape(T, n_heads, HEAD_DIM).transpose(1, 0, 2)
            dqn, dko, dkp, dvo, dvp, dbias, dsink = _attn_bwd(sv["qn"], sv["kn"], sv["v_hm"], sv["bias"], sv["sink_rows"], do_hm, n_kv, group, T)
            th = _tile(T, 2048, 8)
            dq_hm, gs["attn_q_norm"][j] = _head_norm_bwd(f"qnorm_bwd_{i}", dqn.reshape(n_heads * T, HEAD_DIM), sv["q_hm"],
                                                         small["attn_q_norm"][j:j + 1], th)
            dk_hm, gs["attn_k_norm"][j] = _band_merge(f"knorm_bwd_{i}", dko, dkp, sv["k_hm"].reshape(n_kv, T, HEAD_DIM), small["attn_k_norm"][j:j + 1])
            dv_hm, = _band_merge(f"v_merge_{i}", dvo, dvp)
            gs["attn_sinks"][j] = jnp.sum(dsink.reshape(n_heads, QBLOCK), axis=1).reshape(1, n_heads)
            rb = _bias_bwd(dbias.reshape(n_heads, QBLOCK, 2 * QBLOCK), bucket, NUM_BUCKETS).T
            gs["rel_bias"] = rb if gs["rel_bias"] is None else gs["rel_bias"] + rb
            tok = lambda t_, nh: t_.reshape(nh, T, HEAD_DIM).transpose(1, 0, 2).reshape(T, nh * HEAD_DIM)
            dqkv = jnp.concatenate([tok(dq_hm, n_heads), tok(dk_hm, n_kv), tok(dv_hm, n_kv)], axis=1)
            g_qkv, = _mm(f"d_qkv_{i}", "tn", (D, QW, T), (tD, tQ, tT), [(_op(sv["h1"]), _op(dqkv), 0)], 1, _first,
                         [(qkv_w.shape, BF16, j, g_qkv)])
            dh1, = _mm(f"dh_qkv_{i}", "nt", (T, D, QW), (tmm, tD2, _tile(QW, 3072)), [(_op(dqkv), _op(qkv_w, j), 0)], 1, _first,
                       [((T, D), F32, None, None)])
        dx, gs["norm_mix"][i] = _rms_bwd(f"rms_mix_bwd_{i}", dx1, xin, small["norm_mix"][i:i + 1], dh1, tr)

    big = {"gu": g_gu, "down": g_down, "cin": g_cin, "sq": g_sq, "qkv": g_qkv}
    gsmall = {}
    for k, v in gs.items():
        if k == "rel_bias":
            gsmall[k] = v
        else:
            gsmall[k] = jnp.stack([t.reshape(small[k].shape[1:]) for t in v], axis=0)
    return loss, dx, big, gsmall


_ANY = pl.BlockSpec(memory_space=pl.ANY)


def _place():
    x, y, c = lax.axis_index("x"), lax.axis_index("y"), lax.axis_index("c")
    return x, y, c, [(1 - x, y), (x, 1 - y), (1 - x, 1 - y)]


def _lane_start(s, w):
    return pl.multiple_of(s * w, LANES) if w % LANES == 0 else s * w


def _slot(ref, kind, s):
    if kind == "col":
        w = ref.shape[2] // N_SLOTS
        return ref.at[:, :, pl.ds(_lane_start(s, w), w)]
    return ref.at[:, pl.ds(s, 1)]


def _rows_quarter(ref, h, quarter):
    n = ref.shape[-2] // 4
    if len(ref.shape) == 3:
        return ref.at[:, pl.ds((2 * h + quarter) * n, n), :]
    return ref.at[:, :, pl.ds((2 * h + quarter) * n, n), :]


def _rows_half(ref, h):
    n = ref.shape[-2] // 2
    if len(ref.shape) == 3:
        return ref.at[:, pl.ds(h * n, n), :]
    return ref.at[:, :, pl.ds(h * n, n), :]


def _place_own(name, shard, kind, s_arr):
    if kind == "col":
        lead, R, W = shard.shape
        full = (lead, R, N_SLOTS * W)
    else:
        lead, _, R, W = shard.shape
        full = (lead, N_SLOTS, R, W)
    tr = _tile(R, max(16, (1 << 19) // W), 16)
    if kind == "col":
        i_spec = pl.BlockSpec((None, tr, W), lambda l, i, s: (l, i, 0))
        o_spec = pl.BlockSpec((None, tr, W), lambda l, i, s: (l, i, s[0]))
    else:
        i_spec = pl.BlockSpec((None, None, tr, W), lambda l, i, s: (l, 0, i, 0))
        o_spec = pl.BlockSpec((None, None, tr, W), lambda l, i, s: (l, s[0], i, 0))

    def body(s_ref, i_ref, o_ref):
        o_ref[...] = i_ref[...]

    return pl.pallas_call(
        body, name=name,
        grid_spec=pltpu.PrefetchScalarGridSpec(num_scalar_prefetch=1, grid=(lead, R // tr), in_specs=[i_spec], out_specs=o_spec),
        out_shape=jax.ShapeDtypeStruct(full, shard.dtype), compiler_params=_cparams(("arbitrary", "arbitrary")),
    )(s_arr, shard)


def _gather(shards, fulls, kinds):
    ng = len(shards)
    n_sem = 8

    def body(*refs):
        sh, out = refs[:ng], refs[2 * ng:3 * ng]
        send, recv = refs[3 * ng:]
        x, y, c, _ = _place()
        s, s_x, s_y, s_d = 2 * x + y, 2 * (1 - x) + y, 2 * x + (1 - y), 2 * (1 - x) + (1 - y)
        to_x, to_y, sib = (1 - x, y, c), (x, 1 - y, c), (x, y, 1 - c)

        def rcopy(g, k, src, dst, dev):
            return pltpu.make_async_remote_copy(src_ref=src, dst_ref=dst, send_sem=send.at[g * n_sem + k], recv_sem=recv.at[g * n_sem + k],
                                                device_id=dev, device_id_type=MESH)

        def win(g, slot, h, quarter=None):
            w = _slot(out[g], kinds[g], slot)
            return _rows_half(w, h) if quarter is None else _rows_quarter(w, h, quarter)

        sent = []

        def go(cp):
            cp.start()
            sent.append(cp)

        for g in range(ng):
            go(rcopy(g, 0, _rows_half(sh[g], c), win(g, s, c), to_x))
            go(rcopy(g, 1, _rows_half(sh[g], c), win(g, s, c), to_y))
        for g in range(ng):
            rcopy(g, 0, win(g, s_x, c), win(g, s_x, c), to_x).wait_recv()
            go(rcopy(g, 3, win(g, s_x, c, 1), win(g, s_x, c, 1), to_y))
            go(rcopy(g, 4, win(g, s_x, c), win(g, s_x, c), sib))
            rcopy(g, 1, win(g, s_y, c), win(g, s_y, c), to_y).wait_recv()
            go(rcopy(g, 2, win(g, s_y, c, 0), win(g, s_y, c, 0), to_x))
            go(rcopy(g, 5, win(g, s_y, c), win(g, s_y, c), sib))
        for g in range(ng):
            for k, quarter in ((2, 0), (3, 1)):
                rcopy(g, k, win(g, s_d, c, quarter), win(g, s_d, c, quarter), sib).wait_recv()
                go(rcopy(g, 6 + quarter, win(g, s_d, c, quarter), win(g, s_d, c, quarter), sib))
        for g in range(ng):
            rcopy(g, 4, win(g, s_x, 1 - c), win(g, s_x, 1 - c), sib).wait_recv()
            rcopy(g, 5, win(g, s_y, 1 - c), win(g, s_y, 1 - c), sib).wait_recv()
            for quarter in (0, 1):
                rcopy(g, 6 + quarter, win(g, s_d, 1 - c, quarter), win(g, s_d, 1 - c, quarter), sib).wait_recv()
        for cp in sent:
            cp.wait_send()

    return pl.pallas_call(
        body, name="gather_weights", in_specs=[_ANY] * (2 * ng), out_specs=[_ANY] * ng,
        out_shape=[jax.ShapeDtypeStruct(a.shape, a.dtype) for a in fulls],
        input_output_aliases={ng + g: g for g in range(ng)},
        scratch_shapes=[pltpu.SemaphoreType.DMA((n_sem * ng,)), pltpu.SemaphoreType.DMA((n_sem * ng,))],
    )(*shards, *fulls)


def _pair_send(grads):
    ng = len(grads)

    def half_shape(a):
        s = list(a.shape)
        s[-2] //= 2
        return tuple(s)

    def body(*refs):
        gr, out = refs[:ng], refs[ng:2 * ng]
        send, recv = refs[2 * ng:]
        x, y, c, _ = _place()
        cps = [pltpu.make_async_remote_copy(src_ref=_rows_half(gr[g], 1 - c), dst_ref=out[g], send_sem=send.at[g], recv_sem=recv.at[g],
                                            device_id=(x, y, 1 - c), device_id_type=MESH) for g in range(ng)]
        for cp in cps:
            cp.start()
        for cp in cps:
            cp.wait()

    return pl.pallas_call(
        body, name="grad_pair_send", in_specs=[_ANY] * ng, out_specs=[_ANY] * ng,
        out_shape=[jax.ShapeDtypeStruct(half_shape(a), a.dtype) for a in grads],
        scratch_shapes=[pltpu.SemaphoreType.DMA((ng,)), pltpu.SemaphoreType.DMA((ng,))],
    )(*grads)


def _add_half(name, g3, pa3, c_arr):
    n, R, N = g3.shape
    rh = R // 2
    tr = _tile(rh, max(16, (1 << 19) // N), 16)
    nb = rh // tr

    def body(c_ref, g_ref, p_ref, o_ref):
        o_ref[...] = (g_ref[...].astype(F32) + p_ref[...].astype(F32)).astype(o_ref.dtype)

    return pl.pallas_call(
        body, name=name,
        grid_spec=pltpu.PrefetchScalarGridSpec(
            num_scalar_prefetch=1, grid=(n, nb),
            in_specs=[pl.BlockSpec((None, tr, N), lambda l, i, c, nb=nb: (l, c[0] * nb + i, 0)), pl.BlockSpec((None, tr, N), lambda l, i, c: (l, i, 0))],
            out_specs=pl.BlockSpec((None, tr, N), lambda l, i, c: (l, i, 0))),
        out_shape=jax.ShapeDtypeStruct(pa3.shape, g3.dtype), compiler_params=_cparams(("arbitrary", "arbitrary")),
    )(c_arr, g3, pa3)


def _quarter_shape(a, kind):
    s = a.shape
    return (s[0], s[1] // 2, s[2] // N_SLOTS) if kind == "col" else (s[0], 1, s[2] // 2, s[3])


def _ici_exchange_direct(psums, kinds):
    ng = len(psums)

    def body(*refs):
        ps, direct, relay = refs[:ng], refs[ng:2 * ng], refs[2 * ng:3 * ng]
        send, recv = refs[3 * ng:]
        x, y, c, _ = _place()
        s_x, s_y, s_d = 2 * (1 - x) + y, 2 * x + (1 - y), 2 * (1 - x) + (1 - y)
        to_x, to_y = (1 - x, y, c), (x, 1 - y, c)
        cps = []
        for g in range(ng):
            quarter = lambda slot, q, g=g: _rows_half(_slot(ps[g], kinds[g], slot), q)
            plan = ((quarter(s_x, 1), direct[g].at[0], to_x), (quarter(s_y, 0), direct[g].at[1], to_y),
                    (quarter(s_d, 1), relay[g].at[0], to_x), (quarter(s_d, 0), relay[g].at[1], to_y))
            for k, (src, dst, dev) in enumerate(plan):
                cps.append(pltpu.make_async_remote_copy(src_ref=src, dst_ref=dst, send_sem=send.at[g * 4 + k], recv_sem=recv.at[g * 4 + k],
                                                        device_id=dev, device_id_type=MESH))
                cps[-1].start()
        for cp in cps:
            cp.wait()

    shapes = [jax.ShapeDtypeStruct((2,) + _quarter_shape(a, k), a.dtype) for a, k in zip(psums, kinds)]
    res = pl.pallas_call(
        body, name="grad_ici_direct", in_specs=[_ANY] * ng, out_specs=[_ANY] * (2 * ng), out_shape=shapes + shapes,
        scratch_shapes=[pltpu.SemaphoreType.DMA((4 * ng,)), pltpu.SemaphoreType.DMA((4 * ng,))],
    )(*psums)
    return res[:ng], res[ng:]


def _ici_exchange_relayed(sums):
    ng = len(sums)

    def body(*refs):
        sm, out = refs[:ng], refs[ng:2 * ng]
        send, recv = refs[2 * ng:]
        x, y, c, _ = _place()
        cps = []
        for g in range(ng):
            for k, (t, dev) in enumerate(((1, (1 - x, y, c)), (0, (x, 1 - y, c)))):
                cps.append(pltpu.make_async_remote_copy(src_ref=sm[g].at[t], dst_ref=out[g].at[k], send_sem=send.at[g * 2 + k],
                                                        recv_sem=recv.at[g * 2 + k], device_id=dev, device_id_type=MESH))
                cps[-1].start()
        for cp in cps:
            cp.wait()

    return pl.pallas_call(
        body, name="grad_ici_relayed", in_specs=[_ANY] * ng, out_specs=[_ANY] * ng,
        out_shape=[jax.ShapeDtypeStruct(a.shape, a.dtype) for a in sums],
        scratch_shapes=[pltpu.SemaphoreType.DMA((2 * ng,)), pltpu.SemaphoreType.DMA((2 * ng,))],
    )(*sums)


def _quarter_tiles(p3, w):
    rq = p3.shape[1] // 2
    tr = _tile(rq, max(16, (1 << 18) // w), 16)
    return rq, tr, rq // tr


def _p_spec(kind, tr, w, row_block, slot):
    if kind == "col":
        return pl.BlockSpec((None, tr, w), lambda t, l, i, s: (l, row_block(t, i), slot(t, s)))
    return pl.BlockSpec((None, tr, w), lambda t, l, i, s: (l * N_SLOTS + slot(t, s), row_block(t, i), 0))


def _relay_add(name, p3, relay3, yx_arr, lead, kind):
    w = relay3.shape[2]
    rq, tr, nbq = _quarter_tiles(p3, w)
    two = pl.BlockSpec((None, tr, w), lambda t, l, i, s: (t * lead + l, i, 0))

    def body(s_ref, p_ref, r_ref, o_ref):
        o_ref[...] = (p_ref[...].astype(F32) + r_ref[...].astype(F32)).astype(o_ref.dtype)

    return pl.pallas_call(
        body, name=name,
        grid_spec=pltpu.PrefetchScalarGridSpec(
            num_scalar_prefetch=1, grid=(2, lead, nbq),
            in_specs=[_p_spec(kind, tr, w, lambda t, i: (1 - t) * nbq + i, lambda t, s: jnp.where(t == 0, s[0], s[1])), two], out_specs=two),
        out_shape=jax.ShapeDtypeStruct(relay3.shape, relay3.dtype), compiler_params=_cparams(("arbitrary",) * 3),
    )(yx_arr, p3, relay3)


def _slot_sum(name, p3, direct3, relayed3, s_arr, lead, kind):
    w = direct3.shape[2]
    rq, tr, nbq = _quarter_tiles(p3, w)
    d_spec = pl.BlockSpec((None, tr, w), lambda t, l, i, s: ((1 - t) * lead + l, i, 0))
    r_spec = pl.BlockSpec((None, tr, w), lambda t, l, i, s: (t * lead + l, i, 0))

    def body(s_ref, p_ref, d_ref, r_ref, o_ref):
        o_ref[...] = (p_ref[...].astype(F32) + d_ref[...].astype(F32)) + r_ref[...].astype(F32)

    return pl.pallas_call(
        body, name=name,
        grid_spec=pltpu.PrefetchScalarGridSpec(
            num_scalar_prefetch=1, grid=(2, lead, nbq),
            in_specs=[_p_spec(kind, tr, w, lambda t, i: t * nbq + i, lambda t, s: s[0]), d_spec, r_spec],
            out_specs=pl.BlockSpec((None, tr, w), lambda t, l, i, s: (l, t * nbq + i, 0))),
        out_shape=jax.ShapeDtypeStruct((lead, 2 * rq, w), F32), compiler_params=_cparams(("arbitrary",) * 3),
    )(s_arr, p3, direct3, relayed3)


def _pair_swap(halves):
    ng = len(halves)

    def body(*refs):
        hv, out = refs[:ng], refs[ng:2 * ng]
        send, recv = refs[2 * ng:]
        x, y, c, _ = _place()
        cps = [pltpu.make_async_remote_copy(src_ref=hv[g], dst_ref=out[g], send_sem=send.at[g], recv_sem=recv.at[g],
                                            device_id=(x, y, 1 - c), device_id_type=MESH) for g in range(ng)]
        for cp in cps:
            cp.start()
        for cp in cps:
            cp.wait()

    return pl.pallas_call(
        body, name="grad_pair_swap", in_specs=[_ANY] * ng, out_specs=[_ANY] * ng,
        out_shape=[jax.ShapeDtypeStruct(a.shape, a.dtype) for a in halves],
        scratch_shapes=[pltpu.SemaphoreType.DMA((ng,)), pltpu.SemaphoreType.DMA((ng,))],
    )(*halves)


N_DEVICES = 8


def _allreduce_small(v):
    rows, m = v.shape

    def body(v_ref, o_ref, buf, send, recv):
        x, y, c, _ = _place()
        me = 4 * x + 2 * y + c
        buf[me] = v_ref[...]
        cps = []
        for k in range(1, N_DEVICES):
            peer = me ^ k
            cps.append(pltpu.make_async_remote_copy(src_ref=v_ref, dst_ref=buf.at[me], send_sem=send.at[k - 1], recv_sem=recv.at[k - 1],
                                                    device_id=((peer >> 2) & 1, (peer >> 1) & 1, peer & 1), device_id_type=MESH))
            cps[-1].start()
        for k in range(1, N_DEVICES):
            theirs = buf.at[me ^ k]
            pltpu.make_async_remote_copy(src_ref=v_ref, dst_ref=theirs, send_sem=send.at[k - 1], recv_sem=recv.at[k - 1],
                                         device_id=(x, y, c), device_id_type=MESH).wait_recv()
        for cp in cps:
            cp.wait_send()
        acc = buf[0]
        for d in range(1, N_DEVICES):
            acc = acc + buf[d]
        o_ref[...] = acc

    vm = pl.BlockSpec(memory_space=pltpu.VMEM)
    return pl.pallas_call(
        body, name="allreduce_small", in_specs=[vm], out_specs=vm, out_shape=jax.ShapeDtypeStruct(v.shape, F32),
        scratch_shapes=[pltpu.VMEM((N_DEVICES, rows, m), F32), pltpu.SemaphoreType.DMA((N_DEVICES - 1,)), pltpu.SemaphoreType.DMA((N_DEVICES - 1,))],
    )(v)


def _pad_rows(a, mult):
    r = (-a.shape[0]) % mult
    return a if r == 0 else jnp.concatenate([a, jnp.zeros((r,) + a.shape[1:], a.dtype)], axis=0)


def _pack_rows(parts, width, mult=16):
    rows, offs, at = [], [], 0
    for a in parts:
        a2 = _pad_rows(a.reshape(-1, width), mult)
        offs.append((at, a.size // width))
        rows.append(a2)
        at += a2.shape[0]
    return _pad_rows(jnp.concatenate(rows, axis=0), 4 * SUBLANES), offs


SMALL_SHARDED = ("ple_w_proj", "pool_w", "conv_w_dw", "conv_b_dw", "conv_ln_g", "conv_ln_b", "conv_b_out", "conv_b_in")
SMALL_REPLICATED = ("norm_mix", "norm_ffn", "norm_ple", "pool_scale", "attn_q_norm", "attn_k_norm", "attn_sinks", "rel_bias", "ple_b_gate")


def _small_to_full(name, slots):
    if name == "pool_w":
        return jnp.moveaxis(slots, 0, 2).reshape(slots.shape[1], slots.shape[2], N_SLOTS * slots.shape[3], slots.shape[4])
    return jnp.moveaxis(slots, 0, -2).reshape(slots.shape[1:-1] + (N_SLOTS * slots.shape[-1],))


def _small_to_slots(name, full):
    if name == "pool_w":
        nb, ng, gc, _ = full.shape
        return jnp.moveaxis(full.reshape(nb, ng, N_SLOTS, gc // N_SLOTS, gc), 2, 0)
    w = full.shape[-1] // N_SLOTS
    return jnp.moveaxis(full.reshape(full.shape[:-1] + (N_SLOTS, w)), -2, 0)


W_NAMES = ("norm_mix", "norm_ffn", "norm_ple", "conv_w_in", "conv_b_in", "conv_w_dw", "conv_b_dw", "conv_ln_g", "conv_ln_b", "conv_w_out",
           "conv_b_out", "pool_w", "pool_scale", "attn_w_qkv", "attn_q_norm", "attn_k_norm", "attn_sinks", "attn_w_o", "rel_bias",
           "ffn_w_gate", "ffn_w_up", "ffn_w_down", "ple_w_proj", "ple_w_gate", "ple_b_gate")


def _step(x, p, target, w, m, v):
    T, D = x.shape[1], x.shape[2]
    L = p.shape[0]
    NA, NC = w["conv_w_in"].shape[0], w["attn_w_qkv"].shape[0]
    xi, yi, ci = lax.axis_index("x"), lax.axis_index("y"), lax.axis_index("c")
    c_arr = jnp.reshape(ci, (1,)).astype(jnp.int32)
    s_arr = jnp.reshape(2 * xi + yi, (1,)).astype(jnp.int32)

    wq = D // N_SLOTS
    sm_pack, sm_offs = _pack_rows([w[k] for k in SMALL_SHARDED], wq)
    shards = [
        jnp.concatenate([w["ffn_w_gate"], w["ffn_w_up"]], axis=0).astype(BF16),
        w["ffn_w_down"].astype(BF16)[:, None],
        w["conv_w_in"].astype(BF16),
        jnp.concatenate([w["conv_w_out"], w["attn_w_o"], w["ple_w_gate"]], axis=0).astype(BF16)[:, None],
        w["attn_w_qkv"].astype(BF16),
        sm_pack[None, None],
    ]
    kinds = ["col", "row", "col", "row", "col", "row"]
    full = _gather(shards, [_place_own(f"place_own_{g}", a, k, s_arr) for g, (a, k) in enumerate(zip(shards, kinds))], kinds)
    wts = {"gu": full[0], "down": full[1].reshape(L, -1, D), "cin": full[2], "sq": full[3].reshape(NA + NC + L, D, D), "qkv": full[4]}
    small = {k: w[k] for k in SMALL_REPLICATED}
    for k, (at, n) in zip(SMALL_SHARDED, sm_offs):
        small[k] = _small_to_full(k, full[5][0, :, at:at + n].reshape((N_SLOTS,) + w[k].shape))

    loss, dx, big, gsmall = _local_step(x[0], p[:, 0], target[0], wts, small)

    rep_parts = [gsmall[k] for k in SMALL_REPLICATED] + [loss]
    flat = jnp.concatenate([a.reshape(-1) for a in rep_parts])
    n_flat = flat.shape[0]
    m_cols = -(-n_flat // (8 * LANES)) * LANES
    flat = jnp.concatenate([flat, jnp.zeros((8 * m_cols - n_flat,), F32)]).reshape(8, m_cols)
    red = _allreduce_small(flat).reshape(-1)
    grads, at = {}, 0
    for k in SMALL_REPLICATED:
        grads[k] = red[at:at + w[k].size].reshape(w[k].shape)
        at += w[k].size
    loss_out = red[at]

    slots = {k: _small_to_slots(k, gsmall[k]) for k in SMALL_SHARDED}
    gsm = jnp.stack([_pack_rows([slots[k][s] for k in SMALL_SHARDED], wq)[0] for s in range(N_SLOTS)], axis=0)
    local = [big["gu"], big["down"].reshape(L, N_SLOTS, -1, D), big["cin"], big["sq"].reshape(NA + NC + L, N_SLOTS, -1, D), big["qkv"],
             gsm[None]]
    theirs = _pair_send(local)
    psums = []
    for g, (a, t) in enumerate(zip(local, theirs)):
        if kinds[g] == "col":
            psums.append(_add_half(f"pair_add_{g}", a, t, c_arr))
        else:
            n4 = a.shape[0] * N_SLOTS
            psums.append(_add_half(f"pair_add_{g}", a.reshape(n4, a.shape[2], a.shape[3]), t.reshape(n4, t.shape[2], t.shape[3]), c_arr).reshape(t.shape))
    yx_arr = jnp.stack([2 * xi + (1 - yi), 2 * (1 - xi) + yi]).astype(jnp.int32)
    direct, relay = _ici_exchange_direct(psums, kinds)
    p3s = [ps if k == "col" else ps.reshape(ps.shape[0] * N_SLOTS, ps.shape[2], ps.shape[3]) for ps, k in zip(psums, kinds)]
    flat3 = lambda a: a.reshape(2 * a.shape[1], a.shape[-2], a.shape[-1])
    sums = [_relay_add(f"relay_add_{g}", p3, flat3(rl), yx_arr, rl.shape[1], kinds[g]).reshape(rl.shape) for g, (p3, rl) in enumerate(zip(p3s, relay))]
    relayed = _ici_exchange_relayed(sums)
    halves = [_slot_sum(f"slot_sum_{g}", p3, flat3(d), flat3(r), s_arr, d.shape[1], kinds[g]) for g, (p3, d, r) in enumerate(zip(p3s, direct, relayed))]
    first = ci == 0
    gsh = [jnp.concatenate([jnp.where(first, a, b), jnp.where(first, b, a)], axis=1) for a, b in zip(halves, _pair_swap(halves))]
    grads["ffn_w_gate"], grads["ffn_w_up"] = gsh[0][:L], gsh[0][L:]
    grads["ffn_w_down"] = gsh[1]
    grads["conv_w_in"] = gsh[2]
    grads["conv_w_out"], grads["attn_w_o"], grads["ple_w_gate"] = gsh[3][:NA], gsh[3][NA:NA + NC], gsh[3][NA + NC:]
    grads["attn_w_qkv"] = gsh[4]
    for k, (at, n) in zip(SMALL_SHARDED, sm_offs):
        grads[k] = gsh[5][0, at:at + n].reshape(w[k].shape)

    outs_d, outs_m, outs_v = [], [], []
    for k in W_NAMES:
        d_, m_, v_ = _adamw(f"adamw_{k}", w[k], grads[k], m[k], v[k])
        outs_d.append(d_)
        outs_m.append(m_)
        outs_v.append(v_)
    return (loss_out, dx[None], *[grads[k] for k in W_NAMES], *outs_d, *outs_m, *outs_v)


def kernel(x, p, norm_mix, norm_ffn, norm_ple, conv_w_in, conv_b_in, conv_w_dw, conv_b_dw, conv_ln_g, conv_ln_b, conv_w_out, conv_b_out, pool_w, pool_scale, attn_w_qkv, attn_q_norm, attn_k_norm, attn_sinks, attn_w_o, rel_bias, ffn_w_gate, ffn_w_up, ffn_w_down, ple_w_proj, ple_w_gate, ple_b_gate, loss_target, m_norm_mix, m_norm_ffn, m_norm_ple, m_conv_w_in, m_conv_b_in, m_conv_w_dw, m_conv_b_dw, m_conv_ln_g, m_conv_ln_b, m_conv_w_out, m_conv_b_out, m_pool_w, m_pool_scale, m_attn_w_qkv, m_attn_q_norm, m_attn_k_norm, m_attn_sinks, m_attn_w_o, m_rel_bias, m_ffn_w_gate, m_ffn_w_up, m_ffn_w_down, m_ple_w_proj, m_ple_w_gate, m_ple_b_gate, v_norm_mix, v_norm_ffn, v_norm_ple, v_conv_w_in, v_conv_b_in, v_conv_w_dw, v_conv_b_dw, v_conv_ln_g, v_conv_ln_b, v_conv_w_out, v_conv_b_out, v_pool_w, v_pool_scale, v_attn_w_qkv, v_attn_q_norm, v_attn_k_norm, v_attn_sinks, v_attn_w_o, v_rel_bias, v_ffn_w_gate, v_ffn_w_up, v_ffn_w_down, v_ple_w_proj, v_ple_w_gate, v_ple_b_gate):
    ws_ = (norm_mix, norm_ffn, norm_ple, conv_w_in, conv_b_in, conv_w_dw, conv_b_dw, conv_ln_g, conv_ln_b, conv_w_out, conv_b_out, pool_w, pool_scale, attn_w_qkv, attn_q_norm, attn_k_norm, attn_sinks, attn_w_o, rel_bias, ffn_w_gate, ffn_w_up, ffn_w_down, ple_w_proj, ple_w_gate, ple_b_gate)
    ms_ = (m_norm_mix, m_norm_ffn, m_norm_ple, m_conv_w_in, m_conv_b_in, m_conv_w_dw, m_conv_b_dw, m_conv_ln_g, m_conv_ln_b, m_conv_w_out, m_conv_b_out, m_pool_w, m_pool_scale, m_attn_w_qkv, m_attn_q_norm, m_attn_k_norm, m_attn_sinks, m_attn_w_o, m_rel_bias, m_ffn_w_gate, m_ffn_w_up, m_ffn_w_down, m_ple_w_proj, m_ple_w_gate, m_ple_b_gate)
    vs_ = (v_norm_mix, v_norm_ffn, v_norm_ple, v_conv_w_in, v_conv_b_in, v_conv_w_dw, v_conv_b_dw, v_conv_ln_g, v_conv_ln_b, v_conv_w_out, v_conv_b_out, v_pool_w, v_pool_scale, v_attn_w_qkv, v_attn_q_norm, v_attn_k_norm, v_attn_sinks, v_attn_w_o, v_rel_bias, v_ffn_w_gate, v_ffn_w_up, v_ffn_w_down, v_ple_w_proj, v_ple_w_gate, v_ple_b_gate)
    return _step(x, p, loss_target, dict(zip(W_NAMES, ws_)), dict(zip(W_NAMES, ms_)), dict(zip(W_NAMES, vs_)))
```
